```python
import jax
import jax.numpy as jnp
from jax import lax
import numpy as np

D_MODEL = 1024
BATCH = 2
SEQ = 8192
DEPTH = 2

GRID_W = 64
CTX_LEN = 256
EPS = 1e-6

LSTM_HEADS = 4
LSTM_DIM = 128
LSTM_WIDTH = LSTM_HEADS * LSTM_DIM
LSTM_CONV = 5
LSTM_CHUNK = 128

ATTN_HEADS = 8
ATTN_KV_HEADS = 2
ATTN_GROUP = ATTN_HEADS // ATTN_KV_HEADS
ATTN_DIM = 64
ATTN_WINDOW = 128
ATTN_BLOCK = 128
ROPE_BASE = 10000.0

AB_SPLITS = (LSTM_WIDTH, LSTM_WIDTH, LSTM_WIDTH, LSTM_WIDTH, 4 * LSTM_HEADS,
             ATTN_HEADS * ATTN_DIM, ATTN_KV_HEADS * ATTN_DIM, ATTN_KV_HEADS * ATTN_DIM)
AB_IN = sum(AB_SPLITS)
AB_OFFSETS = tuple(sum(AB_SPLITS[:i + 1]) for i in range(len(AB_SPLITS) - 1))
AB_MIX = LSTM_WIDTH + ATTN_HEADS * ATTN_DIM

GM_CHUNK = 128
GM_GROUPS = 8
GM_HALF = 2 * D_MODEL

N_EXPERTS = 16
EC_FACTOR = 2
D_EXPERT = D_MODEL

kernel_name = 'hybrid_mlstm_swa_gmlp_ec_dit'


def rmsnorm(x, g):
    xf = x.astype(jnp.float32)
    y = xf * lax.rsqrt(jnp.mean(xf * xf, axis=-1, keepdims=True) + EPS)
    return (y * g.astype(jnp.float32)).astype(x.dtype)


def layernorm(x, g, b):
    xf = x.astype(jnp.float32)
    mu = jnp.mean(xf, axis=-1, keepdims=True)
    var = jnp.mean(jnp.square(xf - mu), axis=-1, keepdims=True)
    return ((xf - mu) * lax.rsqrt(var + EPS) * g.astype(jnp.float32) + b.astype(jnp.float32)).astype(x.dtype)


def adaln(cond, w, b):
    m = jax.nn.silu(cond) @ w + b
    return jnp.split(m[..., None, :], 6, axis=-1)


def modulate(x, g, shift, scale):
    return rmsnorm(x, g) * (1 + scale) + shift


def axial_rope(n, dim):
    rows = n // GRID_W
    row = jnp.repeat(jnp.arange(rows), GRID_W).astype(jnp.float32)
    col = jnp.tile(jnp.arange(GRID_W), rows).astype(jnp.float32)
    nf = dim // 4
    inv = ROPE_BASE ** (-jnp.arange(nf, dtype=jnp.float32) / nf)
    ang = jnp.concatenate([row[:, None] * inv, col[:, None] * inv], axis=-1)
    return jnp.cos(ang), jnp.sin(ang)


def apply_rope(x, cos, sin):
    x1, x2 = jnp.split(x.astype(jnp.float32), 2, axis=-1)
    c = cos[None, :, None, :]
    s = sin[None, :, None, :]
    return jnp.concatenate([x1 * c - x2 * s, x2 * c + x1 * s], axis=-1).astype(x.dtype)


def centred_depthwise_conv(x, w):
    return lax.conv_general_dilated(
        x, w[:, None, :].astype(x.dtype), window_strides=(1,),
        padding=[(LSTM_CONV // 2, LSTM_CONV // 2)],
        dimension_numbers=('NWC', 'WIO', 'NWC'), feature_group_count=x.shape[-1])


def zero_state(batch):
    return (jnp.zeros((batch, LSTM_HEADS, LSTM_DIM, LSTM_DIM), jnp.float32),
            jnp.zeros((batch, LSTM_HEADS, LSTM_DIM), jnp.float32),
            jnp.zeros((batch, LSTM_HEADS), jnp.float32))


def mlstm_state_update(state, k, v, logi, logf):
    cm, nv, m = state
    b = jnp.cumsum(logf, axis=-1)
    g = b[..., -1]
    w = g[..., None] - b + logi
    m_new = jnp.maximum(g + m, jnp.max(w, axis=-1))
    decay = jnp.exp(g + m - m_new)
    wt = jnp.exp(w - m_new[..., None])
    c_new = decay[..., None, None] * cm + jnp.einsum('bhl,bhlv,bhlk->bhvk', wt, v, k)
    n_new = decay[..., None] * nv + jnp.einsum('bhl,bhlk->bhk', wt, k)
    return (c_new, n_new, m_new)


def mlstm_chunk(state, q, k, v, logi, logf):
    cm, nv, m = state
    length = q.shape[-2]
    b = jnp.cumsum(logf, axis=-1)
    order = jnp.tril(jnp.ones((length, length), bool))
    log_d = jnp.where(order, b[..., :, None] - b[..., None, :] + logi[..., None, :], -jnp.inf)
    log_inter = b + m[..., None]
    m_row = jnp.maximum(log_inter, jnp.max(log_d, axis=-1))
    s = jnp.einsum('bhqd,bhkd->bhqk', q, k) * jnp.exp(log_d - m_row[..., None])
    a = jnp.exp(log_inter - m_row)
    num = jnp.einsum('bhqk,bhkv->bhqv', s, v) + a[..., None] * jnp.einsum('bhvk,bhqk->bhqv', cm, q)
    den = jnp.sum(s, axis=-1) + a * jnp.einsum('bhk,bhqk->bhq', nv, q)
    h = num / jnp.maximum(jnp.abs(den), jnp.exp(-m_row))[..., None]
    return mlstm_state_update(state, k, v, logi, logf), h


def mlstm_scan(q, k, v, logi, logf, state0):
    bsz, heads, n, d = q.shape
    nc = n // LSTM_CHUNK

    def to_chunks(t):
        t = t.reshape(t.shape[:2] + (nc, LSTM_CHUNK) + t.shape[3:])
        return jnp.moveaxis(t, 2, 0)

    state, h = lax.scan(lambda st, xs: mlstm_chunk(st, *xs), state0,
                        tuple(to_chunks(t) for t in (q, k, v, logi, logf)))
    return jnp.moveaxis(h, 0, 2).reshape(bsz, heads, n, d), state


def mlstm_inputs(aq, ak, av, ag, conv_w, gate_b):
    bsz, n, _ = aq.shape
    qk = jax.nn.silu(centred_depthwise_conv(jnp.concatenate([aq, ak], axis=-1), conv_w))
    q, k = jnp.split(qk, 2, axis=-1)
    heads = lambda t: t.reshape(bsz, n, LSTM_HEADS, LSTM_DIM).transpose(0, 2, 1, 3).astype(jnp.float32)
    gates = (ag + gate_b).astype(jnp.float32).reshape(bsz, n, 4, LSTM_HEADS).transpose(2, 0, 3, 1)
    fwd = (gates[0], jax.nn.log_sigmoid(gates[1]))
    bwd = (gates[2], jax.nn.log_sigmoid(gates[3]))
    return heads(q), heads(k) * LSTM_DIM ** -0.5, heads(av), fwd, bwd


def mlstm_output(h, o, g):
    bsz, heads, n, d = h.shape
    h = h.transpose(0, 2, 1, 3)
    h = h * lax.rsqrt(jnp.mean(h * h, axis=-1, keepdims=True) + EPS)
    h = h.reshape(bsz, n, heads * d) * g.astype(jnp.float32)
    return (h * jax.nn.sigmoid(o.astype(jnp.float32))).astype(o.dtype)


def window_attn(q, k, v, kc, vc, sink):
    bsz, n = q.shape[0], q.shape[1]
    nb = n // ATTN_BLOCK
    nctx = kc.shape[1]
    qb = q.reshape(bsz, nb, ATTN_BLOCK, ATTN_KV_HEADS, ATTN_GROUP, ATTN_DIM)

    def windows(t):
        tp = jnp.pad(t, ((0, 0), (ATTN_BLOCK, ATTN_BLOCK), (0, 0), (0, 0)))
        tp = tp.reshape(bsz, nb + 2, ATTN_BLOCK, ATTN_KV_HEADS, ATTN_DIM)
        return jnp.concatenate([tp[:, :-2], tp[:, 1:-1], tp[:, 2:]], axis=2)

    kw, vw = windows(k), windows(v)
    qpos = jnp.arange(nb)[:, None] * ATTN_BLOCK + jnp.arange(ATTN_BLOCK)[None]
    kpos = jnp.arange(nb)[:, None] * ATTN_BLOCK - ATTN_BLOCK + jnp.arange(3 * ATTN_BLOCK)[None]
    valid = ((jnp.abs(qpos[:, :, None] - kpos[:, None, :]) <= ATTN_WINDOW)
             & (kpos[:, None, :] >= 0) & (kpos[:, None, :] < n))
    scale = ATTN_DIM ** -0.5
    s_loc = jnp.einsum('bnqgrd,bnkgd->bngrqk', qb, kw).astype(jnp.float32) * scale
    s_loc = jnp.where(valid[None, :, None, None], s_loc, -jnp.inf)
    s_ctx = jnp.einsum('bnqgrd,bcgd->bngrqc', qb, kc).astype(jnp.float32) * scale
    s_sink = jnp.broadcast_to(sink.reshape(ATTN_KV_HEADS, ATTN_GROUP).astype(jnp.float32)[None, None, :, :, None, None],
                              s_loc.shape[:-1] + (1,))
    p = jax.nn.softmax(jnp.concatenate([s_loc, s_ctx, s_sink], axis=-1), axis=-1).astype(v.dtype)
    w3 = 3 * ATTN_BLOCK
    out = (jnp.einsum('bngrqk,bnkgd->bnqgrd', p[..., :w3], vw)
           + jnp.einsum('bngrqc,bcgd->bnqgrd', p[..., w3:w3 + nctx], vc))
    return out.reshape(bsz, n, ATTN_HEADS * ATTN_DIM)


def context_attn(q, k, v, sink):
    bsz, nctx = q.shape[0], q.shape[1]
    qg = q.reshape(bsz, nctx, ATTN_KV_HEADS, ATTN_GROUP, ATTN_DIM)
    s = jnp.einsum('bqgrd,bkgd->bgrqk', qg, k).astype(jnp.float32) * ATTN_DIM ** -0.5
    s_sink = jnp.broadcast_to(sink.reshape(ATTN_KV_HEADS, ATTN_GROUP).astype(jnp.float32)[None, :, :, None, None],
                              s.shape[:-1] + (1,))
    p = jax.nn.softmax(jnp.concatenate([s, s_sink], axis=-1), axis=-1).astype(v.dtype)
    out = jnp.einsum('bgrqk,bkgd->bqgrd', p[..., :nctx], v)
    return out.reshape(bsz, nctx, ATTN_HEADS * ATTN_DIM)


def mixer_ab(hx, hc, ctx_needed, w_in, conv_w, gate_b, head_g, sink, w_out, cos, sin):
    bsz, n, _ = hx.shape
    nctx = hc.shape[1]
    aqx, akx, avx, aox, agx, bqx, bkx, bvx = jnp.split(hx @ w_in, AB_OFFSETS, axis=-1)
    aqc, akc, avc, aoc, agc, bqc, bkc, bvc = jnp.split(hc @ w_in, AB_OFFSETS, axis=-1)
    rev = lambda t: jnp.flip(t, axis=2)

    qx, kx, vx, fx, bx = mlstm_inputs(aqx, akx, avx, agx, conv_w, gate_b)
    qc, kc, vc, fc, bc = mlstm_inputs(aqc, akc, avc, agc, conv_w, gate_b)
    zero = zero_state(bsz)
    if ctx_needed:
        hcf, st_f = mlstm_scan(qc, kc, vc, fc[0], fc[1], zero)
        hcb, st_b = mlstm_scan(rev(qc), rev(kc), rev(vc), rev(bc[0]), rev(bc[1]), zero)
    else:
        st_f = mlstm_state_update(zero, kc, vc, fc[0], fc[1])
        st_b = mlstm_state_update(zero, rev(kc), rev(vc), rev(bc[0]), rev(bc[1]))
    hxf, _ = mlstm_scan(qx, kx, vx, fx[0], fx[1], st_f)
    hxb, _ = mlstm_scan(rev(qx), rev(kx), rev(vx), rev(bx[0]), rev(bx[1]), st_b)
    a_x = mlstm_output(hxf + rev(hxb), aox, head_g)

    q_lat = apply_rope(bqx.reshape(bsz, n, ATTN_HEADS, ATTN_DIM), cos, sin)
    k_lat = apply_rope(bkx.reshape(bsz, n, ATTN_KV_HEADS, ATTN_DIM), cos, sin)
    v_lat = bvx.reshape(bsz, n, ATTN_KV_HEADS, ATTN_DIM)
    k_ctx = bkc.reshape(bsz, nctx, ATTN_KV_HEADS, ATTN_DIM)
    v_ctx = bvc.reshape(bsz, nctx, ATTN_KV_HEADS, ATTN_DIM)
    b_x = window_attn(q_lat, k_lat, v_lat, k_ctx, v_ctx, sink)

    y_x = jnp.concatenate([a_x, b_x], axis=-1) @ w_out
    y_c = None
    if ctx_needed:
        a_c = mlstm_output(hcf + rev(hcb), aoc, head_g)
        b_c = context_attn(bqc.reshape(bsz, nctx, ATTN_HEADS, ATTN_DIM), k_ctx, v_ctx, sink)
        y_c = jnp.concatenate([a_c, b_c], axis=-1) @ w_out
    return y_x, y_c


def mixer_chunk_gmlp(h, w_in, ln_g, ln_b, w_s, b_s, w_out):
    bsz, n, _ = h.shape
    u, v = jnp.split(jax.nn.gelu(h @ w_in), 2, axis=-1)
    v = layernorm(v, ln_g, ln_b).reshape(bsz, n // GM_CHUNK, GM_CHUNK, GM_GROUPS, GM_HALF // GM_GROUPS)
    v = jnp.einsum('gpq,bnqgc->bnpgc', w_s, v) + b_s.T[None, None, :, :, None]
    return (u * v.reshape(bsz, n, GM_HALF)) @ w_out


def ec_moe(h, w_router, w_gate, w_up, w_down):
    n, d = h.shape[1], h.shape[2]
    cap = max(1, EC_FACTOR * n // N_EXPERTS)
    aff = jax.nn.softmax((h @ w_router).astype(jnp.float32), axis=-1)
    gate, idx = lax.top_k(jnp.swapaxes(aff, 1, 2), cap)
    xe = jax.vmap(lambda hb, ib: hb[ib])(h, idx)
    hid = jax.nn.silu(jnp.einsum('becd,edf->becf', xe, w_gate)) * jnp.einsum('becd,edf->becf', xe, w_up)
    ye = jnp.einsum('becf,efd->becd', hid, w_down) * gate[..., None].astype(h.dtype)
    return jax.vmap(lambda yb, ib: jnp.zeros((n, d), yb.dtype).at[ib.reshape(-1)].add(yb.reshape(-1, d)))(ye, idx)


def setup_inputs(seed: int = 0) -> dict:
    key = jax.random.key(seed)
    ks = iter(jax.random.split(key, 32))
    nrm = lambda shape, scale: jax.random.normal(next(ks), shape, jnp.float32) * scale
    d = D_MODEL
    ne, no = (DEPTH + 1) // 2, DEPTH // 2
    f_bias = jnp.linspace(3.0, 6.0, LSTM_HEADS)
    zh = jnp.zeros((LSTM_HEADS,), jnp.float32)
    gate_base = jnp.concatenate([zh, f_bias, zh, f_bias])
    return {
        'x': nrm((BATCH, SEQ, d), 1.0),
        'c': nrm((BATCH, d), 1.0),
        'ctx': nrm((BATCH, CTX_LEN, d), 1.0),
        'c_ctx': nrm((d,), 1.0),
        'w_mod': nrm((DEPTH, d, 6 * d), 0.25 * d ** -0.5),
        'b_mod': nrm((DEPTH, 6 * d), 0.01),
        'norm_mix_g': 1.0 + nrm((DEPTH, d), 0.01),
        'norm_ffn_g': 1.0 + nrm((DEPTH, d), 0.01),
        'final_norm_g': 1.0 + nrm((d,), 0.01),
        'ab_w_in': nrm((ne, d, AB_IN), d ** -0.5),
        'ab_conv_w': nrm((ne, LSTM_CONV, 2 * LSTM_WIDTH), LSTM_CONV ** -0.5),
        'ab_gate_b': gate_base + nrm((ne, 4 * LSTM_HEADS), 0.1),
        'ab_head_g': 1.0 + nrm((ne, LSTM_WIDTH), 0.01),
        'ab_sink': nrm((ne, ATTN_HEADS), 0.5),
        'ab_w_out': nrm((ne, AB_MIX, d), AB_MIX ** -0.5),
        'gm_w_in': nrm((no, d, 2 * GM_HALF), d ** -0.5),
        'gm_ln_g': 1.0 + nrm((no, GM_HALF), 0.01),
        'gm_ln_b': nrm((no, GM_HALF), 0.01),
        'gm_w_s': nrm((no, GM_GROUPS, GM_CHUNK, GM_CHUNK), GM_CHUNK ** -0.5),
        'gm_b_s': 1.0 + nrm((no, GM_GROUPS, GM_CHUNK), 0.01),
        'gm_w_out': nrm((no, GM_HALF, d), GM_HALF ** -0.5),
        'moe_w_router': nrm((DEPTH, d, N_EXPERTS), d ** -0.5),
        'moe_w_gate': nrm((DEPTH, N_EXPERTS, d, D_EXPERT), d ** -0.5),
        'moe_w_up': nrm((DEPTH, N_EXPERTS, d, D_EXPERT), d ** -0.5),
        'moe_w_down': nrm((DEPTH, N_EXPERTS, D_EXPERT, d), D_EXPERT ** -0.5),
    }


def reference(x, c, ctx, c_ctx, w_mod, b_mod, norm_mix_g, norm_ffn_g, final_norm_g,
              ab_w_in, ab_conv_w, ab_gate_b, ab_head_g, ab_sink, ab_w_out,
              gm_w_in, gm_ln_g, gm_ln_b, gm_w_s, gm_b_s, gm_w_out,
              moe_w_router, moe_w_gate, moe_w_up, moe_w_down):
    n = x.shape[1]
    cos, sin = axial_rope(n, ATTN_DIM)
    for layer in range(DEPTH):
        ctx_needed = any(j % 2 == 0 for j in range(layer + 1, DEPTH))
        even = layer % 2 == 0
        sh1, sc1, gt1, sh2, sc2, gt2 = adaln(c, w_mod[layer], b_mod[layer])
        if even or ctx_needed:
            csh1, csc1, cgt1, csh2, csc2, cgt2 = adaln(c_ctx, w_mod[layer], b_mod[layer])
        hx = modulate(x, norm_mix_g[layer], sh1, sc1)
        if even:
            e = layer // 2
            hc = modulate(ctx, norm_mix_g[layer], csh1, csc1)
            yx, yc = mixer_ab(hx, hc, ctx_needed, ab_w_in[e], ab_conv_w[e], ab_gate_b[e],
                              ab_head_g[e], ab_sink[e], ab_w_out[e], cos, sin)
        else:
            o = layer // 2
            gm = (gm_w_in[o], gm_ln_g[o], gm_ln_b[o], gm_w_s[o], gm_b_s[o], gm_w_out[o])
            yx = mixer_chunk_gmlp(hx, *gm)
            yc = mixer_chunk_gmlp(modulate(ctx, norm_mix_g[layer], csh1, csc1), *gm) if ctx_needed else None
        moe = (moe_w_router[layer], moe_w_gate[layer], moe_w_up[layer], moe_w_down[layer])
        x = x + gt1 * yx
        x = x + gt2 * ec_moe(modulate(x, norm_ffn_g[layer], sh2, sc2), *moe)
        if ctx_needed:
            ctx = ctx + cgt1 * yc
            ctx = ctx + cgt2 * ec_moe(modulate(ctx, norm_ffn_g[layer], csh2, csc2), *moe)
    return rmsnorm(x, final_norm_g)
```

```python
import functools

import jax
import jax.numpy as jnp
from jax import lax
from jax.experimental import pallas as pl
from jax.experimental.pallas import tpu as pltpu

F32 = jnp.float32
BF16 = jnp.bfloat16
I32 = jnp.int32
HI = lax.Precision.HIGHEST

D_MODEL = 1024
GRID_W = 64
EPS = 1e-6
LSTM_HEADS = 4
LSTM_DIM = 128
LSTM_WIDTH = LSTM_HEADS * LSTM_DIM
LSTM_CONV = 5
CHUNK = 128
ATTN_HEADS = 8
ATTN_KV_HEADS = 2
ATTN_GROUP = ATTN_HEADS // ATTN_KV_HEADS
ATTN_DIM = 64
ROPE_BASE = 10000.0
GM_GROUPS = 8
GM_HALF = 2 * D_MODEL
N_EXPERTS = 16
EC_FACTOR = 2
N_GATES = 4 * LSTM_HEADS

LANES = 128
SUBLANES = 8
BF16_ROWS = 16
VMEM_LIMIT_BYTES = 56 * 1024 * 1024

P_COLS = 4 * LSTM_WIDTH + ATTN_HEADS * ATTN_DIM + 2 * ATTN_KV_HEADS * ATTN_DIM
PB_V, PB_O, PB_AQ = 2, 3, 4
PB_AK, PB_AV = 20, 21
MOE_TILE = 256
SLOT_CHUNK = 128


def _cparams(sem, vmem=VMEM_LIMIT_BYTES):
    return pltpu.CompilerParams(dimension_semantics=sem, vmem_limit_bytes=vmem)


def _rms_mod(x, g, sc, sh):
    y = x * lax.rsqrt(jnp.mean(x * x, axis=-1, keepdims=True) + EPS)
    return y * g * (1.0 + sc) + sh


def _silu(x):
    return x * jax.nn.sigmoid(x)


def _gelu_tanh(x):
    return 0.5 * x * (1.0 + jnp.tanh(0.7978845608028654 * (x + 0.044715 * (x * x * x))))


def _log_sigmoid(x):
    return jnp.minimum(x, 0.0) - jnp.log(1.0 + jnp.exp(-jnp.abs(x)))


def _dot_t(a, b):
    return lax.dot_general(a, b, (((1,), (1,)), ((), ())), preferred_element_type=F32)


def _adaln_kernel(c_ref, w_ref, b_ref, o_ref):
    s = _silu(c_ref[...])
    o_ref[...] = jnp.dot(s, w_ref[...], precision=HI, preferred_element_type=F32) + b_ref[...]


def _adaln(cond, w_mod, b_mod):
    depth, d, six_d = w_mod.shape
    tn = six_d // 4
    return pl.pallas_call(
        _adaln_kernel,
        out_shape=jax.ShapeDtypeStruct((depth, SUBLANES, six_d), F32),
        grid=(depth, six_d // tn),
        in_specs=[pl.BlockSpec((SUBLANES, d), lambda l, j: (0, 0)),
                  pl.BlockSpec((None, d, tn), lambda l, j: (l, 0, j)),
                  pl.BlockSpec((None, 1, tn), lambda l, j: (l, 0, j))],
        out_specs=pl.BlockSpec((None, SUBLANES, tn), lambda l, j: (l, 0, j)),
        compiler_params=_cparams(("arbitrary", "arbitrary")),
        name="adaln",
    )(cond, w_mod, b_mod.reshape(depth, 1, six_d))


def _modmm_kernel(x_ref, g_ref, sc_ref, sh_ref, w_ref, *rest, chunks, act, with_gates):
    if with_gates:
        wg_ref, o_ref, og_ref = rest
    else:
        (o_ref,) = rest
    h = _rms_mod(x_ref[...], g_ref[...], sc_ref[...], sh_ref[...])
    hb = h.astype(BF16)
    for lo, hi in chunks:
        y = jnp.dot(hb, w_ref[:, lo:hi], preferred_element_type=F32)
        if act == "gelu":
            y = _gelu_tanh(y)
        o_ref[:, lo:hi] = y.astype(o_ref.dtype)
    if with_gates:
        og_ref[...] = jnp.dot(h, wg_ref[...], precision=HI, preferred_element_type=F32)


def _modmm(x, g, sc, sh, w, wg=None, *, act=None, tm=512, chunk=512, name="modmm"):
    b, n, d = x.shape
    no = w.shape[1]
    tm = min(tm, n)
    chunks = tuple((lo, min(lo + chunk, no)) for lo in range(0, no, chunk))
    in_specs = [pl.BlockSpec((None, tm, d), lambda bi, i: (bi, i, 0)),
                pl.BlockSpec((1, d), lambda bi, i: (0, 0)),
                pl.BlockSpec((None, 1, d), lambda bi, i: (bi, 0, 0)),
                pl.BlockSpec((None, 1, d), lambda bi, i: (bi, 0, 0)),
                pl.BlockSpec((d, no), lambda bi, i: (0, 0))]
    out_shape = [jax.ShapeDtypeStruct((b, n, no), BF16)]
    out_specs = [pl.BlockSpec((None, tm, no), lambda bi, i: (bi, i, 0))]
    args = [x, g, sc, sh, w]
    if wg is not None:
        in_specs.append(pl.BlockSpec((d, LANES), lambda bi, i: (0, 0)))
        out_shape.append(jax.ShapeDtypeStruct((b, n, LANES), F32))
        out_specs.append(pl.BlockSpec((None, tm, LANES), lambda bi, i: (bi, i, 0)))
        args.append(wg)
    res = pl.pallas_call(
        functools.partial(_modmm_kernel, chunks=chunks, act=act, with_gates=wg is not None),
        out_shape=out_shape, grid=(b, n // tm), in_specs=in_specs, out_specs=out_specs,
        compiler_params=_cparams(("parallel", "parallel")), name=name,
    )(*args)
    return res if wg is not None else res[0]


def _conv_silu(pad_ref, w, rows):
    acc = None
    for t in range(LSTM_CONV):
        term = pad_ref[pl.ds(SUBLANES - LSTM_CONV // 2 + t, rows), :] * w[t:t + 1, :]
        acc = term if acc is None else acc + term
    return _silu(acc)


def _tri(n, lower):
    r = lax.broadcasted_iota(I32, (n, n), 0)
    c = lax.broadcasted_iota(I32, (n, n), 1)
    return (c <= r) if lower else (c >= r)


def _mlstm_kernel(qkf_ref, qkfp_ref, qkfn_ref, qkb_ref, qkbp_ref, qkbn_ref, vf_ref, vb_ref, gf_ref, gb_ref,
                  pc_ref, gc_ref, cw_ref, gbias_ref, hf_ref, hb_ref,
                  c_scr, n_scr, m_scr, pad_scr, padc_scr):
    c = pl.program_id(1)
    nc = pl.num_programs(1)
    cw = cw_ref[...]
    gbias = gbias_ref[...]
    kscale = LSTM_DIM ** -0.5

    @pl.when(c == 0)
    def _():
        lc = pc_ref.shape[0]
        padc_scr[...] = jnp.zeros_like(padc_scr)
        padc_scr[pl.ds(SUBLANES, lc), :] = pc_ref[:, LSTM_WIDTH:2 * LSTM_WIDTH].astype(F32)
        kc = _conv_silu(padc_scr, cw[:, LSTM_WIDTH:], lc) * kscale
        vc = pc_ref[:, PB_V * LSTM_WIDTH:(PB_V + 1) * LSTM_WIDTH].astype(F32)
        gates = gc_ref[...] + gbias
        ls = _log_sigmoid(gates)
        for d in range(2):
            tri = jnp.where(_tri(lc, d == 0), 1.0, 0.0).astype(F32)
            bcum = jnp.dot(tri, ls, precision=HI, preferred_element_type=F32)
            last = lc - 1 if d == 0 else 0
            for h in range(LSTM_HEADS):
                ci, cf = 2 * d * LSTM_HEADS + h, (2 * d + 1) * LSTM_HEADS + h
                bcol = bcum[:, cf:cf + 1]
                g = bcum[last:last + 1, cf:cf + 1]
                w = g - bcol + gates[:, ci:ci + 1]
                m_new = jnp.maximum(g, jnp.max(w, axis=0, keepdims=True))
                wt = jnp.exp(w - m_new)
                kh = kc[:, h * LSTM_DIM:(h + 1) * LSTM_DIM]
                vh = vc[:, h * LSTM_DIM:(h + 1) * LSTM_DIM]
                s = d * LSTM_HEADS + h
                c_scr[s] = jnp.dot((vh * wt).T.astype(BF16), kh.astype(BF16), preferred_element_type=F32)
                n_scr[s] = jnp.sum(kh * wt, axis=0, keepdims=True)
                m_scr[s] = jnp.broadcast_to(m_new, (1, LANES))

    tri_l = _tri(CHUNK, True)
    tri_u = _tri(CHUNK, False)
    for d in range(2):
        qk_ref, qkp_ref, qkn_ref = (qkf_ref, qkfp_ref, qkfn_ref) if d == 0 else (qkb_ref, qkbp_ref, qkbn_ref)
        v_ref, g_ref, h_ref = (vf_ref, gf_ref, hf_ref) if d == 0 else (vb_ref, gb_ref, hb_ref)
        cc = c if d == 0 else nc - 1 - c
        has_prev = jnp.where(cc > 0, 1.0, 0.0)
        has_next = jnp.where(cc < nc - 1, 1.0, 0.0)
        pad_scr[pl.ds(0, SUBLANES), :] = qkp_ref[...].astype(F32)[SUBLANES:, :] * has_prev
        pad_scr[pl.ds(SUBLANES, CHUNK), :] = qk_ref[...].astype(F32)
        pad_scr[pl.ds(SUBLANES + CHUNK, SUBLANES), :] = qkn_ref[...].astype(F32)[:SUBLANES, :] * has_next
        qk = _conv_silu(pad_scr, cw, CHUNK)
        vv = v_ref[...]
        gates = g_ref[...] + gbias
        ls = _log_sigmoid(gates)
        mask = tri_l if d == 0 else tri_u
        tri = jnp.where(mask, 1.0, 0.0).astype(F32)
        bcum = jnp.dot(tri, ls, precision=HI, preferred_element_type=F32)
        bcum_t = bcum.T
        gates_t = gates.T
        last = CHUNK - 1 if d == 0 else 0
        for h in range(LSTM_HEADS):
            ci, cf = 2 * d * LSTM_HEADS + h, (2 * d + 1) * LSTM_HEADS + h
            s = d * LSTM_HEADS + h
            qh = qk[:, h * LSTM_DIM:(h + 1) * LSTM_DIM]
            kh = qk[:, LSTM_WIDTH + h * LSTM_DIM:LSTM_WIDTH + (h + 1) * LSTM_DIM] * kscale
            vh = vv[:, h * LSTM_DIM:(h + 1) * LSTM_DIM]
            qb, kb = qh.astype(BF16), kh.astype(BF16)
            cm, nv, m_prev = c_scr[s], n_scr[s], m_scr[s][:, 0:1]
            bcol = bcum[:, cf:cf + 1]
            log_d = jnp.where(mask, bcol - bcum_t[cf:cf + 1, :] + gates_t[ci:ci + 1, :], -jnp.inf)
            m_row = jnp.maximum(bcol + m_prev, jnp.max(log_d, axis=1, keepdims=True))
            sm = _dot_t(qb, kb) * jnp.exp(log_d - m_row)
            a = jnp.exp(bcol + m_prev - m_row)
            num = jnp.dot(sm.astype(BF16), vh, preferred_element_type=F32) + a * _dot_t(qb, cm.astype(BF16))
            den = jnp.sum(sm, axis=1, keepdims=True) + a * jnp.sum(qh * nv, axis=1, keepdims=True)
            h_ref[:, h * LSTM_DIM:(h + 1) * LSTM_DIM] = num / jnp.maximum(jnp.abs(den), jnp.exp(-m_row))
            g = bcum[last:last + 1, cf:cf + 1]
            w = g - bcol + gates[:, ci:ci + 1]
            m_new = jnp.maximum(g + m_prev, jnp.max(w, axis=0, keepdims=True))
            decay = jnp.exp(g + m_prev - m_new)
            wt = jnp.exp(w - m_new)
            c_scr[s] = decay * cm + jnp.dot((vh.astype(F32) * wt).T.astype(BF16), kb, preferred_element_type=F32)
            n_scr[s] = decay * nv + jnp.sum(kh * wt, axis=0, keepdims=True)
            m_scr[s] = jnp.broadcast_to(m_new, (1, LANES))


def _mlstm(p, g, pc, gc, conv_w, gate_bias):
    b, n, _ = p.shape
    lc = pc.shape[1]
    nc = n // CHUNK
    hpc = CHUNK // BF16_ROWS
    nhb = n // BF16_ROWS
    qkw = 2 * LSTM_WIDTH

    def fwd(bi, c):
        return c

    def bwd(bi, c):
        return nc - 1 - c

    def qk_specs(cidx):
        return [pl.BlockSpec((None, CHUNK, qkw), lambda bi, c: (bi, cidx(bi, c), 0)),
                pl.BlockSpec((None, BF16_ROWS, qkw), lambda bi, c: (bi, jnp.maximum(cidx(bi, c) * hpc - 1, 0), 0)),
                pl.BlockSpec((None, BF16_ROWS, qkw),
                             lambda bi, c: (bi, jnp.minimum((cidx(bi, c) + 1) * hpc, nhb - 1), 0))]

    in_specs = (qk_specs(fwd) + qk_specs(bwd) + [
        pl.BlockSpec((None, CHUNK, LSTM_WIDTH), lambda bi, c: (bi, c, PB_V)),
        pl.BlockSpec((None, CHUNK, LSTM_WIDTH), lambda bi, c: (bi, nc - 1 - c, PB_V)),
        pl.BlockSpec((None, CHUNK, LANES), lambda bi, c: (bi, c, 0)),
        pl.BlockSpec((None, CHUNK, LANES), lambda bi, c: (bi, nc - 1 - c, 0)),
        pl.BlockSpec((None, lc, P_COLS), lambda bi, c: (bi, 0, 0)),
        pl.BlockSpec((None, lc, LANES), lambda bi, c: (bi, 0, 0)),
        pl.BlockSpec((LSTM_CONV, qkw), lambda bi, c: (0, 0)),
        pl.BlockSpec((1, LANES), lambda bi, c: (0, 0))])
    out_specs = [pl.BlockSpec((None, CHUNK, LSTM_WIDTH), lambda bi, c: (bi, c, 0)),
                 pl.BlockSpec((None, CHUNK, LSTM_WIDTH), lambda bi, c: (bi, nc - 1 - c, 0))]
    return pl.pallas_call(
        _mlstm_kernel,
        out_shape=[jax.ShapeDtypeStruct((b, n, LSTM_WIDTH), F32)] * 2,
        grid=(b, nc), in_specs=in_specs, out_specs=out_specs,
        scratch_shapes=[pltpu.VMEM((2 * LSTM_HEADS, LSTM_DIM, LSTM_DIM), F32),
                        pltpu.VMEM((2 * LSTM_HEADS, 1, LSTM_DIM), F32),
                        pltpu.VMEM((2 * LSTM_HEADS, 1, LANES), F32),
                        pltpu.VMEM((CHUNK + 2 * SUBLANES, qkw), F32),
                        pltpu.VMEM((lc + 2 * SUBLANES, LSTM_WIDTH), F32)],
        compiler_params=_cparams(("arbitrary", "arbitrary")), name="mlstm",
    )(p, p, p, p, p, p, p, p, g, g, pc, gc, conv_w, gate_bias)


def _rope(x, cos, sin_signed):
    w = x.shape[1]
    lane = lax.broadcasted_iota(I32, x.shape, 1)
    first = (lane & (ATTN_DIM - 1)) < (ATTN_DIM // 2)
    partner = jnp.where(first, pltpu.roll(x, w - ATTN_DIM // 2, 1), pltpu.roll(x, ATTN_DIM // 2, 1))
    return x * cos + partner * sin_signed


def _attn_kernel(q_ref, kp_ref, kc_ref, kn_ref, vp_ref, vc_ref, vn_ref, kctx_ref, vctx_ref,
                 tp_ref, tc_ref, tn_ref, sink_ref, o_ref):
    nb = pl.program_id(1)
    last = pl.num_programs(1) - 1
    nctx = kctx_ref.shape[0]
    wloc = 3 * CHUNK

    def table(t_ref):
        t = t_ref[...]
        return t[:, :LANES], t[:, LANES:]

    cos_c, sin_c = table(tc_ref)
    q = _rope(q_ref[...].astype(F32), jnp.concatenate([cos_c] * 4, axis=1), jnp.concatenate([sin_c] * 4, axis=1))
    q = q * (ATTN_DIM ** -0.5)
    ks = []
    for k_ref, t_ref in ((kp_ref, tp_ref), (kc_ref, tc_ref), (kn_ref, tn_ref)):
        cos_t, sin_t = table(t_ref)
        ks.append(_rope(k_ref[...].astype(F32), cos_t, sin_t).astype(BF16))
    k_all = jnp.concatenate(ks + [kctx_ref[...]], axis=0)
    v_all = jnp.concatenate([vp_ref[...], vc_ref[...], vn_ref[...], vctx_ref[...]], axis=0)

    rows = ATTN_GROUP * CHUNK
    ri = lax.broadcasted_iota(I32, (rows, wloc + nctx), 0) & (CHUNK - 1)
    cj = lax.broadcasted_iota(I32, (rows, wloc + nctx), 1)
    lo = jnp.where(nb == 0, CHUNK, 0)
    hi = jnp.where(nb == last, 2 * CHUNK - 1, wloc - 1)
    valid = (cj >= wloc) | ((cj >= ri) & (cj <= ri + 2 * CHUNK) & (cj >= lo) & (cj <= hi))
    lane = lax.broadcasted_iota(I32, (CHUNK, LANES), 1)
    sink = sink_ref[...]

    pieces = [None] * ATTN_HEADS
    for g in range(ATTN_KV_HEADS):
        half_g = (lane >= ATTN_DIM) if g == 1 else (lane < ATTN_DIM)
        qs = []
        for r in range(ATTN_GROUP):
            h = g * ATTN_GROUP + r
            t = q[:, (h // 2) * LANES:(h // 2 + 1) * LANES]
            if h % 2 != g:
                t = pltpu.roll(t, ATTN_DIM, 1)
            qs.append(jnp.where(half_g, t, 0.0).astype(BF16))
        s = _dot_t(jnp.concatenate(qs, axis=0), k_all)
        s = jnp.where(valid, s, -jnp.inf)
        ps, dens = [], []
        for r in range(ATTN_GROUP):
            h = g * ATTN_GROUP + r
            sr = s[r * CHUNK:(r + 1) * CHUNK]
            snk = sink[:, h:h + 1]
            m = jnp.maximum(jnp.max(sr, axis=1, keepdims=True), snk)
            e = jnp.exp(sr - m)
            dens.append(jnp.sum(e, axis=1, keepdims=True) + jnp.exp(snk - m))
            ps.append(e.astype(BF16))
        pv = jnp.dot(jnp.concatenate(ps, axis=0), v_all, preferred_element_type=F32)
        for r in range(ATTN_GROUP):
            h = g * ATTN_GROUP + r
            o = pv[r * CHUNK:(r + 1) * CHUNK] / dens[r]
            if h % 2 != g:
                o = pltpu.roll(o, ATTN_DIM, 1)
            pieces[h] = o
    first_half = lane < ATTN_DIM
    for j in range(ATTN_HEADS // 2):
        o_ref[:, j * LANES:(j + 1) * LANES] = jnp.where(first_half, pieces[2 * j], pieces[2 * j + 1]).astype(o_ref.dtype)


def _attn(p, pc, table, sink):
    b, n, _ = p.shape
    lc = pc.shape[1]
    nb = n // CHUNK
    qw = ATTN_HEADS * ATTN_DIM

    def blk(col, off):
        return pl.BlockSpec((None, CHUNK, LANES), lambda bi, i: (bi, jnp.clip(i + off, 0, nb - 1), col))

    def tab(off):
        return pl.BlockSpec((CHUNK, 2 * LANES), lambda bi, i: (jnp.clip(i + off, 0, nb - 1), 0))

    in_specs = [pl.BlockSpec((None, CHUNK, qw), lambda bi, i: (bi, i, PB_AQ)),
                blk(PB_AK, -1), blk(PB_AK, 0), blk(PB_AK, 1), blk(PB_AV, -1), blk(PB_AV, 0), blk(PB_AV, 1),
                pl.BlockSpec((None, lc, LANES), lambda bi, i: (bi, 0, PB_AK)),
                pl.BlockSpec((None, lc, LANES), lambda bi, i: (bi, 0, PB_AV)),
                tab(-1), tab(0), tab(1),
                pl.BlockSpec((1, LANES), lambda bi, i: (0, 0))]
    return pl.pallas_call(
        _attn_kernel,
        out_shape=jax.ShapeDtypeStruct((b, n, qw), BF16),
        grid=(b, nb), in_specs=in_specs,
        out_specs=pl.BlockSpec((None, CHUNK, qw), lambda bi, i: (bi, i, 0)),
        compiler_params=_cparams(("parallel", "parallel")), name="window_attn",
    )(p, p, p, p, p, p, p, pc, pc, table, table, table, sink)


def _router_tail(x_new, g2, sc2, sh2, wr_ref, x_out_ref, h2_ref, aff_ref):
    x_out_ref[...] = x_new
    h2 = _rms_mod(x_new, g2, sc2, sh2)
    h2_ref[...] = h2.astype(BF16)
    logits = jnp.dot(h2, wr_ref[...], precision=HI, preferred_element_type=F32)
    lane = lax.broadcasted_iota(I32, logits.shape, 1)
    logits = jnp.where(lane < N_EXPERTS, logits, -jnp.inf)
    e = jnp.exp(logits - jnp.max(logits, axis=1, keepdims=True))
    aff_ref[...] = e / jnp.sum(e, axis=1, keepdims=True)


def _ab_out_kernel(hf_ref, hb_ref, o_ref, at_ref, x_ref, hg_ref, wo_ref, gt1_ref, g2_ref, sc2_ref, sh2_ref, wr_ref,
                   x_out_ref, h2_ref, aff_ref):
    hsum = hf_ref[...] + hb_ref[...]
    og = jax.nn.sigmoid(o_ref[...].astype(F32))
    hg = hg_ref[...]
    parts = []
    for h in range(LSTM_HEADS):
        sl = slice(h * LSTM_DIM, (h + 1) * LSTM_DIM)
        seg = hsum[:, sl]
        seg = seg * lax.rsqrt(jnp.mean(seg * seg, axis=-1, keepdims=True) + EPS)
        parts.append((seg * hg[:, sl] * og[:, sl]).astype(BF16))
    cat = jnp.concatenate(parts + [at_ref[...]], axis=1)
    y = jnp.dot(cat, wo_ref[...], preferred_element_type=F32)
    _router_tail(x_ref[...] + gt1_ref[...] * y, g2_ref[...], sc2_ref[...], sh2_ref[...], wr_ref,
                 x_out_ref, h2_ref, aff_ref)


def _tail_out(b, n, d, tm):
    shapes = [jax.ShapeDtypeStruct((b, n, d), F32), jax.ShapeDtypeStruct((b, n, d), BF16),
              jax.ShapeDtypeStruct((b, n, LANES), F32)]
    specs = [pl.BlockSpec((None, tm, d), lambda bi, i: (bi, i, 0)),
             pl.BlockSpec((None, tm, d), lambda bi, i: (bi, i, 0)),
             pl.BlockSpec((None, tm, LANES), lambda bi, i: (bi, i, 0))]
    return shapes, specs


def _ab_out(hf, hb, p, at, x, head_g, w_out, gt1, g2, sc2, sh2, wr, tm=256):
    b, n, d = x.shape
    row = lambda w: pl.BlockSpec((None, tm, w), lambda bi, i: (bi, i, 0))
    vec = pl.BlockSpec((None, 1, d), lambda bi, i: (bi, 0, 0))
    const = lambda s: pl.BlockSpec(s, lambda bi, i: (0, 0))
    in_specs = [row(LSTM_WIDTH), row(LSTM_WIDTH),
                pl.BlockSpec((None, tm, LSTM_WIDTH), lambda bi, i: (bi, i, PB_O)),
                row(ATTN_HEADS * ATTN_DIM), row(d), const((1, LSTM_WIDTH)), const(w_out.shape),
                vec, const((1, d)), vec, vec, const((d, LANES))]
    shapes, specs = _tail_out(b, n, d, tm)
    return pl.pallas_call(
        _ab_out_kernel, out_shape=shapes, grid=(b, n // tm), in_specs=in_specs, out_specs=specs,
        compiler_params=_cparams(("parallel", "parallel")), name="ab_out",
    )(hf, hb, p, at, x, head_g, w_out, gt1, g2, sc2, sh2, wr)


def _gm_out_kernel(uv_ref, x_ref, lng_ref, lnb_ref, ws_ref, bs_ref, wo_ref, gt1_ref, g2_ref, sc2_ref, sh2_ref, wr_ref,
                   x_out_ref, h2_ref, aff_ref):
    tm = uv_ref.shape[0]
    gw = GM_HALF // GM_GROUPS
    v = uv_ref[:, GM_HALF:].astype(F32)
    mu = jnp.mean(v, axis=-1, keepdims=True)
    vc = v - mu
    var = jnp.mean(vc * vc, axis=-1, keepdims=True)
    vn = (vc * lax.rsqrt(var + EPS) * lng_ref[...] + lnb_ref[...]).astype(BF16)
    zs = []
    for ch in range(tm // CHUNK):
        rows = slice(ch * CHUNK, (ch + 1) * CHUNK)
        cols = []
        for g in range(GM_GROUPS):
            sv = jnp.dot(ws_ref[g], vn[rows, g * gw:(g + 1) * gw], preferred_element_type=F32)
            cols.append(sv + bs_ref[:, g:g + 1])
        sv = jnp.concatenate(cols, axis=1)
        zs.append((uv_ref[rows, :GM_HALF].astype(F32) * sv).astype(BF16))
    z = jnp.concatenate(zs, axis=0)
    y = jnp.dot(z, wo_ref[...], preferred_element_type=F32)
    _router_tail(x_ref[...] + gt1_ref[...] * y, g2_ref[...], sc2_ref[...], sh2_ref[...], wr_ref,
                 x_out_ref, h2_ref, aff_ref)


def _gm_out(uv, x, ln_g, ln_b, w_s, b_s_t, w_out, gt1, g2, sc2, sh2, wr, tm=256):
    b, n, d = x.shape
    vec = pl.BlockSpec((None, 1, d), lambda bi, i: (bi, 0, 0))
    const = lambda s: pl.BlockSpec(s, lambda *_: (0,) * len(s))
    in_specs = [pl.BlockSpec((None, tm, 2 * GM_HALF), lambda bi, i: (bi, i, 0)),
                pl.BlockSpec((None, tm, d), lambda bi, i: (bi, i, 0)),
                const((1, GM_HALF)), const((1, GM_HALF)), const(w_s.shape), const(b_s_t.shape), const(w_out.shape),
                vec, const((1, d)), vec, vec, const((d, LANES))]
    shapes, specs = _tail_out(b, n, d, tm)
    return pl.pallas_call(
        _gm_out_kernel, out_shape=shapes, grid=(b, n // tm), in_specs=in_specs, out_specs=specs,
        compiler_params=_cparams(("parallel", "parallel")), name="gm_out",
    )(uv, x, ln_g, ln_b, w_s, b_s_t, w_out, gt1, g2, sc2, sh2, wr)


def _route_kernel(aff_ref, pos_ref, post_ref, offs_ref, afft_scr, *, cap):
    n = aff_ref.shape[0]
    nblk = n // CHUNK

    def to_expert_major(k, _):
        rows = pl.ds(pl.multiple_of(k * CHUNK, CHUNK), CHUNK)
        afft_scr[k] = aff_ref[rows, :].T[:N_EXPERTS, :]
        return 0

    lax.fori_loop(0, nblk, to_expert_major, 0)

    def count(pred):
        per_lane = jnp.sum(jnp.where(pred, 1.0, 0.0), axis=0)
        return jnp.sum(per_lane, axis=1, keepdims=True)

    def search(i, prefix):
        cand = prefix | jnp.left_shift(jnp.int32(1), 30 - i)
        cand_f = lax.bitcast_convert_type(cand, F32)
        return jnp.where(count(afft_scr[...] >= cand_f[None]) >= cap, cand, prefix)

    thr_col = lax.bitcast_convert_type(lax.fori_loop(0, 31, search, jnp.zeros((N_EXPERTS, 1), I32)), F32)
    need_col = cap - count(afft_scr[...] > thr_col[None])

    def to_row(col):
        full = jnp.concatenate([jnp.broadcast_to(col, (N_EXPERTS, LANES)),
                                jnp.zeros((LANES - N_EXPERTS, LANES), F32)], axis=0)
        return full.T[0:1, :]

    thr, need = to_row(thr_col), to_row(need_col)
    tril = jnp.where(_tri(CHUNK, True), 1.0, 0.0).astype(BF16)

    def block(k, carry):
        run_tie, run_sel = carry
        rows = pl.ds(pl.multiple_of(k * CHUNK, CHUNK), CHUNK)
        a = aff_ref[rows, :]
        gt = a > thr
        tie = jnp.where(a == thr, 1.0, 0.0)
        tie_incl = jnp.dot(tril, tie.astype(BF16), preferred_element_type=F32)
        sel = jnp.where(gt | ((tie > 0.0) & (tie_incl - tie + run_tie < need)), 1.0, 0.0)
        sel_incl = jnp.dot(tril, sel.astype(BF16), preferred_element_type=F32)
        pos = jnp.where(sel > 0.0, sel_incl - sel + run_sel, -1.0)
        pos_ref[rows, :] = pos.astype(I32)
        post_ref[k] = pos.T[:N_EXPERTS, :].astype(I32)
        offs_ref[k] = run_sel.astype(I32)
        return (run_tie + tie_incl[CHUNK - 1:CHUNK, :], run_sel + sel_incl[CHUNK - 1:CHUNK, :])

    zero = jnp.zeros((1, LANES), F32)
    lax.fori_loop(0, nblk, block, (zero, zero))


def _route(aff, cap):
    b, n, _ = aff.shape
    nblk = n // CHUNK
    return pl.pallas_call(
        functools.partial(_route_kernel, cap=cap),
        out_shape=[jax.ShapeDtypeStruct((b, n, LANES), I32),
                   jax.ShapeDtypeStruct((b, nblk, N_EXPERTS, CHUNK), I32),
                   jax.ShapeDtypeStruct((b, nblk, 1, LANES), I32)],
        grid=(b,),
        in_specs=[pl.BlockSpec((None, n, LANES), lambda bi: (bi, 0, 0))],
        out_specs=[pl.BlockSpec((None, n, LANES), lambda bi: (bi, 0, 0)),
                   pl.BlockSpec((None, nblk, N_EXPERTS, CHUNK), lambda bi: (bi, 0, 0, 0)),
                   pl.BlockSpec((None, nblk, 1, LANES), lambda bi: (bi, 0, 0, 0))],
        scratch_shapes=[pltpu.VMEM((nblk, N_EXPERTS, CHUNK), F32)],
        compiler_params=_cparams(("parallel",)), name="route",
    )(aff)


def _window_start(s0, w, cap):
    lo = ((s0 >> 4) << 4) + w * SLOT_CHUNK
    return lo, pl.multiple_of(jnp.minimum(lo, cap - SLOT_CHUNK), BF16_ROWS)


def _num_windows(s0, s1):
    return (s1 - ((s0 >> 4) << 4) + SLOT_CHUNK - 1) >> 7


def _moe_ffn_kernel(cnt_ref, post_ref, h_ref, wg_ref, wu_ref, wd_ref, y_ref, xe_scr, acc_scr, *, ntile, cap, hid_tile):
    bi, e, j = pl.program_id(0), pl.program_id(1), pl.program_id(2)
    base = (bi * N_EXPERTS + e) * (ntile + 1)

    @pl.when(j == 0)
    def _():
        xe_scr[...] = jnp.zeros_like(xe_scr)
        slot = lax.broadcasted_iota(I32, (SLOT_CHUNK, MOE_TILE), 0)

        def tile_body(t, _):
            s0, s1 = cnt_ref[base + t], cnt_ref[base + t + 1]
            posrow = post_ref[t]
            htile = h_ref[pl.ds(pl.multiple_of(t * MOE_TILE, MOE_TILE), MOE_TILE), :]

            def window(w, _):
                lo, start = _window_start(s0, w, cap)
                onehot = jnp.where((posrow - start == slot) & (posrow >= lo), 1.0, 0.0).astype(BF16)
                dst = pl.ds(start, SLOT_CHUNK)
                xe_scr[dst, :] = xe_scr[dst, :] + jnp.dot(onehot, htile, preferred_element_type=F32).astype(BF16)
                return 0

            window(0, 0)
            lax.fori_loop(1, _num_windows(s0, s1), window, 0)
            return 0

        lax.fori_loop(0, ntile, tile_body, 0)

    xe = xe_scr[...]
    for k in range(wg_ref.shape[1] // hid_tile):
        cols = slice(k * hid_tile, (k + 1) * hid_tile)
        gate = jnp.dot(xe, wg_ref[:, cols].astype(BF16), preferred_element_type=F32)
        up = jnp.dot(xe, wu_ref[:, cols].astype(BF16), preferred_element_type=F32)
        hid = (_silu(gate) * up).astype(BF16)
        part = jnp.dot(hid, wd_ref[cols, :].astype(BF16), preferred_element_type=F32)
        if k == 0:
            @pl.when(j == 0)
            def _():
                acc_scr[...] = part

            @pl.when(j != 0)
            def _():
                acc_scr[...] += part
        else:
            acc_scr[...] += part

    @pl.when(j == pl.num_programs(2) - 1)
    def _():
        y_ref[...] = acc_scr[...].astype(y_ref.dtype)


def _moe_ffn(cnt, post, h2, wg, wu, wd, layer, cap, hid_split=2):
    b, n, d = h2.shape
    ntile = n // MOE_TILE
    dh = wg.shape[3] // hid_split
    grid_spec = pltpu.PrefetchScalarGridSpec(
        num_scalar_prefetch=1, grid=(b, N_EXPERTS, hid_split),
        in_specs=[pl.BlockSpec((None, None, ntile, 1, MOE_TILE), lambda bi, e, j, c: (bi, e, 0, 0, 0)),
                  pl.BlockSpec((None, n, d), lambda bi, e, j, c: (bi, 0, 0), pipeline_mode=pl.Buffered(1)),
                  pl.BlockSpec((None, None, d, dh), lambda bi, e, j, c: (layer, e, 0, j)),
                  pl.BlockSpec((None, None, d, dh), lambda bi, e, j, c: (layer, e, 0, j)),
                  pl.BlockSpec((None, None, dh, d), lambda bi, e, j, c: (layer, e, j, 0))],
        out_specs=pl.BlockSpec((None, None, cap, d), lambda bi, e, j, c: (bi, e, 0, 0)),
        scratch_shapes=[pltpu.VMEM((cap, d), BF16), pltpu.VMEM((cap, d), F32)])
    return pl.pallas_call(
        functools.partial(_moe_ffn_kernel, ntile=ntile, cap=cap, hid_tile=256),
        out_shape=jax.ShapeDtypeStruct((b, N_EXPERTS, cap, d), BF16),
        grid_spec=grid_spec, compiler_params=_cparams(("arbitrary", "arbitrary", "arbitrary")), name="moe_ffn",
    )(cnt, post, h2, wg, wu, wd)


def _moe_combine_kernel(cnt_ref, pos_ref, aff_ref, x_ref, gt2_ref, y_ref, fg_ref, o_ref, acc_scr, *, ntile, cap, final):
    bi, t = pl.program_id(0), pl.program_id(1)
    lane = lax.broadcasted_iota(I32, (MOE_TILE, SLOT_CHUNK), 1)
    pos = pos_ref[...]
    aff = aff_ref[...]

    def windows(w):
        total = None
        for e in range(N_EXPERTS):
            lo, start = _window_start(cnt_ref[(bi * N_EXPERTS + e) * (ntile + 1) + t], w, cap)
            pcol = pos[:, e:e + 1]
            onehot = jnp.where((pcol - start == lane) & (pcol >= lo), 1.0, 0.0).astype(BF16)
            part = aff[:, e:e + 1] * jnp.dot(onehot, y_ref[e, pl.ds(start, SLOT_CHUNK), :],
                                             preferred_element_type=F32)
            total = part if total is None else total + part
        return total

    acc_scr[...] = windows(0)
    nwin = 1
    for e in range(N_EXPERTS):
        base = (bi * N_EXPERTS + e) * (ntile + 1) + t
        nwin = jnp.maximum(nwin, _num_windows(cnt_ref[base], cnt_ref[base + 1]))

    def overflow(w, _):
        acc_scr[...] += windows(w)
        return 0

    lax.fori_loop(1, nwin, overflow, 0)
    out = x_ref[...] + gt2_ref[...] * acc_scr[...]
    if final:
        out = out * lax.rsqrt(jnp.mean(out * out, axis=-1, keepdims=True) + EPS) * fg_ref[...]
    o_ref[...] = out


def _moe_combine(cnt, pos, aff, x, gt2, y, final_g, final):
    b, n, d = x.shape
    ntile = n // MOE_TILE
    cap = y.shape[2]
    grid_spec = pltpu.PrefetchScalarGridSpec(
        num_scalar_prefetch=1, grid=(b, ntile),
        in_specs=[pl.BlockSpec((None, MOE_TILE, LANES), lambda bi, t, c: (bi, t, 0)),
                  pl.BlockSpec((None, MOE_TILE, LANES), lambda bi, t, c: (bi, t, 0)),
                  pl.BlockSpec((None, MOE_TILE, d), lambda bi, t, c: (bi, t, 0)),
                  pl.BlockSpec((None, 1, d), lambda bi, t, c: (bi, 0, 0)),
                  pl.BlockSpec((None, N_EXPERTS, cap, d), lambda bi, t, c: (bi, 0, 0, 0),
                               pipeline_mode=pl.Buffered(1)),
                  pl.BlockSpec((1, d), lambda bi, t, c: (0, 0))],
        out_specs=pl.BlockSpec((None, MOE_TILE, d), lambda bi, t, c: (bi, t, 0)),
        scratch_shapes=[pltpu.VMEM((MOE_TILE, d), F32)])
    return pl.pallas_call(
        functools.partial(_moe_combine_kernel, ntile=ntile, cap=cap, final=final),
        out_shape=jax.ShapeDtypeStruct((b, n, d), F32),
        grid_spec=grid_spec, compiler_params=_cparams(("arbitrary", "arbitrary")), name="moe_combine",
    )(cnt, pos, aff, x, gt2, y, final_g)


def _ec_moe(x_mid, h2, aff, gt2, wg, wu, wd, layer, final_g, final):
    b, n, _ = x_mid.shape
    cap = max(1, EC_FACTOR * n // N_EXPERTS)
    ntile = n // MOE_TILE
    pos, post, offs = _route(aff, cap)
    post = post.transpose(0, 2, 1, 3).reshape(b, N_EXPERTS, ntile, 1, MOE_TILE)
    starts = offs[:, ::MOE_TILE // CHUNK, 0, :N_EXPERTS].transpose(0, 2, 1)
    cnt = jnp.concatenate([starts, jnp.full((b, N_EXPERTS, 1), cap, I32)], axis=2).reshape(-1)
    y = _moe_ffn(cnt, post, h2, wg, wu, wd, layer, cap)
    return _moe_combine(cnt, pos, aff, x_mid, gt2, y, final_g, final)


def _rope_table(n):
    rows = n // GRID_W
    row = jnp.repeat(jnp.arange(rows), GRID_W).astype(F32)
    col = jnp.tile(jnp.arange(GRID_W), rows).astype(F32)
    nf = ATTN_DIM // 4
    inv = ROPE_BASE ** (-jnp.arange(nf, dtype=F32) / nf)
    ang = jnp.concatenate([row[:, None] * inv, col[:, None] * inv], axis=-1)
    cos, sin = jnp.cos(ang), jnp.sin(ang)
    reps = LANES // ATTN_DIM
    return jnp.concatenate([jnp.tile(jnp.concatenate([cos, cos], -1), (1, reps)),
                            jnp.tile(jnp.concatenate([-sin, sin], -1), (1, reps))], axis=-1)


def _pad_lanes(a):
    return jnp.pad(a, ((0, 0), (0, LANES - a.shape[1])))


def kernel(x, c, ctx, c_ctx, w_mod, b_mod, norm_mix_g, norm_ffn_g, final_norm_g, ab_w_in, ab_conv_w, ab_gate_b,
           ab_head_g, ab_sink, ab_w_out, gm_w_in, gm_ln_g, gm_ln_b, gm_w_s, gm_b_s, gm_w_out, moe_w_router,
           moe_w_gate, moe_w_up, moe_w_down):
    b, n, d = x.shape
    depth = w_mod.shape[0]
    assert depth <= 2, "context stream is only advanced for deeper stacks; not supported here"
    cond = jnp.zeros((SUBLANES, d), F32).at[:b].set(c).at[b].set(c_ctx)
    mods = _adaln(cond, w_mod, b_mod)

    def mod_rows(layer, rows):
        m = mods[layer, rows].reshape(-1, 6, 1, d)
        return [m[:, i] for i in range(6)]

    row = lambda v: v.reshape(1, -1)
    for layer in range(depth):
        sh1, sc1, gt1, sh2, sc2, gt2 = mod_rows(layer, slice(0, b))
        g1, g2 = row(norm_mix_g[layer]), row(norm_ffn_g[layer])
        wr = _pad_lanes(moe_w_router[layer])
        if layer % 2 == 0:
            e = layer // 2
            csh1, csc1 = (jnp.broadcast_to(v, (b, 1, d)) for v in mod_rows(layer, slice(b, b + 1))[:2])
            w_in = ab_w_in[e]
            g_lo = 4 * LSTM_WIDTH
            w_main = jnp.concatenate([w_in[:, :g_lo], w_in[:, g_lo + N_GATES:]], axis=1).astype(BF16)
            w_gate = _pad_lanes(w_in[:, g_lo:g_lo + N_GATES])
            p, gts = _modmm(x, g1, sc1, sh1, w_main, w_gate, name="ab_in")
            pc, gtc = _modmm(ctx, g1, csc1, csh1, w_main, w_gate, name="ab_in_ctx")
            hf, hb = _mlstm(p, gts, pc, gtc, ab_conv_w[e], _pad_lanes(row(ab_gate_b[e])))
            at = _attn(p, pc, _rope_table(n), _pad_lanes(row(ab_sink[e])))
            x_mid, h2, aff = _ab_out(hf, hb, p, at, x, row(ab_head_g[e]), ab_w_out[e].astype(BF16),
                                     gt1, g2, sc2, sh2, wr)
        else:
            o = layer // 2
            uv = _modmm(x, g1, sc1, sh1, gm_w_in[o].astype(BF16), act="gelu", name="gm_in")
            x_mid, h2, aff = _gm_out(uv, x, row(gm_ln_g[o]), row(gm_ln_b[o]), gm_w_s[o].astype(BF16),
                                     _pad_lanes(gm_b_s[o].T), gm_w_out[o].astype(BF16), gt1, g2, sc2, sh2, wr)
        x = _ec_moe(x_mid, h2, aff, gt2, moe_w_gate, moe_w_up, moe_w_down, layer,
                    row(final_norm_g), layer == depth - 1)
    return x
```

```python
import functools

import jax
import jax.numpy as jnp
from jax import lax
from jax.experimental import pallas as pl
from jax.experimental.pallas import tpu as pltpu

F32 = jnp.float32
BF16 = jnp.bfloat16
I32 = jnp.int32
HI = lax.Precision.HIGHEST

D_MODEL = 1024
GRID_W = 64
EPS = 1e-6
LSTM_HEADS = 4
LSTM_DIM = 128
LSTM_WIDTH = LSTM_HEADS * LSTM_DIM
LSTM_CONV = 5
CHUNK = 128
ATTN_HEADS = 8
ATTN_KV_HEADS = 2
ATTN_GROUP = ATTN_HEADS // ATTN_KV_HEADS
ATTN_DIM = 64
ROPE_BASE = 10000.0
GM_GROUPS = 8
GM_HALF = 2 * D_MODEL
N_EXPERTS = 16
EC_FACTOR = 2
N_GATES = 4 * LSTM_HEADS

LANES = 128
SUBLANES = 8
BF16_ROWS = 16
VMEM_LIMIT_BYTES = 56 * 1024 * 1024

P_COLS = 4 * LSTM_WIDTH + ATTN_HEADS * ATTN_DIM + 2 * ATTN_KV_HEADS * ATTN_DIM
PB_V, PB_O, PB_AQ = 2, 3, 4
PB_AK, PB_AV = 20, 21
MOE_TILE = 256
SLOT_CHUNK = 128


def _cparams(sem, vmem=VMEM_LIMIT_BYTES):
    return pltpu.CompilerParams(dimension_semantics=sem, vmem_limit_bytes=vmem)


def _rms_mod(x, g, sc, sh):
    y = x * lax.rsqrt(jnp.mean(x * x, axis=-1, keepdims=True) + EPS)
    return y * g * (1.0 + sc) + sh


def _silu(x):
    return x * jax.nn.sigmoid(x)


def _gelu_tanh(x):
    return 0.5 * x * (1.0 + jnp.tanh(0.7978845608028654 * (x + 0.044715 * (x * x * x))))


def _log_sigmoid(x):
    return jnp.minimum(x, 0.0) - jnp.log(1.0 + jnp.exp(-jnp.abs(x)))


def _dot_t(a, b):
    return lax.dot_general(a, b, (((1,), (1,)), ((), ())), preferred_element_type=F32)


def _adaln_kernel(c_ref, w_ref, b_ref, o_ref):
    s = _silu(c_ref[...])
    o_ref[...] = jnp.dot(s, w_ref[...], precision=HI, preferred_element_type=F32) + b_ref[...]


def _adaln(cond, w_mod, b_mod):
    depth, d, six_d = w_mod.shape
    tn = six_d // 4
    return pl.pallas_call(
        _adaln_kernel,
        out_shape=jax.ShapeDtypeStruct((depth, SUBLANES, six_d), F32),
        grid=(depth, six_d // tn),
        in_specs=[pl.BlockSpec((SUBLANES, d), lambda l, j: (0, 0)),
                  pl.BlockSpec((None, d, tn), lambda l, j: (l, 0, j)),
                  pl.BlockSpec((None, 1, tn), lambda l, j: (l, 0, j))],
        out_specs=pl.BlockSpec((None, SUBLANES, tn), lambda l, j: (l, 0, j)),
        compiler_params=_cparams(("arbitrary", "arbitrary")),
        name="adaln",
    )(cond, w_mod, b_mod.reshape(depth, 1, six_d))


def _modmm_kernel(x_ref, g_ref, sc_ref, sh_ref, w_ref, *rest, chunks, act, with_gates):
    if with_gates:
        wg_ref, o_ref, og_ref = rest
    else:
        (o_ref,) = rest
    h = _rms_mod(x_ref[...], g_ref[...], sc_ref[...], sh_ref[...])
    hb = h.astype(BF16)
    for lo, hi in chunks:
        y = jnp.dot(hb, w_ref[:, lo:hi], preferred_element_type=F32)
        if act == "gelu":
            y = _gelu_tanh(y)
        o_ref[:, lo:hi] = y.astype(o_ref.dtype)
    if with_gates:
        og_ref[...] = jnp.dot(h, wg_ref[...], precision=HI, preferred_element_type=F32)


def _modmm(x, g, sc, sh, w, wg=None, *, act=None, tm=512, chunk=512, name="modmm"):
    b, n, d = x.shape
    no = w.shape[1]
    tm = min(tm, n)
    chunks = tuple((lo, min(lo + chunk, no)) for lo in range(0, no, chunk))
    in_specs = [pl.BlockSpec((None, tm, d), lambda bi, i: (bi, i, 0)),
                pl.BlockSpec((1, d), lambda bi, i: (0, 0)),
                pl.BlockSpec((None, 1, d), lambda bi, i: (bi, 0, 0)),
                pl.BlockSpec((None, 1, d), lambda bi, i: (bi, 0, 0)),
                pl.BlockSpec((d, no), lambda bi, i: (0, 0))]
    out_shape = [jax.ShapeDtypeStruct((b, n, no), BF16)]
    out_specs = [pl.BlockSpec((None, tm, no), lambda bi, i: (bi, i, 0))]
    args = [x, g, sc, sh, w]
    if wg is not None:
        in_specs.append(pl.BlockSpec((d, LANES), lambda bi, i: (0, 0)))
        out_shape.append(jax.ShapeDtypeStruct((b, n, LANES), F32))
        out_specs.append(pl.BlockSpec((None, tm, LANES), lambda bi, i: (bi, i, 0)))
        args.append(wg)
    res = pl.pallas_call(
        functools.partial(_modmm_kernel, chunks=chunks, act=act, with_gates=wg is not None),
        out_shape=out_shape, grid=(b, n // tm), in_specs=in_specs, out_specs=out_specs,
        compiler_params=_cparams(("parallel", "parallel")), name=name,
    )(*args)
    return res if wg is not None else res[0]


def _conv_silu_kernel(x_ref, xp_ref, xn_ref, w_ref, o_ref, pad_scr):
    i = pl.program_id(1)
    rows = x_ref.shape[0]
    has_prev = jnp.where(i > 0, 1.0, 0.0)
    has_next = jnp.where(i < pl.num_programs(1) - 1, 1.0, 0.0)
    pad_scr[pl.ds(0, SUBLANES), :] = xp_ref[...].astype(F32)[SUBLANES:, :] * has_prev
    pad_scr[pl.ds(SUBLANES, rows), :] = x_ref[...].astype(F32)
    pad_scr[pl.ds(SUBLANES + rows, SUBLANES), :] = xn_ref[...].astype(F32)[:SUBLANES, :] * has_next
    w = w_ref[...]
    acc = None
    for t in range(LSTM_CONV):
        term = pad_scr[pl.ds(SUBLANES - LSTM_CONV // 2 + t, rows), :] * w[t:t + 1, :]
        acc = term if acc is None else acc + term
    y = _silu(acc)
    o_ref[:, :LSTM_WIDTH] = y[:, :LSTM_WIDTH].astype(o_ref.dtype)
    o_ref[:, LSTM_WIDTH:] = (y[:, LSTM_WIDTH:] * (LSTM_DIM ** -0.5)).astype(o_ref.dtype)


def _conv_silu(p, conv_w, tm=512):
    b, n, _ = p.shape
    tm = min(tm, n)
    qkw = 2 * LSTM_WIDTH
    hpt = tm // BF16_ROWS
    nhb = n // BF16_ROWS
    return pl.pallas_call(
        _conv_silu_kernel,
        out_shape=jax.ShapeDtypeStruct((b, n, qkw), BF16),
        grid=(b, n // tm),
        in_specs=[pl.BlockSpec((None, tm, qkw), lambda bi, i: (bi, i, 0)),
                  pl.BlockSpec((None, BF16_ROWS, qkw), lambda bi, i: (bi, jnp.maximum(i * hpt - 1, 0), 0)),
                  pl.BlockSpec((None, BF16_ROWS, qkw), lambda bi, i: (bi, jnp.minimum((i + 1) * hpt, nhb - 1), 0)),
                  pl.BlockSpec((LSTM_CONV, qkw), lambda bi, i: (0, 0))],
        out_specs=pl.BlockSpec((None, tm, qkw), lambda bi, i: (bi, i, 0)),
        scratch_shapes=[pltpu.VMEM((tm + 2 * SUBLANES, qkw), F32)],
        compiler_params=_cparams(("parallel", "parallel")), name="conv_silu",
    )(p, p, p, conv_w)


def _tri(n, lower):
    r = lax.broadcasted_iota(I32, (n, n), 0)
    c = lax.broadcasted_iota(I32, (n, n), 1)
    return (c <= r) if lower else (c >= r)


def _mlstm_kernel(qkf_ref, qkb_ref, vf_ref, vb_ref, gf_ref, gb_ref, kc_ref, vc_ref, gc_ref, gbias_ref,
                  hf_ref, hb_ref, c_scr, n_scr, m_scr):
    c = pl.program_id(1)
    gbias = gbias_ref[...]

    @pl.when(c == 0)
    def _():
        lc = kc_ref.shape[0]
        kc = kc_ref[...].astype(F32)
        vc = vc_ref[...].astype(F32)
        gates = gc_ref[...] + gbias
        ls = _log_sigmoid(gates)
        for d in range(2):
            tri = jnp.where(_tri(lc, d == 0), 1.0, 0.0).astype(F32)
            bcum = jnp.dot(tri, ls, precision=HI, preferred_element_type=F32)
            last = lc - 1 if d == 0 else 0
            for h in range(LSTM_HEADS):
                ci, cf = 2 * d * LSTM_HEADS + h, (2 * d + 1) * LSTM_HEADS + h
                bcol = bcum[:, cf:cf + 1]
                g = bcum[last:last + 1, cf:cf + 1]
                w = g - bcol + gates[:, ci:ci + 1]
                m_new = jnp.maximum(g, jnp.max(w, axis=0, keepdims=True))
                wt = jnp.exp(w - m_new)
                kh = kc[:, h * LSTM_DIM:(h + 1) * LSTM_DIM]
                vh = vc[:, h * LSTM_DIM:(h + 1) * LSTM_DIM]
                s = d * LSTM_HEADS + h
                c_scr[s] = jnp.dot((vh * wt).T.astype(BF16), kh.astype(BF16), preferred_element_type=F32)
                n_scr[s] = jnp.sum(kh * wt, axis=0, keepdims=True)
                m_scr[s] = jnp.broadcast_to(m_new, (1, LANES))

    tri_l = _tri(CHUNK, True)
    tri_u = _tri(CHUNK, False)
    for d in range(2):
        qk_ref, v_ref, g_ref, h_ref = (qkf_ref, vf_ref, gf_ref, hf_ref) if d == 0 else (qkb_ref, vb_ref, gb_ref, hb_ref)
        qk = qk_ref[...]
        vv = v_ref[...]
        gates = g_ref[...] + gbias
        ls = _log_sigmoid(gates)
        mask = tri_l if d == 0 else tri_u
        tri = jnp.where(mask, 1.0, 0.0).astype(F32)
        bcum = jnp.dot(tri, ls, precision=HI, preferred_element_type=F32)
        bcum_t = bcum.T
        gates_t = gates.T
        last = CHUNK - 1 if d == 0 else 0
        for h in range(LSTM_HEADS):
            ci, cf = 2 * d * LSTM_HEADS + h, (2 * d + 1) * LSTM_HEADS + h
            s = d * LSTM_HEADS + h
            qb = qk[:, h * LSTM_DIM:(h + 1) * LSTM_DIM]
            kb = qk[:, LSTM_WIDTH + h * LSTM_DIM:LSTM_WIDTH + (h + 1) * LSTM_DIM]
            vh = vv[:, h * LSTM_DIM:(h + 1) * LSTM_DIM]
            qh, kh = qb.astype(F32), kb.astype(F32)
            cm, nv, m_prev = c_scr[s], n_scr[s], m_scr[s][:, 0:1]
            bcol = bcum[:, cf:cf + 1]
            log_d = jnp.where(mask, bcol - bcum_t[cf:cf + 1, :] + gates_t[ci:ci + 1, :], -jnp.inf)
            m_row = jnp.maximum(bcol + m_prev, jnp.max(log_d, axis=1, keepdims=True))
            sm = _dot_t(qb, kb) * jnp.exp(log_d - m_row)
            a = jnp.exp(bcol + m_prev - m_row)
            num = jnp.dot(sm.astype(BF16), vh, preferred_element_type=F32) + a * _dot_t(qb, cm.astype(BF16))
            den = jnp.sum(sm, axis=1, keepdims=True) + a * jnp.sum(qh * nv, axis=1, keepdims=True)
            h_ref[:, h * LSTM_DIM:(h + 1) * LSTM_DIM] = num / jnp.maximum(jnp.abs(den), jnp.exp(-m_row))
            g = bcum[last:last + 1, cf:cf + 1]
            w = g - bcol + gates[:, ci:ci + 1]
            m_new = jnp.maximum(g + m_prev, jnp.max(w, axis=0, keepdims=True))
            decay = jnp.exp(g + m_prev - m_new)
            wt = jnp.exp(w - m_new)
            c_scr[s] = decay * cm + jnp.dot((vh.astype(F32) * wt).T.astype(BF16), kb, preferred_element_type=F32)
            n_scr[s] = decay * nv + jnp.sum(kh * wt, axis=0, keepdims=True)
            m_scr[s] = jnp.broadcast_to(m_new, (1, LANES))


def _mlstm(qk, p, g, qkc, pc, gc, gate_bias):
    b, n, _ = p.shape
    lc = pc.shape[1]
    nc = n // CHUNK
    qkw = 2 * LSTM_WIDTH
    in_specs = [
        pl.BlockSpec((None, CHUNK, qkw), lambda bi, c: (bi, c, 0)),
        pl.BlockSpec((None, CHUNK, qkw), lambda bi, c: (bi, nc - 1 - c, 0)),
        pl.BlockSpec((None, CHUNK, LSTM_WIDTH), lambda bi, c: (bi, c, PB_V)),
        pl.BlockSpec((None, CHUNK, LSTM_WIDTH), lambda bi, c: (bi, nc - 1 - c, PB_V)),
        pl.BlockSpec((None, CHUNK, LANES), lambda bi, c: (bi, c, 0)),
        pl.BlockSpec((None, CHUNK, LANES), lambda bi, c: (bi, nc - 1 - c, 0)),
        pl.BlockSpec((None, lc, LSTM_WIDTH), lambda bi, c: (bi, 0, 1)),
        pl.BlockSpec((None, lc, LSTM_WIDTH), lambda bi, c: (bi, 0, PB_V)),
        pl.BlockSpec((None, lc, LANES), lambda bi, c: (bi, 0, 0)),
        pl.BlockSpec((1, LANES), lambda bi, c: (0, 0))]
    out_specs = [pl.BlockSpec((None, CHUNK, LSTM_WIDTH), lambda bi, c: (bi, c, 0)),
                 pl.BlockSpec((None, CHUNK, LSTM_WIDTH), lambda bi, c: (bi, nc - 1 - c, 0))]
    return pl.pallas_call(
        _mlstm_kernel,
        out_shape=[jax.ShapeDtypeStruct((b, n, LSTM_WIDTH), F32)] * 2,
        grid=(b, nc), in_specs=in_specs, out_specs=out_specs,
        scratch_shapes=[pltpu.VMEM((2 * LSTM_HEADS, LSTM_DIM, LSTM_DIM), F32),
                        pltpu.VMEM((2 * LSTM_HEADS, 1, LSTM_DIM), F32),
                        pltpu.VMEM((2 * LSTM_HEADS, 1, LANES), F32)],
        compiler_params=_cparams(("arbitrary", "arbitrary")), name="mlstm",
    )(qk, qk, p, p, g, g, qkc, pc, gc, gate_bias)


def _rope(x, cos, sin_signed):
    w = x.shape[1]
    lane = lax.broadcasted_iota(I32, x.shape, 1)
    first = (lane & (ATTN_DIM - 1)) < (ATTN_DIM // 2)
    partner = jnp.where(first, pltpu.roll(x, w - ATTN_DIM // 2, 1), pltpu.roll(x, ATTN_DIM // 2, 1))
    return x * cos + partner * sin_signed


def _attn_kernel(q_ref, kp_ref, kc_ref, kn_ref, vp_ref, vc_ref, vn_ref, kctx_ref, vctx_ref,
                 tp_ref, tc_ref, tn_ref, sink_ref, o_ref):
    nb = pl.program_id(1)
    last = pl.num_programs(1) - 1
    nctx = kctx_ref.shape[0]
    wloc = 3 * CHUNK

    def table(t_ref):
        t = t_ref[...]
        return t[:, :LANES], t[:, LANES:]

    cos_c, sin_c = table(tc_ref)
    q = _rope(q_ref[...].astype(F32), jnp.concatenate([cos_c] * 4, axis=1), jnp.concatenate([sin_c] * 4, axis=1))
    q = q * (ATTN_DIM ** -0.5)
    ks = []
    for k_ref, t_ref in ((kp_ref, tp_ref), (kc_ref, tc_ref), (kn_ref, tn_ref)):
        cos_t, sin_t = table(t_ref)
        ks.append(_rope(k_ref[...].astype(F32), cos_t, sin_t).astype(BF16))
    k_all = jnp.concatenate(ks + [kctx_ref[...]], axis=0)
    v_all = jnp.concatenate([vp_ref[...], vc_ref[...], vn_ref[...], vctx_ref[...]], axis=0)

    rows = ATTN_GROUP * CHUNK
    ri = lax.broadcasted_iota(I32, (rows, wloc + nctx), 0) & (CHUNK - 1)
    cj = lax.broadcasted_iota(I32, (rows, wloc + nctx), 1)
    lo = jnp.where(nb == 0, CHUNK, 0)
    hi = jnp.where(nb == last, 2 * CHUNK - 1, wloc - 1)
    valid = (cj >= wloc) | ((cj >= ri) & (cj <= ri + 2 * CHUNK) & (cj >= lo) & (cj <= hi))
    lane = lax.broadcasted_iota(I32, (CHUNK, LANES), 1)
    sink = sink_ref[...]

    pieces = [None] * ATTN_HEADS
    for g in range(ATTN_KV_HEADS):
        half_g = (lane >= ATTN_DIM) if g == 1 else (lane < ATTN_DIM)
        qs = []
        for r in range(ATTN_GROUP):
            h = g * ATTN_GROUP + r
            t = q[:, (h // 2) * LANES:(h // 2 + 1) * LANES]
            if h % 2 != g:
                t = pltpu.roll(t, ATTN_DIM, 1)
            qs.append(jnp.where(half_g, t, 0.0).astype(BF16))
        s = _dot_t(jnp.concatenate(qs, axis=0), k_all)
        s = jnp.where(valid, s, -jnp.inf)
        ps, dens = [], []
        for r in range(ATTN_GROUP):
            h = g * ATTN_GROUP + r
            sr = s[r * CHUNK:(r + 1) * CHUNK]
            snk = sink[:, h:h + 1]
            m = jnp.maximum(jnp.max(sr, axis=1, keepdims=True), snk)
            e = jnp.exp(sr - m)
            dens.append(jnp.sum(e, axis=1, keepdims=True) + jnp.exp(snk - m))
            ps.append(e.astype(BF16))
        pv = jnp.dot(jnp.concatenate(ps, axis=0), v_all, preferred_element_type=F32)
        for r in range(ATTN_GROUP):
            h = g * ATTN_GROUP + r
            o = pv[r * CHUNK:(r + 1) * CHUNK] / dens[r]
            if h % 2 != g:
                o = pltpu.roll(o, ATTN_DIM, 1)
            pieces[h] = o
    first_half = lane < ATTN_DIM
    for j in range(ATTN_HEADS // 2):
        o_ref[:, j * LANES:(j + 1) * LANES] = jnp.where(first_half, pieces[2 * j], pieces[2 * j + 1]).astype(o_ref.dtype)


def _attn(p, pc, table, sink):
    b, n, _ = p.shape
    lc = pc.shape[1]
    nb = n // CHUNK
    qw = ATTN_HEADS * ATTN_DIM

    def blk(col, off):
        return pl.BlockSpec((None, CHUNK, LANES), lambda bi, i: (bi, jnp.clip(i + off, 0, nb - 1), col))

    def tab(off):
        return pl.BlockSpec((CHUNK, 2 * LANES), lambda bi, i: (jnp.clip(i + off, 0, nb - 1), 0))

    in_specs = [pl.BlockSpec((None, CHUNK, qw), lambda bi, i: (bi, i, PB_AQ)),
                blk(PB_AK, -1), blk(PB_AK, 0), blk(PB_AK, 1), blk(PB_AV, -1), blk(PB_AV, 0), blk(PB_AV, 1),
                pl.BlockSpec((None, lc, LANES), lambda bi, i: (bi, 0, PB_AK)),
                pl.BlockSpec((None, lc, LANES), lambda bi, i: (bi, 0, PB_AV)),
                tab(-1), tab(0), tab(1),
                pl.BlockSpec((1, LANES), lambda bi, i: (0, 0))]
    return pl.pallas_call(
        _attn_kernel,
        out_shape=jax.ShapeDtypeStruct((b, n, qw), BF16),
        grid=(b, nb), in_specs=in_specs,
        out_specs=pl.BlockSpec((None, CHUNK, qw), lambda bi, i: (bi, i, 0)),
        compiler_params=_cparams(("parallel", "parallel")), name="window_attn",
    )(p, p, p, p, p, p, p, pc, pc, table, table, table, sink)


def _router_tail(x_new, g2, sc2, sh2, wr_ref, x_out_ref, h2_ref, aff_ref):
    x_out_ref[...] = x_new
    h2 = _rms_mod(x_new, g2, sc2, sh2)
    h_hi = h2.astype(BF16)
    h2_ref[...] = h_hi
    h_lo = (h2 - h_hi.astype(F32)).astype(BF16)
    logits = jnp.dot(jnp.concatenate([h_hi, h_lo, h_hi], axis=1), wr_ref[...],
                     preferred_element_type=F32)
    lane = lax.broadcasted_iota(I32, logits.shape, 1)
    logits = jnp.where(lane < N_EXPERTS, logits, -jnp.inf)
    e = jnp.exp(logits - jnp.max(logits, axis=1, keepdims=True))
    aff_ref[...] = e / jnp.sum(e, axis=1, keepdims=True)


def _ab_out_kernel(hf_ref, hb_ref, o_ref, at_ref, x_ref, hg_ref, wo_ref, gt1_ref, g2_ref, sc2_ref, sh2_ref, wr_ref,
                   x_out_ref, h2_ref, aff_ref):
    hsum = hf_ref[...] + hb_ref[...]
    og = jax.nn.sigmoid(o_ref[...].astype(F32))
    hg = hg_ref[...]
    parts = []
    for h in range(LSTM_HEADS):
        sl = slice(h * LSTM_DIM, (h + 1) * LSTM_DIM)
        seg = hsum[:, sl]
        seg = seg * lax.rsqrt(jnp.mean(seg * seg, axis=-1, keepdims=True) + EPS)
        parts.append((seg * hg[:, sl] * og[:, sl]).astype(BF16))
    cat = jnp.concatenate(parts + [at_ref[...]], axis=1)
    y = jnp.dot(cat, wo_ref[...], preferred_element_type=F32)
    _router_tail(x_ref[...] + gt1_ref[...] * y, g2_ref[...], sc2_ref[...], sh2_ref[...], wr_ref,
                 x_out_ref, h2_ref, aff_ref)


def _tail_out(b, n, d, tm):
    shapes = [jax.ShapeDtypeStruct((b, n, d), F32), jax.ShapeDtypeStruct((b, n, d), BF16),
              jax.ShapeDtypeStruct((b, n, LANES), F32)]
    specs = [pl.BlockSpec((None, tm, d), lambda bi, i: (bi, i, 0)),
             pl.BlockSpec((None, tm, d), lambda bi, i: (bi, i, 0)),
             pl.BlockSpec((None, tm, LANES), lambda bi, i: (bi, i, 0))]
    return shapes, specs


def _ab_out(hf, hb, p, at, x, head_g, w_out, gt1, g2, sc2, sh2, wr, tm=256):
    b, n, d = x.shape
    row = lambda w: pl.BlockSpec((None, tm, w), lambda bi, i: (bi, i, 0))
    vec = pl.BlockSpec((None, 1, d), lambda bi, i: (bi, 0, 0))
    const = lambda s: pl.BlockSpec(s, lambda bi, i: (0, 0))
    in_specs = [row(LSTM_WIDTH), row(LSTM_WIDTH),
                pl.BlockSpec((None, tm, LSTM_WIDTH), lambda bi, i: (bi, i, PB_O)),
                row(ATTN_HEADS * ATTN_DIM), row(d), const((1, LSTM_WIDTH)), const(w_out.shape),
                vec, const((1, d)), vec, vec, const((3 * d, LANES))]
    shapes, specs = _tail_out(b, n, d, tm)
    return pl.pallas_call(
        _ab_out_kernel, out_shape=shapes, grid=(b, n // tm), in_specs=in_specs, out_specs=specs,
        compiler_params=_cparams(("parallel", "parallel")), name="ab_out",
    )(hf, hb, p, at, x, head_g, w_out, gt1, g2, sc2, sh2, wr)


def _gm_out_kernel(uv_ref, x_ref, lng_ref, lnb_ref, ws_ref, bs_ref, wo_ref, gt1_ref, g2_ref, sc2_ref, sh2_ref, wr_ref,
                   x_out_ref, h2_ref, aff_ref):
    tm = uv_ref.shape[0]
    gw = GM_HALF // GM_GROUPS
    v = uv_ref[:, GM_HALF:].astype(F32)
    mu = jnp.mean(v, axis=-1, keepdims=True)
    vc = v - mu
    var = jnp.mean(vc * vc, axis=-1, keepdims=True)
    vn = (vc * lax.rsqrt(var + EPS) * lng_ref[...] + lnb_ref[...]).astype(BF16)
    zs = []
    for ch in range(tm // CHUNK):
        rows = slice(ch * CHUNK, (ch + 1) * CHUNK)
        cols = []
        for g in range(GM_GROUPS):
            sv = jnp.dot(ws_ref[g], vn[rows, g * gw:(g + 1) * gw], preferred_element_type=F32)
            cols.append(sv + bs_ref[:, g:g + 1])
        sv = jnp.concatenate(cols, axis=1)
        zs.append((uv_ref[rows, :GM_HALF].astype(F32) * sv).astype(BF16))
    z = jnp.concatenate(zs, axis=0)
    y = jnp.dot(z, wo_ref[...], preferred_element_type=F32)
    _router_tail(x_ref[...] + gt1_ref[...] * y, g2_ref[...], sc2_ref[...], sh2_ref[...], wr_ref,
                 x_out_ref, h2_ref, aff_ref)


def _gm_out(uv, x, ln_g, ln_b, w_s, b_s_t, w_out, gt1, g2, sc2, sh2, wr, tm=256):
    b, n, d = x.shape
    vec = pl.BlockSpec((None, 1, d), lambda bi, i: (bi, 0, 0))
    const = lambda s: pl.BlockSpec(s, lambda *_: (0,) * len(s))
    in_specs = [pl.BlockSpec((None, tm, 2 * GM_HALF), lambda bi, i: (bi, i, 0)),
                pl.BlockSpec((None, tm, d), lambda bi, i: (bi, i, 0)),
                const((1, GM_HALF)), const((1, GM_HALF)), const(w_s.shape), const(b_s_t.shape), const(w_out.shape),
                vec, const((1, d)), vec, vec, const((3 * d, LANES))]
    shapes, specs = _tail_out(b, n, d, tm)
    return pl.pallas_call(
        _gm_out_kernel, out_shape=shapes, grid=(b, n // tm), in_specs=in_specs, out_specs=specs,
        compiler_params=_cparams(("parallel", "parallel")), name="gm_out",
    )(uv, x, ln_g, ln_b, w_s, b_s_t, w_out, gt1, g2, sc2, sh2, wr)


def _route_kernel(aff_ref, pos_ref, post_ref, offs_ref, afft_scr, *, cap):
    n = aff_ref.shape[0]
    nblk = n // CHUNK

    def to_expert_major(k, _):
        rows = pl.ds(pl.multiple_of(k * CHUNK, CHUNK), CHUNK)
        afft_scr[k] = aff_ref[rows, :].T[:N_EXPERTS, :]
        return 0

    lax.fori_loop(0, nblk, to_expert_major, 0)

    def count(pred):
        per_lane = jnp.sum(jnp.where(pred, 1.0, 0.0), axis=0)
        return jnp.sum(per_lane, axis=1, keepdims=True)

    def search(i, prefix):
        cand = prefix | jnp.left_shift(jnp.int32(1), 30 - i)
        cand_f = lax.bitcast_convert_type(cand, F32)
        return jnp.where(count(afft_scr[...] >= cand_f[None]) >= cap, cand, prefix)

    thr_col = lax.bitcast_convert_type(lax.fori_loop(0, 31, search, jnp.zeros((N_EXPERTS, 1), I32)), F32)
    need_col = cap - count(afft_scr[...] > thr_col[None])

    def to_row(col):
        full = jnp.concatenate([jnp.broadcast_to(col, (N_EXPERTS, LANES)),
                                jnp.zeros((LANES - N_EXPERTS, LANES), F32)], axis=0)
        return full.T[0:1, :]

    thr, need = to_row(thr_col), to_row(need_col)
    tril = jnp.where(_tri(CHUNK, True), 1.0, 0.0).astype(BF16)

    def block(k, carry):
        run_tie, run_sel = carry
        rows = pl.ds(pl.multiple_of(k * CHUNK, CHUNK), CHUNK)
        a = aff_ref[rows, :]
        gt = a > thr
        tie = jnp.where(a == thr, 1.0, 0.0)
        tie_incl = jnp.dot(tril, tie.astype(BF16), preferred_element_type=F32)
        sel = jnp.where(gt | ((tie > 0.0) & (tie_incl - tie + run_tie < need)), 1.0, 0.0)
        sel_incl = jnp.dot(tril, sel.astype(BF16), preferred_element_type=F32)
        pos = jnp.where(sel > 0.0, sel_incl - sel + run_sel, -1.0)
        pos_ref[rows, :] = pos.astype(I32)
        post_ref[k] = pos.T[:N_EXPERTS, :].astype(I32)
        offs_ref[k] = run_sel.astype(I32)
        return (run_tie + tie_incl[CHUNK - 1:CHUNK, :], run_sel + sel_incl[CHUNK - 1:CHUNK, :])

    zero = jnp.zeros((1, LANES), F32)
    lax.fori_loop(0, nblk, block, (zero, zero))


def _route(aff, cap):
    b, n, _ = aff.shape
    nblk = n // CHUNK
    return pl.pallas_call(
        functools.partial(_route_kernel, cap=cap),
        out_shape=[jax.ShapeDtypeStruct((b, n, LANES), I32),
                   jax.ShapeDtypeStruct((b, nblk, N_EXPERTS, CHUNK), I32),
                   jax.ShapeDtypeStruct((b, nblk, 1, LANES), I32)],
        grid=(b,),
        in_specs=[pl.BlockSpec((None, n, LANES), lambda bi: (bi, 0, 0))],
        out_specs=[pl.BlockSpec((None, n, LANES), lambda bi: (bi, 0, 0)),
                   pl.BlockSpec((None, nblk, N_EXPERTS, CHUNK), lambda bi: (bi, 0, 0, 0)),
                   pl.BlockSpec((None, nblk, 1, LANES), lambda bi: (bi, 0, 0, 0))],
        scratch_shapes=[pltpu.VMEM((nblk, N_EXPERTS, CHUNK), F32)],
        compiler_params=_cparams(("parallel",)), name="route",
    )(aff)


def _window_start(s0, w, cap):
    lo = ((s0 >> 4) << 4) + w * SLOT_CHUNK
    return lo, pl.multiple_of(jnp.minimum(lo, cap - SLOT_CHUNK), BF16_ROWS)


def _num_windows(s0, s1):
    return (s1 - ((s0 >> 4) << 4) + SLOT_CHUNK - 1) >> 7


def _moe_gather_kernel(cnt_ref, post_ref, h_ref, xe_ref, *, ntile, cap, group):
    bi, eg, t = pl.program_id(0), pl.program_id(1), pl.program_id(2)

    @pl.when(t == 0)
    def _():
        xe_ref[...] = jnp.zeros_like(xe_ref)

    slot = lax.broadcasted_iota(I32, (SLOT_CHUNK, MOE_TILE), 0)
    htile = h_ref[...]
    bases = [(bi * N_EXPERTS + eg * group + g) * (ntile + 1) + t for g in range(group)]

    def windows(w, _):
        starts, onehots = [], []
        for g in range(group):
            lo, start = _window_start(cnt_ref[bases[g]], w, cap)
            posrow = post_ref[g]
            onehots.append(jnp.where((posrow - start == slot) & (posrow >= lo), 1.0, 0.0).astype(BF16))
            starts.append(start)
        rows = jnp.dot(jnp.concatenate(onehots, axis=0), htile, preferred_element_type=F32)
        for g in range(group):
            dst = pl.ds(starts[g], SLOT_CHUNK)
            xe_ref[g, dst, :] = xe_ref[g, dst, :] + rows[g * SLOT_CHUNK:(g + 1) * SLOT_CHUNK].astype(BF16)
        return 0

    windows(0, 0)
    nwin = 1
    for g in range(group):
        nwin = jnp.maximum(nwin, _num_windows(cnt_ref[bases[g]], cnt_ref[bases[g] + 1]))
    lax.fori_loop(1, nwin, windows, 0)


def _moe_gather(cnt, post, h2, cap, group=8):
    b, n, d = h2.shape
    ntile = n // MOE_TILE
    grid_spec = pltpu.PrefetchScalarGridSpec(
        num_scalar_prefetch=1, grid=(b, N_EXPERTS // group, ntile),
        in_specs=[pl.BlockSpec((None, group, None, 1, MOE_TILE), lambda bi, eg, t, c: (bi, eg, t, 0, 0)),
                  pl.BlockSpec((None, MOE_TILE, d), lambda bi, eg, t, c: (bi, t, 0))],
        out_specs=pl.BlockSpec((None, group, cap, d), lambda bi, eg, t, c: (bi, eg, 0, 0)))
    return pl.pallas_call(
        functools.partial(_moe_gather_kernel, ntile=ntile, cap=cap, group=group),
        out_shape=jax.ShapeDtypeStruct((b, N_EXPERTS, cap, d), BF16),
        grid_spec=grid_spec, compiler_params=_cparams(("arbitrary", "arbitrary", "arbitrary")), name="moe_gather",
    )(cnt, post, h2)


def _moe_ffn_kernel(xe_ref, wg_ref, wu_ref, wd_ref, y_ref, acc_scr, *, hid_tile):
    j = pl.program_id(2)
    xe = xe_ref[...]
    for k in range(wg_ref.shape[1] // hid_tile):
        cols = slice(k * hid_tile, (k + 1) * hid_tile)
        gate = jnp.dot(xe, wg_ref[:, cols].astype(BF16), preferred_element_type=F32)
        up = jnp.dot(xe, wu_ref[:, cols].astype(BF16), preferred_element_type=F32)
        hid = (_silu(gate) * up).astype(BF16)
        part = jnp.dot(hid, wd_ref[cols, :].astype(BF16), preferred_element_type=F32)
        if k == 0:
            @pl.when(j == 0)
            def _():
                acc_scr[...] = part

            @pl.when(j != 0)
            def _():
                acc_scr[...] += part
        else:
            acc_scr[...] += part

    @pl.when(j == pl.num_programs(2) - 1)
    def _():
        y_ref[...] = acc_scr[...].astype(y_ref.dtype)


def _moe_ffn(xe, wg, wu, wd, layer, hid_split=2):
    b, ne, cap, d = xe.shape
    dh = wg.shape[3] // hid_split
    return pl.pallas_call(
        functools.partial(_moe_ffn_kernel, hid_tile=256),
        out_shape=jax.ShapeDtypeStruct((b, ne, cap, d), BF16),
        grid=(b, ne, hid_split),
        in_specs=[pl.BlockSpec((None, None, cap, d), lambda bi, e, j: (bi, e, 0, 0)),
                  pl.BlockSpec((None, None, d, dh), lambda bi, e, j: (layer, e, 0, j)),
                  pl.BlockSpec((None, None, d, dh), lambda bi, e, j: (layer, e, 0, j)),
                  pl.BlockSpec((None, None, dh, d), lambda bi, e, j: (layer, e, j, 0))],
        out_specs=pl.BlockSpec((None, None, cap, d), lambda bi, e, j: (bi, e, 0, 0)),
        scratch_shapes=[pltpu.VMEM((cap, d), F32)],
        compiler_params=_cparams(("parallel", "parallel", "arbitrary")), name="moe_ffn",
    )(xe, wg, wu, wd)


def _moe_combine_kernel(cnt_ref, pos_ref, aff_ref, x_ref, gt2_ref, y_ref, fg_ref, o_ref, acc_scr, *, ntile, cap, final):
    bi, t = pl.program_id(0), pl.program_id(1)
    lane = lax.broadcasted_iota(I32, (MOE_TILE, SLOT_CHUNK), 1)
    pos = pos_ref[...]
    aff = aff_ref[...]

    def windows(w):
        total = None
        for e in range(N_EXPERTS):
            lo, start = _window_start(cnt_ref[(bi * N_EXPERTS + e) * (ntile + 1) + t], w, cap)
            pcol = pos[:, e:e + 1]
            onehot = jnp.where((pcol - start == lane) & (pcol >= lo), 1.0, 0.0).astype(BF16)
            part = aff[:, e:e + 1] * jnp.dot(onehot, y_ref[e, pl.ds(start, SLOT_CHUNK), :],
                                             preferred_element_type=F32)
            total = part if total is None else total + part
        return total

    acc_scr[...] = windows(0)
    nwin = 1
    for e in range(N_EXPERTS):
        base = (bi * N_EXPERTS + e) * (ntile + 1) + t
        nwin = jnp.maximum(nwin, _num_windows(cnt_ref[base], cnt_ref[base + 1]))

    def overflow(w, _):
        acc_scr[...] += windows(w)
        return 0

    lax.fori_loop(1, nwin, overflow, 0)
    out = x_ref[...] + gt2_ref[...] * acc_scr[...]
    if final:
        out = out * lax.rsqrt(jnp.mean(out * out, axis=-1, keepdims=True) + EPS) * fg_ref[...]
    o_ref[...] = out


def _moe_combine(cnt, pos, aff, x, gt2, y, final_g, final):
    b, n, d = x.shape
    ntile = n // MOE_TILE
    cap = y.shape[2]
    grid_spec = pltpu.PrefetchScalarGridSpec(
        num_scalar_prefetch=1, grid=(b, ntile),
        in_specs=[pl.BlockSpec((None, MOE_TILE, LANES), lambda bi, t, c: (bi, t, 0)),
                  pl.BlockSpec((None, MOE_TILE, LANES), lambda bi, t, c: (bi, t, 0)),
                  pl.BlockSpec((None, MOE_TILE, d), lambda bi, t, c: (bi, t, 0)),
                  pl.BlockSpec((None, 1, d), lambda bi, t, c: (bi, 0, 0)),
                  pl.BlockSpec((None, N_EXPERTS, cap, d), lambda bi, t, c: (bi, 0, 0, 0),
                               pipeline_mode=pl.Buffered(1)),
                  pl.BlockSpec((1, d), lambda bi, t, c: (0, 0))],
        out_specs=pl.BlockSpec((None, MOE_TILE, d), lambda bi, t, c: (bi, t, 0)),
        scratch_shapes=[pltpu.VMEM((MOE_TILE, d), F32)])
    return pl.pallas_call(
        functools.partial(_moe_combine_kernel, ntile=ntile, cap=cap, final=final),
        out_shape=jax.ShapeDtypeStruct((b, n, d), F32),
        grid_spec=grid_spec, compiler_params=_cparams(("arbitrary", "arbitrary")), name="moe_combine",
    )(cnt, pos, aff, x, gt2, y, final_g)


def _ec_moe(x_mid, h2, aff, gt2, wg, wu, wd, layer, final_g, final):
    b, n, _ = x_mid.shape
    cap = max(1, EC_FACTOR * n // N_EXPERTS)
    ntile = n // MOE_TILE
    pos, post, offs = _route(aff, cap)
    post = post.transpose(0, 2, 1, 3).reshape(b, N_EXPERTS, ntile, 1, MOE_TILE)
    starts = offs[:, ::MOE_TILE // CHUNK, 0, :N_EXPERTS].transpose(0, 2, 1)
    cnt = jnp.concatenate([starts, jnp.full((b, N_EXPERTS, 1), cap, I32)], axis=2).reshape(-1)
    y = _moe_ffn(_moe_gather(cnt, post, h2, cap), wg, wu, wd, layer)
    return _moe_combine(cnt, pos, aff, x_mid, gt2, y, final_g, final)


def _rope_table(n):
    rows = n // GRID_W
    row = jnp.repeat(jnp.arange(rows), GRID_W).astype(F32)
    col = jnp.tile(jnp.arange(GRID_W), rows).astype(F32)
    nf = ATTN_DIM // 4
    inv = ROPE_BASE ** (-jnp.arange(nf, dtype=F32) / nf)
    ang = jnp.concatenate([row[:, None] * inv, col[:, None] * inv], axis=-1)
    cos, sin = jnp.cos(ang), jnp.sin(ang)
    reps = LANES // ATTN_DIM
    return jnp.concatenate([jnp.tile(jnp.concatenate([cos, cos], -1), (1, reps)),
                            jnp.tile(jnp.concatenate([-sin, sin], -1), (1, reps))], axis=-1)


def _pad_lanes(a):
    return jnp.pad(a, ((0, 0), (0, LANES - a.shape[1])))


def kernel(x, c, ctx, c_ctx, w_mod, b_mod, norm_mix_g, norm_ffn_g, final_norm_g, ab_w_in, ab_conv_w, ab_gate_b,
           ab_head_g, ab_sink, ab_w_out, gm_w_in, gm_ln_g, gm_ln_b, gm_w_s, gm_b_s, gm_w_out, moe_w_router,
           moe_w_gate, moe_w_up, moe_w_down):
    b, n, d = x.shape
    depth = w_mod.shape[0]
    assert depth <= 2, "context stream is only advanced for deeper stacks; not supported here"
    cond = jnp.zeros((SUBLANES, d), F32).at[:b].set(c).at[b].set(c_ctx)
    mods = _adaln(cond, w_mod, b_mod)

    def mod_rows(layer, rows):
        m = mods[layer, rows].reshape(-1, 6, 1, d)
        return [m[:, i] for i in range(6)]

    row = lambda v: v.reshape(1, -1)
    for layer in range(depth):
        sh1, sc1, gt1, sh2, sc2, gt2 = mod_rows(layer, slice(0, b))
        g1, g2 = row(norm_mix_g[layer]), row(norm_ffn_g[layer])
        wr = _pad_lanes(moe_w_router[layer])
        wr_hi = wr.astype(BF16)
        wr = jnp.concatenate([wr_hi, wr_hi, (wr - wr_hi.astype(F32)).astype(BF16)], axis=0)
        if layer % 2 == 0:
            e = layer // 2
            csh1, csc1 = (jnp.broadcast_to(v, (b, 1, d)) for v in mod_rows(layer, slice(b, b + 1))[:2])
            w_in = ab_w_in[e]
            g_lo = 4 * LSTM_WIDTH
            w_main = jnp.concatenate([w_in[:, :g_lo], w_in[:, g_lo + N_GATES:]], axis=1).astype(BF16)
            w_gate = _pad_lanes(w_in[:, g_lo:g_lo + N_GATES])
            p, gts = _modmm(x, g1, sc1, sh1, w_main, w_gate, name="ab_in")
            pc, gtc = _modmm(ctx, g1, csc1, csh1, w_main, w_gate, name="ab_in_ctx")
            hf, hb = _mlstm(_conv_silu(p, ab_conv_w[e]), p, gts, _conv_silu(pc, ab_conv_w[e]), pc, gtc,
                            _pad_lanes(row(ab_gate_b[e])))
            at = _attn(p, pc, _rope_table(n), _pad_lanes(row(ab_sink[e])))
            x_mid, h2, aff = _ab_out(hf, hb, p, at, x, row(ab_head_g[e]), ab_w_out[e].astype(BF16),
                                     gt1, g2, sc2, sh2, wr)
        else:
            o = layer // 2
            uv = _modmm(x, g1, sc1, sh1, gm_w_in[o].astype(BF16), act="gelu", name="gm_in")
            x_mid, h2, aff = _gm_out(uv, x, row(gm_ln_g[o]), row(gm_ln_b[o]), gm_w_s[o].astype(BF16),
                                     _pad_lanes(gm_b_s[o].T), gm_w_out[o].astype(BF16), gt1, g2, sc2, sh2, wr)
        x = _ec_moe(x_mid, h2, aff, gt2, moe_w_gate, moe_w_up, moe_w_down, layer,
                    row(final_norm_g), layer == depth - 1)
    return x
```

```python
import functools

import jax
import jax.numpy as jnp
from jax import lax
from jax.experimental import pallas as pl
from jax.experimental.pallas import tpu as pltpu

F32 = jnp.float32
BF16 = jnp.bfloat16
I32 = jnp.int32
HI = lax.Precision.HIGHEST

D_MODEL = 1024
GRID_W = 64
EPS = 1e-6
LSTM_HEADS = 4
LSTM_DIM = 128
LSTM_WIDTH = LSTM_HEADS * LSTM_DIM
LSTM_CONV = 5
CHUNK = 128
ATTN_HEADS = 8
ATTN_KV_HEADS = 2
ATTN_GROUP = ATTN_HEADS // ATTN_KV_HEADS
ATTN_DIM = 64
ROPE_BASE = 10000.0
GM_GROUPS = 8
GM_HALF = 2 * D_MODEL
N_EXPERTS = 16
EC_FACTOR = 2
N_GATES = 4 * LSTM_HEADS

LANES = 128
SUBLANES = 8
BF16_ROWS = 16
VMEM_LIMIT_BYTES = 56 * 1024 * 1024

P_COLS = 4 * LSTM_WIDTH + ATTN_HEADS * ATTN_DIM + 2 * ATTN_KV_HEADS * ATTN_DIM
PB_V, PB_O, PB_AQ = 2, 3, 4
PB_AK, PB_AV = 20, 21
MOE_TILE = 256
SLOT_CHUNK = 64


def _cparams(sem, vmem=VMEM_LIMIT_BYTES):
    return pltpu.CompilerParams(dimension_semantics=sem, vmem_limit_bytes=vmem)


def _rms_mod(x, g, sc, sh):
    y = x * lax.rsqrt(jnp.mean(x * x, axis=-1, keepdims=True) + EPS)
    return y * g * (1.0 + sc) + sh


def _silu(x):
    return x * jax.nn.sigmoid(x)


def _gelu_tanh(x):
    return 0.5 * x * (1.0 + jnp.tanh(0.7978845608028654 * (x + 0.044715 * (x * x * x))))


def _log_sigmoid(x):
    return jnp.minimum(x, 0.0) - jnp.log(1.0 + jnp.exp(-jnp.abs(x)))


def _dot_t(a, b):
    return lax.dot_general(a, b, (((1,), (1,)), ((), ())), preferred_element_type=F32)


def _adaln_kernel(c_ref, w_ref, b_ref, o_ref):
    s = _silu(c_ref[...])
    o_ref[...] = jnp.dot(s, w_ref[...], precision=HI, preferred_element_type=F32) + b_ref[...]


def _adaln(cond, w_mod, b_mod):
    depth, d, six_d = w_mod.shape
    tn = six_d // 4
    return pl.pallas_call(
        _adaln_kernel,
        out_shape=jax.ShapeDtypeStruct((depth, SUBLANES, six_d), F32),
        grid=(depth, six_d // tn),
        in_specs=[pl.BlockSpec((SUBLANES, d), lambda l, j: (0, 0)),
                  pl.BlockSpec((None, d, tn), lambda l, j: (l, 0, j)),
                  pl.BlockSpec((None, 1, tn), lambda l, j: (l, 0, j))],
        out_specs=pl.BlockSpec((None, SUBLANES, tn), lambda l, j: (l, 0, j)),
        compiler_params=_cparams(("arbitrary", "arbitrary")),
        name="adaln",
    )(cond, w_mod, b_mod.reshape(depth, 1, six_d))


def _modmm_kernel(x_ref, g_ref, sc_ref, sh_ref, w_ref, *rest, chunks, act, with_gates):
    if with_gates:
        wg_ref, o_ref, og_ref = rest
    else:
        (o_ref,) = rest
    h = _rms_mod(x_ref[...], g_ref[...], sc_ref[...], sh_ref[...])
    hb = h.astype(BF16)
    for lo, hi in chunks:
        y = jnp.dot(hb, w_ref[:, lo:hi], preferred_element_type=F32)
        if act == "gelu":
            y = _gelu_tanh(y)
        o_ref[:, lo:hi] = y.astype(o_ref.dtype)
    if with_gates:
        og_ref[...] = jnp.dot(h, wg_ref[...], precision=HI, preferred_element_type=F32)


def _modmm(x, g, sc, sh, w, wg=None, *, act=None, tm=512, chunk=512, name="modmm"):
    b, n, d = x.shape
    no = w.shape[1]
    tm = min(tm, n)
    chunks = tuple((lo, min(lo + chunk, no)) for lo in range(0, no, chunk))
    in_specs = [pl.BlockSpec((None, tm, d), lambda bi, i: (bi, i, 0)),
                pl.BlockSpec((1, d), lambda bi, i: (0, 0)),
                pl.BlockSpec((None, 1, d), lambda bi, i: (bi, 0, 0)),
                pl.BlockSpec((None, 1, d), lambda bi, i: (bi, 0, 0)),
                pl.BlockSpec((d, no), lambda bi, i: (0, 0))]
    out_shape = [jax.ShapeDtypeStruct((b, n, no), BF16)]
    out_specs = [pl.BlockSpec((None, tm, no), lambda bi, i: (bi, i, 0))]
    args = [x, g, sc, sh, w]
    if wg is not None:
        in_specs.append(pl.BlockSpec((d, LANES), lambda bi, i: (0, 0)))
        out_shape.append(jax.ShapeDtypeStruct((b, n, LANES), F32))
        out_specs.append(pl.BlockSpec((None, tm, LANES), lambda bi, i: (bi, i, 0)))
        args.append(wg)
    res = pl.pallas_call(
        functools.partial(_modmm_kernel, chunks=chunks, act=act, with_gates=wg is not None),
        out_shape=out_shape, grid=(b, n // tm), in_specs=in_specs, out_specs=out_specs,
        compiler_params=_cparams(("parallel", "parallel")), name=name,
    )(*args)
    return res if wg is not None else res[0]


def _conv_silu_kernel(x_ref, xp_ref, xn_ref, w_ref, o_ref, pad_scr):
    i = pl.program_id(1)
    rows = x_ref.shape[0]
    has_prev = jnp.where(i > 0, 1.0, 0.0)
    has_next = jnp.where(i < pl.num_programs(1) - 1, 1.0, 0.0)
    pad_scr[pl.ds(0, SUBLANES), :] = xp_ref[...].astype(F32)[SUBLANES:, :] * has_prev
    pad_scr[pl.ds(SUBLANES, rows), :] = x_ref[...].astype(F32)
    pad_scr[pl.ds(SUBLANES + rows, SUBLANES), :] = xn_ref[...].astype(F32)[:SUBLANES, :] * has_next
    w = w_ref[...]
    acc = None
    for t in range(LSTM_CONV):
        term = pad_scr[pl.ds(SUBLANES - LSTM_CONV // 2 + t, rows), :] * w[t:t + 1, :]
        acc = term if acc is None else acc + term
    y = _silu(acc)
    o_ref[:, :LSTM_WIDTH] = y[:, :LSTM_WIDTH].astype(o_ref.dtype)
    o_ref[:, LSTM_WIDTH:] = (y[:, LSTM_WIDTH:] * (LSTM_DIM ** -0.5)).astype(o_ref.dtype)


def _conv_silu(p, conv_w, tm=512):
    b, n, _ = p.shape
    tm = min(tm, n)
    qkw = 2 * LSTM_WIDTH
    hpt = tm // BF16_ROWS
    nhb = n // BF16_ROWS
    return pl.pallas_call(
        _conv_silu_kernel,
        out_shape=jax.ShapeDtypeStruct((b, n, qkw), BF16),
        grid=(b, n // tm),
        in_specs=[pl.BlockSpec((None, tm, qkw), lambda bi, i: (bi, i, 0)),
                  pl.BlockSpec((None, BF16_ROWS, qkw), lambda bi, i: (bi, jnp.maximum(i * hpt - 1, 0), 0)),
                  pl.BlockSpec((None, BF16_ROWS, qkw), lambda bi, i: (bi, jnp.minimum((i + 1) * hpt, nhb - 1), 0)),
                  pl.BlockSpec((LSTM_CONV, qkw), lambda bi, i: (0, 0))],
        out_specs=pl.BlockSpec((None, tm, qkw), lambda bi, i: (bi, i, 0)),
        scratch_shapes=[pltpu.VMEM((tm + 2 * SUBLANES, qkw), F32)],
        compiler_params=_cparams(("parallel", "parallel")), name="conv_silu",
    )(p, p, p, conv_w)


def _tri(n, lower):
    r = lax.broadcasted_iota(I32, (n, n), 0)
    c = lax.broadcasted_iota(I32, (n, n), 1)
    return (c <= r) if lower else (c >= r)


def _mlstm_kernel(qkf_ref, qkb_ref, vf_ref, vb_ref, gf_ref, gb_ref, kc_ref, vc_ref, gc_ref, gbias_ref,
                  hf_ref, hb_ref, c_scr, n_scr, m_scr):
    c = pl.program_id(1)
    gbias = gbias_ref[...]

    @pl.when(c == 0)
    def _():
        lc = kc_ref.shape[0]
        kc = kc_ref[...].astype(F32)
        vc = vc_ref[...].astype(F32)
        gates = gc_ref[...] + gbias
        ls = _log_sigmoid(gates)
        for d in range(2):
            tri = jnp.where(_tri(lc, d == 0), 1.0, 0.0).astype(F32)
            bcum = jnp.dot(tri, ls, precision=HI, preferred_element_type=F32)
            last = lc - 1 if d == 0 else 0
            for h in range(LSTM_HEADS):
                ci, cf = 2 * d * LSTM_HEADS + h, (2 * d + 1) * LSTM_HEADS + h
                bcol = bcum[:, cf:cf + 1]
                g = bcum[last:last + 1, cf:cf + 1]
                w = g - bcol + gates[:, ci:ci + 1]
                m_new = jnp.maximum(g, jnp.max(w, axis=0, keepdims=True))
                wt = jnp.exp(w - m_new)
                kh = kc[:, h * LSTM_DIM:(h + 1) * LSTM_DIM]
                vh = vc[:, h * LSTM_DIM:(h + 1) * LSTM_DIM]
                s = d * LSTM_HEADS + h
                c_scr[s] = jnp.dot((vh * wt).T.astype(BF16), kh.astype(BF16), preferred_element_type=F32)
                n_scr[s] = jnp.sum(kh * wt, axis=0, keepdims=True)
                m_scr[s] = jnp.broadcast_to(m_new, (1, LANES))

    tri_l = _tri(CHUNK, True)
    tri_u = _tri(CHUNK, False)
    for d in range(2):
        qk_ref, v_ref, g_ref, h_ref = (qkf_ref, vf_ref, gf_ref, hf_ref) if d == 0 else (qkb_ref, vb_ref, gb_ref, hb_ref)
        qk = qk_ref[...]
        vv = v_ref[...]
        gates = g_ref[...] + gbias
        ls = _log_sigmoid(gates)
        mask = tri_l if d == 0 else tri_u
        tri = jnp.where(mask, 1.0, 0.0).astype(F32)
        bcum = jnp.dot(tri, ls, precision=HI, preferred_element_type=F32)
        bcum_t = bcum.T
        gates_t = gates.T
        last = CHUNK - 1 if d == 0 else 0
        for h in range(LSTM_HEADS):
            ci, cf = 2 * d * LSTM_HEADS + h, (2 * d + 1) * LSTM_HEADS + h
            s = d * LSTM_HEADS + h
            qb = qk[:, h * LSTM_DIM:(h + 1) * LSTM_DIM]
            kb = qk[:, LSTM_WIDTH + h * LSTM_DIM:LSTM_WIDTH + (h + 1) * LSTM_DIM]
            vh = vv[:, h * LSTM_DIM:(h + 1) * LSTM_DIM]
            qh, kh = qb.astype(F32), kb.astype(F32)
            cm, nv, m_prev = c_scr[s], n_scr[s], m_scr[s][:, 0:1]
            bcol = bcum[:, cf:cf + 1]
            log_d = jnp.where(mask, bcol - bcum_t[cf:cf + 1, :] + gates_t[ci:ci + 1, :], -jnp.inf)
            m_row = jnp.maximum(bcol + m_prev, jnp.max(log_d, axis=1, keepdims=True))
            sm = _dot_t(qb, kb) * jnp.exp(log_d - m_row)
            a = jnp.exp(bcol + m_prev - m_row)
            num = jnp.dot(sm.astype(BF16), vh, preferred_element_type=F32) + a * _dot_t(qb, cm.astype(BF16))
            den = jnp.sum(sm, axis=1, keepdims=True) + a * jnp.sum(qh * nv, axis=1, keepdims=True)
            h_ref[:, h * LSTM_DIM:(h + 1) * LSTM_DIM] = num / jnp.maximum(jnp.abs(den), jnp.exp(-m_row))
            g = bcum[last:last + 1, cf:cf + 1]
            w = g - bcol + gates[:, ci:ci + 1]
            m_new = jnp.maximum(g + m_prev, jnp.max(w, axis=0, keepdims=True))
            decay = jnp.exp(g + m_prev - m_new)
            wt = jnp.exp(w - m_new)
            c_scr[s] = decay * cm + jnp.dot((vh.astype(F32) * wt).T.astype(BF16), kb, preferred_element_type=F32)
            n_scr[s] = decay * nv + jnp.sum(kh * wt, axis=0, keepdims=True)
            m_scr[s] = jnp.broadcast_to(m_new, (1, LANES))


def _mlstm(qk, p, g, qkc, pc, gc, gate_bias):
    b, n, _ = p.shape
    lc = pc.shape[1]
    nc = n // CHUNK
    qkw = 2 * LSTM_WIDTH
    in_specs = [
        pl.BlockSpec((None, CHUNK, qkw), lambda bi, c: (bi, c, 0)),
        pl.BlockSpec((None, CHUNK, qkw), lambda bi, c: (bi, nc - 1 - c, 0)),
        pl.BlockSpec((None, CHUNK, LSTM_WIDTH), lambda bi, c: (bi, c, PB_V)),
        pl.BlockSpec((None, CHUNK, LSTM_WIDTH), lambda bi, c: (bi, nc - 1 - c, PB_V)),
        pl.BlockSpec((None, CHUNK, LANES), lambda bi, c: (bi, c, 0)),
        pl.BlockSpec((None, CHUNK, LANES), lambda bi, c: (bi, nc - 1 - c, 0)),
        pl.BlockSpec((None, lc, LSTM_WIDTH), lambda bi, c: (bi, 0, 1)),
        pl.BlockSpec((None, lc, LSTM_WIDTH), lambda bi, c: (bi, 0, PB_V)),
        pl.BlockSpec((None, lc, LANES), lambda bi, c: (bi, 0, 0)),
        pl.BlockSpec((1, LANES), lambda bi, c: (0, 0))]
    out_specs = [pl.BlockSpec((None, CHUNK, LSTM_WIDTH), lambda bi, c: (bi, c, 0)),
                 pl.BlockSpec((None, CHUNK, LSTM_WIDTH), lambda bi, c: (bi, nc - 1 - c, 0))]
    return pl.pallas_call(
        _mlstm_kernel,
        out_shape=[jax.ShapeDtypeStruct((b, n, LSTM_WIDTH), F32)] * 2,
        grid=(b, nc), in_specs=in_specs, out_specs=out_specs,
        scratch_shapes=[pltpu.VMEM((2 * LSTM_HEADS, LSTM_DIM, LSTM_DIM), F32),
                        pltpu.VMEM((2 * LSTM_HEADS, 1, LSTM_DIM), F32),
                        pltpu.VMEM((2 * LSTM_HEADS, 1, LANES), F32)],
        compiler_params=_cparams(("arbitrary", "arbitrary")), name="mlstm",
    )(qk, qk, p, p, g, g, qkc, pc, gc, gate_bias)


def _rope(x, cos, sin_signed):
    w = x.shape[1]
    lane = lax.broadcasted_iota(I32, x.shape, 1)
    first = (lane & (ATTN_DIM - 1)) < (ATTN_DIM // 2)
    partner = jnp.where(first, pltpu.roll(x, w - ATTN_DIM // 2, 1), pltpu.roll(x, ATTN_DIM // 2, 1))
    return x * cos + partner * sin_signed


def _attn_kernel(q_ref, kp_ref, kc_ref, kn_ref, vp_ref, vc_ref, vn_ref, kctx_ref, vctx_ref,
                 tp_ref, tc_ref, tn_ref, sink_ref, o_ref):
    nb = pl.program_id(1)
    last = pl.num_programs(1) - 1
    nctx = kctx_ref.shape[0]
    wloc = 3 * CHUNK

    def table(t_ref):
        t = t_ref[...]
        return t[:, :LANES], t[:, LANES:]

    cos_c, sin_c = table(tc_ref)
    q = _rope(q_ref[...].astype(F32), jnp.concatenate([cos_c] * 4, axis=1), jnp.concatenate([sin_c] * 4, axis=1))
    q = q * (ATTN_DIM ** -0.5)
    ks = []
    for k_ref, t_ref in ((kp_ref, tp_ref), (kc_ref, tc_ref), (kn_ref, tn_ref)):
        cos_t, sin_t = table(t_ref)
        ks.append(_rope(k_ref[...].astype(F32), cos_t, sin_t).astype(BF16))
    k_all = jnp.concatenate(ks + [kctx_ref[...]], axis=0)
    v_all = jnp.concatenate([vp_ref[...], vc_ref[...], vn_ref[...], vctx_ref[...]], axis=0)

    rows = ATTN_GROUP * CHUNK
    ri = lax.broadcasted_iota(I32, (rows, wloc + nctx), 0) & (CHUNK - 1)
    cj = lax.broadcasted_iota(I32, (rows, wloc + nctx), 1)
    lo = jnp.where(nb == 0, CHUNK, 0)
    hi = jnp.where(nb == last, 2 * CHUNK - 1, wloc - 1)
    valid = (cj >= wloc) | ((cj >= ri) & (cj <= ri + 2 * CHUNK) & (cj >= lo) & (cj <= hi))
    lane = lax.broadcasted_iota(I32, (CHUNK, LANES), 1)
    sink = sink_ref[...]

    pieces = [None] * ATTN_HEADS
    for g in range(ATTN_KV_HEADS):
        half_g = (lane >= ATTN_DIM) if g == 1 else (lane < ATTN_DIM)
        qs = []
        for r in range(ATTN_GROUP):
            h = g * ATTN_GROUP + r
            t = q[:, (h // 2) * LANES:(h // 2 + 1) * LANES]
            if h % 2 != g:
                t = pltpu.roll(t, ATTN_DIM, 1)
            qs.append(jnp.where(half_g, t, 0.0).astype(BF16))
        s = _dot_t(jnp.concatenate(qs, axis=0), k_all)
        s = jnp.where(valid, s, -jnp.inf)
        ps, dens = [], []
        for r in range(ATTN_GROUP):
            h = g * ATTN_GROUP + r
            sr = s[r * CHUNK:(r + 1) * CHUNK]
            snk = sink[:, h:h + 1]
            m = jnp.maximum(jnp.max(sr, axis=1, keepdims=True), snk)
            e = jnp.exp(sr - m)
            dens.append(jnp.sum(e, axis=1, keepdims=True) + jnp.exp(snk - m))
            ps.append(e.astype(BF16))
        pv = jnp.dot(jnp.concatenate(ps, axis=0), v_all, preferred_element_type=F32)
        for r in range(ATTN_GROUP):
            h = g * ATTN_GROUP + r
            o = pv[r * CHUNK:(r + 1) * CHUNK] / dens[r]
            if h % 2 != g:
                o = pltpu.roll(o, ATTN_DIM, 1)
            pieces[h] = o
    first_half = lane < ATTN_DIM
    for j in range(ATTN_HEADS // 2):
        o_ref[:, j * LANES:(j + 1) * LANES] = jnp.where(first_half, pieces[2 * j], pieces[2 * j + 1]).astype(o_ref.dtype)


def _attn(p, pc, table, sink):
    b, n, _ = p.shape
    lc = pc.shape[1]
    nb = n // CHUNK
    qw = ATTN_HEADS * ATTN_DIM

    def blk(col, off):
        return pl.BlockSpec((None, CHUNK, LANES), lambda bi, i: (bi, jnp.clip(i + off, 0, nb - 1), col))

    def tab(off):
        return pl.BlockSpec((CHUNK, 2 * LANES), lambda bi, i: (jnp.clip(i + off, 0, nb - 1), 0))

    in_specs = [pl.BlockSpec((None, CHUNK, qw), lambda bi, i: (bi, i, PB_AQ)),
                blk(PB_AK, -1), blk(PB_AK, 0), blk(PB_AK, 1), blk(PB_AV, -1), blk(PB_AV, 0), blk(PB_AV, 1),
                pl.BlockSpec((None, lc, LANES), lambda bi, i: (bi, 0, PB_AK)),
                pl.BlockSpec((None, lc, LANES), lambda bi, i: (bi, 0, PB_AV)),
                tab(-1), tab(0), tab(1),
                pl.BlockSpec((1, LANES), lambda bi, i: (0, 0))]
    return pl.pallas_call(
        _attn_kernel,
        out_shape=jax.ShapeDtypeStruct((b, n, qw), BF16),
        grid=(b, nb), in_specs=in_specs,
        out_specs=pl.BlockSpec((None, CHUNK, qw), lambda bi, i: (bi, i, 0)),
        compiler_params=_cparams(("parallel", "parallel")), name="window_attn",
    )(p, p, p, p, p, p, p, pc, pc, table, table, table, sink)


def _router_tail(x_new, g2, sc2, sh2, wr_ref, x_out_ref, h2_ref, aff_ref):
    x_out_ref[...] = x_new
    h2 = _rms_mod(x_new, g2, sc2, sh2)
    h_hi = h2.astype(BF16)
    h2_ref[...] = h_hi
    h_lo = (h2 - h_hi.astype(F32)).astype(BF16)
    logits = jnp.dot(jnp.concatenate([h_hi, h_lo, h_hi], axis=1), wr_ref[...],
                     preferred_element_type=F32)
    lane = lax.broadcasted_iota(I32, logits.shape, 1)
    logits = jnp.where(lane < N_EXPERTS, logits, -jnp.inf)
    e = jnp.exp(logits - jnp.max(logits, axis=1, keepdims=True))
    aff_ref[...] = e / jnp.sum(e, axis=1, keepdims=True)


def _ab_out_kernel(hf_ref, hb_ref, o_ref, at_ref, x_ref, hg_ref, wo_ref, gt1_ref, g2_ref, sc2_ref, sh2_ref, wr_ref,
                   x_out_ref, h2_ref, aff_ref):
    hsum = hf_ref[...] + hb_ref[...]
    og = jax.nn.sigmoid(o_ref[...].astype(F32))
    hg = hg_ref[...]
    parts = []
    for h in range(LSTM_HEADS):
        sl = slice(h * LSTM_DIM, (h + 1) * LSTM_DIM)
        seg = hsum[:, sl]
        seg = seg * lax.rsqrt(jnp.mean(seg * seg, axis=-1, keepdims=True) + EPS)
        parts.append((seg * hg[:, sl] * og[:, sl]).astype(BF16))
    cat = jnp.concatenate(parts + [at_ref[...]], axis=1)
    y = jnp.dot(cat, wo_ref[...], preferred_element_type=F32)
    _router_tail(x_ref[...] + gt1_ref[...] * y, g2_ref[...], sc2_ref[...], sh2_ref[...], wr_ref,
                 x_out_ref, h2_ref, aff_ref)


def _tail_out(b, n, d, tm):
    shapes = [jax.ShapeDtypeStruct((b, n, d), F32), jax.ShapeDtypeStruct((b, n, d), BF16),
              jax.ShapeDtypeStruct((b, n, LANES), F32)]
    specs = [pl.BlockSpec((None, tm, d), lambda bi, i: (bi, i, 0)),
             pl.BlockSpec((None, tm, d), lambda bi, i: (bi, i, 0)),
             pl.BlockSpec((None, tm, LANES), lambda bi, i: (bi, i, 0))]
    return shapes, specs


def _ab_out(hf, hb, p, at, x, head_g, w_out, gt1, g2, sc2, sh2, wr, tm=256):
    b, n, d = x.shape
    row = lambda w: pl.BlockSpec((None, tm, w), lambda bi, i: (bi, i, 0))
    vec = pl.BlockSpec((None, 1, d), lambda bi, i: (bi, 0, 0))
    const = lambda s: pl.BlockSpec(s, lambda bi, i: (0, 0))
    in_specs = [row(LSTM_WIDTH), row(LSTM_WIDTH),
                pl.BlockSpec((None, tm, LSTM_WIDTH), lambda bi, i: (bi, i, PB_O)),
                row(ATTN_HEADS * ATTN_DIM), row(d), const((1, LSTM_WIDTH)), const(w_out.shape),
                vec, const((1, d)), vec, vec, const((3 * d, LANES))]
    shapes, specs = _tail_out(b, n, d, tm)
    return pl.pallas_call(
        _ab_out_kernel, out_shape=shapes, grid=(b, n // tm), in_specs=in_specs, out_specs=specs,
        compiler_params=_cparams(("parallel", "parallel")), name="ab_out",
    )(hf, hb, p, at, x, head_g, w_out, gt1, g2, sc2, sh2, wr)


def _gm_out_kernel(uv_ref, x_ref, lng_ref, lnb_ref, ws_ref, bs_ref, wo_ref, gt1_ref, g2_ref, sc2_ref, sh2_ref, wr_ref,
                   x_out_ref, h2_ref, aff_ref):
    tm = uv_ref.shape[0]
    gw = GM_HALF // GM_GROUPS
    v = uv_ref[:, GM_HALF:].astype(F32)
    mu = jnp.mean(v, axis=-1, keepdims=True)
    vc = v - mu
    var = jnp.mean(vc * vc, axis=-1, keepdims=True)
    vn = (vc * lax.rsqrt(var + EPS) * lng_ref[...] + lnb_ref[...]).astype(BF16)
    zs = []
    for ch in range(tm // CHUNK):
        rows = slice(ch * CHUNK, (ch + 1) * CHUNK)
        cols = []
        for g in range(GM_GROUPS):
            sv = jnp.dot(ws_ref[g], vn[rows, g * gw:(g + 1) * gw], preferred_element_type=F32)
            cols.append(sv + bs_ref[:, g:g + 1])
        sv = jnp.concatenate(cols, axis=1)
        zs.append((uv_ref[rows, :GM_HALF].astype(F32) * sv).astype(BF16))
    z = jnp.concatenate(zs, axis=0)
    y = jnp.dot(z, wo_ref[...], preferred_element_type=F32)
    _router_tail(x_ref[...] + gt1_ref[...] * y, g2_ref[...], sc2_ref[...], sh2_ref[...], wr_ref,
                 x_out_ref, h2_ref, aff_ref)


def _gm_out(uv, x, ln_g, ln_b, w_s, b_s_t, w_out, gt1, g2, sc2, sh2, wr, tm=256):
    b, n, d = x.shape
    vec = pl.BlockSpec((None, 1, d), lambda bi, i: (bi, 0, 0))
    const = lambda s: pl.BlockSpec(s, lambda *_: (0,) * len(s))
    in_specs = [pl.BlockSpec((None, tm, 2 * GM_HALF), lambda bi, i: (bi, i, 0)),
                pl.BlockSpec((None, tm, d), lambda bi, i: (bi, i, 0)),
                const((1, GM_HALF)), const((1, GM_HALF)), const(w_s.shape), const(b_s_t.shape), const(w_out.shape),
                vec, const((1, d)), vec, vec, const((3 * d, LANES))]
    shapes, specs = _tail_out(b, n, d, tm)
    return pl.pallas_call(
        _gm_out_kernel, out_shape=shapes, grid=(b, n // tm), in_specs=in_specs, out_specs=specs,
        compiler_params=_cparams(("parallel", "parallel")), name="gm_out",
    )(uv, x, ln_g, ln_b, w_s, b_s_t, w_out, gt1, g2, sc2, sh2, wr)


def _route_kernel(aff_ref, pos_ref, post_ref, offs_ref, afft_ref, *, cap):
    n = aff_ref.shape[0]
    nblk = n // CHUNK

    def to_expert_major(k, _):
        rows = pl.ds(pl.multiple_of(k * CHUNK, CHUNK), CHUNK)
        afft_ref[k] = aff_ref[rows, :].T[:N_EXPERTS, :]
        return 0

    lax.fori_loop(0, nblk, to_expert_major, 0)

    def count(pred):
        per_lane = jnp.sum(jnp.where(pred, 1.0, 0.0), axis=0)
        return jnp.sum(per_lane, axis=1, keepdims=True)

    def search(i, prefix):
        cand = prefix | jnp.left_shift(jnp.int32(1), 30 - i)
        cand_f = lax.bitcast_convert_type(cand, F32)
        return jnp.where(count(afft_ref[...] >= cand_f[None]) >= cap, cand, prefix)

    thr_col = lax.bitcast_convert_type(lax.fori_loop(0, 31, search, jnp.zeros((N_EXPERTS, 1), I32)), F32)
    need_col = cap - count(afft_ref[...] > thr_col[None])

    def to_row(col):
        full = jnp.concatenate([jnp.broadcast_to(col, (N_EXPERTS, LANES)),
                                jnp.zeros((LANES - N_EXPERTS, LANES), F32)], axis=0)
        return full.T[0:1, :]

    thr, need = to_row(thr_col), to_row(need_col)
    tril = jnp.where(_tri(CHUNK, True), 1.0, 0.0).astype(BF16)

    def block(k, carry):
        run_tie, run_sel = carry
        rows = pl.ds(pl.multiple_of(k * CHUNK, CHUNK), CHUNK)
        a = aff_ref[rows, :]
        gt = a > thr
        tie = jnp.where(a == thr, 1.0, 0.0)
        tie_incl = jnp.dot(tril, tie.astype(BF16), preferred_element_type=F32)
        sel = jnp.where(gt | ((tie > 0.0) & (tie_incl - tie + run_tie < need)), 1.0, 0.0)
        sel_incl = jnp.dot(tril, sel.astype(BF16), preferred_element_type=F32)
        pos = jnp.where(sel > 0.0, sel_incl - sel + run_sel, -1.0)
        pos_ref[rows, :] = pos.astype(I32)
        post_ref[k] = pos.T[:N_EXPERTS, :].astype(I32)
        offs_ref[k] = run_sel.astype(I32)
        return (run_tie + tie_incl[CHUNK - 1:CHUNK, :], run_sel + sel_incl[CHUNK - 1:CHUNK, :])

    zero = jnp.zeros((1, LANES), F32)
    lax.fori_loop(0, nblk, block, (zero, zero))


def _route(aff, cap):
    b, n, _ = aff.shape
    nblk = n // CHUNK
    return pl.pallas_call(
        functools.partial(_route_kernel, cap=cap),
        out_shape=[jax.ShapeDtypeStruct((b, n, LANES), I32),
                   jax.ShapeDtypeStruct((b, nblk, N_EXPERTS, CHUNK), I32),
                   jax.ShapeDtypeStruct((b, nblk, 1, LANES), I32),
                   jax.ShapeDtypeStruct((b, nblk, N_EXPERTS, CHUNK), F32)],
        grid=(b,),
        in_specs=[pl.BlockSpec((None, n, LANES), lambda bi: (bi, 0, 0))],
        out_specs=[pl.BlockSpec((None, n, LANES), lambda bi: (bi, 0, 0)),
                   pl.BlockSpec((None, nblk, N_EXPERTS, CHUNK), lambda bi: (bi, 0, 0, 0)),
                   pl.BlockSpec((None, nblk, 1, LANES), lambda bi: (bi, 0, 0, 0)),
                   pl.BlockSpec((None, nblk, N_EXPERTS, CHUNK), lambda bi: (bi, 0, 0, 0))],
        compiler_params=_cparams(("parallel",)), name="route",
    )(aff)


def _window_start(s0, w, cap):
    lo = ((s0 >> 4) << 4) + w * SLOT_CHUNK
    return lo, pl.multiple_of(jnp.minimum(lo, cap - SLOT_CHUNK), BF16_ROWS)


def _num_windows(s0, s1):
    return (s1 - ((s0 >> 4) << 4) + SLOT_CHUNK - 1) >> (SLOT_CHUNK.bit_length() - 1)


def _moe_gather_kernel(cnt_ref, post_ref, afft_ref, h_ref, xe_ref, gate_ref, *, ntile, cap, group):
    bi, eg, t = pl.program_id(0), pl.program_id(1), pl.program_id(2)

    @pl.when(t == 0)
    def _():
        xe_ref[...] = jnp.zeros_like(xe_ref)
        gate_ref[...] = jnp.zeros_like(gate_ref)

    slot = lax.broadcasted_iota(I32, (SLOT_CHUNK, MOE_TILE), 0)
    htile = h_ref[...]
    bases = [(bi * N_EXPERTS + eg * group + g) * (ntile + 1) + t for g in range(group)]

    def windows(w, _):
        starts, onehots = [], []
        for g in range(group):
            lo, start = _window_start(cnt_ref[bases[g]], w, cap)
            posrow = post_ref[g]
            hit = (posrow - start == slot) & (posrow >= lo)
            onehots.append(jnp.where(hit, 1.0, 0.0).astype(BF16))
            gates = jnp.sum(jnp.where(hit, afft_ref[g], 0.0), axis=1, keepdims=True)
            dst = pl.ds(start, SLOT_CHUNK)
            gate_ref[g, dst, :] = gate_ref[g, dst, :] + jnp.broadcast_to(gates, (SLOT_CHUNK, LANES))
            starts.append(start)
        rows = jnp.dot(jnp.concatenate(onehots, axis=0), htile, preferred_element_type=F32)
        for g in range(group):
            dst = pl.ds(starts[g], SLOT_CHUNK)
            xe_ref[g, dst, :] = xe_ref[g, dst, :] + rows[g * SLOT_CHUNK:(g + 1) * SLOT_CHUNK].astype(BF16)
        return 0

    windows(0, 0)
    nwin = 1
    for g in range(group):
        nwin = jnp.maximum(nwin, _num_windows(cnt_ref[bases[g]], cnt_ref[bases[g] + 1]))
    lax.fori_loop(1, nwin, windows, 0)


def _moe_gather(cnt, post, afft, h2, cap, group=8):
    b, n, d = h2.shape
    ntile = n // MOE_TILE
    per_tile = pl.BlockSpec((None, group, None, 1, MOE_TILE), lambda bi, eg, t, c: (bi, eg, t, 0, 0))
    grid_spec = pltpu.PrefetchScalarGridSpec(
        num_scalar_prefetch=1, grid=(b, N_EXPERTS // group, ntile),
        in_specs=[per_tile, per_tile, pl.BlockSpec((None, MOE_TILE, d), lambda bi, eg, t, c: (bi, t, 0))],
        out_specs=[pl.BlockSpec((None, group, cap, d), lambda bi, eg, t, c: (bi, eg, 0, 0)),
                   pl.BlockSpec((None, group, cap, LANES), lambda bi, eg, t, c: (bi, eg, 0, 0))])
    return pl.pallas_call(
        functools.partial(_moe_gather_kernel, ntile=ntile, cap=cap, group=group),
        out_shape=[jax.ShapeDtypeStruct((b, N_EXPERTS, cap, d), BF16),
                   jax.ShapeDtypeStruct((b, N_EXPERTS, cap, LANES), F32)],
        grid_spec=grid_spec, compiler_params=_cparams(("arbitrary", "arbitrary", "arbitrary")), name="moe_gather",
    )(cnt, post, afft, h2)


def _moe_ffn_kernel(xe_ref, gate_ref, wg_ref, wu_ref, wd_ref, y_ref, acc_scr, *, hid_tile):
    j = pl.program_id(2)
    xe = xe_ref[...]
    for k in range(wg_ref.shape[1] // hid_tile):
        cols = slice(k * hid_tile, (k + 1) * hid_tile)
        gate = jnp.dot(xe, wg_ref[:, cols].astype(BF16), preferred_element_type=F32)
        up = jnp.dot(xe, wu_ref[:, cols].astype(BF16), preferred_element_type=F32)
        hid = (_silu(gate) * up).astype(BF16)
        part = jnp.dot(hid, wd_ref[cols, :].astype(BF16), preferred_element_type=F32)
        if k == 0:
            @pl.when(j == 0)
            def _():
                acc_scr[...] = part

            @pl.when(j != 0)
            def _():
                acc_scr[...] += part
        else:
            acc_scr[...] += part

    @pl.when(j == pl.num_programs(2) - 1)
    def _():
        y_ref[...] = (acc_scr[...] * gate_ref[:, 0:1]).astype(y_ref.dtype)


def _moe_ffn(xe, gate, wg, wu, wd, layer, hid_split=2):
    b, ne, cap, d = xe.shape
    dh = wg.shape[3] // hid_split
    return pl.pallas_call(
        functools.partial(_moe_ffn_kernel, hid_tile=256),
        out_shape=jax.ShapeDtypeStruct((b, ne, cap, d), BF16),
        grid=(b, ne, hid_split),
        in_specs=[pl.BlockSpec((None, None, cap, d), lambda bi, e, j: (bi, e, 0, 0)),
                  pl.BlockSpec((None, None, cap, LANES), lambda bi, e, j: (bi, e, 0, 0)),
                  pl.BlockSpec((None, None, d, dh), lambda bi, e, j: (layer, e, 0, j)),
                  pl.BlockSpec((None, None, d, dh), lambda bi, e, j: (layer, e, 0, j)),
                  pl.BlockSpec((None, None, dh, d), lambda bi, e, j: (layer, e, j, 0))],
        out_specs=pl.BlockSpec((None, None, cap, d), lambda bi, e, j: (bi, e, 0, 0)),
        scratch_shapes=[pltpu.VMEM((cap, d), F32)],
        compiler_params=_cparams(("parallel", "parallel", "arbitrary")), name="moe_ffn",
    )(xe, gate, wg, wu, wd)


def _moe_combine_kernel(cnt_ref, pos_ref, x_ref, gt2_ref, y_ref, fg_ref, o_ref, acc_scr, ycat_scr, *, ntile, cap, final):
    bi, t = pl.program_id(0), pl.program_id(1)
    lane = lax.broadcasted_iota(I32, (MOE_TILE, SLOT_CHUNK), 1)
    pos = pos_ref[...]

    def windows(w):
        onehots = []
        for e in range(N_EXPERTS):
            lo, start = _window_start(cnt_ref[(bi * N_EXPERTS + e) * (ntile + 1) + t], w, cap)
            pcol = pos[:, e:e + 1]
            onehots.append(jnp.where((pcol - start == lane) & (pcol >= lo), 1.0, 0.0).astype(BF16))
            ycat_scr[e * SLOT_CHUNK:(e + 1) * SLOT_CHUNK, :] = y_ref[e, pl.ds(start, SLOT_CHUNK), :]
        return jnp.dot(jnp.concatenate(onehots, axis=1), ycat_scr[...], preferred_element_type=F32)

    acc_scr[...] = windows(0)
    nwin = 1
    for e in range(N_EXPERTS):
        base = (bi * N_EXPERTS + e) * (ntile + 1) + t
        nwin = jnp.maximum(nwin, _num_windows(cnt_ref[base], cnt_ref[base + 1]))

    def overflow(w, _):
        acc_scr[...] += windows(w)
        return 0

    lax.fori_loop(1, nwin, overflow, 0)
    out = x_ref[...] + gt2_ref[...] * acc_scr[...]
    if final:
        out = out * lax.rsqrt(jnp.mean(out * out, axis=-1, keepdims=True) + EPS) * fg_ref[...]
    o_ref[...] = out


def _moe_combine(cnt, pos, x, gt2, y, final_g, final):
    b, n, d = x.shape
    ntile = n // MOE_TILE
    cap = y.shape[2]
    grid_spec = pltpu.PrefetchScalarGridSpec(
        num_scalar_prefetch=1, grid=(b, ntile),
        in_specs=[pl.BlockSpec((None, MOE_TILE, LANES), lambda bi, t, c: (bi, t, 0)),
                  pl.BlockSpec((None, MOE_TILE, d), lambda bi, t, c: (bi, t, 0)),
                  pl.BlockSpec((None, 1, d), lambda bi, t, c: (bi, 0, 0)),
                  pl.BlockSpec((None, N_EXPERTS, cap, d), lambda bi, t, c: (bi, 0, 0, 0),
                               pipeline_mode=pl.Buffered(1)),
                  pl.BlockSpec((1, d), lambda bi, t, c: (0, 0))],
        out_specs=pl.BlockSpec((None, MOE_TILE, d), lambda bi, t, c: (bi, t, 0)),
        scratch_shapes=[pltpu.VMEM((MOE_TILE, d), F32), pltpu.VMEM((N_EXPERTS * SLOT_CHUNK, d), BF16)])
    return pl.pallas_call(
        functools.partial(_moe_combine_kernel, ntile=ntile, cap=cap, final=final),
        out_shape=jax.ShapeDtypeStruct((b, n, d), F32),
        grid_spec=grid_spec, compiler_params=_cparams(("arbitrary", "arbitrary")), name="moe_combine",
    )(cnt, pos, x, gt2, y, final_g)


def _ec_moe(x_mid, h2, aff, gt2, wg, wu, wd, layer, final_g, final):
    b, n, _ = x_mid.shape
    cap = max(1, EC_FACTOR * n // N_EXPERTS)
    ntile = n // MOE_TILE
    pos, post, offs, afft = _route(aff, cap)
    per_tile = lambda a: a.transpose(0, 2, 1, 3).reshape(b, N_EXPERTS, ntile, 1, MOE_TILE)
    starts = offs[:, ::MOE_TILE // CHUNK, 0, :N_EXPERTS].transpose(0, 2, 1)
    cnt = jnp.concatenate([starts, jnp.full((b, N_EXPERTS, 1), cap, I32)], axis=2).reshape(-1)
    xe, gate = _moe_gather(cnt, per_tile(post), per_tile(afft), h2, cap)
    y = _moe_ffn(xe, gate, wg, wu, wd, layer)
    return _moe_combine(cnt, pos, x_mid, gt2, y, final_g, final)


def _rope_table(n):
    rows = n // GRID_W
    row = jnp.repeat(jnp.arange(rows), GRID_W).astype(F32)
    col = jnp.tile(jnp.arange(GRID_W), rows).astype(F32)
    nf = ATTN_DIM // 4
    inv = ROPE_BASE ** (-jnp.arange(nf, dtype=F32) / nf)
    ang = jnp.concatenate([row[:, None] * inv, col[:, None] * inv], axis=-1)
    cos, sin = jnp.cos(ang), jnp.sin(ang)
    reps = LANES // ATTN_DIM
    return jnp.concatenate([jnp.tile(jnp.concatenate([cos, cos], -1), (1, reps)),
                            jnp.tile(jnp.concatenate([-sin, sin], -1), (1, reps))], axis=-1)


def _pad_lanes(a):
    return jnp.pad(a, ((0, 0), (0, LANES - a.shape[1])))


def kernel(x, c, ctx, c_ctx, w_mod, b_mod, norm_mix_g, norm_ffn_g, final_norm_g, ab_w_in, ab_conv_w, ab_gate_b,
           ab_head_g, ab_sink, ab_w_out, gm_w_in, gm_ln_g, gm_ln_b, gm_w_s, gm_b_s, gm_w_out, moe_w_router,
           moe_w_gate, moe_w_up, moe_w_down):
    b, n, d = x.shape
    depth = w_mod.shape[0]
    assert depth <= 2, "context stream is only advanced for deeper stacks; not supported here"
    cond = jnp.zeros((SUBLANES, d), F32).at[:b].set(c).at[b].set(c_ctx)
    mods = _adaln(cond, w_mod, b_mod)

    def mod_rows(layer, rows):
        m = mods[layer, rows].reshape(-1, 6, 1, d)
        return [m[:, i] for i in range(6)]

    row = lambda v: v.reshape(1, -1)
    for layer in range(depth):
        sh1, sc1, gt1, sh2, sc2, gt2 = mod_rows(layer, slice(0, b))
        g1, g2 = row(norm_mix_g[layer]), row(norm_ffn_g[layer])
        wr = _pad_lanes(moe_w_router[layer])
        wr_hi = wr.astype(BF16)
        wr = jnp.concatenate([wr_hi, wr_hi, (wr - wr_hi.astype(F32)).astype(BF16)], axis=0)
        if layer % 2 == 0:
            e = layer // 2
            csh1, csc1 = (jnp.broadcast_to(v, (b, 1, d)) for v in mod_rows(layer, slice(b, b + 1))[:2])
            w_in = ab_w_in[e]
            g_lo = 4 * LSTM_WIDTH
            w_main = jnp.concatenate([w_in[:, :g_lo], w_in[:, g_lo + N_GATES:]], axis=1).astype(BF16)
            w_gate = _pad_lanes(w_in[:, g_lo:g_lo + N_GATES])
            p, gts = _modmm(x, g1, sc1, sh1, w_main, w_gate, name="ab_in")
            pc, gtc = _modmm(ctx, g1, csc1, csh1, w_main, w_gate, name="ab_in_ctx")
            hf, hb = _mlstm(_conv_silu(p, ab_conv_w[e]), p, gts, _conv_silu(pc, ab_conv_w[e]), pc, gtc,
                            _pad_lanes(row(ab_gate_b[e])))
            at = _attn(p, pc, _rope_table(n), _pad_lanes(row(ab_sink[e])))
            x_mid, h2, aff = _ab_out(hf, hb, p, at, x, row(ab_head_g[e]), ab_w_out[e].astype(BF16),
                                     gt1, g2, sc2, sh2, wr)
        else:
            o = layer // 2
            uv = _modmm(x, g1, sc1, sh1, gm_w_in[o].astype(BF16), act="gelu", name="gm_in")
            x_mid, h2, aff = _gm_out(uv, x, row(gm_ln_g[o]), row(gm_ln_b[o]), gm_w_s[o].astype(BF16),
                                     _pad_lanes(gm_b_s[o].T), gm_w_out[o].astype(BF16), gt1, g2, sc2, sh2, wr)
        x = _ec_moe(x_mid, h2, aff, gt2, moe_w_gate, moe_w_up, moe_w_down, layer,
                    row(final_norm_g), layer == depth - 1)
    return x
```

```python
import functools

import jax
import jax.numpy as jnp
from jax import lax
from jax.experimental import pallas as pl
from jax.experimental.pallas import tpu as pltpu

F32 = jnp.float32
BF16 = jnp.bfloat16
I32 = jnp.int32
HI = lax.Precision.HIGHEST

D_MODEL = 1024
GRID_W = 64
EPS = 1e-6
LSTM_HEADS = 4
LSTM_DIM = 128
LSTM_WIDTH = LSTM_HEADS * LSTM_DIM
LSTM_CONV = 5
CHUNK = 128
ATTN_HEADS = 8
ATTN_KV_HEADS = 2
ATTN_GROUP = ATTN_HEADS // ATTN_KV_HEADS
ATTN_DIM = 64
ROPE_BASE = 10000.0
GM_GROUPS = 8
GM_HALF = 2 * D_MODEL
N_EXPERTS = 16
EC_FACTOR = 2
N_GATES = 4 * LSTM_HEADS

LANES = 128
SUBLANES = 8
BF16_ROWS = 16
VMEM_LIMIT_BYTES = 56 * 1024 * 1024

P_COLS = 4 * LSTM_WIDTH + ATTN_HEADS * ATTN_DIM + 2 * ATTN_KV_HEADS * ATTN_DIM
PB_V, PB_O, PB_AQ = 2, 3, 4
PB_AK, PB_AV = 20, 21
MOE_TILE = 256
SLOT_CHUNK = 64


def _cparams(sem, vmem=VMEM_LIMIT_BYTES):
    return pltpu.CompilerParams(dimension_semantics=sem, vmem_limit_bytes=vmem)


def _rms_mod(x, g, sc, sh):
    y = x * lax.rsqrt(jnp.mean(x * x, axis=-1, keepdims=True) + EPS)
    return y * g * (1.0 + sc) + sh


def _silu(x):
    return x * jax.nn.sigmoid(x)


def _gelu_tanh(x):
    return 0.5 * x * (1.0 + jnp.tanh(0.7978845608028654 * (x + 0.044715 * (x * x * x))))


def _log_sigmoid(x):
    return jnp.minimum(x, 0.0) - jnp.log(1.0 + jnp.exp(-jnp.abs(x)))


def _dot_t(a, b):
    return lax.dot_general(a, b, (((1,), (1,)), ((), ())), preferred_element_type=F32)


def _adaln_kernel(c_ref, w_ref, b_ref, o_ref):
    s = _silu(c_ref[...])
    o_ref[...] = jnp.dot(s, w_ref[...], precision=HI, preferred_element_type=F32) + b_ref[...]


def _adaln(cond, w_mod, b_mod):
    depth, d, six_d = w_mod.shape
    tn = six_d // 4
    return pl.pallas_call(
        _adaln_kernel,
        out_shape=jax.ShapeDtypeStruct((depth, SUBLANES, six_d), F32),
        grid=(depth, six_d // tn),
        in_specs=[pl.BlockSpec((SUBLANES, d), lambda l, j: (0, 0)),
                  pl.BlockSpec((None, d, tn), lambda l, j: (l, 0, j)),
                  pl.BlockSpec((None, 1, tn), lambda l, j: (l, 0, j))],
        out_specs=pl.BlockSpec((None, SUBLANES, tn), lambda l, j: (l, 0, j)),
        compiler_params=_cparams(("arbitrary", "arbitrary")),
        name="adaln",
    )(cond, w_mod, b_mod.reshape(depth, 1, six_d))


def _hi_lo_lhs(h, h_hi):
    return jnp.concatenate([h_hi, (h - h_hi.astype(F32)).astype(BF16), h_hi], axis=1)


def _hi_lo_rhs(w):
    w_hi = w.astype(BF16)
    return jnp.concatenate([w_hi, w_hi, (w - w_hi.astype(F32)).astype(BF16)], axis=0)


def _modmm_kernel(x_ref, g_ref, sc_ref, sh_ref, w_ref, *rest, chunks, act, with_gates):
    if with_gates:
        wg_ref, o_ref, og_ref, vt_ref = rest
    else:
        (o_ref,) = rest
    h = _rms_mod(x_ref[...], g_ref[...], sc_ref[...], sh_ref[...])
    hb = h.astype(BF16)
    for lo, hi in chunks:
        y = jnp.dot(hb, w_ref[:, lo:hi], preferred_element_type=F32)
        if act == "gelu":
            y = _gelu_tanh(y)
        o_ref[:, lo:hi] = y.astype(o_ref.dtype)
        if with_gates and lo == PB_V * LSTM_WIDTH:
            vt_ref[...] = y.T.astype(vt_ref.dtype)
    if with_gates:
        og_ref[...] = jnp.dot(_hi_lo_lhs(h, hb), wg_ref[...], preferred_element_type=F32)


def _modmm(x, g, sc, sh, w, wg=None, *, act=None, tm=512, chunk=512, name="modmm"):
    b, n, d = x.shape
    no = w.shape[1]
    tm = min(tm, n)
    assert chunk == LSTM_WIDTH
    chunks = tuple((lo, min(lo + chunk, no)) for lo in range(0, no, chunk))
    in_specs = [pl.BlockSpec((None, tm, d), lambda bi, i: (bi, i, 0)),
                pl.BlockSpec((1, d), lambda bi, i: (0, 0)),
                pl.BlockSpec((None, 1, d), lambda bi, i: (bi, 0, 0)),
                pl.BlockSpec((None, 1, d), lambda bi, i: (bi, 0, 0)),
                pl.BlockSpec((d, no), lambda bi, i: (0, 0))]
    out_shape = [jax.ShapeDtypeStruct((b, n, no), BF16)]
    out_specs = [pl.BlockSpec((None, tm, no), lambda bi, i: (bi, i, 0))]
    args = [x, g, sc, sh, w]
    if wg is not None:
        in_specs.append(pl.BlockSpec(wg.shape, lambda bi, i: (0, 0)))
        out_shape += [jax.ShapeDtypeStruct((b, n, LANES), F32), jax.ShapeDtypeStruct((b, LSTM_WIDTH, n), BF16)]
        out_specs += [pl.BlockSpec((None, tm, LANES), lambda bi, i: (bi, i, 0)),
                      pl.BlockSpec((None, LSTM_WIDTH, tm), lambda bi, i: (bi, 0, i))]
        args.append(wg)
    res = pl.pallas_call(
        functools.partial(_modmm_kernel, chunks=chunks, act=act, with_gates=wg is not None),
        out_shape=out_shape, grid=(b, n // tm), in_specs=in_specs, out_specs=out_specs,
        compiler_params=_cparams(("parallel", "parallel")), name=name,
    )(*args)
    return res if wg is not None else res[0]


def _conv_silu_kernel(x_ref, xp_ref, xn_ref, w_ref, o_ref, pad_scr):
    i = pl.program_id(1)
    rows = x_ref.shape[0]
    has_prev = jnp.where(i > 0, 1.0, 0.0)
    has_next = jnp.where(i < pl.num_programs(1) - 1, 1.0, 0.0)
    pad_scr[pl.ds(0, SUBLANES), :] = xp_ref[...].astype(F32)[SUBLANES:, :] * has_prev
    pad_scr[pl.ds(SUBLANES, rows), :] = x_ref[...].astype(F32)
    pad_scr[pl.ds(SUBLANES + rows, SUBLANES), :] = xn_ref[...].astype(F32)[:SUBLANES, :] * has_next
    w = w_ref[...]
    acc = None
    for t in range(LSTM_CONV):
        term = pad_scr[pl.ds(SUBLANES - LSTM_CONV // 2 + t, rows), :] * w[t:t + 1, :]
        acc = term if acc is None else acc + term
    y = _silu(acc)
    o_ref[:, :LSTM_WIDTH] = y[:, :LSTM_WIDTH].astype(o_ref.dtype)
    o_ref[:, LSTM_WIDTH:] = (y[:, LSTM_WIDTH:] * (LSTM_DIM ** -0.5)).astype(o_ref.dtype)


def _conv_silu(p, conv_w, tm=512):
    b, n, _ = p.shape
    tm = min(tm, n)
    qkw = 2 * LSTM_WIDTH
    hpt = tm // BF16_ROWS
    nhb = n // BF16_ROWS
    return pl.pallas_call(
        _conv_silu_kernel,
        out_shape=jax.ShapeDtypeStruct((b, n, qkw), BF16),
        grid=(b, n // tm),
        in_specs=[pl.BlockSpec((None, tm, qkw), lambda bi, i: (bi, i, 0)),
                  pl.BlockSpec((None, BF16_ROWS, qkw), lambda bi, i: (bi, jnp.maximum(i * hpt - 1, 0), 0)),
                  pl.BlockSpec((None, BF16_ROWS, qkw), lambda bi, i: (bi, jnp.minimum((i + 1) * hpt, nhb - 1), 0)),
                  pl.BlockSpec((LSTM_CONV, qkw), lambda bi, i: (0, 0))],
        out_specs=pl.BlockSpec((None, tm, qkw), lambda bi, i: (bi, i, 0)),
        scratch_shapes=[pltpu.VMEM((tm + 2 * SUBLANES, qkw), F32)],
        compiler_params=_cparams(("parallel", "parallel")), name="conv_silu",
    )(p, p, p, conv_w)


def _tri(n, lower):
    r = lax.broadcasted_iota(I32, (n, n), 0)
    c = lax.broadcasted_iota(I32, (n, n), 1)
    return (c <= r) if lower else (c >= r)


STATE_ROWS = LSTM_DIM + BF16_ROWS


def _mlstm_segment(d, gates, k_all, vt_all, q_all, cn_scr, m_scr, ht_ref, first):
    seg = gates.shape[0]
    r = lax.broadcasted_iota(I32, (seg, seg), 0)
    c = lax.broadcasted_iota(I32, (seg, seg), 1)
    before = (r <= c) if d == 0 else (r >= c)
    tri = jnp.where(before, 1.0, 0.0).astype(F32)
    gates_t = gates.T
    bcum_t = jnp.dot(_log_sigmoid(gates_t), tri, precision=HI, preferred_element_type=F32)
    last = seg - 1 if d == 0 else 0
    for h in range(LSTM_HEADS):
        ci, cf = 2 * d * LSTM_HEADS + h, (2 * d + 1) * LSTM_HEADS + h
        s = d * LSTM_HEADS + h
        hd = slice(h * LSTM_DIM, (h + 1) * LSTM_DIM)
        kb, vt = k_all[:, hd], vt_all[hd, :]
        b_row, li_row = bcum_t[cf:cf + 1, :], gates_t[ci:ci + 1, :]
        m_prev = jnp.zeros((1, 1), F32) if first else m_scr[s][:, 0:1]
        if q_all is not None:
            qb = q_all[:, hd]
            cn = cn_scr[s]
            per_key = jnp.broadcast_to(li_row - b_row, (seg, seg)).T
            log_d = jnp.where(before, b_row + per_key, -jnp.inf)
            m_row = jnp.maximum(b_row + m_prev, jnp.max(log_d, axis=0, keepdims=True))
            sm = _dot_t(kb, qb) * jnp.exp(log_d - m_row)
            a = jnp.exp(b_row + m_prev - m_row)
            qc = _dot_t(cn.astype(BF16), qb)
            num = jnp.dot(vt, sm.astype(BF16), preferred_element_type=F32) + a * qc[:LSTM_DIM]
            den = jnp.sum(sm, axis=0, keepdims=True) + a * qc[LSTM_DIM:LSTM_DIM + 1]
            ht_ref[hd, :] = num / jnp.maximum(jnp.abs(den), jnp.exp(-m_row))
        g = b_row[:, last:last + 1]
        w = g - b_row + li_row
        m_new = jnp.maximum(g + m_prev, jnp.max(w, axis=1, keepdims=True))
        wt = jnp.exp(w - m_new)
        aug = jnp.concatenate([(vt.astype(F32) * wt).astype(BF16),
                               jnp.broadcast_to(wt, (BF16_ROWS, seg)).astype(BF16)], axis=0)
        upd = jnp.dot(aug, kb, preferred_element_type=F32)
        cn_scr[s] = upd if first else jnp.exp(g + m_prev - m_new) * cn_scr[s] + upd
        m_scr[s] = jnp.broadcast_to(m_new, (1, LANES))


def _mlstm_kernel(qkf_ref, qkb_ref, vtf_ref, vtb_ref, gf_ref, gb_ref, kc_ref, vtc_ref, gc_ref, gbias_ref,
                  hf_ref, hb_ref, cn_scr, m_scr):
    gbias = gbias_ref[...]

    @pl.when(pl.program_id(1) == 0)
    def _():
        gates = gc_ref[...] + gbias
        for d in range(2):
            _mlstm_segment(d, gates, kc_ref[...], vtc_ref[...], None, cn_scr, m_scr, None, True)

    for d, (qk_ref, vt_ref, g_ref, h_ref) in enumerate(((qkf_ref, vtf_ref, gf_ref, hf_ref),
                                                        (qkb_ref, vtb_ref, gb_ref, hb_ref))):
        qk = qk_ref[...]
        _mlstm_segment(d, g_ref[...] + gbias, qk[:, LSTM_WIDTH:], vt_ref[...], qk[:, :LSTM_WIDTH],
                       cn_scr, m_scr, h_ref, False)


def _mlstm(qk, vt, g, qkc, vtc, gc, gate_bias):
    b, n, _ = qk.shape
    lc = qkc.shape[1]
    nc = n // CHUNK
    qkw = 2 * LSTM_WIDTH
    in_specs = [
        pl.BlockSpec((None, CHUNK, qkw), lambda bi, c: (bi, c, 0)),
        pl.BlockSpec((None, CHUNK, qkw), lambda bi, c: (bi, nc - 1 - c, 0)),
        pl.BlockSpec((None, LSTM_WIDTH, CHUNK), lambda bi, c: (bi, 0, c)),
        pl.BlockSpec((None, LSTM_WIDTH, CHUNK), lambda bi, c: (bi, 0, nc - 1 - c)),
        pl.BlockSpec((None, CHUNK, LANES), lambda bi, c: (bi, c, 0)),
        pl.BlockSpec((None, CHUNK, LANES), lambda bi, c: (bi, nc - 1 - c, 0)),
        pl.BlockSpec((None, lc, LSTM_WIDTH), lambda bi, c: (bi, 0, 1)),
        pl.BlockSpec((None, LSTM_WIDTH, lc), lambda bi, c: (bi, 0, 0)),
        pl.BlockSpec((None, lc, LANES), lambda bi, c: (bi, 0, 0)),
        pl.BlockSpec((1, LANES), lambda bi, c: (0, 0))]
    out_specs = [pl.BlockSpec((None, LSTM_WIDTH, CHUNK), lambda bi, c: (bi, 0, c)),
                 pl.BlockSpec((None, LSTM_WIDTH, CHUNK), lambda bi, c: (bi, 0, nc - 1 - c))]
    return pl.pallas_call(
        _mlstm_kernel,
        out_shape=[jax.ShapeDtypeStruct((b, LSTM_WIDTH, n), F32)] * 2,
        grid=(b, nc), in_specs=in_specs, out_specs=out_specs,
        scratch_shapes=[pltpu.VMEM((2 * LSTM_HEADS, STATE_ROWS, LSTM_DIM), F32),
                        pltpu.VMEM((2 * LSTM_HEADS, 1, LANES), F32)],
        compiler_params=_cparams(("arbitrary", "arbitrary")), name="mlstm",
    )(qk, qk, vt, vt, g, g, qkc, vtc, gc, gate_bias)


def _rope(x, cos, sin_signed):
    w = x.shape[1]
    lane = lax.broadcasted_iota(I32, x.shape, 1)
    first = (lane & (ATTN_DIM - 1)) < (ATTN_DIM // 2)
    partner = jnp.where(first, pltpu.roll(x, w - ATTN_DIM // 2, 1), pltpu.roll(x, ATTN_DIM // 2, 1))
    return x * cos + partner * sin_signed


def _attn_kernel(q_ref, kp_ref, kc_ref, kn_ref, vp_ref, vc_ref, vn_ref, kctx_ref, vctx_ref,
                 tp_ref, tc_ref, tn_ref, sink_ref, bias_ref, o_ref):
    def table(t_ref):
        t = t_ref[...]
        return t[:, :LANES], t[:, LANES:]

    cos_c, sin_c = table(tc_ref)
    q = _rope(q_ref[...].astype(F32), jnp.concatenate([cos_c] * 4, axis=1), jnp.concatenate([sin_c] * 4, axis=1))
    q = q * (ATTN_DIM ** -0.5)
    ks = []
    for k_ref, t_ref in ((kp_ref, tp_ref), (kc_ref, tc_ref), (kn_ref, tn_ref)):
        cos_t, sin_t = table(t_ref)
        ks.append(_rope(k_ref[...].astype(F32), cos_t, sin_t).astype(BF16))
    k_all = jnp.concatenate(ks + [kctx_ref[...]], axis=0)
    v_all = jnp.concatenate([vp_ref[...], vc_ref[...], vn_ref[...], vctx_ref[...]], axis=0)

    bias = bias_ref[...]
    lane = lax.broadcasted_iota(I32, (CHUNK, LANES), 1)
    sink = sink_ref[...]

    pieces = [None] * ATTN_HEADS
    for g in range(ATTN_KV_HEADS):
        half_g = (lane >= ATTN_DIM) if g == 1 else (lane < ATTN_DIM)
        qs = []
        for r in range(ATTN_GROUP):
            h = g * ATTN_GROUP + r
            t = q[:, (h // 2) * LANES:(h // 2 + 1) * LANES]
            if h % 2 != g:
                t = pltpu.roll(t, ATTN_DIM, 1)
            qs.append(jnp.where(half_g, t, 0.0).astype(BF16))
        s = _dot_t(jnp.concatenate(qs, axis=0), k_all)
        ps, dens = [], []
        for r in range(ATTN_GROUP):
            h = g * ATTN_GROUP + r
            sr = s[r * CHUNK:(r + 1) * CHUNK] + bias
            snk = sink[:, h:h + 1]
            m = jnp.maximum(jnp.max(sr, axis=1, keepdims=True), snk)
            e = jnp.exp(sr - m)
            dens.append(jnp.sum(e, axis=1, keepdims=True) + jnp.exp(snk - m))
            ps.append(e.astype(BF16))
        pv = jnp.dot(jnp.concatenate(ps, axis=0), v_all, preferred_element_type=F32)
        for r in range(ATTN_GROUP):
            h = g * ATTN_GROUP + r
            o = pv[r * CHUNK:(r + 1) * CHUNK] / dens[r]
            if h % 2 != g:
                o = pltpu.roll(o, ATTN_DIM, 1)
            pieces[h] = o
    first_half = lane < ATTN_DIM
    for j in range(ATTN_HEADS // 2):
        o_ref[:, j * LANES:(j + 1) * LANES] = jnp.where(first_half, pieces[2 * j], pieces[2 * j + 1]).astype(o_ref.dtype)


def _attn_bias(nctx):
    i = jnp.arange(CHUNK)[:, None]
    j = jnp.arange(3 * CHUNK)[None, :]
    band = (j >= i) & (j <= i + 2 * CHUNK)
    local = jnp.stack([band & (j >= CHUNK), band, band & (j < 2 * CHUNK)])
    return jnp.concatenate([jnp.where(local, 0.0, -jnp.inf).astype(F32), jnp.zeros((3, CHUNK, nctx), F32)], axis=2)


def _attn(p, pc, table, sink):
    b, n, _ = p.shape
    lc = pc.shape[1]
    nb = n // CHUNK
    assert nb >= 2
    qw = ATTN_HEADS * ATTN_DIM
    bias = _attn_bias(lc)

    def blk(col, off):
        return pl.BlockSpec((None, CHUNK, LANES), lambda bi, i: (bi, jnp.clip(i + off, 0, nb - 1), col))

    def tab(off):
        return pl.BlockSpec((CHUNK, 2 * LANES), lambda bi, i: (jnp.clip(i + off, 0, nb - 1), 0))

    in_specs = [pl.BlockSpec((None, CHUNK, qw), lambda bi, i: (bi, i, PB_AQ)),
                blk(PB_AK, -1), blk(PB_AK, 0), blk(PB_AK, 1), blk(PB_AV, -1), blk(PB_AV, 0), blk(PB_AV, 1),
                pl.BlockSpec((None, lc, LANES), lambda bi, i: (bi, 0, PB_AK)),
                pl.BlockSpec((None, lc, LANES), lambda bi, i: (bi, 0, PB_AV)),
                tab(-1), tab(0), tab(1),
                pl.BlockSpec((1, LANES), lambda bi, i: (0, 0)),
                pl.BlockSpec((None,) + bias.shape[1:],
                             lambda bi, i: (jnp.where(i == 0, 0, jnp.where(i == nb - 1, 2, 1)), 0, 0))]
    return pl.pallas_call(
        _attn_kernel,
        out_shape=jax.ShapeDtypeStruct((b, n, qw), BF16),
        grid=(b, nb), in_specs=in_specs,
        out_specs=pl.BlockSpec((None, CHUNK, qw), lambda bi, i: (bi, i, 0)),
        compiler_params=_cparams(("parallel", "parallel")), name="window_attn",
    )(p, p, p, p, p, p, p, pc, pc, table, table, table, sink, bias)


def _router_tail(x_new, g2, sc2, sh2, wr_ref, x_out_ref, h2_ref, aff_ref):
    x_out_ref[...] = x_new
    h2 = _rms_mod(x_new, g2, sc2, sh2)
    h_hi = h2.astype(BF16)
    h2_ref[...] = h_hi
    logits = jnp.dot(_hi_lo_lhs(h2, h_hi), wr_ref[...], preferred_element_type=F32)
    lane = lax.broadcasted_iota(I32, logits.shape, 1)
    logits = jnp.where(lane < N_EXPERTS, logits, -jnp.inf)
    e = jnp.exp(logits - jnp.max(logits, axis=1, keepdims=True))
    aff_ref[...] = e / jnp.sum(e, axis=1, keepdims=True)


def _ab_out_kernel(hf_ref, hb_ref, o_ref, at_ref, x_ref, hg_ref, wo_ref, gt1_ref, g2_ref, sc2_ref, sh2_ref, wr_ref,
                   x_out_ref, h2_ref, aff_ref):
    hsum = (hf_ref[...] + hb_ref[...]).T
    og = jax.nn.sigmoid(o_ref[...].astype(F32))
    hg = hg_ref[...]
    parts = []
    for h in range(LSTM_HEADS):
        sl = slice(h * LSTM_DIM, (h + 1) * LSTM_DIM)
        seg = hsum[:, sl]
        seg = seg * lax.rsqrt(jnp.mean(seg * seg, axis=-1, keepdims=True) + EPS)
        parts.append((seg * hg[:, sl] * og[:, sl]).astype(BF16))
    cat = jnp.concatenate(parts + [at_ref[...]], axis=1)
    y = jnp.dot(cat, wo_ref[...], preferred_element_type=F32)
    _router_tail(x_ref[...] + gt1_ref[...] * y, g2_ref[...], sc2_ref[...], sh2_ref[...], wr_ref,
                 x_out_ref, h2_ref, aff_ref)


def _tail_out(b, n, d, tm):
    shapes = [jax.ShapeDtypeStruct((b, n, d), F32), jax.ShapeDtypeStruct((b, n, d), BF16),
              jax.ShapeDtypeStruct((b, n, LANES), F32)]
    specs = [pl.BlockSpec((None, tm, d), lambda bi, i: (bi, i, 0)),
             pl.BlockSpec((None, tm, d), lambda bi, i: (bi, i, 0)),
             pl.BlockSpec((None, tm, LANES), lambda bi, i: (bi, i, 0))]
    return shapes, specs


def _ab_out(hf, hb, p, at, x, head_g, w_out, gt1, g2, sc2, sh2, wr, tm=256):
    b, n, d = x.shape
    row = lambda w: pl.BlockSpec((None, tm, w), lambda bi, i: (bi, i, 0))
    vec = pl.BlockSpec((None, 1, d), lambda bi, i: (bi, 0, 0))
    const = lambda s: pl.BlockSpec(s, lambda bi, i: (0, 0))
    scan_out = pl.BlockSpec((None, LSTM_WIDTH, tm), lambda bi, i: (bi, 0, i))
    in_specs = [scan_out, scan_out,
                pl.BlockSpec((None, tm, LSTM_WIDTH), lambda bi, i: (bi, i, PB_O)),
                row(ATTN_HEADS * ATTN_DIM), row(d), const((1, LSTM_WIDTH)), const(w_out.shape),
                vec, const((1, d)), vec, vec, const((3 * d, LANES))]
    shapes, specs = _tail_out(b, n, d, tm)
    return pl.pallas_call(
        _ab_out_kernel, out_shape=shapes, grid=(b, n // tm), in_specs=in_specs, out_specs=specs,
        compiler_params=_cparams(("parallel", "parallel")), name="ab_out",
    )(hf, hb, p, at, x, head_g, w_out, gt1, g2, sc2, sh2, wr)


def _gm_out_kernel(uv_ref, x_ref, lng_ref, lnb_ref, ws_ref, bs_ref, wo_ref, gt1_ref, g2_ref, sc2_ref, sh2_ref, wr_ref,
                   x_out_ref, h2_ref, aff_ref):
    tm = uv_ref.shape[0]
    gw = GM_HALF // GM_GROUPS
    v = uv_ref[:, GM_HALF:].astype(F32)
    mu = jnp.mean(v, axis=-1, keepdims=True)
    vc = v - mu
    var = jnp.mean(vc * vc, axis=-1, keepdims=True)
    vn = (vc * lax.rsqrt(var + EPS) * lng_ref[...] + lnb_ref[...]).astype(BF16)
    zs = []
    for ch in range(tm // CHUNK):
        rows = slice(ch * CHUNK, (ch + 1) * CHUNK)
        cols = []
        for g in range(GM_GROUPS):
            sv = jnp.dot(ws_ref[g], vn[rows, g * gw:(g + 1) * gw], preferred_element_type=F32)
            cols.append(sv + bs_ref[:, g:g + 1])
        sv = jnp.concatenate(cols, axis=1)
        zs.append((uv_ref[rows, :GM_HALF].astype(F32) * sv).astype(BF16))
    z = jnp.concatenate(zs, axis=0)
    y = jnp.dot(z, wo_ref[...], preferred_element_type=F32)
    _router_tail(x_ref[...] + gt1_ref[...] * y, g2_ref[...], sc2_ref[...], sh2_ref[...], wr_ref,
                 x_out_ref, h2_ref, aff_ref)


def _gm_out(uv, x, ln_g, ln_b, w_s, b_s_t, w_out, gt1, g2, sc2, sh2, wr, tm=256):
    b, n, d = x.shape
    vec = pl.BlockSpec((None, 1, d), lambda bi, i: (bi, 0, 0))
    const = lambda s: pl.BlockSpec(s, lambda *_: (0,) * len(s))
    in_specs = [pl.BlockSpec((None, tm, 2 * GM_HALF), lambda bi, i: (bi, i, 0)),
                pl.BlockSpec((None, tm, d), lambda bi, i: (bi, i, 0)),
                const((1, GM_HALF)), const((1, GM_HALF)), const(w_s.shape), const(b_s_t.shape), const(w_out.shape),
                vec, const((1, d)), vec, vec, const((3 * d, LANES))]
    shapes, specs = _tail_out(b, n, d, tm)
    return pl.pallas_call(
        _gm_out_kernel, out_shape=shapes, grid=(b, n // tm), in_specs=in_specs, out_specs=specs,
        compiler_params=_cparams(("parallel", "parallel")), name="gm_out",
    )(uv, x, ln_g, ln_b, w_s, b_s_t, w_out, gt1, g2, sc2, sh2, wr)


def _route_kernel(aff_ref, pos_ref, post_ref, offs_ref, afft_ref, *, cap):
    n = aff_ref.shape[0]
    nblk = n // CHUNK

    def to_expert_major(k, _):
        rows = pl.ds(pl.multiple_of(k * CHUNK, CHUNK), CHUNK)
        afft_ref[k] = aff_ref[rows, :].T[:N_EXPERTS, :]
        return 0

    lax.fori_loop(0, nblk, to_expert_major, 0)

    def count(pred):
        per_lane = jnp.sum(jnp.where(pred, 1.0, 0.0), axis=0)
        return jnp.sum(per_lane, axis=1, keepdims=True)

    def search(i, prefix):
        cand = prefix | jnp.left_shift(jnp.int32(1), 30 - i)
        cand_f = lax.bitcast_convert_type(cand, F32)
        return jnp.where(count(afft_ref[...] >= cand_f[None]) >= cap, cand, prefix)

    thr_col = lax.bitcast_convert_type(lax.fori_loop(0, 31, search, jnp.zeros((N_EXPERTS, 1), I32)), F32)
    need_col = cap - count(afft_ref[...] > thr_col[None])

    def to_row(col):
        full = jnp.concatenate([jnp.broadcast_to(col, (N_EXPERTS, LANES)),
                                jnp.zeros((LANES - N_EXPERTS, LANES), F32)], axis=0)
        return full.T[0:1, :]

    thr, need = to_row(thr_col), to_row(need_col)
    tril = jnp.where(_tri(CHUNK, True), 1.0, 0.0).astype(BF16)

    def block(k, carry):
        run_tie, run_sel = carry
        rows = pl.ds(pl.multiple_of(k * CHUNK, CHUNK), CHUNK)
        a = aff_ref[rows, :]
        gt = a > thr
        tie = jnp.where(a == thr, 1.0, 0.0)
        tie_incl = jnp.dot(tril, tie.astype(BF16), preferred_element_type=F32)
        sel = jnp.where(gt | ((tie > 0.0) & (tie_incl - tie + run_tie < need)), 1.0, 0.0)
        sel_incl = jnp.dot(tril, sel.astype(BF16), preferred_element_type=F32)
        pos = jnp.where(sel > 0.0, sel_incl - sel + run_sel, -1.0)
        pos_ref[rows, :] = pos.astype(I32)
        post_ref[k] = pos.T[:N_EXPERTS, :].astype(I32)
        offs_ref[k] = run_sel.astype(I32)
        return (run_tie + tie_incl[CHUNK - 1:CHUNK, :], run_sel + sel_incl[CHUNK - 1:CHUNK, :])

    zero = jnp.zeros((1, LANES), F32)
    lax.fori_loop(0, nblk, block, (zero, zero))


def _route(aff, cap):
    b, n, _ = aff.shape
    nblk = n // CHUNK
    return pl.pallas_call(
        functools.partial(_route_kernel, cap=cap),
        out_shape=[jax.ShapeDtypeStruct((b, n, LANES), I32),
                   jax.ShapeDtypeStruct((b, nblk, N_EXPERTS, CHUNK), I32),
                   jax.ShapeDtypeStruct((b, nblk, 1, LANES), I32),
                   jax.ShapeDtypeStruct((b, nblk, N_EXPERTS, CHUNK), F32)],
        grid=(b,),
        in_specs=[pl.BlockSpec((None, n, LANES), lambda bi: (bi, 0, 0))],
        out_specs=[pl.BlockSpec((None, n, LANES), lambda bi: (bi, 0, 0)),
                   pl.BlockSpec((None, nblk, N_EXPERTS, CHUNK), lambda bi: (bi, 0, 0, 0)),
                   pl.BlockSpec((None, nblk, 1, LANES), lambda bi: (bi, 0, 0, 0)),
                   pl.BlockSpec((None, nblk, N_EXPERTS, CHUNK), lambda bi: (bi, 0, 0, 0))],
        compiler_params=_cparams(("parallel",)), name="route",
    )(aff)


def _window_start(s0, w, cap):
    lo = ((s0 >> 4) << 4) + w * SLOT_CHUNK
    return lo, pl.multiple_of(jnp.minimum(lo, cap - SLOT_CHUNK), BF16_ROWS)


def _num_windows(s0, s1):
    return (s1 - ((s0 >> 4) << 4) + SLOT_CHUNK - 1) >> (SLOT_CHUNK.bit_length() - 1)


def _moe_gather_kernel(cnt_ref, post_ref, afft_ref, h_ref, xe_ref, gate_ref, *, ntile, cap, group, tiles_per_step):
    bi, eg, ts = pl.program_id(0), pl.program_id(1), pl.program_id(2)

    @pl.when(ts == 0)
    def _():
        xe_ref[...] = jnp.zeros_like(xe_ref)
        gate_ref[...] = jnp.zeros_like(gate_ref)

    slot = lax.broadcasted_iota(I32, (SLOT_CHUNK, MOE_TILE), 0)
    for u in range(tiles_per_step):
        htile = h_ref[u * MOE_TILE:(u + 1) * MOE_TILE, :]
        t = ts * tiles_per_step + u
        bases = [(bi * N_EXPERTS + eg * group + g) * (ntile + 1) + t for g in range(group)]

        def windows(w, _, u=u, htile=htile, bases=bases):
            starts, onehots = [], []
            for g in range(group):
                lo, start = _window_start(cnt_ref[bases[g]], w, cap)
                posrow = post_ref[g, u]
                hit = (posrow - start == slot) & (posrow >= lo)
                onehots.append(jnp.where(hit, 1.0, 0.0).astype(BF16))
                gates = jnp.sum(jnp.where(hit, afft_ref[g, u], 0.0), axis=1, keepdims=True)
                dst = pl.ds(start, SLOT_CHUNK)
                gate_ref[g, dst, :] = gate_ref[g, dst, :] + jnp.broadcast_to(gates, (SLOT_CHUNK, LANES))
                starts.append(start)
            rows = jnp.dot(jnp.concatenate(onehots, axis=0), htile, preferred_element_type=F32)
            for g in range(group):
                dst = pl.ds(starts[g], SLOT_CHUNK)
                xe_ref[g, dst, :] = xe_ref[g, dst, :] + rows[g * SLOT_CHUNK:(g + 1) * SLOT_CHUNK].astype(BF16)
            return 0

        windows(0, 0)
        nwin = 1
        for g in range(group):
            nwin = jnp.maximum(nwin, _num_windows(cnt_ref[bases[g]], cnt_ref[bases[g] + 1]))
        lax.fori_loop(1, nwin, windows, 0)


def _moe_gather(cnt, post, afft, h2, cap, group=8, tiles_per_step=4):
    b, n, d = h2.shape
    ntile = n // MOE_TILE
    per_tile = pl.BlockSpec((None, group, tiles_per_step, 1, MOE_TILE), lambda bi, eg, ts, c: (bi, eg, ts, 0, 0))
    grid_spec = pltpu.PrefetchScalarGridSpec(
        num_scalar_prefetch=1, grid=(b, N_EXPERTS // group, ntile // tiles_per_step),
        in_specs=[per_tile, per_tile,
                  pl.BlockSpec((None, tiles_per_step * MOE_TILE, d), lambda bi, eg, ts, c: (bi, ts, 0))],
        out_specs=[pl.BlockSpec((None, group, cap, d), lambda bi, eg, ts, c: (bi, eg, 0, 0)),
                   pl.BlockSpec((None, group, cap, LANES), lambda bi, eg, ts, c: (bi, eg, 0, 0))])
    return pl.pallas_call(
        functools.partial(_moe_gather_kernel, ntile=ntile, cap=cap, group=group, tiles_per_step=tiles_per_step),
        out_shape=[jax.ShapeDtypeStruct((b, N_EXPERTS, cap, d), BF16),
                   jax.ShapeDtypeStruct((b, N_EXPERTS, cap, LANES), F32)],
        grid_spec=grid_spec, compiler_params=_cparams(("arbitrary", "arbitrary", "arbitrary")), name="moe_gather",
    )(cnt, post, afft, h2)


def _moe_ffn_kernel(xe_ref, gate_ref, wg_ref, wu_ref, wd_ref, y_ref, acc_scr, *, hid_tile):
    j = pl.program_id(2)
    xe = xe_ref[...]
    for k in range(wg_ref.shape[1] // hid_tile):
        cols = slice(k * hid_tile, (k + 1) * hid_tile)
        gate = jnp.dot(xe, wg_ref[:, cols].astype(BF16), preferred_element_type=F32)
        up = jnp.dot(xe, wu_ref[:, cols].astype(BF16), preferred_element_type=F32)
        hid = (_silu(gate) * up).astype(BF16)
        part = jnp.dot(hid, wd_ref[cols, :].astype(BF16), preferred_element_type=F32)
        if k == 0:
            @pl.when(j == 0)
            def _():
                acc_scr[...] = part

            @pl.when(j != 0)
            def _():
                acc_scr[...] += part
        else:
            acc_scr[...] += part

    @pl.when(j == pl.num_programs(2) - 1)
    def _():
        y_ref[...] = (acc_scr[...] * gate_ref[:, 0:1]).astype(y_ref.dtype)


def _moe_ffn(xe, gate, wg, wu, wd, layer, hid_split=2):
    b, ne, cap, d = xe.shape
    dh = wg.shape[3] // hid_split
    return pl.pallas_call(
        functools.partial(_moe_ffn_kernel, hid_tile=256),
        out_shape=jax.ShapeDtypeStruct((b, ne, cap, d), BF16),
        grid=(b, ne, hid_split),
        in_specs=[pl.BlockSpec((None, None, cap, d), lambda bi, e, j: (bi, e, 0, 0)),
                  pl.BlockSpec((None, None, cap, LANES), lambda bi, e, j: (bi, e, 0, 0)),
                  pl.BlockSpec((None, None, d, dh), lambda bi, e, j: (layer, e, 0, j)),
                  pl.BlockSpec((None, None, d, dh), lambda bi, e, j: (layer, e, 0, j)),
                  pl.BlockSpec((None, None, dh, d), lambda bi, e, j: (layer, e, j, 0))],
        out_specs=pl.BlockSpec((None, None, cap, d), lambda bi, e, j: (bi, e, 0, 0)),
        scratch_shapes=[pltpu.VMEM((cap, d), F32)],
        compiler_params=_cparams(("parallel", "parallel", "arbitrary")), name="moe_ffn",
    )(xe, gate, wg, wu, wd)


def _moe_combine_kernel(cnt_ref, pos_ref, x_ref, gt2_ref, y_ref, fg_ref, o_ref, acc_scr, ycat_scr, *, ntile, cap, final):
    bi, t = pl.program_id(0), pl.program_id(1)
    lane = lax.broadcasted_iota(I32, (MOE_TILE, SLOT_CHUNK), 1)
    pos = pos_ref[...]

    def windows(w):
        onehots = []
        for e in range(N_EXPERTS):
            lo, start = _window_start(cnt_ref[(bi * N_EXPERTS + e) * (ntile + 1) + t], w, cap)
            pcol = pos[:, e:e + 1]
            onehots.append(jnp.where((pcol - start == lane) & (pcol >= lo), 1.0, 0.0).astype(BF16))
            ycat_scr[e * SLOT_CHUNK:(e + 1) * SLOT_CHUNK, :] = y_ref[e, pl.ds(start, SLOT_CHUNK), :]
        return jnp.dot(jnp.concatenate(onehots, axis=1), ycat_scr[...], preferred_element_type=F32)

    acc_scr[...] = windows(0)
    nwin = 1
    for e in range(N_EXPERTS):
        base = (bi * N_EXPERTS + e) * (ntile + 1) + t
        nwin = jnp.maximum(nwin, _num_windows(cnt_ref[base], cnt_ref[base + 1]))

    def overflow(w, _):
        acc_scr[...] += windows(w)
        return 0

    lax.fori_loop(1, nwin, overflow, 0)
    out = x_ref[...] + gt2_ref[...] * acc_scr[...]
    if final:
        out = out * lax.rsqrt(jnp.mean(out * out, axis=-1, keepdims=True) + EPS) * fg_ref[...]
    o_ref[...] = out


def _moe_combine(cnt, pos, x, gt2, y, final_g, final):
    b, n, d = x.shape
    ntile = n // MOE_TILE
    cap = y.shape[2]
    grid_spec = pltpu.PrefetchScalarGridSpec(
        num_scalar_prefetch=1, grid=(b, ntile),
        in_specs=[pl.BlockSpec((None, MOE_TILE, LANES), lambda bi, t, c: (bi, t, 0)),
                  pl.BlockSpec((None, MOE_TILE, d), lambda bi, t, c: (bi, t, 0)),
                  pl.BlockSpec((None, 1, d), lambda bi, t, c: (bi, 0, 0)),
                  pl.BlockSpec((None, N_EXPERTS, cap, d), lambda bi, t, c: (bi, 0, 0, 0),
                               pipeline_mode=pl.Buffered(1)),
                  pl.BlockSpec((1, d), lambda bi, t, c: (0, 0))],
        out_specs=pl.BlockSpec((None, MOE_TILE, d), lambda bi, t, c: (bi, t, 0)),
        scratch_shapes=[pltpu.VMEM((MOE_TILE, d), F32), pltpu.VMEM((N_EXPERTS * SLOT_CHUNK, d), BF16)])
    return pl.pallas_call(
        functools.partial(_moe_combine_kernel, ntile=ntile, cap=cap, final=final),
        out_shape=jax.ShapeDtypeStruct((b, n, d), F32),
        grid_spec=grid_spec, compiler_params=_cparams(("arbitrary", "arbitrary")), name="moe_combine",
    )(cnt, pos, x, gt2, y, final_g)


def _ec_moe(x_mid, h2, aff, gt2, wg, wu, wd, layer, final_g, final):
    b, n, _ = x_mid.shape
    cap = max(1, EC_FACTOR * n // N_EXPERTS)
    ntile = n // MOE_TILE
    pos, post, offs, afft = _route(aff, cap)
    per_tile = lambda a: a.transpose(0, 2, 1, 3).reshape(b, N_EXPERTS, ntile, 1, MOE_TILE)
    starts = offs[:, ::MOE_TILE // CHUNK, 0, :N_EXPERTS].transpose(0, 2, 1)
    cnt = jnp.concatenate([starts, jnp.full((b, N_EXPERTS, 1), cap, I32)], axis=2).reshape(-1)
    xe, gate = _moe_gather(cnt, per_tile(post), per_tile(afft), h2, cap)
    y = _moe_ffn(xe, gate, wg, wu, wd, layer)
    return _moe_combine(cnt, pos, x_mid, gt2, y, final_g, final)


def _rope_table(n):
    rows = n // GRID_W
    row = jnp.repeat(jnp.arange(rows), GRID_W).astype(F32)
    col = jnp.tile(jnp.arange(GRID_W), rows).astype(F32)
    nf = ATTN_DIM // 4
    inv = ROPE_BASE ** (-jnp.arange(nf, dtype=F32) / nf)
    ang = jnp.concatenate([row[:, None] * inv, col[:, None] * inv], axis=-1)
    cos, sin = jnp.cos(ang), jnp.sin(ang)
    reps = LANES // ATTN_DIM
    return jnp.concatenate([jnp.tile(jnp.concatenate([cos, cos], -1), (1, reps)),
                            jnp.tile(jnp.concatenate([-sin, sin], -1), (1, reps))], axis=-1)


def _pad_lanes(a):
    return jnp.pad(a, ((0, 0), (0, LANES - a.shape[1])))


def kernel(x, c, ctx, c_ctx, w_mod, b_mod, norm_mix_g, norm_ffn_g, final_norm_g, ab_w_in, ab_conv_w, ab_gate_b,
           ab_head_g, ab_sink, ab_w_out, gm_w_in, gm_ln_g, gm_ln_b, gm_w_s, gm_b_s, gm_w_out, moe_w_router,
           moe_w_gate, moe_w_up, moe_w_down):
    b, n, d = x.shape
    depth = w_mod.shape[0]
    assert depth <= 2, "context stream is only advanced for deeper stacks; not supported here"
    cond = jnp.zeros((SUBLANES, d), F32).at[:b].set(c).at[b].set(c_ctx)
    mods = _adaln(cond, w_mod, b_mod)

    def mod_rows(layer, rows):
        m = mods[layer, rows].reshape(-1, 6, 1, d)
        return [m[:, i] for i in range(6)]

    row = lambda v: v.reshape(1, -1)
    for layer in range(depth):
        sh1, sc1, gt1, sh2, sc2, gt2 = mod_rows(layer, slice(0, b))
        g1, g2 = row(norm_mix_g[layer]), row(norm_ffn_g[layer])
        wr = _hi_lo_rhs(_pad_lanes(moe_w_router[layer]))
        if layer % 2 == 0:
            e = layer // 2
            csh1, csc1 = (jnp.broadcast_to(v, (b, 1, d)) for v in mod_rows(layer, slice(b, b + 1))[:2])
            w_in = ab_w_in[e]
            g_lo = 4 * LSTM_WIDTH
            w_main = jnp.concatenate([w_in[:, :g_lo], w_in[:, g_lo + N_GATES:]], axis=1).astype(BF16)
            w_gate = _hi_lo_rhs(_pad_lanes(w_in[:, g_lo:g_lo + N_GATES]))
            p, gts, vt = _modmm(x, g1, sc1, sh1, w_main, w_gate, name="ab_in")
            pc, gtc, vtc = _modmm(ctx, g1, csc1, csh1, w_main, w_gate, name="ab_in_ctx")
            hf, hb = _mlstm(_conv_silu(p, ab_conv_w[e]), vt, gts, _conv_silu(pc, ab_conv_w[e]), vtc, gtc,
                            _pad_lanes(row(ab_gate_b[e])))
            at = _attn(p, pc, _rope_table(n), _pad_lanes(row(ab_sink[e])))
            x_mid, h2, aff = _ab_out(hf, hb, p, at, x, row(ab_head_g[e]), ab_w_out[e].astype(BF16),
                                     gt1, g2, sc2, sh2, wr)
        else:
            o = layer // 2
            uv = _modmm(x, g1, sc1, sh1, gm_w_in[o].astype(BF16), act="gelu", name="gm_in")
            x_mid, h2, aff = _gm_out(uv, x, row(gm_ln_g[o]), row(gm_ln_b[o]), gm_w_s[o].astype(BF16),
                                     _pad_lanes(gm_b_s[o].T), gm_w_out[o].astype(BF16), gt1, g2, sc2, sh2, wr)
        x = _ec_moe(x_mid, h2, aff, gt2, moe_w_gate, moe_w_up, moe_w_down, layer,
                    row(final_norm_g), layer == depth - 1)
    return x
```

```python
import functools

import jax
import jax.numpy as jnp
from jax import lax
from jax.experimental import pallas as pl
from jax.experimental.pallas import tpu as pltpu

F32 = jnp.float32
BF16 = jnp.bfloat16
I32 = jnp.int32
HI = lax.Precision.HIGHEST

D_MODEL = 1024
GRID_W = 64
EPS = 1e-6
LSTM_HEADS = 4
LSTM_DIM = 128
LSTM_WIDTH = LSTM_HEADS * LSTM_DIM
LSTM_CONV = 5
CHUNK = 128
ATTN_HEADS = 8
ATTN_KV_HEADS = 2
ATTN_GROUP = ATTN_HEADS // ATTN_KV_HEADS
ATTN_DIM = 64
ROPE_BASE = 10000.0
GM_GROUPS = 8
GM_HALF = 2 * D_MODEL
N_EXPERTS = 16
EC_FACTOR = 2
N_GATES = 4 * LSTM_HEADS

LANES = 128
SUBLANES = 8
BF16_ROWS = 16
VMEM_LIMIT_BYTES = 56 * 1024 * 1024

P_COLS = 4 * LSTM_WIDTH + ATTN_HEADS * ATTN_DIM + 2 * ATTN_KV_HEADS * ATTN_DIM
PB_V, PB_O, PB_AQ = 2, 3, 4
PB_AK, PB_AV = 20, 21
MOE_TILE = 256
SLOT_CHUNK = 64


def _cparams(sem, vmem=VMEM_LIMIT_BYTES):
    return pltpu.CompilerParams(dimension_semantics=sem, vmem_limit_bytes=vmem)


def _rms_mod(x, g, sc, sh):
    y = x * lax.rsqrt(jnp.mean(x * x, axis=-1, keepdims=True) + EPS)
    return y * g * (1.0 + sc) + sh


def _silu(x):
    return x * jax.nn.sigmoid(x)


def _gelu_tanh(x):
    return 0.5 * x * (1.0 + jnp.tanh(0.7978845608028654 * (x + 0.044715 * (x * x * x))))


def _log_sigmoid(x):
    return jnp.minimum(x, 0.0) - jnp.log(1.0 + jnp.exp(-jnp.abs(x)))


def _dot_t(a, b):
    return lax.dot_general(a, b, (((1,), (1,)), ((), ())), preferred_element_type=F32)


def _adaln_kernel(c_ref, w_ref, b_ref, o_ref):
    s = _silu(c_ref[...])
    o_ref[...] = jnp.dot(s, w_ref[...], precision=HI, preferred_element_type=F32) + b_ref[...]


def _adaln(cond, w_mod, b_mod):
    depth, d, six_d = w_mod.shape
    tn = six_d // 4
    return pl.pallas_call(
        _adaln_kernel,
        out_shape=jax.ShapeDtypeStruct((depth, SUBLANES, six_d), F32),
        grid=(depth, six_d // tn),
        in_specs=[pl.BlockSpec((SUBLANES, d), lambda l, j: (0, 0)),
                  pl.BlockSpec((None, d, tn), lambda l, j: (l, 0, j)),
                  pl.BlockSpec((None, 1, tn), lambda l, j: (l, 0, j))],
        out_specs=pl.BlockSpec((None, SUBLANES, tn), lambda l, j: (l, 0, j)),
        compiler_params=_cparams(("arbitrary", "arbitrary")),
        name="adaln",
    )(cond, w_mod, b_mod.reshape(depth, 1, six_d))


def _hi_lo_lhs(h, h_hi):
    return jnp.concatenate([h_hi, (h - h_hi.astype(F32)).astype(BF16), h_hi], axis=1)


def _hi_lo_rhs(w):
    w_hi = w.astype(BF16)
    return jnp.concatenate([w_hi, w_hi, (w - w_hi.astype(F32)).astype(BF16)], axis=0)


def _modmm_kernel(x_ref, g_ref, sc_ref, sh_ref, w_ref, *rest, chunks, act, with_gates):
    if with_gates:
        wg_ref, o_ref, og_ref, vt_ref, avt_ref = rest
    else:
        (o_ref,) = rest
    h = _rms_mod(x_ref[...], g_ref[...], sc_ref[...], sh_ref[...])
    hb = h.astype(BF16)
    for lo, hi in chunks:
        y = jnp.dot(hb, w_ref[:, lo:hi], preferred_element_type=F32)
        if act == "gelu":
            y = _gelu_tanh(y)
        o_ref[:, lo:hi] = y.astype(o_ref.dtype)
        if with_gates and lo == PB_V * LSTM_WIDTH:
            vt_ref[...] = y.T.astype(vt_ref.dtype)
        if with_gates and lo <= PB_AV * LANES < hi:
            av = y[:, PB_AV * LANES - lo:(PB_AV + 1) * LANES - lo]
            avt_ref[...] = av.T.astype(avt_ref.dtype)
    if with_gates:
        og_ref[...] = jnp.dot(_hi_lo_lhs(h, hb), wg_ref[...], preferred_element_type=F32)


def _modmm(x, g, sc, sh, w, wg=None, *, act=None, tm=512, chunk=512, name="modmm"):
    b, n, d = x.shape
    no = w.shape[1]
    tm = min(tm, n)
    assert chunk == LSTM_WIDTH
    chunks = tuple((lo, min(lo + chunk, no)) for lo in range(0, no, chunk))
    in_specs = [pl.BlockSpec((None, tm, d), lambda bi, i: (bi, i, 0)),
                pl.BlockSpec((1, d), lambda bi, i: (0, 0)),
                pl.BlockSpec((None, 1, d), lambda bi, i: (bi, 0, 0)),
                pl.BlockSpec((None, 1, d), lambda bi, i: (bi, 0, 0)),
                pl.BlockSpec((d, no), lambda bi, i: (0, 0))]
    out_shape = [jax.ShapeDtypeStruct((b, n, no), BF16)]
    out_specs = [pl.BlockSpec((None, tm, no), lambda bi, i: (bi, i, 0))]
    args = [x, g, sc, sh, w]
    if wg is not None:
        in_specs.append(pl.BlockSpec(wg.shape, lambda bi, i: (0, 0)))
        out_shape += [jax.ShapeDtypeStruct((b, n, LANES), F32), jax.ShapeDtypeStruct((b, LSTM_WIDTH, n), BF16),
                      jax.ShapeDtypeStruct((b, LANES, n), BF16)]
        out_specs += [pl.BlockSpec((None, tm, LANES), lambda bi, i: (bi, i, 0)),
                      pl.BlockSpec((None, LSTM_WIDTH, tm), lambda bi, i: (bi, 0, i)),
                      pl.BlockSpec((None, LANES, tm), lambda bi, i: (bi, 0, i))]
        args.append(wg)
    res = pl.pallas_call(
        functools.partial(_modmm_kernel, chunks=chunks, act=act, with_gates=wg is not None),
        out_shape=out_shape, grid=(b, n // tm), in_specs=in_specs, out_specs=out_specs,
        compiler_params=_cparams(("parallel", "parallel")), name=name,
    )(*args)
    return res if wg is not None else res[0]


def _conv_silu_kernel(x_ref, xp_ref, xn_ref, w_ref, o_ref, pad_scr):
    i = pl.program_id(1)
    rows = x_ref.shape[0]
    has_prev = jnp.where(i > 0, 1.0, 0.0)
    has_next = jnp.where(i < pl.num_programs(1) - 1, 1.0, 0.0)
    pad_scr[pl.ds(0, SUBLANES), :] = xp_ref[...].astype(F32)[SUBLANES:, :] * has_prev
    pad_scr[pl.ds(SUBLANES, rows), :] = x_ref[...].astype(F32)
    pad_scr[pl.ds(SUBLANES + rows, SUBLANES), :] = xn_ref[...].astype(F32)[:SUBLANES, :] * has_next
    w = w_ref[...]
    acc = None
    for t in range(LSTM_CONV):
        term = pad_scr[pl.ds(SUBLANES - LSTM_CONV // 2 + t, rows), :] * w[t:t + 1, :]
        acc = term if acc is None else acc + term
    y = _silu(acc)
    o_ref[:, :LSTM_WIDTH] = y[:, :LSTM_WIDTH].astype(o_ref.dtype)
    o_ref[:, LSTM_WIDTH:] = (y[:, LSTM_WIDTH:] * (LSTM_DIM ** -0.5)).astype(o_ref.dtype)


def _conv_silu(p, conv_w, tm=512):
    b, n, _ = p.shape
    tm = min(tm, n)
    qkw = 2 * LSTM_WIDTH
    hpt = tm // BF16_ROWS
    nhb = n // BF16_ROWS
    return pl.pallas_call(
        _conv_silu_kernel,
        out_shape=jax.ShapeDtypeStruct((b, n, qkw), BF16),
        grid=(b, n // tm),
        in_specs=[pl.BlockSpec((None, tm, qkw), lambda bi, i: (bi, i, 0)),
                  pl.BlockSpec((None, BF16_ROWS, qkw), lambda bi, i: (bi, jnp.maximum(i * hpt - 1, 0), 0)),
                  pl.BlockSpec((None, BF16_ROWS, qkw), lambda bi, i: (bi, jnp.minimum((i + 1) * hpt, nhb - 1), 0)),
                  pl.BlockSpec((LSTM_CONV, qkw), lambda bi, i: (0, 0))],
        out_specs=pl.BlockSpec((None, tm, qkw), lambda bi, i: (bi, i, 0)),
        scratch_shapes=[pltpu.VMEM((tm + 2 * SUBLANES, qkw), F32)],
        compiler_params=_cparams(("parallel", "parallel")), name="conv_silu",
    )(p, p, p, conv_w)


def _tri(n, lower):
    r = lax.broadcasted_iota(I32, (n, n), 0)
    c = lax.broadcasted_iota(I32, (n, n), 1)
    return (c <= r) if lower else (c >= r)


STATE_ROWS = LSTM_DIM + BF16_ROWS


def _mlstm_segment(d, gates, k_all, vt_all, q_all, cn_scr, m_scr, ht_ref, first):
    seg = gates.shape[0]
    r = lax.broadcasted_iota(I32, (seg, seg), 0)
    c = lax.broadcasted_iota(I32, (seg, seg), 1)
    before = (r <= c) if d == 0 else (r >= c)
    tri = jnp.where(before, 1.0, 0.0).astype(F32)
    gates_t = gates.T
    bcum_t = jnp.dot(_log_sigmoid(gates_t), tri, precision=HI, preferred_element_type=F32)
    last = seg - 1 if d == 0 else 0
    for h in range(LSTM_HEADS):
        ci, cf = 2 * d * LSTM_HEADS + h, (2 * d + 1) * LSTM_HEADS + h
        s = d * LSTM_HEADS + h
        hd = slice(h * LSTM_DIM, (h + 1) * LSTM_DIM)
        kb, vt = k_all[:, hd], vt_all[hd, :]
        b_row, li_row = bcum_t[cf:cf + 1, :], gates_t[ci:ci + 1, :]
        m_prev = jnp.zeros((1, 1), F32) if first else m_scr[s][:, 0:1]
        if q_all is not None:
            qb = q_all[:, hd]
            cn = cn_scr[s]
            per_key = jnp.broadcast_to(li_row - b_row, (seg, seg)).T
            log_d = jnp.where(before, b_row + per_key, -jnp.inf)
            m_row = jnp.maximum(b_row + m_prev, jnp.max(log_d, axis=0, keepdims=True))
            sm = _dot_t(kb, qb) * jnp.exp(log_d - m_row)
            a = jnp.exp(b_row + m_prev - m_row)
            qc = _dot_t(cn.astype(BF16), qb)
            num = jnp.dot(vt, sm.astype(BF16), preferred_element_type=F32) + a * qc[:LSTM_DIM]
            den = jnp.sum(sm, axis=0, keepdims=True) + a * qc[LSTM_DIM:LSTM_DIM + 1]
            ht_ref[hd, :] = num / jnp.maximum(jnp.abs(den), jnp.exp(-m_row))
        g = b_row[:, last:last + 1]
        w = g - b_row + li_row
        m_new = jnp.maximum(g + m_prev, jnp.max(w, axis=1, keepdims=True))
        wt = jnp.exp(w - m_new)
        aug = jnp.concatenate([(vt.astype(F32) * wt).astype(BF16),
                               jnp.broadcast_to(wt, (BF16_ROWS, seg)).astype(BF16)], axis=0)
        upd = jnp.dot(aug, kb, preferred_element_type=F32)
        cn_scr[s] = upd if first else jnp.exp(g + m_prev - m_new) * cn_scr[s] + upd
        m_scr[s] = jnp.broadcast_to(m_new, (1, LANES))


def _mlstm_kernel(qkf_ref, qkb_ref, vtf_ref, vtb_ref, gf_ref, gb_ref, kc_ref, vtc_ref, gc_ref, gbias_ref,
                  hf_ref, hb_ref, cn_scr, m_scr):
    gbias = gbias_ref[...]

    @pl.when(pl.program_id(1) == 0)
    def _():
        gates = gc_ref[...] + gbias
        for d in range(2):
            _mlstm_segment(d, gates, kc_ref[...], vtc_ref[...], None, cn_scr, m_scr, None, True)

    for d, (qk_ref, vt_ref, g_ref, h_ref) in enumerate(((qkf_ref, vtf_ref, gf_ref, hf_ref),
                                                        (qkb_ref, vtb_ref, gb_ref, hb_ref))):
        qk = qk_ref[...]
        _mlstm_segment(d, g_ref[...] + gbias, qk[:, LSTM_WIDTH:], vt_ref[...], qk[:, :LSTM_WIDTH],
                       cn_scr, m_scr, h_ref, False)


def _mlstm(qk, vt, g, qkc, vtc, gc, gate_bias):
    b, n, _ = qk.shape
    lc = qkc.shape[1]
    nc = n // CHUNK
    qkw = 2 * LSTM_WIDTH
    in_specs = [
        pl.BlockSpec((None, CHUNK, qkw), lambda bi, c: (bi, c, 0)),
        pl.BlockSpec((None, CHUNK, qkw), lambda bi, c: (bi, nc - 1 - c, 0)),
        pl.BlockSpec((None, LSTM_WIDTH, CHUNK), lambda bi, c: (bi, 0, c)),
        pl.BlockSpec((None, LSTM_WIDTH, CHUNK), lambda bi, c: (bi, 0, nc - 1 - c)),
        pl.BlockSpec((None, CHUNK, LANES), lambda bi, c: (bi, c, 0)),
        pl.BlockSpec((None, CHUNK, LANES), lambda bi, c: (bi, nc - 1 - c, 0)),
        pl.BlockSpec((None, lc, LSTM_WIDTH), lambda bi, c: (bi, 0, 1)),
        pl.BlockSpec((None, LSTM_WIDTH, lc), lambda bi, c: (bi, 0, 0)),
        pl.BlockSpec((None, lc, LANES), lambda bi, c: (bi, 0, 0)),
        pl.BlockSpec((1, LANES), lambda bi, c: (0, 0))]
    out_specs = [pl.BlockSpec((None, LSTM_WIDTH, CHUNK), lambda bi, c: (bi, 0, c)),
                 pl.BlockSpec((None, LSTM_WIDTH, CHUNK), lambda bi, c: (bi, 0, nc - 1 - c))]
    return pl.pallas_call(
        _mlstm_kernel,
        out_shape=[jax.ShapeDtypeStruct((b, LSTM_WIDTH, n), F32)] * 2,
        grid=(b, nc), in_specs=in_specs, out_specs=out_specs,
        scratch_shapes=[pltpu.VMEM((2 * LSTM_HEADS, STATE_ROWS, LSTM_DIM), F32),
                        pltpu.VMEM((2 * LSTM_HEADS, 1, LANES), F32)],
        compiler_params=_cparams(("arbitrary", "arbitrary")), name="mlstm",
    )(qk, qk, vt, vt, g, g, qkc, vtc, gc, gate_bias)


def _rope(x, cos, sin_signed):
    w = x.shape[1]
    lane = lax.broadcasted_iota(I32, x.shape, 1)
    first = (lane & (ATTN_DIM - 1)) < (ATTN_DIM // 2)
    partner = jnp.where(first, pltpu.roll(x, w - ATTN_DIM // 2, 1), pltpu.roll(x, ATTN_DIM // 2, 1))
    return x * cos + partner * sin_signed


def _attn_kernel(q_ref, kp_ref, kc_ref, kn_ref, vtp_ref, vtc_ref, vtn_ref, kctx_ref, vtctx_ref,
                 tp_ref, tc_ref, tn_ref, sink_ref, bias_ref, o_ref):
    def table(t_ref):
        t = t_ref[...]
        return t[:, :LANES], t[:, LANES:]

    cos_c, sin_c = table(tc_ref)
    q = _rope(q_ref[...].astype(F32), jnp.concatenate([cos_c] * 4, axis=1), jnp.concatenate([sin_c] * 4, axis=1))
    q = q * (ATTN_DIM ** -0.5)
    ks = []
    for k_ref, t_ref in ((kp_ref, tp_ref), (kc_ref, tc_ref), (kn_ref, tn_ref)):
        cos_t, sin_t = table(t_ref)
        ks.append(_rope(k_ref[...].astype(F32), cos_t, sin_t).astype(BF16))
    k_all = jnp.concatenate(ks + [kctx_ref[...]], axis=0)
    vt_all = jnp.concatenate([vtp_ref[...], vtc_ref[...], vtn_ref[...], vtctx_ref[...]], axis=1)

    bias = jnp.concatenate([bias_ref[...]] * ATTN_GROUP, axis=1)
    lane = lax.broadcasted_iota(I32, (CHUNK, LANES), 1)
    sink = sink_ref[...]

    for g in range(ATTN_KV_HEADS):
        half_g = (lane >= ATTN_DIM) if g == 1 else (lane < ATTN_DIM)
        qs, snk = [], []
        for r in range(ATTN_GROUP):
            h = g * ATTN_GROUP + r
            t = q[:, (h // 2) * LANES:(h // 2 + 1) * LANES]
            if h % 2 != g:
                t = pltpu.roll(t, ATTN_DIM, 1)
            qs.append(jnp.where(half_g, t, 0.0).astype(BF16))
            snk.append(jnp.broadcast_to(sink[:, h:h + 1], (1, CHUNK)))
        snk = jnp.concatenate(snk, axis=1)
        st = _dot_t(k_all, jnp.concatenate(qs, axis=0)) + bias
        m = jnp.maximum(jnp.max(st, axis=0, keepdims=True), snk)
        e = jnp.exp(st - m)
        den = jnp.sum(e, axis=0, keepdims=True) + jnp.exp(snk - m)
        pv = jnp.dot(vt_all, e.astype(BF16), preferred_element_type=F32)
        o = (pv[g * ATTN_DIM:(g + 1) * ATTN_DIM] / den).astype(o_ref.dtype)
        for r in range(ATTN_GROUP):
            h = g * ATTN_GROUP + r
            o_ref[h * ATTN_DIM:(h + 1) * ATTN_DIM, :] = o[:, r * CHUNK:(r + 1) * CHUNK]


def _attn_bias(nctx):
    i = jnp.arange(CHUNK)[None, :]
    j = jnp.arange(3 * CHUNK)[:, None]
    band = (j >= i) & (j <= i + 2 * CHUNK)
    local = jnp.stack([band & (j >= CHUNK), band, band & (j < 2 * CHUNK)])
    return jnp.concatenate([jnp.where(local, 0.0, -jnp.inf).astype(F32), jnp.zeros((3, nctx, CHUNK), F32)], axis=1)


def _attn(p, avt, pc, avtc, table, sink):
    b, n, _ = p.shape
    lc = pc.shape[1]
    nb = n // CHUNK
    assert nb >= 2
    qw = ATTN_HEADS * ATTN_DIM
    bias = _attn_bias(lc)

    def blk(col, off):
        return pl.BlockSpec((None, CHUNK, LANES), lambda bi, i: (bi, jnp.clip(i + off, 0, nb - 1), col))

    def vblk(off):
        return pl.BlockSpec((None, LANES, CHUNK), lambda bi, i: (bi, 0, jnp.clip(i + off, 0, nb - 1)))

    def tab(off):
        return pl.BlockSpec((CHUNK, 2 * LANES), lambda bi, i: (jnp.clip(i + off, 0, nb - 1), 0))

    in_specs = [pl.BlockSpec((None, CHUNK, qw), lambda bi, i: (bi, i, PB_AQ)),
                blk(PB_AK, -1), blk(PB_AK, 0), blk(PB_AK, 1), vblk(-1), vblk(0), vblk(1),
                pl.BlockSpec((None, lc, LANES), lambda bi, i: (bi, 0, PB_AK)),
                pl.BlockSpec((None, LANES, lc), lambda bi, i: (bi, 0, 0)),
                tab(-1), tab(0), tab(1),
                pl.BlockSpec((1, LANES), lambda bi, i: (0, 0)),
                pl.BlockSpec((None,) + bias.shape[1:],
                             lambda bi, i: (jnp.where(i == 0, 0, jnp.where(i == nb - 1, 2, 1)), 0, 0))]
    return pl.pallas_call(
        _attn_kernel,
        out_shape=jax.ShapeDtypeStruct((b, qw, n), BF16),
        grid=(b, nb), in_specs=in_specs,
        out_specs=pl.BlockSpec((None, qw, CHUNK), lambda bi, i: (bi, 0, i)),
        compiler_params=_cparams(("parallel", "parallel")), name="window_attn",
    )(p, p, p, p, avt, avt, avt, pc, avtc, table, table, table, sink, bias)


def _router_tail(x_new, g2, sc2, sh2, wr_ref, x_out_ref, h2_ref, aff_ref):
    x_out_ref[...] = x_new
    h2 = _rms_mod(x_new, g2, sc2, sh2)
    h_hi = h2.astype(BF16)
    h2_ref[...] = h_hi
    logits = jnp.dot(_hi_lo_lhs(h2, h_hi), wr_ref[...], preferred_element_type=F32)
    lane = lax.broadcasted_iota(I32, logits.shape, 1)
    logits = jnp.where(lane < N_EXPERTS, logits, -jnp.inf)
    e = jnp.exp(logits - jnp.max(logits, axis=1, keepdims=True))
    aff_ref[...] = e / jnp.sum(e, axis=1, keepdims=True)


def _ab_out_kernel(hf_ref, hb_ref, o_ref, at_ref, x_ref, hg_ref, wo_ref, gt1_ref, g2_ref, sc2_ref, sh2_ref, wr_ref,
                   x_out_ref, h2_ref, aff_ref):
    hsum = (hf_ref[...] + hb_ref[...]).T
    og = jax.nn.sigmoid(o_ref[...].astype(F32))
    hg = hg_ref[...]
    parts = []
    for h in range(LSTM_HEADS):
        sl = slice(h * LSTM_DIM, (h + 1) * LSTM_DIM)
        seg = hsum[:, sl]
        seg = seg * lax.rsqrt(jnp.mean(seg * seg, axis=-1, keepdims=True) + EPS)
        parts.append((seg * hg[:, sl] * og[:, sl]).astype(BF16))
    cat = jnp.concatenate(parts + [at_ref[...].astype(F32).T.astype(BF16)], axis=1)
    y = jnp.dot(cat, wo_ref[...], preferred_element_type=F32)
    _router_tail(x_ref[...] + gt1_ref[...] * y, g2_ref[...], sc2_ref[...], sh2_ref[...], wr_ref,
                 x_out_ref, h2_ref, aff_ref)


def _tail_out(b, n, d, tm):
    shapes = [jax.ShapeDtypeStruct((b, n, d), F32), jax.ShapeDtypeStruct((b, n, d), BF16),
              jax.ShapeDtypeStruct((b, n, LANES), F32)]
    specs = [pl.BlockSpec((None, tm, d), lambda bi, i: (bi, i, 0)),
             pl.BlockSpec((None, tm, d), lambda bi, i: (bi, i, 0)),
             pl.BlockSpec((None, tm, LANES), lambda bi, i: (bi, i, 0))]
    return shapes, specs


def _ab_out(hf, hb, p, at, x, head_g, w_out, gt1, g2, sc2, sh2, wr, tm=256):
    b, n, d = x.shape
    row = lambda w: pl.BlockSpec((None, tm, w), lambda bi, i: (bi, i, 0))
    vec = pl.BlockSpec((None, 1, d), lambda bi, i: (bi, 0, 0))
    const = lambda s: pl.BlockSpec(s, lambda bi, i: (0, 0))
    scan_out = pl.BlockSpec((None, LSTM_WIDTH, tm), lambda bi, i: (bi, 0, i))
    in_specs = [scan_out, scan_out,
                pl.BlockSpec((None, tm, LSTM_WIDTH), lambda bi, i: (bi, i, PB_O)),
                pl.BlockSpec((None, ATTN_HEADS * ATTN_DIM, tm), lambda bi, i: (bi, 0, i)),
                row(d), const((1, LSTM_WIDTH)), const(w_out.shape),
                vec, const((1, d)), vec, vec, const((3 * d, LANES))]
    shapes, specs = _tail_out(b, n, d, tm)
    return pl.pallas_call(
        _ab_out_kernel, out_shape=shapes, grid=(b, n // tm), in_specs=in_specs, out_specs=specs,
        compiler_params=_cparams(("parallel", "parallel")), name="ab_out",
    )(hf, hb, p, at, x, head_g, w_out, gt1, g2, sc2, sh2, wr)


def _gm_out_kernel(uv_ref, x_ref, lng_ref, lnb_ref, ws_ref, bs_ref, wo_ref, gt1_ref, g2_ref, sc2_ref, sh2_ref, wr_ref,
                   x_out_ref, h2_ref, aff_ref):
    tm = uv_ref.shape[0]
    gw = GM_HALF // GM_GROUPS
    v = uv_ref[:, GM_HALF:].astype(F32)
    mu = jnp.mean(v, axis=-1, keepdims=True)
    vc = v - mu
    var = jnp.mean(vc * vc, axis=-1, keepdims=True)
    vn = (vc * lax.rsqrt(var + EPS) * lng_ref[...] + lnb_ref[...]).astype(BF16)
    zs = []
    for ch in range(tm // CHUNK):
        rows = slice(ch * CHUNK, (ch + 1) * CHUNK)
        cols = []
        for g in range(GM_GROUPS):
            sv = jnp.dot(ws_ref[g], vn[rows, g * gw:(g + 1) * gw], preferred_element_type=F32)
            cols.append(sv + bs_ref[:, g:g + 1])
        sv = jnp.concatenate(cols, axis=1)
        zs.append((uv_ref[rows, :GM_HALF].astype(F32) * sv).astype(BF16))
    z = jnp.concatenate(zs, axis=0)
    y = jnp.dot(z, wo_ref[...], preferred_element_type=F32)
    _router_tail(x_ref[...] + gt1_ref[...] * y, g2_ref[...], sc2_ref[...], sh2_ref[...], wr_ref,
                 x_out_ref, h2_ref, aff_ref)


def _gm_out(uv, x, ln_g, ln_b, w_s, b_s_t, w_out, gt1, g2, sc2, sh2, wr, tm=256):
    b, n, d = x.shape
    vec = pl.BlockSpec((None, 1, d), lambda bi, i: (bi, 0, 0))
    const = lambda s: pl.BlockSpec(s, lambda *_: (0,) * len(s))
    in_specs = [pl.BlockSpec((None, tm, 2 * GM_HALF), lambda bi, i: (bi, i, 0)),
                pl.BlockSpec((None, tm, d), lambda bi, i: (bi, i, 0)),
                const((1, GM_HALF)), const((1, GM_HALF)), const(w_s.shape), const(b_s_t.shape), const(w_out.shape),
                vec, const((1, d)), vec, vec, const((3 * d, LANES))]
    shapes, specs = _tail_out(b, n, d, tm)
    return pl.pallas_call(
        _gm_out_kernel, out_shape=shapes, grid=(b, n // tm), in_specs=in_specs, out_specs=specs,
        compiler_params=_cparams(("parallel", "parallel")), name="gm_out",
    )(uv, x, ln_g, ln_b, w_s, b_s_t, w_out, gt1, g2, sc2, sh2, wr)


def _route_kernel(aff_ref, pos_ref, post_ref, offs_ref, afft_ref, *, cap):
    n = aff_ref.shape[0]
    nblk = n // CHUNK

    def to_expert_major(k, _):
        rows = pl.ds(pl.multiple_of(k * CHUNK, CHUNK), CHUNK)
        afft_ref[k] = aff_ref[rows, :].T[:N_EXPERTS, :]
        return 0

    lax.fori_loop(0, nblk, to_expert_major, 0, unroll=4)

    def count(pred):
        per_lane = jnp.sum(jnp.where(pred, 1.0, 0.0), axis=0)
        return jnp.sum(per_lane, axis=1, keepdims=True)

    def search(i, prefix):
        cand = prefix | jnp.left_shift(jnp.int32(1), 30 - i)
        cand_f = lax.bitcast_convert_type(cand, F32)
        return jnp.where(count(afft_ref[...] >= cand_f[None]) >= cap, cand, prefix)

    thr_col = lax.bitcast_convert_type(lax.fori_loop(0, 31, search, jnp.zeros((N_EXPERTS, 1), I32)), F32)
    need_col = cap - count(afft_ref[...] > thr_col[None])

    def to_row(col):
        full = jnp.concatenate([jnp.broadcast_to(col, (N_EXPERTS, LANES)),
                                jnp.zeros((LANES - N_EXPERTS, LANES), F32)], axis=0)
        return full.T[0:1, :]

    thr, need = to_row(thr_col), to_row(need_col)
    tril = jnp.where(_tri(CHUNK, True), 1.0, 0.0).astype(BF16)

    def block(k, carry):
        run_tie, run_sel = carry
        rows = pl.ds(pl.multiple_of(k * CHUNK, CHUNK), CHUNK)
        a = aff_ref[rows, :]
        gt = a > thr
        tie = jnp.where(a == thr, 1.0, 0.0)
        tie_incl = jnp.dot(tril, tie.astype(BF16), preferred_element_type=F32)
        sel = jnp.where(gt | ((tie > 0.0) & (tie_incl - tie + run_tie < need)), 1.0, 0.0)
        sel_incl = jnp.dot(tril, sel.astype(BF16), preferred_element_type=F32)
        pos = jnp.where(sel > 0.0, sel_incl - sel + run_sel, -1.0)
        pos_ref[rows, :] = pos.astype(I32)
        post_ref[k] = pos.T[:N_EXPERTS, :].astype(I32)
        offs_ref[k] = run_sel.astype(I32)
        return (run_tie + tie_incl[CHUNK - 1:CHUNK, :], run_sel + sel_incl[CHUNK - 1:CHUNK, :])

    zero = jnp.zeros((1, LANES), F32)
    lax.fori_loop(0, nblk, block, (zero, zero), unroll=4)


def _route(aff, cap):
    b, n, _ = aff.shape
    nblk = n // CHUNK
    return pl.pallas_call(
        functools.partial(_route_kernel, cap=cap),
        out_shape=[jax.ShapeDtypeStruct((b, n, LANES), I32),
                   jax.ShapeDtypeStruct((b, nblk, N_EXPERTS, CHUNK), I32),
                   jax.ShapeDtypeStruct((b, nblk, 1, LANES), I32),
                   jax.ShapeDtypeStruct((b, nblk, N_EXPERTS, CHUNK), F32)],
        grid=(b,),
        in_specs=[pl.BlockSpec((None, n, LANES), lambda bi: (bi, 0, 0))],
        out_specs=[pl.BlockSpec((None, n, LANES), lambda bi: (bi, 0, 0)),
                   pl.BlockSpec((None, nblk, N_EXPERTS, CHUNK), lambda bi: (bi, 0, 0, 0)),
                   pl.BlockSpec((None, nblk, 1, LANES), lambda bi: (bi, 0, 0, 0)),
                   pl.BlockSpec((None, nblk, N_EXPERTS, CHUNK), lambda bi: (bi, 0, 0, 0))],
        compiler_params=_cparams(("parallel",)), name="route",
    )(aff)


def _window_start(s0, w, cap):
    lo = ((s0 >> 4) << 4) + w * SLOT_CHUNK
    return lo, pl.multiple_of(jnp.minimum(lo, cap - SLOT_CHUNK), BF16_ROWS)


def _num_windows(s0, s1):
    return (s1 - ((s0 >> 4) << 4) + SLOT_CHUNK - 1) >> (SLOT_CHUNK.bit_length() - 1)


def _moe_gather_kernel(cnt_ref, post_ref, afft_ref, h_ref, xe_ref, gate_ref, *, ntile, cap, group, tiles_per_step):
    bi, eg, ts = pl.program_id(0), pl.program_id(1), pl.program_id(2)

    @pl.when(ts == 0)
    def _():
        xe_ref[...] = jnp.zeros_like(xe_ref)
        gate_ref[...] = jnp.zeros_like(gate_ref)

    slot = lax.broadcasted_iota(I32, (SLOT_CHUNK, MOE_TILE), 0)
    for u in range(tiles_per_step):
        htile = h_ref[u * MOE_TILE:(u + 1) * MOE_TILE, :]
        t = ts * tiles_per_step + u
        bases = [(bi * N_EXPERTS + eg * group + g) * (ntile + 1) + t for g in range(group)]

        def windows(w, _, u=u, htile=htile, bases=bases):
            starts, onehots = [], []
            for g in range(group):
                lo, start = _window_start(cnt_ref[bases[g]], w, cap)
                posrow = post_ref[g, u]
                hit = (posrow - start == slot) & (posrow >= lo)
                onehots.append(jnp.where(hit, 1.0, 0.0).astype(BF16))
                gates = jnp.sum(jnp.where(hit, afft_ref[g, u], 0.0), axis=1, keepdims=True)
                dst = pl.ds(start, SLOT_CHUNK)
                gate_ref[g, dst, :] = gate_ref[g, dst, :] + jnp.broadcast_to(gates, (SLOT_CHUNK, LANES))
                starts.append(start)
            rows = jnp.dot(jnp.concatenate(onehots, axis=0), htile, preferred_element_type=F32)
            for g in range(group):
                dst = pl.ds(starts[g], SLOT_CHUNK)
                xe_ref[g, dst, :] = xe_ref[g, dst, :] + rows[g * SLOT_CHUNK:(g + 1) * SLOT_CHUNK].astype(BF16)
            return 0

        windows(0, 0)
        nwin = 1
        for g in range(group):
            nwin = jnp.maximum(nwin, _num_windows(cnt_ref[bases[g]], cnt_ref[bases[g] + 1]))
        lax.fori_loop(1, nwin, windows, 0)


def _moe_gather(cnt, post, afft, h2, cap, group=8, tiles_per_step=4):
    b, n, d = h2.shape
    ntile = n // MOE_TILE
    per_tile = pl.BlockSpec((None, group, tiles_per_step, 1, MOE_TILE), lambda bi, eg, ts, c: (bi, eg, ts, 0, 0))
    grid_spec = pltpu.PrefetchScalarGridSpec(
        num_scalar_prefetch=1, grid=(b, N_EXPERTS // group, ntile // tiles_per_step),
        in_specs=[per_tile, per_tile,
                  pl.BlockSpec((None, tiles_per_step * MOE_TILE, d), lambda bi, eg, ts, c: (bi, ts, 0))],
        out_specs=[pl.BlockSpec((None, group, cap, d), lambda bi, eg, ts, c: (bi, eg, 0, 0)),
                   pl.BlockSpec((None, group, cap, LANES), lambda bi, eg, ts, c: (bi, eg, 0, 0))])
    return pl.pallas_call(
        functools.partial(_moe_gather_kernel, ntile=ntile, cap=cap, group=group, tiles_per_step=tiles_per_step),
        out_shape=[jax.ShapeDtypeStruct((b, N_EXPERTS, cap, d), BF16),
                   jax.ShapeDtypeStruct((b, N_EXPERTS, cap, LANES), F32)],
        grid_spec=grid_spec, compiler_params=_cparams(("arbitrary", "arbitrary", "arbitrary")), name="moe_gather",
    )(cnt, post, afft, h2)


def _moe_ffn_kernel(xe_ref, gate_ref, wg_ref, wu_ref, wd_ref, y_ref, acc_scr, *, hid_tile):
    j = pl.program_id(2)
    xe = xe_ref[...]
    for k in range(wg_ref.shape[1] // hid_tile):
        cols = slice(k * hid_tile, (k + 1) * hid_tile)
        gate = jnp.dot(xe, wg_ref[:, cols].astype(BF16), preferred_element_type=F32)
        up = jnp.dot(xe, wu_ref[:, cols].astype(BF16), preferred_element_type=F32)
        hid = (_silu(gate) * up).astype(BF16)
        part = jnp.dot(hid, wd_ref[cols, :].astype(BF16), preferred_element_type=F32)
        if k == 0:
            @pl.when(j == 0)
            def _():
                acc_scr[...] = part

            @pl.when(j != 0)
            def _():
                acc_scr[...] += part
        else:
            acc_scr[...] += part

    @pl.when(j == pl.num_programs(2) - 1)
    def _():
        y_ref[...] = (acc_scr[...] * gate_ref[:, 0:1]).astype(y_ref.dtype)


def _moe_ffn(xe, gate, wg, wu, wd, layer, hid_split=2):
    b, ne, cap, d = xe.shape
    dh = wg.shape[3] // hid_split
    return pl.pallas_call(
        functools.partial(_moe_ffn_kernel, hid_tile=256),
        out_shape=jax.ShapeDtypeStruct((b, ne, cap, d), BF16),
        grid=(b, ne, hid_split),
        in_specs=[pl.BlockSpec((None, None, cap, d), lambda bi, e, j: (bi, e, 0, 0)),
                  pl.BlockSpec((None, None, cap, LANES), lambda bi, e, j: (bi, e, 0, 0)),
                  pl.BlockSpec((None, None, d, dh), lambda bi, e, j: (layer, e, 0, j)),
                  pl.BlockSpec((None, None, d, dh), lambda bi, e, j: (layer, e, 0, j)),
                  pl.BlockSpec((None, None, dh, d), lambda bi, e, j: (layer, e, j, 0))],
        out_specs=pl.BlockSpec((None, None, cap, d), lambda bi, e, j: (bi, e, 0, 0)),
        scratch_shapes=[pltpu.VMEM((cap, d), F32)],
        compiler_params=_cparams(("parallel", "parallel", "arbitrary")), name="moe_ffn",
    )(xe, gate, wg, wu, wd)


def _moe_combine_kernel(cnt_ref, pos_ref, x_ref, gt2_ref, y_ref, fg_ref, o_ref, acc_scr, ycat_scr, *, ntile, cap, final):
    bi, t = pl.program_id(0), pl.program_id(1)
    kw = N_EXPERTS * SLOT_CHUNK
    shift = SLOT_CHUNK.bit_length() - 1
    sel = jnp.where(lax.broadcasted_iota(I32, (LANES, kw), 0) == (lax.broadcasted_iota(I32, (LANES, kw), 1) >> shift),
                    1.0, 0.0).astype(BF16)

    def spread(v):
        hi = (v >> 5).astype(F32).astype(BF16)
        lo = (v & 31).astype(F32).astype(BF16)
        return 32.0 * jnp.dot(hi, sel, preferred_element_type=F32) + jnp.dot(lo, sel, preferred_element_type=F32)

    pos_w = spread(pos_ref[...] + 1) - 1.0
    within = (lax.broadcasted_iota(I32, (1, kw), 1) & (SLOT_CHUNK - 1)).astype(F32)
    lane = lax.broadcasted_iota(I32, (SUBLANES, LANES), 1)

    def windows(w):
        starts = jnp.zeros((SUBLANES, LANES), I32)
        los = jnp.zeros((SUBLANES, LANES), I32)
        for e in range(N_EXPERTS):
            lo, start = _window_start(cnt_ref[(bi * N_EXPERTS + e) * (ntile + 1) + t], w, cap)
            starts = jnp.where(lane == e, start, starts)
            los = jnp.where(lane == e, lo, los)
            ycat_scr[e * SLOT_CHUNK:(e + 1) * SLOT_CHUNK, :] = y_ref[e, pl.ds(start, SLOT_CHUNK), :]
        want = spread(starts)[0:1] + within
        onehot = jnp.where((pos_w == want) & (pos_w >= spread(los)[0:1]), 1.0, 0.0).astype(BF16)
        return jnp.dot(onehot, ycat_scr[...], preferred_element_type=F32)

    acc_scr[...] = windows(0)
    nwin = 1
    for e in range(N_EXPERTS):
        base = (bi * N_EXPERTS + e) * (ntile + 1) + t
        nwin = jnp.maximum(nwin, _num_windows(cnt_ref[base], cnt_ref[base + 1]))

    def overflow(w, _):
        acc_scr[...] += windows(w)
        return 0

    lax.fori_loop(1, nwin, overflow, 0)
    out = x_ref[...] + gt2_ref[...] * acc_scr[...]
    if final:
        out = out * lax.rsqrt(jnp.mean(out * out, axis=-1, keepdims=True) + EPS) * fg_ref[...]
    o_ref[...] = out


def _moe_combine(cnt, pos, x, gt2, y, final_g, final):
    b, n, d = x.shape
    ntile = n // MOE_TILE
    cap = y.shape[2]
    grid_spec = pltpu.PrefetchScalarGridSpec(
        num_scalar_prefetch=1, grid=(b, ntile),
        in_specs=[pl.BlockSpec((None, MOE_TILE, LANES), lambda bi, t, c: (bi, t, 0)),
                  pl.BlockSpec((None, MOE_TILE, d), lambda bi, t, c: (bi, t, 0)),
                  pl.BlockSpec((None, 1, d), lambda bi, t, c: (bi, 0, 0)),
                  pl.BlockSpec((None, N_EXPERTS, cap, d), lambda bi, t, c: (bi, 0, 0, 0),
                               pipeline_mode=pl.Buffered(1)),
                  pl.BlockSpec((1, d), lambda bi, t, c: (0, 0))],
        out_specs=pl.BlockSpec((None, MOE_TILE, d), lambda bi, t, c: (bi, t, 0)),
        scratch_shapes=[pltpu.VMEM((MOE_TILE, d), F32), pltpu.VMEM((N_EXPERTS * SLOT_CHUNK, d), BF16)])
    return pl.pallas_call(
        functools.partial(_moe_combine_kernel, ntile=ntile, cap=cap, final=final),
        out_shape=jax.ShapeDtypeStruct((b, n, d), F32),
        grid_spec=grid_spec, compiler_params=_cparams(("arbitrary", "arbitrary")), name="moe_combine",
    )(cnt, pos, x, gt2, y, final_g)


def _ec_moe(x_mid, h2, aff, gt2, wg, wu, wd, layer, final_g, final):
    b, n, _ = x_mid.shape
    cap = max(1, EC_FACTOR * n // N_EXPERTS)
    ntile = n // MOE_TILE
    pos, post, offs, afft = _route(aff, cap)
    per_tile = lambda a: a.transpose(0, 2, 1, 3).reshape(b, N_EXPERTS, ntile, 1, MOE_TILE)
    starts = offs[:, ::MOE_TILE // CHUNK, 0, :N_EXPERTS].transpose(0, 2, 1)
    cnt = jnp.concatenate([starts, jnp.full((b, N_EXPERTS, 1), cap, I32)], axis=2).reshape(-1)
    xe, gate = _moe_gather(cnt, per_tile(post), per_tile(afft), h2, cap)
    y = _moe_ffn(xe, gate, wg, wu, wd, layer)
    return _moe_combine(cnt, pos, x_mid, gt2, y, final_g, final)


def _rope_table(n):
    rows = n // GRID_W
    row = jnp.repeat(jnp.arange(rows), GRID_W).astype(F32)
    col = jnp.tile(jnp.arange(GRID_W), rows).astype(F32)
    nf = ATTN_DIM // 4
    inv = ROPE_BASE ** (-jnp.arange(nf, dtype=F32) / nf)
    ang = jnp.concatenate([row[:, None] * inv, col[:, None] * inv], axis=-1)
    cos, sin = jnp.cos(ang), jnp.sin(ang)
    reps = LANES // ATTN_DIM
    return jnp.concatenate([jnp.tile(jnp.concatenate([cos, cos], -1), (1, reps)),
                            jnp.tile(jnp.concatenate([-sin, sin], -1), (1, reps))], axis=-1)


def _pad_lanes(a):
    return jnp.pad(a, ((0, 0), (0, LANES - a.shape[1])))


def kernel(x, c, ctx, c_ctx, w_mod, b_mod, norm_mix_g, norm_ffn_g, final_norm_g, ab_w_in, ab_conv_w, ab_gate_b,
           ab_head_g, ab_sink, ab_w_out, gm_w_in, gm_ln_g, gm_ln_b, gm_w_s, gm_b_s, gm_w_out, moe_w_router,
           moe_w_gate, moe_w_up, moe_w_down):
    b, n, d = x.shape
    depth = w_mod.shape[0]
    assert depth <= 2, "context stream is only advanced for deeper stacks; not supported here"
    cond = jnp.zeros((SUBLANES, d), F32).at[:b].set(c).at[b].set(c_ctx)
    mods = _adaln(cond, w_mod, b_mod)

    def mod_rows(layer, rows):
        m = mods[layer, rows].reshape(-1, 6, 1, d)
        return [m[:, i] for i in range(6)]

    row = lambda v: v.reshape(1, -1)
    for layer in range(depth):
        sh1, sc1, gt1, sh2, sc2, gt2 = mod_rows(layer, slice(0, b))
        g1, g2 = row(norm_mix_g[layer]), row(norm_ffn_g[layer])
        wr = _hi_lo_rhs(_pad_lanes(moe_w_router[layer]))
        if layer % 2 == 0:
            e = layer // 2
            csh1, csc1 = (jnp.broadcast_to(v, (b, 1, d)) for v in mod_rows(layer, slice(b, b + 1))[:2])
            w_in = ab_w_in[e]
            g_lo = 4 * LSTM_WIDTH
            w_main = jnp.concatenate([w_in[:, :g_lo], w_in[:, g_lo + N_GATES:]], axis=1).astype(BF16)
            w_gate = _hi_lo_rhs(_pad_lanes(w_in[:, g_lo:g_lo + N_GATES]))
            p, gts, vt, avt = _modmm(x, g1, sc1, sh1, w_main, w_gate, name="ab_in")
            pc, gtc, vtc, avtc = _modmm(ctx, g1, csc1, csh1, w_main, w_gate, name="ab_in_ctx")
            hf, hb = _mlstm(_conv_silu(p, ab_conv_w[e]), vt, gts, _conv_silu(pc, ab_conv_w[e]), vtc, gtc,
                            _pad_lanes(row(ab_gate_b[e])))
            at = _attn(p, avt, pc, avtc, _rope_table(n), _pad_lanes(row(ab_sink[e])))
            x_mid, h2, aff = _ab_out(hf, hb, p, at, x, row(ab_head_g[e]), ab_w_out[e].astype(BF16),
                                     gt1, g2, sc2, sh2, wr)
        else:
            o = layer // 2
            uv = _modmm(x, g1, sc1, sh1, gm_w_in[o].astype(BF16), act="gelu", name="gm_in")
            x_mid, h2, aff = _gm_out(uv, x, row(gm_ln_g[o]), row(gm_ln_b[o]), gm_w_s[o].astype(BF16),
                                     _pad_lanes(gm_b_s[o].T), gm_w_out[o].astype(BF16), gt1, g2, sc2, sh2, wr)
        x = _ec_moe(x_mid, h2, aff, gt2, moe_w_gate, moe_w_up, moe_w_down, layer,
                    row(final_norm_g), layer == depth - 1)
    return x
```

```python
import functools

import jax
import jax.numpy as jnp
from jax import lax
from jax.experimental import pallas as pl
from jax.experimental.pallas import tpu as pltpu

F32 = jnp.float32
BF16 = jnp.bfloat16
I32 = jnp.int32
HI = lax.Precision.HIGHEST

D_MODEL = 1024
GRID_W = 64
EPS = 1e-6
LSTM_HEADS = 4
LSTM_DIM = 128
LSTM_WIDTH = LSTM_HEADS * LSTM_DIM
LSTM_CONV = 5
CHUNK = 128
ATTN_HEADS = 8
ATTN_KV_HEADS = 2
ATTN_GROUP = ATTN_HEADS // ATTN_KV_HEADS
ATTN_DIM = 64
ROPE_BASE = 10000.0
GM_GROUPS = 8
GM_HALF = 2 * D_MODEL
N_EXPERTS = 16
EC_FACTOR = 2
N_GATES = 4 * LSTM_HEADS

LANES = 128
SUBLANES = 8
BF16_ROWS = 16
VMEM_LIMIT_BYTES = 56 * 1024 * 1024

P_COLS = 4 * LSTM_WIDTH + ATTN_HEADS * ATTN_DIM + 2 * ATTN_KV_HEADS * ATTN_DIM
PB_V, PB_O, PB_AQ = 2, 3, 4
PB_AK, PB_AV = 20, 21
MOE_TILE = 256
SLOT_CHUNK = 64


def _cparams(sem, vmem=VMEM_LIMIT_BYTES):
    return pltpu.CompilerParams(dimension_semantics=sem, vmem_limit_bytes=vmem)


def _rms_mod(x, g, sc, sh):
    y = x * lax.rsqrt(jnp.mean(x * x, axis=-1, keepdims=True) + EPS)
    return y * g * (1.0 + sc) + sh


def _silu(x):
    return x * jax.nn.sigmoid(x)


def _gelu_tanh(x):
    return 0.5 * x * (1.0 + jnp.tanh(0.7978845608028654 * (x + 0.044715 * (x * x * x))))


def _log_sigmoid(x):
    return jnp.minimum(x, 0.0) - jnp.log(1.0 + jnp.exp(-jnp.abs(x)))


def _dot_t(a, b):
    return lax.dot_general(a, b, (((1,), (1,)), ((), ())), preferred_element_type=F32)


def _adaln_kernel(c_ref, w_ref, b_ref, o_ref):
    s = _silu(c_ref[...])
    o_ref[...] = jnp.dot(s, w_ref[...], precision=HI, preferred_element_type=F32) + b_ref[...]


def _adaln(cond, w_mod, b_mod):
    depth, d, six_d = w_mod.shape
    tn = six_d // 4
    return pl.pallas_call(
        _adaln_kernel,
        out_shape=jax.ShapeDtypeStruct((depth, SUBLANES, six_d), F32),
        grid=(depth, six_d // tn),
        in_specs=[pl.BlockSpec((SUBLANES, d), lambda l, j: (0, 0)),
                  pl.BlockSpec((None, d, tn), lambda l, j: (l, 0, j)),
                  pl.BlockSpec((None, 1, tn), lambda l, j: (l, 0, j))],
        out_specs=pl.BlockSpec((None, SUBLANES, tn), lambda l, j: (l, 0, j)),
        compiler_params=_cparams(("arbitrary", "arbitrary")),
        name="adaln",
    )(cond, w_mod, b_mod.reshape(depth, 1, six_d))


def _hi_lo_lhs(h, h_hi):
    return jnp.concatenate([h_hi, (h - h_hi.astype(F32)).astype(BF16), h_hi], axis=1)


def _hi_lo_rhs(w):
    w_hi = w.astype(BF16)
    return jnp.concatenate([w_hi, w_hi, (w - w_hi.astype(F32)).astype(BF16)], axis=0)


def _modmm_kernel(x_ref, g_ref, sc_ref, sh_ref, w_ref, *rest, chunks, act, with_gates):
    if with_gates:
        wg_ref, o_ref, og_ref, vt_ref, avt_ref = rest
    else:
        (o_ref,) = rest
    h = _rms_mod(x_ref[...], g_ref[...], sc_ref[...], sh_ref[...])
    hb = h.astype(BF16)
    for lo, hi in chunks:
        y = jnp.dot(hb, w_ref[:, lo:hi], preferred_element_type=F32)
        if act == "gelu":
            y = _gelu_tanh(y)
        o_ref[:, lo:hi] = y.astype(o_ref.dtype)
        if with_gates and lo == PB_V * LSTM_WIDTH:
            vt_ref[...] = y.T.astype(vt_ref.dtype)
        if with_gates and lo <= PB_AV * LANES < hi:
            av = y[:, PB_AV * LANES - lo:(PB_AV + 1) * LANES - lo]
            avt_ref[...] = av.T.astype(avt_ref.dtype)
    if with_gates:
        og_ref[...] = jnp.dot(_hi_lo_lhs(h, hb), wg_ref[...], preferred_element_type=F32)


def _modmm(x, g, sc, sh, w, wg=None, *, act=None, tm=512, chunk=512, name="modmm"):
    b, n, d = x.shape
    no = w.shape[1]
    tm = min(tm, n)
    assert chunk == LSTM_WIDTH
    chunks = tuple((lo, min(lo + chunk, no)) for lo in range(0, no, chunk))
    in_specs = [pl.BlockSpec((None, tm, d), lambda bi, i: (bi, i, 0)),
                pl.BlockSpec((1, d), lambda bi, i: (0, 0)),
                pl.BlockSpec((None, 1, d), lambda bi, i: (bi, 0, 0)),
                pl.BlockSpec((None, 1, d), lambda bi, i: (bi, 0, 0)),
                pl.BlockSpec((d, no), lambda bi, i: (0, 0))]
    out_shape = [jax.ShapeDtypeStruct((b, n, no), BF16)]
    out_specs = [pl.BlockSpec((None, tm, no), lambda bi, i: (bi, i, 0))]
    args = [x, g, sc, sh, w]
    if wg is not None:
        in_specs.append(pl.BlockSpec(wg.shape, lambda bi, i: (0, 0)))
        out_shape += [jax.ShapeDtypeStruct((b, n, LANES), F32), jax.ShapeDtypeStruct((b, LSTM_WIDTH, n), BF16),
                      jax.ShapeDtypeStruct((b, LANES, n), BF16)]
        out_specs += [pl.BlockSpec((None, tm, LANES), lambda bi, i: (bi, i, 0)),
                      pl.BlockSpec((None, LSTM_WIDTH, tm), lambda bi, i: (bi, 0, i)),
                      pl.BlockSpec((None, LANES, tm), lambda bi, i: (bi, 0, i))]
        args.append(wg)
    res = pl.pallas_call(
        functools.partial(_modmm_kernel, chunks=chunks, act=act, with_gates=wg is not None),
        out_shape=out_shape, grid=(b, n // tm), in_specs=in_specs, out_specs=out_specs,
        compiler_params=_cparams(("parallel", "parallel")), name=name,
    )(*args)
    return res if wg is not None else res[0]


def _conv_silu_kernel(x_ref, xp_ref, xn_ref, w_ref, o_ref, pad_scr):
    i = pl.program_id(1)
    rows = x_ref.shape[0]
    has_prev = jnp.where(i > 0, 1.0, 0.0)
    has_next = jnp.where(i < pl.num_programs(1) - 1, 1.0, 0.0)
    pad_scr[pl.ds(0, SUBLANES), :] = xp_ref[...].astype(F32)[SUBLANES:, :] * has_prev
    pad_scr[pl.ds(SUBLANES, rows), :] = x_ref[...].astype(F32)
    pad_scr[pl.ds(SUBLANES + rows, SUBLANES), :] = xn_ref[...].astype(F32)[:SUBLANES, :] * has_next
    w = w_ref[...]
    acc = None
    for t in range(LSTM_CONV):
        term = pad_scr[pl.ds(SUBLANES - LSTM_CONV // 2 + t, rows), :] * w[t:t + 1, :]
        acc = term if acc is None else acc + term
    y = _silu(acc)
    o_ref[:, :LSTM_WIDTH] = y[:, :LSTM_WIDTH].astype(o_ref.dtype)
    o_ref[:, LSTM_WIDTH:] = (y[:, LSTM_WIDTH:] * (LSTM_DIM ** -0.5)).astype(o_ref.dtype)


def _conv_silu(p, conv_w, tm=512):
    b, n, _ = p.shape
    tm = min(tm, n)
    qkw = 2 * LSTM_WIDTH
    hpt = tm // BF16_ROWS
    nhb = n // BF16_ROWS
    return pl.pallas_call(
        _conv_silu_kernel,
        out_shape=jax.ShapeDtypeStruct((b, n, qkw), BF16),
        grid=(b, n // tm),
        in_specs=[pl.BlockSpec((None, tm, qkw), lambda bi, i: (bi, i, 0)),
                  pl.BlockSpec((None, BF16_ROWS, qkw), lambda bi, i: (bi, jnp.maximum(i * hpt - 1, 0), 0)),
                  pl.BlockSpec((None, BF16_ROWS, qkw), lambda bi, i: (bi, jnp.minimum((i + 1) * hpt, nhb - 1), 0)),
                  pl.BlockSpec((LSTM_CONV, qkw), lambda bi, i: (0, 0))],
        out_specs=pl.BlockSpec((None, tm, qkw), lambda bi, i: (bi, i, 0)),
        scratch_shapes=[pltpu.VMEM((tm + 2 * SUBLANES, qkw), F32)],
        compiler_params=_cparams(("parallel", "parallel")), name="conv_silu",
    )(p, p, p, conv_w)


def _tri(n, lower):
    r = lax.broadcasted_iota(I32, (n, n), 0)
    c = lax.broadcasted_iota(I32, (n, n), 1)
    return (c <= r) if lower else (c >= r)


STATE_ROWS = LSTM_DIM + BF16_ROWS
MLSTM_STEP_CHUNKS = 4


def _mlstm_segment(d, gates, k_all, vt_all, q_all, cn_scr, m_scr, ht_ref, cols, first):
    seg = gates.shape[0]
    r = lax.broadcasted_iota(I32, (seg, seg), 0)
    c = lax.broadcasted_iota(I32, (seg, seg), 1)
    before = (r <= c) if d == 0 else (r >= c)
    tri = jnp.where(before, 1.0, 0.0).astype(F32)
    gates_t = gates.T
    bcum_t = jnp.dot(_log_sigmoid(gates_t), tri, precision=HI, preferred_element_type=F32)
    last = seg - 1 if d == 0 else 0
    for h in range(LSTM_HEADS):
        ci, cf = 2 * d * LSTM_HEADS + h, (2 * d + 1) * LSTM_HEADS + h
        s = d * LSTM_HEADS + h
        hd = slice(h * LSTM_DIM, (h + 1) * LSTM_DIM)
        kb, vt = k_all[:, hd], vt_all[hd, :]
        b_row, li_row = bcum_t[cf:cf + 1, :], gates_t[ci:ci + 1, :]
        m_prev = jnp.zeros((1, 1), F32) if first else m_scr[s][:, 0:1]
        if q_all is not None:
            qb = q_all[:, hd]
            cn = cn_scr[s]
            per_key = jnp.broadcast_to(li_row - b_row, (seg, seg)).T
            log_d = jnp.where(before, b_row + per_key, -jnp.inf)
            m_row = jnp.maximum(b_row + m_prev, jnp.max(log_d, axis=0, keepdims=True))
            sm = _dot_t(kb, qb) * jnp.exp(log_d - m_row)
            a = jnp.exp(b_row + m_prev - m_row)
            qc = _dot_t(cn.astype(BF16), qb)
            num = jnp.dot(vt, sm.astype(BF16), preferred_element_type=F32) + a * qc[:LSTM_DIM]
            den = jnp.sum(sm, axis=0, keepdims=True) + a * qc[LSTM_DIM:LSTM_DIM + 1]
            ht_ref[hd, cols] = num / jnp.maximum(jnp.abs(den), jnp.exp(-m_row))
        g = b_row[:, last:last + 1]
        w = g - b_row + li_row
        m_new = jnp.maximum(g + m_prev, jnp.max(w, axis=1, keepdims=True))
        wt = jnp.exp(w - m_new)
        aug = jnp.concatenate([(vt.astype(F32) * wt).astype(BF16),
                               jnp.broadcast_to(wt, (BF16_ROWS, seg)).astype(BF16)], axis=0)
        upd = jnp.dot(aug, kb, preferred_element_type=F32)
        cn_scr[s] = upd if first else jnp.exp(g + m_prev - m_new) * cn_scr[s] + upd
        m_scr[s] = jnp.broadcast_to(m_new, (1, LANES))


def _mlstm_kernel(qkf_ref, qkb_ref, vtf_ref, vtb_ref, gf_ref, gb_ref, kc_ref, vtc_ref, gc_ref, gbias_ref,
                  hf_ref, hb_ref, cn_scr, m_scr):
    gbias = gbias_ref[...]

    @pl.when(pl.program_id(1) == 0)
    def _():
        gates = gc_ref[...] + gbias
        for d in range(2):
            _mlstm_segment(d, gates, kc_ref[...], vtc_ref[...], None, cn_scr, m_scr, None, None, True)

    per_step = qkf_ref.shape[0] // CHUNK
    for u in range(per_step):
        for d, (qk_ref, vt_ref, g_ref, h_ref) in enumerate(((qkf_ref, vtf_ref, gf_ref, hf_ref),
                                                            (qkb_ref, vtb_ref, gb_ref, hb_ref))):
            j = u if d == 0 else per_step - 1 - u
            rows = slice(j * CHUNK, (j + 1) * CHUNK)
            qk = qk_ref[rows, :]
            _mlstm_segment(d, g_ref[rows, :] + gbias, qk[:, LSTM_WIDTH:], vt_ref[:, rows], qk[:, :LSTM_WIDTH],
                           cn_scr, m_scr, h_ref, rows, False)


def _mlstm(qk, vt, g, qkc, vtc, gc, gate_bias):
    b, n, _ = qk.shape
    lc = qkc.shape[1]
    rows = MLSTM_STEP_CHUNKS * CHUNK
    nc = n // rows
    qkw = 2 * LSTM_WIDTH
    in_specs = [
        pl.BlockSpec((None, rows, qkw), lambda bi, c: (bi, c, 0)),
        pl.BlockSpec((None, rows, qkw), lambda bi, c: (bi, nc - 1 - c, 0)),
        pl.BlockSpec((None, LSTM_WIDTH, rows), lambda bi, c: (bi, 0, c)),
        pl.BlockSpec((None, LSTM_WIDTH, rows), lambda bi, c: (bi, 0, nc - 1 - c)),
        pl.BlockSpec((None, rows, LANES), lambda bi, c: (bi, c, 0)),
        pl.BlockSpec((None, rows, LANES), lambda bi, c: (bi, nc - 1 - c, 0)),
        pl.BlockSpec((None, lc, LSTM_WIDTH), lambda bi, c: (bi, 0, 1)),
        pl.BlockSpec((None, LSTM_WIDTH, lc), lambda bi, c: (bi, 0, 0)),
        pl.BlockSpec((None, lc, LANES), lambda bi, c: (bi, 0, 0)),
        pl.BlockSpec((1, LANES), lambda bi, c: (0, 0))]
    out_specs = [pl.BlockSpec((None, LSTM_WIDTH, rows), lambda bi, c: (bi, 0, c)),
                 pl.BlockSpec((None, LSTM_WIDTH, rows), lambda bi, c: (bi, 0, nc - 1 - c))]
    return pl.pallas_call(
        _mlstm_kernel,
        out_shape=[jax.ShapeDtypeStruct((b, LSTM_WIDTH, n), F32)] * 2,
        grid=(b, nc), in_specs=in_specs, out_specs=out_specs,
        scratch_shapes=[pltpu.VMEM((2 * LSTM_HEADS, STATE_ROWS, LSTM_DIM), F32),
                        pltpu.VMEM((2 * LSTM_HEADS, 1, LANES), F32)],
        compiler_params=_cparams(("arbitrary", "arbitrary")), name="mlstm",
    )(qk, qk, vt, vt, g, g, qkc, vtc, gc, gate_bias)


def _rope(x, cos, sin_signed):
    w = x.shape[1]
    lane = lax.broadcasted_iota(I32, x.shape, 1)
    first = (lane & (ATTN_DIM - 1)) < (ATTN_DIM // 2)
    partner = jnp.where(first, pltpu.roll(x, w - ATTN_DIM // 2, 1), pltpu.roll(x, ATTN_DIM // 2, 1))
    return x * cos + partner * sin_signed


def _attn_kernel(q_ref, kp_ref, kc_ref, kn_ref, vtp_ref, vtc_ref, vtn_ref, kctx_ref, vtctx_ref,
                 tp_ref, tc_ref, tn_ref, sink_ref, bias_ref, o_ref):
    def table(t_ref):
        t = t_ref[...]
        return t[:, :LANES], t[:, LANES:]

    cos_c, sin_c = table(tc_ref)
    q = _rope(q_ref[...].astype(F32), jnp.concatenate([cos_c] * 4, axis=1), jnp.concatenate([sin_c] * 4, axis=1))
    q = q * (ATTN_DIM ** -0.5)
    ks = []
    for k_ref, t_ref in ((kp_ref, tp_ref), (kc_ref, tc_ref), (kn_ref, tn_ref)):
        cos_t, sin_t = table(t_ref)
        ks.append(_rope(k_ref[...].astype(F32), cos_t, sin_t).astype(BF16))
    k_all = jnp.concatenate(ks + [kctx_ref[...]], axis=0)
    vt_all = jnp.concatenate([vtp_ref[...], vtc_ref[...], vtn_ref[...], vtctx_ref[...]], axis=1)

    bias = jnp.concatenate([bias_ref[...]] * ATTN_GROUP, axis=1)
    lane = lax.broadcasted_iota(I32, (CHUNK, LANES), 1)
    sink = sink_ref[...]

    for g in range(ATTN_KV_HEADS):
        half_g = (lane >= ATTN_DIM) if g == 1 else (lane < ATTN_DIM)
        qs, snk = [], []
        for r in range(ATTN_GROUP):
            h = g * ATTN_GROUP + r
            t = q[:, (h // 2) * LANES:(h // 2 + 1) * LANES]
            if h % 2 != g:
                t = pltpu.roll(t, ATTN_DIM, 1)
            qs.append(jnp.where(half_g, t, 0.0).astype(BF16))
            snk.append(jnp.broadcast_to(sink[:, h:h + 1], (1, CHUNK)))
        snk = jnp.concatenate(snk, axis=1)
        st = _dot_t(k_all, jnp.concatenate(qs, axis=0)) + bias
        m = jnp.maximum(jnp.max(st, axis=0, keepdims=True), snk)
        e = jnp.exp(st - m)
        den = jnp.sum(e, axis=0, keepdims=True) + jnp.exp(snk - m)
        pv = jnp.dot(vt_all, e.astype(BF16), preferred_element_type=F32)
        o = (pv[g * ATTN_DIM:(g + 1) * ATTN_DIM] / den).astype(o_ref.dtype)
        for r in range(ATTN_GROUP):
            h = g * ATTN_GROUP + r
            o_ref[h * ATTN_DIM:(h + 1) * ATTN_DIM, :] = o[:, r * CHUNK:(r + 1) * CHUNK]


def _attn_bias(nctx):
    i = jnp.arange(CHUNK)[None, :]
    j = jnp.arange(3 * CHUNK)[:, None]
    band = (j >= i) & (j <= i + 2 * CHUNK)
    local = jnp.stack([band & (j >= CHUNK), band, band & (j < 2 * CHUNK)])
    return jnp.concatenate([jnp.where(local, 0.0, -jnp.inf).astype(F32), jnp.zeros((3, nctx, CHUNK), F32)], axis=1)


def _attn(p, avt, pc, avtc, table, sink):
    b, n, _ = p.shape
    lc = pc.shape[1]
    nb = n // CHUNK
    assert nb >= 2
    qw = ATTN_HEADS * ATTN_DIM
    bias = _attn_bias(lc)

    def blk(col, off):
        return pl.BlockSpec((None, CHUNK, LANES), lambda bi, i: (bi, jnp.clip(i + off, 0, nb - 1), col))

    def vblk(off):
        return pl.BlockSpec((None, LANES, CHUNK), lambda bi, i: (bi, 0, jnp.clip(i + off, 0, nb - 1)))

    def tab(off):
        return pl.BlockSpec((CHUNK, 2 * LANES), lambda bi, i: (jnp.clip(i + off, 0, nb - 1), 0))

    in_specs = [pl.BlockSpec((None, CHUNK, qw), lambda bi, i: (bi, i, PB_AQ)),
                blk(PB_AK, -1), blk(PB_AK, 0), blk(PB_AK, 1), vblk(-1), vblk(0), vblk(1),
                pl.BlockSpec((None, lc, LANES), lambda bi, i: (bi, 0, PB_AK)),
                pl.BlockSpec((None, LANES, lc), lambda bi, i: (bi, 0, 0)),
                tab(-1), tab(0), tab(1),
                pl.BlockSpec((1, LANES), lambda bi, i: (0, 0)),
                pl.BlockSpec((None,) + bias.shape[1:],
                             lambda bi, i: (jnp.where(i == 0, 0, jnp.where(i == nb - 1, 2, 1)), 0, 0))]
    return pl.pallas_call(
        _attn_kernel,
        out_shape=jax.ShapeDtypeStruct((b, qw, n), BF16),
        grid=(b, nb), in_specs=in_specs,
        out_specs=pl.BlockSpec((None, qw, CHUNK), lambda bi, i: (bi, 0, i)),
        compiler_params=_cparams(("parallel", "parallel")), name="window_attn",
    )(p, p, p, p, avt, avt, avt, pc, avtc, table, table, table, sink, bias)


def _router_tail(x_new, g2, sc2, sh2, wr_ref, x_out_ref, h2_ref, aff_ref):
    x_out_ref[...] = x_new
    h2 = _rms_mod(x_new, g2, sc2, sh2)
    h_hi = h2.astype(BF16)
    h2_ref[...] = h_hi
    logits = jnp.dot(_hi_lo_lhs(h2, h_hi), wr_ref[...], preferred_element_type=F32)
    lane = lax.broadcasted_iota(I32, logits.shape, 1)
    logits = jnp.where(lane < N_EXPERTS, logits, -jnp.inf)
    e = jnp.exp(logits - jnp.max(logits, axis=1, keepdims=True))
    aff_ref[...] = e / jnp.sum(e, axis=1, keepdims=True)


def _ab_out_kernel(hf_ref, hb_ref, o_ref, at_ref, x_ref, hg_ref, wo_ref, gt1_ref, g2_ref, sc2_ref, sh2_ref, wr_ref,
                   x_out_ref, h2_ref, aff_ref):
    hsum = (hf_ref[...] + hb_ref[...]).T
    og = jax.nn.sigmoid(o_ref[...].astype(F32))
    hg = hg_ref[...]
    parts = []
    for h in range(LSTM_HEADS):
        sl = slice(h * LSTM_DIM, (h + 1) * LSTM_DIM)
        seg = hsum[:, sl]
        seg = seg * lax.rsqrt(jnp.mean(seg * seg, axis=-1, keepdims=True) + EPS)
        parts.append((seg * hg[:, sl] * og[:, sl]).astype(BF16))
    cat = jnp.concatenate(parts + [at_ref[...].astype(F32).T.astype(BF16)], axis=1)
    y = jnp.dot(cat, wo_ref[...], preferred_element_type=F32)
    _router_tail(x_ref[...] + gt1_ref[...] * y, g2_ref[...], sc2_ref[...], sh2_ref[...], wr_ref,
                 x_out_ref, h2_ref, aff_ref)


def _tail_out(b, n, d, tm):
    shapes = [jax.ShapeDtypeStruct((b, n, d), F32), jax.ShapeDtypeStruct((b, n, d), BF16),
              jax.ShapeDtypeStruct((b, n, LANES), F32)]
    specs = [pl.BlockSpec((None, tm, d), lambda bi, i: (bi, i, 0)),
             pl.BlockSpec((None, tm, d), lambda bi, i: (bi, i, 0)),
             pl.BlockSpec((None, tm, LANES), lambda bi, i: (bi, i, 0))]
    return shapes, specs


def _ab_out(hf, hb, p, at, x, head_g, w_out, gt1, g2, sc2, sh2, wr, tm=256):
    b, n, d = x.shape
    row = lambda w: pl.BlockSpec((None, tm, w), lambda bi, i: (bi, i, 0))
    vec = pl.BlockSpec((None, 1, d), lambda bi, i: (bi, 0, 0))
    const = lambda s: pl.BlockSpec(s, lambda bi, i: (0, 0))
    scan_out = pl.BlockSpec((None, LSTM_WIDTH, tm), lambda bi, i: (bi, 0, i))
    in_specs = [scan_out, scan_out,
                pl.BlockSpec((None, tm, LSTM_WIDTH), lambda bi, i: (bi, i, PB_O)),
                pl.BlockSpec((None, ATTN_HEADS * ATTN_DIM, tm), lambda bi, i: (bi, 0, i)),
                row(d), const((1, LSTM_WIDTH)), const(w_out.shape),
                vec, const((1, d)), vec, vec, const((3 * d, LANES))]
    shapes, specs = _tail_out(b, n, d, tm)
    return pl.pallas_call(
        _ab_out_kernel, out_shape=shapes, grid=(b, n // tm), in_specs=in_specs, out_specs=specs,
        compiler_params=_cparams(("parallel", "parallel")), name="ab_out",
    )(hf, hb, p, at, x, head_g, w_out, gt1, g2, sc2, sh2, wr)


def _gm_out_kernel(uv_ref, x_ref, lng_ref, lnb_ref, ws_ref, bs_ref, wo_ref, gt1_ref, g2_ref, sc2_ref, sh2_ref, wr_ref,
                   x_out_ref, h2_ref, aff_ref):
    tm = uv_ref.shape[0]
    gw = GM_HALF // GM_GROUPS
    v = uv_ref[:, GM_HALF:].astype(F32)
    mu = jnp.mean(v, axis=-1, keepdims=True)
    vc = v - mu
    var = jnp.mean(vc * vc, axis=-1, keepdims=True)
    vn = (vc * lax.rsqrt(var + EPS) * lng_ref[...] + lnb_ref[...]).astype(BF16)
    zs = []
    for ch in range(tm // CHUNK):
        rows = slice(ch * CHUNK, (ch + 1) * CHUNK)
        cols = []
        for g in range(GM_GROUPS):
            sv = jnp.dot(ws_ref[g], vn[rows, g * gw:(g + 1) * gw], preferred_element_type=F32)
            cols.append(sv + bs_ref[:, g:g + 1])
        sv = jnp.concatenate(cols, axis=1)
        zs.append((uv_ref[rows, :GM_HALF].astype(F32) * sv).astype(BF16))
    z = jnp.concatenate(zs, axis=0)
    y = jnp.dot(z, wo_ref[...], preferred_element_type=F32)
    _router_tail(x_ref[...] + gt1_ref[...] * y, g2_ref[...], sc2_ref[...], sh2_ref[...], wr_ref,
                 x_out_ref, h2_ref, aff_ref)


def _gm_out(uv, x, ln_g, ln_b, w_s, b_s_t, w_out, gt1, g2, sc2, sh2, wr, tm=256):
    b, n, d = x.shape
    vec = pl.BlockSpec((None, 1, d), lambda bi, i: (bi, 0, 0))
    const = lambda s: pl.BlockSpec(s, lambda *_: (0,) * len(s))
    in_specs = [pl.BlockSpec((None, tm, 2 * GM_HALF), lambda bi, i: (bi, i, 0)),
                pl.BlockSpec((None, tm, d), lambda bi, i: (bi, i, 0)),
                const((1, GM_HALF)), const((1, GM_HALF)), const(w_s.shape), const(b_s_t.shape), const(w_out.shape),
                vec, const((1, d)), vec, vec, const((3 * d, LANES))]
    shapes, specs = _tail_out(b, n, d, tm)
    return pl.pallas_call(
        _gm_out_kernel, out_shape=shapes, grid=(b, n // tm), in_specs=in_specs, out_specs=specs,
        compiler_params=_cparams(("parallel", "parallel")), name="gm_out",
    )(uv, x, ln_g, ln_b, w_s, b_s_t, w_out, gt1, g2, sc2, sh2, wr)


def _route_kernel(aff_ref, pos_ref, post_ref, offs_ref, afft_ref, *, cap):
    n = aff_ref.shape[0]
    nblk = n // CHUNK

    def to_expert_major(k, _):
        rows = pl.ds(pl.multiple_of(k * CHUNK, CHUNK), CHUNK)
        afft_ref[k] = aff_ref[rows, :].T[:N_EXPERTS, :]
        return 0

    lax.fori_loop(0, nblk, to_expert_major, 0, unroll=4)

    def count(pred):
        per_lane = jnp.sum(jnp.where(pred, 1.0, 0.0), axis=0)
        return jnp.sum(per_lane, axis=1, keepdims=True)

    def search(i, prefix):
        cand = prefix | jnp.left_shift(jnp.int32(1), 30 - i)
        cand_f = lax.bitcast_convert_type(cand, F32)
        return jnp.where(count(afft_ref[...] >= cand_f[None]) >= cap, cand, prefix)

    thr_col = lax.bitcast_convert_type(lax.fori_loop(0, 31, search, jnp.zeros((N_EXPERTS, 1), I32)), F32)
    need_col = cap - count(afft_ref[...] > thr_col[None])

    def to_row(col):
        full = jnp.concatenate([jnp.broadcast_to(col, (N_EXPERTS, LANES)),
                                jnp.zeros((LANES - N_EXPERTS, LANES), F32)], axis=0)
        return full.T[0:1, :]

    thr, need = to_row(thr_col), to_row(need_col)
    tril = jnp.where(_tri(CHUNK, True), 1.0, 0.0).astype(BF16)

    def block(k, carry):
        run_tie, run_sel = carry
        rows = pl.ds(pl.multiple_of(k * CHUNK, CHUNK), CHUNK)
        a = aff_ref[rows, :]
        gt = a > thr
        tie = jnp.where(a == thr, 1.0, 0.0)
        tie_incl = jnp.dot(tril, tie.astype(BF16), preferred_element_type=F32)
        sel = jnp.where(gt | ((tie > 0.0) & (tie_incl - tie + run_tie < need)), 1.0, 0.0)
        sel_incl = jnp.dot(tril, sel.astype(BF16), preferred_element_type=F32)
        pos = jnp.where(sel > 0.0, sel_incl - sel + run_sel, -1.0)
        pos_ref[rows, :] = pos.astype(I32)
        post_ref[k] = pos.T[:N_EXPERTS, :].astype(I32)
        offs_ref[k] = run_sel.astype(I32)
        return (run_tie + tie_incl[CHUNK - 1:CHUNK, :], run_sel + sel_incl[CHUNK - 1:CHUNK, :])

    zero = jnp.zeros((1, LANES), F32)
    lax.fori_loop(0, nblk, block, (zero, zero), unroll=4)


def _route(aff, cap):
    b, n, _ = aff.shape
    nblk = n // CHUNK
    return pl.pallas_call(
        functools.partial(_route_kernel, cap=cap),
        out_shape=[jax.ShapeDtypeStruct((b, n, LANES), I32),
                   jax.ShapeDtypeStruct((b, nblk, N_EXPERTS, CHUNK), I32),
                   jax.ShapeDtypeStruct((b, nblk, 1, LANES), I32),
                   jax.ShapeDtypeStruct((b, nblk, N_EXPERTS, CHUNK), F32)],
        grid=(b,),
        in_specs=[pl.BlockSpec((None, n, LANES), lambda bi: (bi, 0, 0))],
        out_specs=[pl.BlockSpec((None, n, LANES), lambda bi: (bi, 0, 0)),
                   pl.BlockSpec((None, nblk, N_EXPERTS, CHUNK), lambda bi: (bi, 0, 0, 0)),
                   pl.BlockSpec((None, nblk, 1, LANES), lambda bi: (bi, 0, 0, 0)),
                   pl.BlockSpec((None, nblk, N_EXPERTS, CHUNK), lambda bi: (bi, 0, 0, 0))],
        compiler_params=_cparams(("parallel",)), name="route",
    )(aff)


def _window_start(s0, w, cap):
    lo = ((s0 >> 4) << 4) + w * SLOT_CHUNK
    return lo, pl.multiple_of(jnp.minimum(lo, cap - SLOT_CHUNK), BF16_ROWS)


def _num_windows(s0, s1):
    return (s1 - ((s0 >> 4) << 4) + SLOT_CHUNK - 1) >> (SLOT_CHUNK.bit_length() - 1)


def _moe_gather_kernel(cnt_ref, post_ref, afft_ref, h_ref, xe_ref, gate_ref, *, ntile, cap, group, tiles_per_step):
    bi, eg, ts = pl.program_id(0), pl.program_id(1), pl.program_id(2)

    @pl.when(ts == 0)
    def _():
        xe_ref[...] = jnp.zeros_like(xe_ref)
        gate_ref[...] = jnp.zeros_like(gate_ref)

    slot = lax.broadcasted_iota(I32, (SLOT_CHUNK, MOE_TILE), 0)
    for u in range(tiles_per_step):
        htile = h_ref[u * MOE_TILE:(u + 1) * MOE_TILE, :]
        t = ts * tiles_per_step + u
        bases = [(bi * N_EXPERTS + eg * group + g) * (ntile + 1) + t for g in range(group)]

        def windows(w, _, u=u, htile=htile, bases=bases):
            starts, onehots = [], []
            for g in range(group):
                lo, start = _window_start(cnt_ref[bases[g]], w, cap)
                posrow = post_ref[g, u]
                hit = (posrow - start == slot) & (posrow >= lo)
                onehots.append(jnp.where(hit, 1.0, 0.0).astype(BF16))
                gates = jnp.sum(jnp.where(hit, afft_ref[g, u], 0.0), axis=1, keepdims=True)
                dst = pl.ds(start, SLOT_CHUNK)
                gate_ref[g, dst, :] = gate_ref[g, dst, :] + jnp.broadcast_to(gates, (SLOT_CHUNK, LANES))
                starts.append(start)
            rows = jnp.dot(jnp.concatenate(onehots, axis=0), htile, preferred_element_type=F32)
            for g in range(group):
                dst = pl.ds(starts[g], SLOT_CHUNK)
                xe_ref[g, dst, :] = xe_ref[g, dst, :] + rows[g * SLOT_CHUNK:(g + 1) * SLOT_CHUNK].astype(BF16)
            return 0

        windows(0, 0)
        nwin = 1
        for g in range(group):
            nwin = jnp.maximum(nwin, _num_windows(cnt_ref[bases[g]], cnt_ref[bases[g] + 1]))
        lax.fori_loop(1, nwin, windows, 0)


def _moe_gather(cnt, post, afft, h2, cap, group=8, tiles_per_step=4):
    b, n, d = h2.shape
    ntile = n // MOE_TILE
    per_tile = pl.BlockSpec((None, group, tiles_per_step, 1, MOE_TILE), lambda bi, eg, ts, c: (bi, eg, ts, 0, 0))
    grid_spec = pltpu.PrefetchScalarGridSpec(
        num_scalar_prefetch=1, grid=(b, N_EXPERTS // group, ntile // tiles_per_step),
        in_specs=[per_tile, per_tile,
                  pl.BlockSpec((None, tiles_per_step * MOE_TILE, d), lambda bi, eg, ts, c: (bi, ts, 0))],
        out_specs=[pl.BlockSpec((None, group, cap, d), lambda bi, eg, ts, c: (bi, eg, 0, 0)),
                   pl.BlockSpec((None, group, cap, LANES), lambda bi, eg, ts, c: (bi, eg, 0, 0))])
    return pl.pallas_call(
        functools.partial(_moe_gather_kernel, ntile=ntile, cap=cap, group=group, tiles_per_step=tiles_per_step),
        out_shape=[jax.ShapeDtypeStruct((b, N_EXPERTS, cap, d), BF16),
                   jax.ShapeDtypeStruct((b, N_EXPERTS, cap, LANES), F32)],
        grid_spec=grid_spec, compiler_params=_cparams(("arbitrary", "arbitrary", "arbitrary")), name="moe_gather",
    )(cnt, post, afft, h2)


def _moe_ffn_kernel(xe_ref, gate_ref, wg_ref, wu_ref, wd_ref, y_ref, acc_scr, *, hid_tile):
    j = pl.program_id(2)
    xe = xe_ref[...]
    for k in range(wg_ref.shape[1] // hid_tile):
        cols = slice(k * hid_tile, (k + 1) * hid_tile)
        gate = jnp.dot(xe, wg_ref[:, cols].astype(BF16), preferred_element_type=F32)
        up = jnp.dot(xe, wu_ref[:, cols].astype(BF16), preferred_element_type=F32)
        hid = (_silu(gate) * up).astype(BF16)
        part = jnp.dot(hid, wd_ref[cols, :].astype(BF16), preferred_element_type=F32)
        if k == 0:
            @pl.when(j == 0)
            def _():
                acc_scr[...] = part

            @pl.when(j != 0)
            def _():
                acc_scr[...] += part
        else:
            acc_scr[...] += part

    @pl.when(j == pl.num_programs(2) - 1)
    def _():
        y_ref[...] = (acc_scr[...] * gate_ref[:, 0:1]).astype(y_ref.dtype)


def _moe_ffn(xe, gate, wg, wu, wd, layer, hid_split=1):
    b, ne, cap, d = xe.shape
    dh = wg.shape[3] // hid_split
    return pl.pallas_call(
        functools.partial(_moe_ffn_kernel, hid_tile=256),
        out_shape=jax.ShapeDtypeStruct((b, ne, cap, d), BF16),
        grid=(b, ne, hid_split),
        in_specs=[pl.BlockSpec((None, None, cap, d), lambda bi, e, j: (bi, e, 0, 0)),
                  pl.BlockSpec((None, None, cap, LANES), lambda bi, e, j: (bi, e, 0, 0)),
                  pl.BlockSpec((None, None, d, dh), lambda bi, e, j: (layer, e, 0, j)),
                  pl.BlockSpec((None, None, d, dh), lambda bi, e, j: (layer, e, 0, j)),
                  pl.BlockSpec((None, None, dh, d), lambda bi, e, j: (layer, e, j, 0))],
        out_specs=pl.BlockSpec((None, None, cap, d), lambda bi, e, j: (bi, e, 0, 0)),
        scratch_shapes=[pltpu.VMEM((cap, d), F32)],
        compiler_params=_cparams(("parallel", "parallel", "arbitrary")), name="moe_ffn",
    )(xe, gate, wg, wu, wd)


def _moe_combine_kernel(cnt_ref, pos_ref, x_ref, gt2_ref, y_ref, fg_ref, o_ref, acc_scr, ycat_scr, *, ntile, cap, final):
    bi, t = pl.program_id(0), pl.program_id(1)
    kw = N_EXPERTS * SLOT_CHUNK
    shift = SLOT_CHUNK.bit_length() - 1
    sel = jnp.where(lax.broadcasted_iota(I32, (LANES, kw), 0) == (lax.broadcasted_iota(I32, (LANES, kw), 1) >> shift),
                    1.0, 0.0).astype(BF16)

    def spread(v):
        hi = (v >> 5).astype(F32).astype(BF16)
        lo = (v & 31).astype(F32).astype(BF16)
        return 32.0 * jnp.dot(hi, sel, preferred_element_type=F32) + jnp.dot(lo, sel, preferred_element_type=F32)

    pos_w = spread(pos_ref[...] + 1) - 1.0
    within = (lax.broadcasted_iota(I32, (1, kw), 1) & (SLOT_CHUNK - 1)).astype(F32)
    lane = lax.broadcasted_iota(I32, (SUBLANES, LANES), 1)

    def windows(w):
        starts = jnp.zeros((SUBLANES, LANES), I32)
        los = jnp.zeros((SUBLANES, LANES), I32)
        for e in range(N_EXPERTS):
            lo, start = _window_start(cnt_ref[(bi * N_EXPERTS + e) * (ntile + 1) + t], w, cap)
            starts = jnp.where(lane == e, start, starts)
            los = jnp.where(lane == e, lo, los)
            ycat_scr[e * SLOT_CHUNK:(e + 1) * SLOT_CHUNK, :] = y_ref[e, pl.ds(start, SLOT_CHUNK), :]
        want = spread(starts)[0:1] + within
        onehot = jnp.where((pos_w == want) & (pos_w >= spread(los)[0:1]), 1.0, 0.0).astype(BF16)
        return jnp.dot(onehot, ycat_scr[...], preferred_element_type=F32)

    acc_scr[...] = windows(0)
    nwin = 1
    for e in range(N_EXPERTS):
        base = (bi * N_EXPERTS + e) * (ntile + 1) + t
        nwin = jnp.maximum(nwin, _num_windows(cnt_ref[base], cnt_ref[base + 1]))

    def overflow(w, _):
        acc_scr[...] += windows(w)
        return 0

    lax.fori_loop(1, nwin, overflow, 0)
    out = x_ref[...] + gt2_ref[...] * acc_scr[...]
    if final:
        out = out * lax.rsqrt(jnp.mean(out * out, axis=-1, keepdims=True) + EPS) * fg_ref[...]
    o_ref[...] = out


def _moe_combine(cnt, pos, x, gt2, y, final_g, final):
    b, n, d = x.shape
    ntile = n // MOE_TILE
    cap = y.shape[2]
    grid_spec = pltpu.PrefetchScalarGridSpec(
        num_scalar_prefetch=1, grid=(b, ntile),
        in_specs=[pl.BlockSpec((None, MOE_TILE, LANES), lambda bi, t, c: (bi, t, 0)),
                  pl.BlockSpec((None, MOE_TILE, d), lambda bi, t, c: (bi, t, 0)),
                  pl.BlockSpec((None, 1, d), lambda bi, t, c: (bi, 0, 0)),
                  pl.BlockSpec((None, N_EXPERTS, cap, d), lambda bi, t, c: (bi, 0, 0, 0),
                               pipeline_mode=pl.Buffered(1)),
                  pl.BlockSpec((1, d), lambda bi, t, c: (0, 0))],
        out_specs=pl.BlockSpec((None, MOE_TILE, d), lambda bi, t, c: (bi, t, 0)),
        scratch_shapes=[pltpu.VMEM((MOE_TILE, d), F32), pltpu.VMEM((N_EXPERTS * SLOT_CHUNK, d), BF16)])
    return pl.pallas_call(
        functools.partial(_moe_combine_kernel, ntile=ntile, cap=cap, final=final),
        out_shape=jax.ShapeDtypeStruct((b, n, d), F32),
        grid_spec=grid_spec, compiler_params=_cparams(("arbitrary", "arbitrary")), name="moe_combine",
    )(cnt, pos, x, gt2, y, final_g)


def _ec_moe(x_mid, h2, aff, gt2, wg, wu, wd, layer, final_g, final):
    b, n, _ = x_mid.shape
    cap = max(1, EC_FACTOR * n // N_EXPERTS)
    ntile = n // MOE_TILE
    pos, post, offs, afft = _route(aff, cap)
    per_tile = lambda a: a.transpose(0, 2, 1, 3).reshape(b, N_EXPERTS, ntile, 1, MOE_TILE)
    starts = offs[:, ::MOE_TILE // CHUNK, 0, :N_EXPERTS].transpose(0, 2, 1)
    cnt = jnp.concatenate([starts, jnp.full((b, N_EXPERTS, 1), cap, I32)], axis=2).reshape(-1)
    xe, gate = _moe_gather(cnt, per_tile(post), per_tile(afft), h2, cap)
    y = _moe_ffn(xe, gate, wg, wu, wd, layer)
    return _moe_combine(cnt, pos, x_mid, gt2, y, final_g, final)


def _rope_table(n):
    rows = n // GRID_W
    row = jnp.repeat(jnp.arange(rows), GRID_W).astype(F32)
    col = jnp.tile(jnp.arange(GRID_W), rows).astype(F32)
    nf = ATTN_DIM // 4
    inv = ROPE_BASE ** (-jnp.arange(nf, dtype=F32) / nf)
    ang = jnp.concatenate([row[:, None] * inv, col[:, None] * inv], axis=-1)
    cos, sin = jnp.cos(ang), jnp.sin(ang)
    reps = LANES // ATTN_DIM
    return jnp.concatenate([jnp.tile(jnp.concatenate([cos, cos], -1), (1, reps)),
                            jnp.tile(jnp.concatenate([-sin, sin], -1), (1, reps))], axis=-1)


def _pad_lanes(a):
    return jnp.pad(a, ((0, 0), (0, LANES - a.shape[1])))


def kernel(x, c, ctx, c_ctx, w_mod, b_mod, norm_mix_g, norm_ffn_g, final_norm_g, ab_w_in, ab_conv_w, ab_gate_b,
           ab_head_g, ab_sink, ab_w_out, gm_w_in, gm_ln_g, gm_ln_b, gm_w_s, gm_b_s, gm_w_out, moe_w_router,
           moe_w_gate, moe_w_up, moe_w_down):
    b, n, d = x.shape
    depth = w_mod.shape[0]
    assert depth <= 2, "context stream is only advanced for deeper stacks; not supported here"
    cond = jnp.zeros((SUBLANES, d), F32).at[:b].set(c).at[b].set(c_ctx)
    mods = _adaln(cond, w_mod, b_mod)

    def mod_rows(layer, rows):
        m = mods[layer, rows].reshape(-1, 6, 1, d)
        return [m[:, i] for i in range(6)]

    row = lambda v: v.reshape(1, -1)
    for layer in range(depth):
        sh1, sc1, gt1, sh2, sc2, gt2 = mod_rows(layer, slice(0, b))
        g1, g2 = row(norm_mix_g[layer]), row(norm_ffn_g[layer])
        wr = _hi_lo_rhs(_pad_lanes(moe_w_router[layer]))
        if layer % 2 == 0:
            e = layer // 2
            csh1, csc1 = (jnp.broadcast_to(v, (b, 1, d)) for v in mod_rows(layer, slice(b, b + 1))[:2])
            w_in = ab_w_in[e]
            g_lo = 4 * LSTM_WIDTH
            w_main = jnp.concatenate([w_in[:, :g_lo], w_in[:, g_lo + N_GATES:]], axis=1).astype(BF16)
            w_gate = _hi_lo_rhs(_pad_lanes(w_in[:, g_lo:g_lo + N_GATES]))
            p, gts, vt, avt = _modmm(x, g1, sc1, sh1, w_main, w_gate, tm=1024, name="ab_in")
            pc, gtc, vtc, avtc = _modmm(ctx, g1, csc1, csh1, w_main, w_gate, name="ab_in_ctx")
            hf, hb = _mlstm(_conv_silu(p, ab_conv_w[e]), vt, gts, _conv_silu(pc, ab_conv_w[e]), vtc, gtc,
                            _pad_lanes(row(ab_gate_b[e])))
            at = _attn(p, avt, pc, avtc, _rope_table(n), _pad_lanes(row(ab_sink[e])))
            x_mid, h2, aff = _ab_out(hf, hb, p, at, x, row(ab_head_g[e]), ab_w_out[e].astype(BF16),
                                     gt1, g2, sc2, sh2, wr)
        else:
            o = layer // 2
            uv = _modmm(x, g1, sc1, sh1, gm_w_in[o].astype(BF16), act="gelu", tm=1024, name="gm_in")
            x_mid, h2, aff = _gm_out(uv, x, row(gm_ln_g[o]), row(gm_ln_b[o]), gm_w_s[o].astype(BF16),
                                     _pad_lanes(gm_b_s[o].T), gm_w_out[o].astype(BF16), gt1, g2, sc2, sh2, wr)
        x = _ec_moe(x_mid, h2, aff, gt2, moe_w_gate, moe_w_up, moe_w_down, layer,
                    row(final_norm_g), layer == depth - 1)
    return x
```

```python
import functools

import jax
import jax.numpy as jnp
from jax import lax
from jax.experimental import pallas as pl
from jax.experimental.pallas import tpu as pltpu

F32 = jnp.float32
BF16 = jnp.bfloat16
I32 = jnp.int32
HI = lax.Precision.HIGHEST

D_MODEL = 1024
GRID_W = 64
EPS = 1e-6
LSTM_HEADS = 4
LSTM_DIM = 128
LSTM_WIDTH = LSTM_HEADS * LSTM_DIM
LSTM_CONV = 5
CHUNK = 128
ATTN_HEADS = 8
ATTN_KV_HEADS = 2
ATTN_GROUP = ATTN_HEADS // ATTN_KV_HEADS
ATTN_DIM = 64
ROPE_BASE = 10000.0
GM_GROUPS = 8
GM_HALF = 2 * D_MODEL
N_EXPERTS = 16
EC_FACTOR = 2
N_GATES = 4 * LSTM_HEADS

LANES = 128
SUBLANES = 8
BF16_ROWS = 16
VMEM_LIMIT_BYTES = 56 * 1024 * 1024

P_COLS = 4 * LSTM_WIDTH + ATTN_HEADS * ATTN_DIM + 2 * ATTN_KV_HEADS * ATTN_DIM
PB_V, PB_O, PB_AQ = 2, 3, 4
PB_AK, PB_AV = 20, 21
MOE_TILE = 256
SLOT_CHUNK = 64


def _cparams(sem, vmem=VMEM_LIMIT_BYTES):
    return pltpu.CompilerParams(dimension_semantics=sem, vmem_limit_bytes=vmem)


def _rms_mod(x, g, sc, sh):
    y = x * lax.rsqrt(jnp.mean(x * x, axis=-1, keepdims=True) + EPS)
    return y * g * (1.0 + sc) + sh


def _silu(x):
    return x * jax.nn.sigmoid(x)


def _gelu_tanh(x):
    return 0.5 * x * (1.0 + jnp.tanh(0.7978845608028654 * (x + 0.044715 * (x * x * x))))


def _log_sigmoid(x):
    return jnp.minimum(x, 0.0) - jnp.log(1.0 + jnp.exp(-jnp.abs(x)))


def _dot_t(a, b):
    return lax.dot_general(a, b, (((1,), (1,)), ((), ())), preferred_element_type=F32)


def _adaln_kernel(c_ref, w_ref, b_ref, o_ref):
    s = _silu(c_ref[...])
    o_ref[...] = jnp.dot(s, w_ref[...], precision=HI, preferred_element_type=F32) + b_ref[...]


def _adaln(cond, w_mod, b_mod):
    depth, d, six_d = w_mod.shape
    tn = six_d // 4
    return pl.pallas_call(
        _adaln_kernel,
        out_shape=jax.ShapeDtypeStruct((depth, SUBLANES, six_d), F32),
        grid=(depth, six_d // tn),
        in_specs=[pl.BlockSpec((SUBLANES, d), lambda l, j: (0, 0)),
                  pl.BlockSpec((None, d, tn), lambda l, j: (l, 0, j)),
                  pl.BlockSpec((None, 1, tn), lambda l, j: (l, 0, j))],
        out_specs=pl.BlockSpec((None, SUBLANES, tn), lambda l, j: (l, 0, j)),
        compiler_params=_cparams(("arbitrary", "arbitrary")),
        name="adaln",
    )(cond, w_mod, b_mod.reshape(depth, 1, six_d))


def _hi_lo_lhs(h, h_hi):
    return jnp.concatenate([h_hi, (h - h_hi.astype(F32)).astype(BF16), h_hi], axis=1)


def _hi_lo_rhs(w):
    w_hi = w.astype(BF16)
    return jnp.concatenate([w_hi, w_hi, (w - w_hi.astype(F32)).astype(BF16)], axis=0)


def _modmm_kernel(x_ref, g_ref, sc_ref, sh_ref, w_ref, *rest, chunks, act, with_gates):
    if with_gates:
        wg_ref, o_ref, og_ref, vt_ref, avt_ref = rest
    else:
        (o_ref,) = rest
    h = _rms_mod(x_ref[...], g_ref[...], sc_ref[...], sh_ref[...])
    hb = h.astype(BF16)
    for lo, hi in chunks:
        y = jnp.dot(hb, w_ref[:, lo:hi], preferred_element_type=F32)
        if act == "gelu":
            y = _gelu_tanh(y)
        o_ref[:, lo:hi] = y.astype(o_ref.dtype)
        if with_gates and lo == PB_V * LSTM_WIDTH:
            vt_ref[...] = y.T.astype(vt_ref.dtype)
        if with_gates and lo <= PB_AV * LANES < hi:
            av = y[:, PB_AV * LANES - lo:(PB_AV + 1) * LANES - lo]
            avt_ref[...] = av.T.astype(avt_ref.dtype)
    if with_gates:
        og_ref[...] = jnp.dot(_hi_lo_lhs(h, hb), wg_ref[...], preferred_element_type=F32)


def _modmm(x, g, sc, sh, w, wg=None, *, act=None, tm=512, chunk=512, name="modmm"):
    b, n, d = x.shape
    no = w.shape[1]
    tm = min(tm, n)
    assert chunk == LSTM_WIDTH
    chunks = tuple((lo, min(lo + chunk, no)) for lo in range(0, no, chunk))
    in_specs = [pl.BlockSpec((None, tm, d), lambda bi, i: (bi, i, 0)),
                pl.BlockSpec((1, d), lambda bi, i: (0, 0)),
                pl.BlockSpec((None, 1, d), lambda bi, i: (bi, 0, 0)),
                pl.BlockSpec((None, 1, d), lambda bi, i: (bi, 0, 0)),
                pl.BlockSpec((d, no), lambda bi, i: (0, 0))]
    out_shape = [jax.ShapeDtypeStruct((b, n, no), BF16)]
    out_specs = [pl.BlockSpec((None, tm, no), lambda bi, i: (bi, i, 0))]
    args = [x, g, sc, sh, w]
    if wg is not None:
        in_specs.append(pl.BlockSpec(wg.shape, lambda bi, i: (0, 0)))
        out_shape += [jax.ShapeDtypeStruct((b, n, LANES), F32), jax.ShapeDtypeStruct((b, LSTM_WIDTH, n), BF16),
                      jax.ShapeDtypeStruct((b, LANES, n), BF16)]
        out_specs += [pl.BlockSpec((None, tm, LANES), lambda bi, i: (bi, i, 0)),
                      pl.BlockSpec((None, LSTM_WIDTH, tm), lambda bi, i: (bi, 0, i)),
                      pl.BlockSpec((None, LANES, tm), lambda bi, i: (bi, 0, i))]
        args.append(wg)
    res = pl.pallas_call(
        functools.partial(_modmm_kernel, chunks=chunks, act=act, with_gates=wg is not None),
        out_shape=out_shape, grid=(b, n // tm), in_specs=in_specs, out_specs=out_specs,
        compiler_params=_cparams(("parallel", "parallel")), name=name,
    )(*args)
    return res if wg is not None else res[0]


def _conv_silu_kernel(x_ref, xp_ref, xn_ref, w_ref, o_ref, pad_scr):
    i = pl.program_id(1)
    rows = x_ref.shape[0]
    halo = BF16_ROWS
    has_prev = jnp.where(i > 0, 1.0, 0.0)
    has_next = jnp.where(i < pl.num_programs(1) - 1, 1.0, 0.0)
    pad_scr[pl.ds(0, halo), :] = (xp_ref[...].astype(F32) * has_prev).astype(BF16)
    pad_scr[pl.ds(halo, rows), :] = x_ref[...]
    pad_scr[pl.ds(halo + rows, halo), :] = (xn_ref[...].astype(F32) * has_next).astype(BF16)
    w = w_ref[...]
    win = CHUNK + 2 * halo
    r = lax.broadcasted_iota(I32, (CHUNK, win), 0)
    c = lax.broadcasted_iota(I32, (CHUNK, win), 1)
    half = LSTM_CONV // 2
    shifts = {t: jnp.where(c == r + halo + t - half, 1.0, 0.0).astype(BF16) for t in range(LSTM_CONV) if t != half}
    scale = LSTM_DIM ** -0.5
    for blk in range(rows // CHUNK):
        xw = pad_scr[pl.ds(blk * CHUNK, win), :]
        acc = xw[halo:halo + CHUNK].astype(F32) * w[half:half + 1, :]
        for t, s in shifts.items():
            acc = acc + jnp.dot(s, xw, preferred_element_type=F32) * w[t:t + 1, :]
        y = _silu(acc)
        out = pl.ds(blk * CHUNK, CHUNK)
        o_ref[out, :LSTM_WIDTH] = y[:, :LSTM_WIDTH].astype(o_ref.dtype)
        o_ref[out, LSTM_WIDTH:] = (y[:, LSTM_WIDTH:] * scale).astype(o_ref.dtype)


def _conv_silu(p, conv_w, tm=512):
    b, n, _ = p.shape
    tm = min(tm, n)
    qkw = 2 * LSTM_WIDTH
    hpt = tm // BF16_ROWS
    nhb = n // BF16_ROWS
    return pl.pallas_call(
        _conv_silu_kernel,
        out_shape=jax.ShapeDtypeStruct((b, n, qkw), BF16),
        grid=(b, n // tm),
        in_specs=[pl.BlockSpec((None, tm, qkw), lambda bi, i: (bi, i, 0)),
                  pl.BlockSpec((None, BF16_ROWS, qkw), lambda bi, i: (bi, jnp.maximum(i * hpt - 1, 0), 0)),
                  pl.BlockSpec((None, BF16_ROWS, qkw), lambda bi, i: (bi, jnp.minimum((i + 1) * hpt, nhb - 1), 0)),
                  pl.BlockSpec((LSTM_CONV, qkw), lambda bi, i: (0, 0))],
        out_specs=pl.BlockSpec((None, tm, qkw), lambda bi, i: (bi, i, 0)),
        scratch_shapes=[pltpu.VMEM((tm + 2 * BF16_ROWS, qkw), BF16)],
        compiler_params=_cparams(("parallel", "parallel")), name="conv_silu",
    )(p, p, p, conv_w)


def _tri(n, lower):
    r = lax.broadcasted_iota(I32, (n, n), 0)
    c = lax.broadcasted_iota(I32, (n, n), 1)
    return (c <= r) if lower else (c >= r)


STATE_ROWS = LSTM_DIM + BF16_ROWS
MLSTM_STEP_CHUNKS = 4


def _mlstm_segment(d, gates, k_all, vt_all, q_all, state, ht_ref, cols):
    first = state is None
    new_state = []
    seg = gates.shape[0]
    r = lax.broadcasted_iota(I32, (seg, seg), 0)
    c = lax.broadcasted_iota(I32, (seg, seg), 1)
    before = (r <= c) if d == 0 else (r >= c)
    tri = jnp.where(before, 1.0, 0.0).astype(F32)
    gates_t = gates.T[:N_GATES]
    bcum_t = jnp.dot(_log_sigmoid(gates_t), tri, precision=HI, preferred_element_type=F32)
    last = seg - 1 if d == 0 else 0
    for h in range(LSTM_HEADS):
        ci, cf = 2 * d * LSTM_HEADS + h, (2 * d + 1) * LSTM_HEADS + h
        hd = slice(h * LSTM_DIM, (h + 1) * LSTM_DIM)
        kb, vt = k_all[:, hd], vt_all[hd, :]
        b_row, li_row = bcum_t[cf:cf + 1, :], gates_t[ci:ci + 1, :]
        cn, m_prev = (None, jnp.zeros((1, 1), F32)) if first else state[h]
        if q_all is not None:
            qb = q_all[:, hd]
            per_key = jnp.broadcast_to(li_row - b_row, (seg, seg)).T
            log_d = jnp.where(before, b_row + per_key, -jnp.inf)
            m_row = jnp.maximum(b_row + m_prev, jnp.max(log_d, axis=0, keepdims=True))
            sm = _dot_t(kb, qb) * jnp.exp(log_d - m_row)
            a = jnp.exp(b_row + m_prev - m_row)
            qc = _dot_t(cn.astype(BF16), qb)
            num = jnp.dot(vt, sm.astype(BF16), preferred_element_type=F32) + a * qc[:LSTM_DIM]
            den = jnp.sum(sm, axis=0, keepdims=True) + a * qc[LSTM_DIM:LSTM_DIM + 1]
            ht_ref[hd, cols] = num / jnp.maximum(jnp.abs(den), jnp.exp(-m_row))
        g = b_row[:, last:last + 1]
        w = g - b_row + li_row
        m_new = jnp.maximum(g + m_prev, jnp.max(w, axis=1, keepdims=True))
        wt = jnp.exp(w - m_new)
        aug = jnp.concatenate([(vt.astype(F32) * wt).astype(BF16),
                               jnp.broadcast_to(wt, (BF16_ROWS, seg)).astype(BF16)], axis=0)
        upd = jnp.dot(aug, kb, preferred_element_type=F32)
        new_state.append((upd if first else jnp.exp(g + m_prev - m_new) * cn + upd, m_new))
    return new_state


def _mlstm_kernel(qkf_ref, qkb_ref, vtf_ref, vtb_ref, gf_ref, gb_ref, kc_ref, vtc_ref, gc_ref, gbias_ref,
                  hf_ref, hb_ref, cn_scr, m_scr):
    gbias = gbias_ref[...]

    def save(d, state):
        for h, (cn, m) in enumerate(state):
            cn_scr[d * LSTM_HEADS + h] = cn
            m_scr[d * LSTM_HEADS + h] = jnp.broadcast_to(m, (1, LANES))

    @pl.when(pl.program_id(1) == 0)
    def _():
        gates = gc_ref[...] + gbias
        for d in range(2):
            save(d, _mlstm_segment(d, gates, kc_ref[...], vtc_ref[...], None, None, None, None))

    states = [[(cn_scr[d * LSTM_HEADS + h], m_scr[d * LSTM_HEADS + h][:, 0:1]) for h in range(LSTM_HEADS)]
              for d in range(2)]
    per_step = qkf_ref.shape[0] // CHUNK
    for u in range(per_step):
        for d, (qk_ref, vt_ref, g_ref, h_ref) in enumerate(((qkf_ref, vtf_ref, gf_ref, hf_ref),
                                                            (qkb_ref, vtb_ref, gb_ref, hb_ref))):
            j = u if d == 0 else per_step - 1 - u
            rows = slice(j * CHUNK, (j + 1) * CHUNK)
            qk = qk_ref[rows, :]
            states[d] = _mlstm_segment(d, g_ref[rows, :] + gbias, qk[:, LSTM_WIDTH:], vt_ref[:, rows],
                                       qk[:, :LSTM_WIDTH], states[d], h_ref, rows)
    for d in range(2):
        save(d, states[d])


def _mlstm(qk, vt, g, qkc, vtc, gc, gate_bias):
    b, n, _ = qk.shape
    lc = qkc.shape[1]
    rows = MLSTM_STEP_CHUNKS * CHUNK
    nc = n // rows
    qkw = 2 * LSTM_WIDTH
    in_specs = [
        pl.BlockSpec((None, rows, qkw), lambda bi, c: (bi, c, 0)),
        pl.BlockSpec((None, rows, qkw), lambda bi, c: (bi, nc - 1 - c, 0)),
        pl.BlockSpec((None, LSTM_WIDTH, rows), lambda bi, c: (bi, 0, c)),
        pl.BlockSpec((None, LSTM_WIDTH, rows), lambda bi, c: (bi, 0, nc - 1 - c)),
        pl.BlockSpec((None, rows, LANES), lambda bi, c: (bi, c, 0)),
        pl.BlockSpec((None, rows, LANES), lambda bi, c: (bi, nc - 1 - c, 0)),
        pl.BlockSpec((None, lc, LSTM_WIDTH), lambda bi, c: (bi, 0, 1)),
        pl.BlockSpec((None, LSTM_WIDTH, lc), lambda bi, c: (bi, 0, 0)),
        pl.BlockSpec((None, lc, LANES), lambda bi, c: (bi, 0, 0)),
        pl.BlockSpec((1, LANES), lambda bi, c: (0, 0))]
    out_specs = [pl.BlockSpec((None, LSTM_WIDTH, rows), lambda bi, c: (bi, 0, c)),
                 pl.BlockSpec((None, LSTM_WIDTH, rows), lambda bi, c: (bi, 0, nc - 1 - c))]
    return pl.pallas_call(
        _mlstm_kernel,
        out_shape=[jax.ShapeDtypeStruct((b, LSTM_WIDTH, n), F32)] * 2,
        grid=(b, nc), in_specs=in_specs, out_specs=out_specs,
        scratch_shapes=[pltpu.VMEM((2 * LSTM_HEADS, STATE_ROWS, LSTM_DIM), F32),
                        pltpu.VMEM((2 * LSTM_HEADS, 1, LANES), F32)],
        compiler_params=_cparams(("arbitrary", "arbitrary")), name="mlstm",
    )(qk, qk, vt, vt, g, g, qkc, vtc, gc, gate_bias)


def _rope(x, cos, sin_signed):
    w = x.shape[1]
    lane = lax.broadcasted_iota(I32, x.shape, 1)
    first = (lane & (ATTN_DIM - 1)) < (ATTN_DIM // 2)
    partner = jnp.where(first, pltpu.roll(x, w - ATTN_DIM // 2, 1), pltpu.roll(x, ATTN_DIM // 2, 1))
    return x * cos + partner * sin_signed


def _attn_kernel(q_ref, kp_ref, kc_ref, kn_ref, vtp_ref, vtc_ref, vtn_ref, kctx_ref, vtctx_ref,
                 tp_ref, tc_ref, tn_ref, sink_ref, bias_ref, o_ref):
    def table(t_ref):
        t = t_ref[...]
        return t[:, :LANES], t[:, LANES:]

    cos_c, sin_c = table(tc_ref)
    q = _rope(q_ref[...].astype(F32), jnp.concatenate([cos_c] * 4, axis=1), jnp.concatenate([sin_c] * 4, axis=1))
    q = q * (ATTN_DIM ** -0.5)
    ks = []
    for k_ref, t_ref in ((kp_ref, tp_ref), (kc_ref, tc_ref), (kn_ref, tn_ref)):
        cos_t, sin_t = table(t_ref)
        ks.append(_rope(k_ref[...].astype(F32), cos_t, sin_t).astype(BF16))
    k_all = jnp.concatenate(ks + [kctx_ref[...]], axis=0)
    vt_all = jnp.concatenate([vtp_ref[...], vtc_ref[...], vtn_ref[...], vtctx_ref[...]], axis=1)

    bias = jnp.concatenate([bias_ref[...]] * ATTN_GROUP, axis=1)
    lane = lax.broadcasted_iota(I32, (CHUNK, LANES), 1)
    sink = sink_ref[...]

    for g in range(ATTN_KV_HEADS):
        half_g = (lane >= ATTN_DIM) if g == 1 else (lane < ATTN_DIM)
        qs, snk = [], []
        for r in range(ATTN_GROUP):
            h = g * ATTN_GROUP + r
            t = q[:, (h // 2) * LANES:(h // 2 + 1) * LANES]
            if h % 2 != g:
                t = pltpu.roll(t, ATTN_DIM, 1)
            qs.append(jnp.where(half_g, t, 0.0).astype(BF16))
            snk.append(jnp.broadcast_to(sink[:, h:h + 1], (1, CHUNK)))
        snk = jnp.concatenate(snk, axis=1)
        st = _dot_t(k_all, jnp.concatenate(qs, axis=0)) + bias
        m = jnp.maximum(jnp.max(st, axis=0, keepdims=True), snk)
        e = jnp.exp(st - m)
        den = jnp.sum(e, axis=0, keepdims=True) + jnp.exp(snk - m)
        pv = jnp.dot(vt_all, e.astype(BF16), preferred_element_type=F32)
        o = (pv[g * ATTN_DIM:(g + 1) * ATTN_DIM] / den).astype(o_ref.dtype)
        for r in range(ATTN_GROUP):
            h = g * ATTN_GROUP + r
            o_ref[h * ATTN_DIM:(h + 1) * ATTN_DIM, :] = o[:, r * CHUNK:(r + 1) * CHUNK]


def _attn_bias(nctx):
    i = jnp.arange(CHUNK)[None, :]
    j = jnp.arange(3 * CHUNK)[:, None]
    band = (j >= i) & (j <= i + 2 * CHUNK)
    local = jnp.stack([band & (j >= CHUNK), band, band & (j < 2 * CHUNK)])
    return jnp.concatenate([jnp.where(local, 0.0, -jnp.inf).astype(F32), jnp.zeros((3, nctx, CHUNK), F32)], axis=1)


def _attn(p, avt, pc, avtc, table, sink):
    b, n, _ = p.shape
    lc = pc.shape[1]
    nb = n // CHUNK
    assert nb >= 2
    qw = ATTN_HEADS * ATTN_DIM
    bias = _attn_bias(lc)

    def blk(col, off):
        return pl.BlockSpec((None, CHUNK, LANES), lambda bi, i: (bi, jnp.clip(i + off, 0, nb - 1), col))

    def vblk(off):
        return pl.BlockSpec((None, LANES, CHUNK), lambda bi, i: (bi, 0, jnp.clip(i + off, 0, nb - 1)))

    def tab(off):
        return pl.BlockSpec((CHUNK, 2 * LANES), lambda bi, i: (jnp.clip(i + off, 0, nb - 1), 0))

    in_specs = [pl.BlockSpec((None, CHUNK, qw), lambda bi, i: (bi, i, PB_AQ)),
                blk(PB_AK, -1), blk(PB_AK, 0), blk(PB_AK, 1), vblk(-1), vblk(0), vblk(1),
                pl.BlockSpec((None, lc, LANES), lambda bi, i: (bi, 0, PB_AK)),
                pl.BlockSpec((None, LANES, lc), lambda bi, i: (bi, 0, 0)),
                tab(-1), tab(0), tab(1),
                pl.BlockSpec((1, LANES), lambda bi, i: (0, 0)),
                pl.BlockSpec((None,) + bias.shape[1:],
                             lambda bi, i: (jnp.where(i == 0, 0, jnp.where(i == nb - 1, 2, 1)), 0, 0))]
    return pl.pallas_call(
        _attn_kernel,
        out_shape=jax.ShapeDtypeStruct((b, qw, n), BF16),
        grid=(b, nb), in_specs=in_specs,
        out_specs=pl.BlockSpec((None, qw, CHUNK), lambda bi, i: (bi, 0, i)),
        compiler_params=_cparams(("parallel", "parallel")), name="window_attn",
    )(p, p, p, p, avt, avt, avt, pc, avtc, table, table, table, sink, bias)


def _router_tail(x_new, g2, sc2, sh2, wr_ref, x_out_ref, h2_ref, aff_ref):
    x_out_ref[...] = x_new
    h2 = _rms_mod(x_new, g2, sc2, sh2)
    h_hi = h2.astype(BF16)
    h2_ref[...] = h_hi
    logits = jnp.dot(_hi_lo_lhs(h2, h_hi), wr_ref[...], preferred_element_type=F32)
    lane = lax.broadcasted_iota(I32, logits.shape, 1)
    logits = jnp.where(lane < N_EXPERTS, logits, -jnp.inf)
    e = jnp.exp(logits - jnp.max(logits, axis=1, keepdims=True))
    aff_ref[...] = e / jnp.sum(e, axis=1, keepdims=True)


def _ab_out_kernel(hf_ref, hb_ref, o_ref, at_ref, x_ref, hg_ref, wo_ref, gt1_ref, g2_ref, sc2_ref, sh2_ref, wr_ref,
                   x_out_ref, h2_ref, aff_ref):
    hsum = (hf_ref[...] + hb_ref[...]).T
    og = jax.nn.sigmoid(o_ref[...].astype(F32))
    hg = hg_ref[...]
    parts = []
    for h in range(LSTM_HEADS):
        sl = slice(h * LSTM_DIM, (h + 1) * LSTM_DIM)
        seg = hsum[:, sl]
        seg = seg * lax.rsqrt(jnp.mean(seg * seg, axis=-1, keepdims=True) + EPS)
        parts.append((seg * hg[:, sl] * og[:, sl]).astype(BF16))
    cat = jnp.concatenate(parts + [at_ref[...].astype(F32).T.astype(BF16)], axis=1)
    y = jnp.dot(cat, wo_ref[...], preferred_element_type=F32)
    _router_tail(x_ref[...] + gt1_ref[...] * y, g2_ref[...], sc2_ref[...], sh2_ref[...], wr_ref,
                 x_out_ref, h2_ref, aff_ref)


def _tail_out(b, n, d, tm):
    shapes = [jax.ShapeDtypeStruct((b, n, d), F32), jax.ShapeDtypeStruct((b, n, d), BF16),
              jax.ShapeDtypeStruct((b, n, LANES), F32)]
    specs = [pl.BlockSpec((None, tm, d), lambda bi, i: (bi, i, 0)),
             pl.BlockSpec((None, tm, d), lambda bi, i: (bi, i, 0)),
             pl.BlockSpec((None, tm, LANES), lambda bi, i: (bi, i, 0))]
    return shapes, specs


def _ab_out(hf, hb, p, at, x, head_g, w_out, gt1, g2, sc2, sh2, wr, tm=256):
    b, n, d = x.shape
    row = lambda w: pl.BlockSpec((None, tm, w), lambda bi, i: (bi, i, 0))
    vec = pl.BlockSpec((None, 1, d), lambda bi, i: (bi, 0, 0))
    const = lambda s: pl.BlockSpec(s, lambda bi, i: (0, 0))
    scan_out = pl.BlockSpec((None, LSTM_WIDTH, tm), lambda bi, i: (bi, 0, i))
    in_specs = [scan_out, scan_out,
                pl.BlockSpec((None, tm, LSTM_WIDTH), lambda bi, i: (bi, i, PB_O)),
                pl.BlockSpec((None, ATTN_HEADS * ATTN_DIM, tm), lambda bi, i: (bi, 0, i)),
                row(d), const((1, LSTM_WIDTH)), const(w_out.shape),
                vec, const((1, d)), vec, vec, const((3 * d, LANES))]
    shapes, specs = _tail_out(b, n, d, tm)
    return pl.pallas_call(
        _ab_out_kernel, out_shape=shapes, grid=(b, n // tm), in_specs=in_specs, out_specs=specs,
        compiler_params=_cparams(("parallel", "parallel")), name="ab_out",
    )(hf, hb, p, at, x, head_g, w_out, gt1, g2, sc2, sh2, wr)


def _gm_out_kernel(uv_ref, x_ref, lng_ref, lnb_ref, ws_ref, bs_ref, wo_ref, gt1_ref, g2_ref, sc2_ref, sh2_ref, wr_ref,
                   x_out_ref, h2_ref, aff_ref):
    tm = uv_ref.shape[0]
    gw = GM_HALF // GM_GROUPS
    v = uv_ref[:, GM_HALF:].astype(F32)
    mu = jnp.mean(v, axis=-1, keepdims=True)
    vc = v - mu
    var = jnp.mean(vc * vc, axis=-1, keepdims=True)
    vn = (vc * lax.rsqrt(var + EPS) * lng_ref[...] + lnb_ref[...]).astype(BF16)
    zs = []
    for ch in range(tm // CHUNK):
        rows = slice(ch * CHUNK, (ch + 1) * CHUNK)
        cols = []
        for g in range(GM_GROUPS):
            sv = jnp.dot(ws_ref[g], vn[rows, g * gw:(g + 1) * gw], preferred_element_type=F32)
            cols.append(sv + bs_ref[:, g:g + 1])
        sv = jnp.concatenate(cols, axis=1)
        zs.append((uv_ref[rows, :GM_HALF].astype(F32) * sv).astype(BF16))
    z = jnp.concatenate(zs, axis=0)
    y = jnp.dot(z, wo_ref[...], preferred_element_type=F32)
    _router_tail(x_ref[...] + gt1_ref[...] * y, g2_ref[...], sc2_ref[...], sh2_ref[...], wr_ref,
                 x_out_ref, h2_ref, aff_ref)


def _gm_out(uv, x, ln_g, ln_b, w_s, b_s_t, w_out, gt1, g2, sc2, sh2, wr, tm=256):
    b, n, d = x.shape
    vec = pl.BlockSpec((None, 1, d), lambda bi, i: (bi, 0, 0))
    const = lambda s: pl.BlockSpec(s, lambda *_: (0,) * len(s))
    in_specs = [pl.BlockSpec((None, tm, 2 * GM_HALF), lambda bi, i: (bi, i, 0)),
                pl.BlockSpec((None, tm, d), lambda bi, i: (bi, i, 0)),
                const((1, GM_HALF)), const((1, GM_HALF)), const(w_s.shape), const(b_s_t.shape), const(w_out.shape),
                vec, const((1, d)), vec, vec, const((3 * d, LANES))]
    shapes, specs = _tail_out(b, n, d, tm)
    return pl.pallas_call(
        _gm_out_kernel, out_shape=shapes, grid=(b, n // tm), in_specs=in_specs, out_specs=specs,
        compiler_params=_cparams(("parallel", "parallel")), name="gm_out",
    )(uv, x, ln_g, ln_b, w_s, b_s_t, w_out, gt1, g2, sc2, sh2, wr)


def _route_kernel(aff_ref, pos_ref, post_ref, offs_ref, afft_ref, *, cap):
    n = aff_ref.shape[0]
    nblk = n // CHUNK

    def to_expert_major(k, _):
        rows = pl.ds(pl.multiple_of(k * CHUNK, CHUNK), CHUNK)
        afft_ref[k] = aff_ref[rows, :].T[:N_EXPERTS, :]
        return 0

    lax.fori_loop(0, nblk, to_expert_major, 0, unroll=4)

    def count(pred):
        per_lane = jnp.sum(jnp.where(pred, 1.0, 0.0), axis=0)
        return jnp.sum(per_lane, axis=1, keepdims=True)

    def search(i, prefix):
        cand = prefix | jnp.left_shift(jnp.int32(1), 30 - i)
        cand_f = lax.bitcast_convert_type(cand, F32)
        return jnp.where(count(afft_ref[...] >= cand_f[None]) >= cap, cand, prefix)

    thr_col = lax.bitcast_convert_type(lax.fori_loop(0, 31, search, jnp.zeros((N_EXPERTS, 1), I32)), F32)
    need_col = cap - count(afft_ref[...] > thr_col[None])

    def to_row(col):
        full = jnp.concatenate([jnp.broadcast_to(col, (N_EXPERTS, LANES)),
                                jnp.zeros((LANES - N_EXPERTS, LANES), F32)], axis=0)
        return full.T[0:1, :]

    thr, need = to_row(thr_col), to_row(need_col)
    tril = jnp.where(_tri(CHUNK, True), 1.0, 0.0).astype(BF16)

    def block(k, carry):
        run_tie, run_sel = carry
        rows = pl.ds(pl.multiple_of(k * CHUNK, CHUNK), CHUNK)
        a = aff_ref[rows, :]
        gt = a > thr
        tie = jnp.where(a == thr, 1.0, 0.0)
        tie_incl = jnp.dot(tril, tie.astype(BF16), preferred_element_type=F32)
        sel = jnp.where(gt | ((tie > 0.0) & (tie_incl - tie + run_tie < need)), 1.0, 0.0)
        sel_incl = jnp.dot(tril, sel.astype(BF16), preferred_element_type=F32)
        pos = jnp.where(sel > 0.0, sel_incl - sel + run_sel, -1.0)
        pos_ref[rows, :] = pos.astype(I32)
        post_ref[k] = pos.T[:N_EXPERTS, :].astype(I32)
        offs_ref[k] = run_sel.astype(I32)
        return (run_tie + tie_incl[CHUNK - 1:CHUNK, :], run_sel + sel_incl[CHUNK - 1:CHUNK, :])

    zero = jnp.zeros((1, LANES), F32)
    lax.fori_loop(0, nblk, block, (zero, zero), unroll=4)


def _route(aff, cap):
    b, n, _ = aff.shape
    nblk = n // CHUNK
    return pl.pallas_call(
        functools.partial(_route_kernel, cap=cap),
        out_shape=[jax.ShapeDtypeStruct((b, n, LANES), I32),
                   jax.ShapeDtypeStruct((b, nblk, N_EXPERTS, CHUNK), I32),
                   jax.ShapeDtypeStruct((b, nblk, 1, LANES), I32),
                   jax.ShapeDtypeStruct((b, nblk, N_EXPERTS, CHUNK), F32)],
        grid=(b,),
        in_specs=[pl.BlockSpec((None, n, LANES), lambda bi: (bi, 0, 0))],
        out_specs=[pl.BlockSpec((None, n, LANES), lambda bi: (bi, 0, 0)),
                   pl.BlockSpec((None, nblk, N_EXPERTS, CHUNK), lambda bi: (bi, 0, 0, 0)),
                   pl.BlockSpec((None, nblk, 1, LANES), lambda bi: (bi, 0, 0, 0)),
                   pl.BlockSpec((None, nblk, N_EXPERTS, CHUNK), lambda bi: (bi, 0, 0, 0))],
        compiler_params=_cparams(("parallel",)), name="route",
    )(aff)


def _window_start(s0, w, cap):
    lo = ((s0 >> 4) << 4) + w * SLOT_CHUNK
    return lo, pl.multiple_of(jnp.minimum(lo, cap - SLOT_CHUNK), BF16_ROWS)


def _num_windows(s0, s1):
    return (s1 - ((s0 >> 4) << 4) + SLOT_CHUNK - 1) >> (SLOT_CHUNK.bit_length() - 1)


def _moe_gather_kernel(cnt_ref, post_ref, afft_ref, h_ref, xe_ref, gate_ref, *, ntile, cap, group, tiles_per_step):
    bi, eg, ts = pl.program_id(0), pl.program_id(1), pl.program_id(2)

    @pl.when(ts == 0)
    def _():
        xe_ref[...] = jnp.zeros_like(xe_ref)
        gate_ref[...] = jnp.zeros_like(gate_ref)

    slot = lax.broadcasted_iota(I32, (SLOT_CHUNK, MOE_TILE), 0)
    for u in range(tiles_per_step):
        htile = h_ref[u * MOE_TILE:(u + 1) * MOE_TILE, :]
        t = ts * tiles_per_step + u
        bases = [(bi * N_EXPERTS + eg * group + g) * (ntile + 1) + t for g in range(group)]

        def windows(w, _, u=u, htile=htile, bases=bases):
            starts, onehots = [], []
            for g in range(group):
                lo, start = _window_start(cnt_ref[bases[g]], w, cap)
                posrow = post_ref[g, u]
                hit = (posrow - start == slot) & (posrow >= lo)
                onehots.append(jnp.where(hit, 1.0, 0.0).astype(BF16))
                gates = jnp.sum(jnp.where(hit, afft_ref[g, u], 0.0), axis=1, keepdims=True)
                dst = pl.ds(start, SLOT_CHUNK)
                gate_ref[g, dst, :] = gate_ref[g, dst, :] + jnp.broadcast_to(gates, (SLOT_CHUNK, LANES))
                starts.append(start)
            rows = jnp.dot(jnp.concatenate(onehots, axis=0), htile, preferred_element_type=F32)
            for g in range(group):
                dst = pl.ds(starts[g], SLOT_CHUNK)
                xe_ref[g, dst, :] = xe_ref[g, dst, :] + rows[g * SLOT_CHUNK:(g + 1) * SLOT_CHUNK].astype(BF16)
            return 0

        windows(0, 0)
        nwin = 1
        for g in range(group):
            nwin = jnp.maximum(nwin, _num_windows(cnt_ref[bases[g]], cnt_ref[bases[g] + 1]))
        lax.fori_loop(1, nwin, windows, 0)


def _moe_gather(cnt, post, afft, h2, cap, group=8, tiles_per_step=4):
    b, n, d = h2.shape
    ntile = n // MOE_TILE
    per_tile = pl.BlockSpec((None, group, tiles_per_step, 1, MOE_TILE), lambda bi, eg, ts, c: (bi, eg, ts, 0, 0))
    grid_spec = pltpu.PrefetchScalarGridSpec(
        num_scalar_prefetch=1, grid=(b, N_EXPERTS // group, ntile // tiles_per_step),
        in_specs=[per_tile, per_tile,
                  pl.BlockSpec((None, tiles_per_step * MOE_TILE, d), lambda bi, eg, ts, c: (bi, ts, 0))],
        out_specs=[pl.BlockSpec((None, group, cap, d), lambda bi, eg, ts, c: (bi, eg, 0, 0)),
                   pl.BlockSpec((None, group, cap, LANES), lambda bi, eg, ts, c: (bi, eg, 0, 0))])
    return pl.pallas_call(
        functools.partial(_moe_gather_kernel, ntile=ntile, cap=cap, group=group, tiles_per_step=tiles_per_step),
        out_shape=[jax.ShapeDtypeStruct((b, N_EXPERTS, cap, d), BF16),
                   jax.ShapeDtypeStruct((b, N_EXPERTS, cap, LANES), F32)],
        grid_spec=grid_spec, compiler_params=_cparams(("arbitrary", "arbitrary", "arbitrary")), name="moe_gather",
    )(cnt, post, afft, h2)


def _moe_ffn_kernel(xe_ref, gate_ref, wg_ref, wu_ref, wd_ref, y_ref, acc_scr, *, hid_tile):
    j = pl.program_id(2)
    xe = xe_ref[...]
    for k in range(wg_ref.shape[1] // hid_tile):
        cols = slice(k * hid_tile, (k + 1) * hid_tile)
        gate = jnp.dot(xe, wg_ref[:, cols].astype(BF16), preferred_element_type=F32)
        up = jnp.dot(xe, wu_ref[:, cols].astype(BF16), preferred_element_type=F32)
        hid = (_silu(gate) * up).astype(BF16)
        part = jnp.dot(hid, wd_ref[cols, :].astype(BF16), preferred_element_type=F32)
        if k == 0:
            @pl.when(j == 0)
            def _():
                acc_scr[...] = part

            @pl.when(j != 0)
            def _():
                acc_scr[...] += part
        else:
            acc_scr[...] += part

    @pl.when(j == pl.num_programs(2) - 1)
    def _():
        y_ref[...] = (acc_scr[...] * gate_ref[:, 0:1]).astype(y_ref.dtype)


def _moe_ffn(xe, gate, wg, wu, wd, layer, hid_split=1):
    b, ne, cap, d = xe.shape
    dh = wg.shape[3] // hid_split
    return pl.pallas_call(
        functools.partial(_moe_ffn_kernel, hid_tile=256),
        out_shape=jax.ShapeDtypeStruct((b, ne, cap, d), BF16),
        grid=(b, ne, hid_split),
        in_specs=[pl.BlockSpec((None, None, cap, d), lambda bi, e, j: (bi, e, 0, 0)),
                  pl.BlockSpec((None, None, cap, LANES), lambda bi, e, j: (bi, e, 0, 0)),
                  pl.BlockSpec((None, None, d, dh), lambda bi, e, j: (layer, e, 0, j)),
                  pl.BlockSpec((None, None, d, dh), lambda bi, e, j: (layer, e, 0, j)),
                  pl.BlockSpec((None, None, dh, d), lambda bi, e, j: (layer, e, j, 0))],
        out_specs=pl.BlockSpec((None, None, cap, d), lambda bi, e, j: (bi, e, 0, 0)),
        scratch_shapes=[pltpu.VMEM((cap, d), F32)],
        compiler_params=_cparams(("parallel", "parallel", "arbitrary")), name="moe_ffn",
    )(xe, gate, wg, wu, wd)


def _moe_combine_kernel(cnt_ref, pos_ref, x_ref, gt2_ref, y_ref, fg_ref, o_ref, acc_scr, ycat_scr, *, ntile, cap, final):
    bi, t = pl.program_id(0), pl.program_id(1)
    kw = N_EXPERTS * SLOT_CHUNK
    shift = SLOT_CHUNK.bit_length() - 1
    sel = jnp.where(lax.broadcasted_iota(I32, (LANES, kw), 0) == (lax.broadcasted_iota(I32, (LANES, kw), 1) >> shift),
                    1.0, 0.0).astype(BF16)

    def spread(v):
        hi = (v >> 5).astype(F32).astype(BF16)
        lo = (v & 31).astype(F32).astype(BF16)
        return 32.0 * jnp.dot(hi, sel, preferred_element_type=F32) + jnp.dot(lo, sel, preferred_element_type=F32)

    pos_w = spread(pos_ref[...] + 1) - 1.0
    within = (lax.broadcasted_iota(I32, (1, kw), 1) & (SLOT_CHUNK - 1)).astype(F32)
    lane = lax.broadcasted_iota(I32, (SUBLANES, LANES), 1)

    def windows(w):
        starts = jnp.zeros((SUBLANES, LANES), I32)
        los = jnp.zeros((SUBLANES, LANES), I32)
        for e in range(N_EXPERTS):
            lo, start = _window_start(cnt_ref[(bi * N_EXPERTS + e) * (ntile + 1) + t], w, cap)
            starts = jnp.where(lane == e, start, starts)
            los = jnp.where(lane == e, lo, los)
            ycat_scr[e * SLOT_CHUNK:(e + 1) * SLOT_CHUNK, :] = y_ref[e, pl.ds(start, SLOT_CHUNK), :]
        want = spread(starts)[0:1] + within
        onehot = jnp.where((pos_w == want) & (pos_w >= spread(los)[0:1]), 1.0, 0.0).astype(BF16)
        return jnp.dot(onehot, ycat_scr[...], preferred_element_type=F32)

    acc_scr[...] = windows(0)
    nwin = 1
    for e in range(N_EXPERTS):
        base = (bi * N_EXPERTS + e) * (ntile + 1) + t
        nwin = jnp.maximum(nwin, _num_windows(cnt_ref[base], cnt_ref[base + 1]))

    def overflow(w, _):
        acc_scr[...] += windows(w)
        return 0

    lax.fori_loop(1, nwin, overflow, 0)
    out = x_ref[...] + gt2_ref[...] * acc_scr[...]
    if final:
        out = out * lax.rsqrt(jnp.mean(out * out, axis=-1, keepdims=True) + EPS) * fg_ref[...]
    o_ref[...] = out


def _moe_combine(cnt, pos, x, gt2, y, final_g, final):
    b, n, d = x.shape
    ntile = n // MOE_TILE
    cap = y.shape[2]
    grid_spec = pltpu.PrefetchScalarGridSpec(
        num_scalar_prefetch=1, grid=(b, ntile),
        in_specs=[pl.BlockSpec((None, MOE_TILE, LANES), lambda bi, t, c: (bi, t, 0)),
                  pl.BlockSpec((None, MOE_TILE, d), lambda bi, t, c: (bi, t, 0)),
                  pl.BlockSpec((None, 1, d), lambda bi, t, c: (bi, 0, 0)),
                  pl.BlockSpec((None, N_EXPERTS, cap, d), lambda bi, t, c: (bi, 0, 0, 0),
                               pipeline_mode=pl.Buffered(1)),
                  pl.BlockSpec((1, d), lambda bi, t, c: (0, 0))],
        out_specs=pl.BlockSpec((None, MOE_TILE, d), lambda bi, t, c: (bi, t, 0)),
        scratch_shapes=[pltpu.VMEM((MOE_TILE, d), F32), pltpu.VMEM((N_EXPERTS * SLOT_CHUNK, d), BF16)])
    return pl.pallas_call(
        functools.partial(_moe_combine_kernel, ntile=ntile, cap=cap, final=final),
        out_shape=jax.ShapeDtypeStruct((b, n, d), F32),
        grid_spec=grid_spec, compiler_params=_cparams(("arbitrary", "arbitrary")), name="moe_combine",
    )(cnt, pos, x, gt2, y, final_g)


def _ec_moe(x_mid, h2, aff, gt2, wg, wu, wd, layer, final_g, final):
    b, n, _ = x_mid.shape
    cap = max(1, EC_FACTOR * n // N_EXPERTS)
    ntile = n // MOE_TILE
    pos, post, offs, afft = _route(aff, cap)
    per_tile = lambda a: a.transpose(0, 2, 1, 3).reshape(b, N_EXPERTS, ntile, 1, MOE_TILE)
    starts = offs[:, ::MOE_TILE // CHUNK, 0, :N_EXPERTS].transpose(0, 2, 1)
    cnt = jnp.concatenate([starts, jnp.full((b, N_EXPERTS, 1), cap, I32)], axis=2).reshape(-1)
    xe, gate = _moe_gather(cnt, per_tile(post), per_tile(afft), h2, cap)
    y = _moe_ffn(xe, gate, wg, wu, wd, layer)
    return _moe_combine(cnt, pos, x_mid, gt2, y, final_g, final)


def _rope_table(n):
    rows = n // GRID_W
    row = jnp.repeat(jnp.arange(rows), GRID_W).astype(F32)
    col = jnp.tile(jnp.arange(GRID_W), rows).astype(F32)
    nf = ATTN_DIM // 4
    inv = ROPE_BASE ** (-jnp.arange(nf, dtype=F32) / nf)
    ang = jnp.concatenate([row[:, None] * inv, col[:, None] * inv], axis=-1)
    cos, sin = jnp.cos(ang), jnp.sin(ang)
    reps = LANES // ATTN_DIM
    return jnp.concatenate([jnp.tile(jnp.concatenate([cos, cos], -1), (1, reps)),
                            jnp.tile(jnp.concatenate([-sin, sin], -1), (1, reps))], axis=-1)


def _pad_lanes(a):
    return jnp.pad(a, ((0, 0), (0, LANES - a.shape[1])))


def kernel(x, c, ctx, c_ctx, w_mod, b_mod, norm_mix_g, norm_ffn_g, final_norm_g, ab_w_in, ab_conv_w, ab_gate_b,
           ab_head_g, ab_sink, ab_w_out, gm_w_in, gm_ln_g, gm_ln_b, gm_w_s, gm_b_s, gm_w_out, moe_w_router,
           moe_w_gate, moe_w_up, moe_w_down):
    b, n, d = x.shape
    depth = w_mod.shape[0]
    assert depth <= 2, "context stream is only advanced for deeper stacks; not supported here"
    cond = jnp.zeros((SUBLANES, d), F32).at[:b].set(c).at[b].set(c_ctx)
    mods = _adaln(cond, w_mod, b_mod)

    def mod_rows(layer, rows):
        m = mods[layer, rows].reshape(-1, 6, 1, d)
        return [m[:, i] for i in range(6)]

    row = lambda v: v.reshape(1, -1)
    for layer in range(depth):
        sh1, sc1, gt1, sh2, sc2, gt2 = mod_rows(layer, slice(0, b))
        g1, g2 = row(norm_mix_g[layer]), row(norm_ffn_g[layer])
        wr = _hi_lo_rhs(_pad_lanes(moe_w_router[layer]))
        if layer % 2 == 0:
            e = layer // 2
            csh1, csc1 = (jnp.broadcast_to(v, (b, 1, d)) for v in mod_rows(layer, slice(b, b + 1))[:2])
            w_in = ab_w_in[e]
            g_lo = 4 * LSTM_WIDTH
            w_main = jnp.concatenate([w_in[:, :g_lo], w_in[:, g_lo + N_GATES:]], axis=1).astype(BF16)
            w_gate = _hi_lo_rhs(_pad_lanes(w_in[:, g_lo:g_lo + N_GATES]))
            p, gts, vt, avt = _modmm(x, g1, sc1, sh1, w_main, w_gate, tm=1024, name="ab_in")
            pc, gtc, vtc, avtc = _modmm(ctx, g1, csc1, csh1, w_main, w_gate, name="ab_in_ctx")
            hf, hb = _mlstm(_conv_silu(p, ab_conv_w[e]), vt, gts, _conv_silu(pc, ab_conv_w[e]), vtc, gtc,
                            _pad_lanes(row(ab_gate_b[e])))
            at = _attn(p, avt, pc, avtc, _rope_table(n), _pad_lanes(row(ab_sink[e])))
            x_mid, h2, aff = _ab_out(hf, hb, p, at, x, row(ab_head_g[e]), ab_w_out[e].astype(BF16),
                                     gt1, g2, sc2, sh2, wr)
        else:
            o = layer // 2
            uv = _modmm(x, g1, sc1, sh1, gm_w_in[o].astype(BF16), act="gelu", tm=1024, name="gm_in")
            x_mid, h2, aff = _gm_out(uv, x, row(gm_ln_g[o]), row(gm_ln_b[o]), gm_w_s[o].astype(BF16),
                                     _pad_lanes(gm_b_s[o].T), gm_w_out[o].astype(BF16), gt1, g2, sc2, sh2, wr)
        x = _ec_moe(x_mid, h2, aff, gt2, moe_w_gate, moe_w_up, moe_w_down, layer,
                    row(final_norm_g), layer == depth - 1)
    return x
```

```python
import functools

import jax
import jax.numpy as jnp
from jax import lax
from jax.experimental import pallas as pl
from jax.experimental.pallas import tpu as pltpu

F32 = jnp.float32
BF16 = jnp.bfloat16
I32 = jnp.int32
HI = lax.Precision.HIGHEST

D_MODEL = 1024
GRID_W = 64
EPS = 1e-6
LSTM_HEADS = 4
LSTM_DIM = 128
LSTM_WIDTH = LSTM_HEADS * LSTM_DIM
LSTM_CONV = 5
CHUNK = 128
ATTN_HEADS = 8
ATTN_KV_HEADS = 2
ATTN_GROUP = ATTN_HEADS // ATTN_KV_HEADS
ATTN_DIM = 64
ROPE_BASE = 10000.0
GM_GROUPS = 8
GM_HALF = 2 * D_MODEL
N_EXPERTS = 16
EC_FACTOR = 2
N_GATES = 4 * LSTM_HEADS

LANES = 128
SUBLANES = 8
BF16_ROWS = 16
VMEM_LIMIT_BYTES = 56 * 1024 * 1024

P_COLS = 4 * LSTM_WIDTH + ATTN_HEADS * ATTN_DIM + 2 * ATTN_KV_HEADS * ATTN_DIM
PB_V, PB_O, PB_AQ = 2, 3, 4
PB_AK, PB_AV = 20, 21
ATTN_HEADS_PER_DOT = 8
MOE_TILE = 256
SLOT_CHUNK = 64


def _cparams(sem, vmem=VMEM_LIMIT_BYTES):
    return pltpu.CompilerParams(dimension_semantics=sem, vmem_limit_bytes=vmem)


def _rms_mod(x, g, sc, sh):
    y = x * lax.rsqrt(jnp.mean(x * x, axis=-1, keepdims=True) + EPS)
    return y * g * (1.0 + sc) + sh


def _silu(x):
    return x * jax.nn.sigmoid(x)


def _gelu_tanh(x):
    return 0.5 * x * (1.0 + jnp.tanh(0.7978845608028654 * (x + 0.044715 * (x * x * x))))


def _log_sigmoid(x):
    return jnp.minimum(x, 0.0) - jnp.log(1.0 + jnp.exp(-jnp.abs(x)))


def _dot_t(a, b):
    return lax.dot_general(a, b, (((1,), (1,)), ((), ())), preferred_element_type=F32)


def _adaln_kernel(c_ref, w_ref, b_ref, o_ref):
    s = _silu(c_ref[...])
    o_ref[...] = jnp.dot(s, w_ref[...], precision=HI, preferred_element_type=F32) + b_ref[...]


def _adaln(cond, w_mod, b_mod):
    depth, d, six_d = w_mod.shape
    tn = six_d // 4
    return pl.pallas_call(
        _adaln_kernel,
        out_shape=jax.ShapeDtypeStruct((depth, SUBLANES, six_d), F32),
        grid=(depth, six_d // tn),
        in_specs=[pl.BlockSpec((SUBLANES, d), lambda l, j: (0, 0)),
                  pl.BlockSpec((None, d, tn), lambda l, j: (l, 0, j)),
                  pl.BlockSpec((None, 1, tn), lambda l, j: (l, 0, j))],
        out_specs=pl.BlockSpec((None, SUBLANES, tn), lambda l, j: (l, 0, j)),
        compiler_params=_cparams(("arbitrary", "arbitrary")),
        name="adaln",
    )(cond, w_mod, b_mod.reshape(depth, 1, six_d))


def _hi_lo_lhs(h, h_hi):
    return jnp.concatenate([h_hi, (h - h_hi.astype(F32)).astype(BF16), h_hi], axis=1)


def _hi_lo_rhs(w):
    w_hi = w.astype(BF16)
    return jnp.concatenate([w_hi, w_hi, (w - w_hi.astype(F32)).astype(BF16)], axis=0)


def _modmm_kernel(x_ref, g_ref, sc_ref, sh_ref, w_ref, *rest, chunks, act, with_gates):
    if with_gates:
        wg_ref, o_ref, og_ref, vt_ref, avt_ref = rest
    else:
        (o_ref,) = rest
    h = _rms_mod(x_ref[...], g_ref[...], sc_ref[...], sh_ref[...])
    hb = h.astype(BF16)
    for lo, hi in chunks:
        y = jnp.dot(hb, w_ref[:, lo:hi], preferred_element_type=F32)
        if act == "gelu":
            y = _gelu_tanh(y)
        o_ref[:, lo:hi] = y.astype(o_ref.dtype)
        if with_gates and lo == PB_V * LSTM_WIDTH:
            vt_ref[...] = y.T.astype(vt_ref.dtype)
        if with_gates and lo <= PB_AV * LANES < hi:
            av = y[:, PB_AV * LANES - lo:(PB_AV + 1) * LANES - lo]
            avt_ref[...] = av.T.astype(avt_ref.dtype)
    if with_gates:
        og_ref[...] = jnp.dot(_hi_lo_lhs(h, hb), wg_ref[...], preferred_element_type=F32)


def _modmm(x, g, sc, sh, w, wg=None, *, act=None, tm=512, chunk=512, name="modmm"):
    b, n, d = x.shape
    no = w.shape[1]
    tm = min(tm, n)
    assert chunk == LSTM_WIDTH
    chunks = tuple((lo, min(lo + chunk, no)) for lo in range(0, no, chunk))
    in_specs = [pl.BlockSpec((None, tm, d), lambda bi, i: (bi, i, 0)),
                pl.BlockSpec((1, d), lambda bi, i: (0, 0)),
                pl.BlockSpec((None, 1, d), lambda bi, i: (bi, 0, 0)),
                pl.BlockSpec((None, 1, d), lambda bi, i: (bi, 0, 0)),
                pl.BlockSpec((d, no), lambda bi, i: (0, 0))]
    out_shape = [jax.ShapeDtypeStruct((b, n, no), BF16)]
    out_specs = [pl.BlockSpec((None, tm, no), lambda bi, i: (bi, i, 0))]
    args = [x, g, sc, sh, w]
    if wg is not None:
        in_specs.append(pl.BlockSpec(wg.shape, lambda bi, i: (0, 0)))
        out_shape += [jax.ShapeDtypeStruct((b, n, LANES), F32), jax.ShapeDtypeStruct((b, LSTM_WIDTH, n), BF16),
                      jax.ShapeDtypeStruct((b, LANES, n), BF16)]
        out_specs += [pl.BlockSpec((None, tm, LANES), lambda bi, i: (bi, i, 0)),
                      pl.BlockSpec((None, LSTM_WIDTH, tm), lambda bi, i: (bi, 0, i)),
                      pl.BlockSpec((None, LANES, tm), lambda bi, i: (bi, 0, i))]
        args.append(wg)
    res = pl.pallas_call(
        functools.partial(_modmm_kernel, chunks=chunks, act=act, with_gates=wg is not None),
        out_shape=out_shape, grid=(b, n // tm), in_specs=in_specs, out_specs=out_specs,
        compiler_params=_cparams(("parallel", "parallel")), name=name,
    )(*args)
    return res if wg is not None else res[0]


def _conv_silu_kernel(x_ref, xp_ref, xn_ref, w_ref, o_ref, pad_scr):
    i = pl.program_id(1)
    rows = x_ref.shape[0]
    halo = BF16_ROWS
    has_prev = jnp.where(i > 0, 1.0, 0.0)
    has_next = jnp.where(i < pl.num_programs(1) - 1, 1.0, 0.0)
    pad_scr[pl.ds(0, halo), :] = (xp_ref[...].astype(F32) * has_prev).astype(BF16)
    pad_scr[pl.ds(halo, rows), :] = x_ref[...]
    pad_scr[pl.ds(halo + rows, halo), :] = (xn_ref[...].astype(F32) * has_next).astype(BF16)
    w = w_ref[...]
    win = CHUNK + 2 * halo
    r = lax.broadcasted_iota(I32, (CHUNK, win), 0)
    c = lax.broadcasted_iota(I32, (CHUNK, win), 1)
    half = LSTM_CONV // 2
    shifts = {t: jnp.where(c == r + halo + t - half, 1.0, 0.0).astype(BF16) for t in range(LSTM_CONV) if t != half}
    scale = LSTM_DIM ** -0.5
    for blk in range(rows // CHUNK):
        xw = pad_scr[pl.ds(blk * CHUNK, win), :]
        acc = xw[halo:halo + CHUNK].astype(F32) * w[half:half + 1, :]
        for t, s in shifts.items():
            acc = acc + jnp.dot(s, xw, preferred_element_type=F32) * w[t:t + 1, :]
        y = _silu(acc)
        out = pl.ds(blk * CHUNK, CHUNK)
        o_ref[out, :LSTM_WIDTH] = y[:, :LSTM_WIDTH].astype(o_ref.dtype)
        o_ref[out, LSTM_WIDTH:] = (y[:, LSTM_WIDTH:] * scale).astype(o_ref.dtype)


def _conv_silu(p, conv_w, tm=512):
    b, n, _ = p.shape
    tm = min(tm, n)
    qkw = 2 * LSTM_WIDTH
    hpt = tm // BF16_ROWS
    nhb = n // BF16_ROWS
    return pl.pallas_call(
        _conv_silu_kernel,
        out_shape=jax.ShapeDtypeStruct((b, n, qkw), BF16),
        grid=(b, n // tm),
        in_specs=[pl.BlockSpec((None, tm, qkw), lambda bi, i: (bi, i, 0)),
                  pl.BlockSpec((None, BF16_ROWS, qkw), lambda bi, i: (bi, jnp.maximum(i * hpt - 1, 0), 0)),
                  pl.BlockSpec((None, BF16_ROWS, qkw), lambda bi, i: (bi, jnp.minimum((i + 1) * hpt, nhb - 1), 0)),
                  pl.BlockSpec((LSTM_CONV, qkw), lambda bi, i: (0, 0))],
        out_specs=pl.BlockSpec((None, tm, qkw), lambda bi, i: (bi, i, 0)),
        scratch_shapes=[pltpu.VMEM((tm + 2 * BF16_ROWS, qkw), BF16)],
        compiler_params=_cparams(("parallel", "parallel")), name="conv_silu",
    )(p, p, p, conv_w)


def _tri(n, lower):
    r = lax.broadcasted_iota(I32, (n, n), 0)
    c = lax.broadcasted_iota(I32, (n, n), 1)
    return (c <= r) if lower else (c >= r)


STATE_ROWS = LSTM_DIM + BF16_ROWS
MLSTM_STEP_CHUNKS = 4


def _mlstm_segment(d, gates, k_all, vt_all, q_all, state, ht_ref, cols):
    first = state is None
    nh = LSTM_HEADS
    seg = gates.shape[0]

    def side_by_side(pieces):
        return jnp.concatenate(pieces, axis=1)

    def block_diag(x):
        w = x.shape[1] // nh
        lane = lax.broadcasted_iota(I32, x.shape, 1)
        return jnp.concatenate([jnp.where((lane >= h * w) & (lane < (h + 1) * w), x, jnp.zeros_like(x))
                                for h in range(nh)], axis=0)

    r = lax.broadcasted_iota(I32, (seg, seg), 0)
    c = lax.broadcasted_iota(I32, (seg, seg), 1)
    before = (r <= c) if d == 0 else (r >= c)
    gates_t = gates.T[:N_GATES]
    ls = _log_sigmoid(gates_t)
    ls_hi = ls.astype(BF16)
    rest = ls - ls_hi.astype(F32)
    ls_mid = rest.astype(BF16)
    ls_lo = (rest - ls_mid.astype(F32)).astype(BF16)
    parts = jnp.dot(jnp.concatenate([ls_hi, ls_mid, ls_lo], axis=0), jnp.where(before, 1.0, 0.0).astype(BF16),
                    preferred_element_type=F32)
    bcum_t = parts[:N_GATES] + parts[N_GATES:2 * N_GATES] + parts[2 * N_GATES:]
    last = seg - 1 if d == 0 else 0
    ci = [2 * d * nh + h for h in range(nh)]
    cf = [(2 * d + 1) * nh + h for h in range(nh)]
    b_rows = side_by_side([bcum_t[c:c + 1, :] for c in cf])
    li_rows = side_by_side([gates_t[c:c + 1, :] for c in ci])
    g_heads = [bcum_t[c:c + 1, last:last + 1] for c in cf]
    cn, m_heads = (None, [jnp.zeros((1, 1), F32)] * nh) if first else state
    over_seg = lambda xs: side_by_side([jnp.broadcast_to(x, (1, seg)) for x in xs])
    m_rows, g_rows = over_seg(m_heads), over_seg(g_heads)
    vt_cat = side_by_side([vt_all[h * LSTM_DIM:(h + 1) * LSTM_DIM, :] for h in range(nh)])
    if q_all is not None:
        q_blk = block_diag(q_all)
        per_key = side_by_side([jnp.broadcast_to(gates_t[i:i + 1, :] - bcum_t[f:f + 1, :], (seg, seg)).T
                                for i, f in zip(ci, cf)])
        log_d = jnp.where(side_by_side([before] * nh), b_rows + per_key, -jnp.inf)
        m_row = jnp.maximum(b_rows + m_rows, jnp.max(log_d, axis=0, keepdims=True))
        sm = _dot_t(k_all, q_blk) * jnp.exp(log_d - m_row)
        a = jnp.exp(b_rows + m_rows - m_row)
        qc = _dot_t(cn.astype(BF16), q_blk)
        num = jnp.dot(vt_cat, block_diag(sm.astype(BF16)), preferred_element_type=F32) + a * qc[:LSTM_DIM]
        den = jnp.sum(sm, axis=0, keepdims=True) + a * qc[LSTM_DIM:LSTM_DIM + 1]
        out = num / jnp.maximum(jnp.abs(den), jnp.exp(-m_row))
        for h in range(nh):
            ht_ref[h * LSTM_DIM:(h + 1) * LSTM_DIM, cols] = out[:, h * seg:(h + 1) * seg]
    w = g_rows - b_rows + li_rows
    m_new = [jnp.maximum(g_heads[h] + m_heads[h], jnp.max(w[:, h * seg:(h + 1) * seg], axis=1, keepdims=True))
             for h in range(nh)]
    wt = jnp.exp(w - over_seg(m_new))
    aug = jnp.concatenate([(vt_cat.astype(F32) * wt).astype(BF16),
                           jnp.broadcast_to(wt, (BF16_ROWS, nh * seg)).astype(BF16)], axis=0)
    upd = jnp.dot(aug, block_diag(k_all), preferred_element_type=F32)
    if not first:
        decay = side_by_side([jnp.broadcast_to(jnp.exp(g_heads[h] + m_heads[h] - m_new[h]), (1, LSTM_DIM))
                              for h in range(nh)])
        upd = decay * cn + upd
    return upd, m_new


def _mlstm_kernel(qkf_ref, qkb_ref, vtf_ref, vtb_ref, gf_ref, gb_ref, kc_ref, vtc_ref, gc_ref, gbias_ref,
                  hf_ref, hb_ref, cn_scr, m_scr):
    gbias = gbias_ref[...]

    def save(d, state):
        cn, m_heads = state
        cn_scr[d] = cn
        for h, m in enumerate(m_heads):
            m_scr[d * LSTM_HEADS + h] = jnp.broadcast_to(m, (1, LANES))

    @pl.when(pl.program_id(1) == 0)
    def _():
        gates = gc_ref[...] + gbias
        for d in range(2):
            save(d, _mlstm_segment(d, gates, kc_ref[...], vtc_ref[...], None, None, None, None))

    states = [(cn_scr[d], [m_scr[d * LSTM_HEADS + h][:, 0:1] for h in range(LSTM_HEADS)]) for d in range(2)]
    per_step = qkf_ref.shape[0] // CHUNK
    for u in range(per_step):
        for d, (qk_ref, vt_ref, g_ref, h_ref) in enumerate(((qkf_ref, vtf_ref, gf_ref, hf_ref),
                                                            (qkb_ref, vtb_ref, gb_ref, hb_ref))):
            j = u if d == 0 else per_step - 1 - u
            rows = slice(j * CHUNK, (j + 1) * CHUNK)
            qk = qk_ref[rows, :]
            states[d] = _mlstm_segment(d, g_ref[rows, :] + gbias, qk[:, LSTM_WIDTH:], vt_ref[:, rows],
                                       qk[:, :LSTM_WIDTH], states[d], h_ref, rows)
    for d in range(2):
        save(d, states[d])


def _mlstm(qk, vt, g, qkc, vtc, gc, gate_bias):
    b, n, _ = qk.shape
    lc = qkc.shape[1]
    rows = MLSTM_STEP_CHUNKS * CHUNK
    nc = n // rows
    qkw = 2 * LSTM_WIDTH
    in_specs = [
        pl.BlockSpec((None, rows, qkw), lambda bi, c: (bi, c, 0)),
        pl.BlockSpec((None, rows, qkw), lambda bi, c: (bi, nc - 1 - c, 0)),
        pl.BlockSpec((None, LSTM_WIDTH, rows), lambda bi, c: (bi, 0, c)),
        pl.BlockSpec((None, LSTM_WIDTH, rows), lambda bi, c: (bi, 0, nc - 1 - c)),
        pl.BlockSpec((None, rows, LANES), lambda bi, c: (bi, c, 0)),
        pl.BlockSpec((None, rows, LANES), lambda bi, c: (bi, nc - 1 - c, 0)),
        pl.BlockSpec((None, lc, LSTM_WIDTH), lambda bi, c: (bi, 0, 1)),
        pl.BlockSpec((None, LSTM_WIDTH, lc), lambda bi, c: (bi, 0, 0)),
        pl.BlockSpec((None, lc, LANES), lambda bi, c: (bi, 0, 0)),
        pl.BlockSpec((1, LANES), lambda bi, c: (0, 0))]
    out_specs = [pl.BlockSpec((None, LSTM_WIDTH, rows), lambda bi, c: (bi, 0, c)),
                 pl.BlockSpec((None, LSTM_WIDTH, rows), lambda bi, c: (bi, 0, nc - 1 - c))]
    return pl.pallas_call(
        _mlstm_kernel,
        out_shape=[jax.ShapeDtypeStruct((b, LSTM_WIDTH, n), F32)] * 2,
        grid=(b, nc), in_specs=in_specs, out_specs=out_specs,
        scratch_shapes=[pltpu.VMEM((2, STATE_ROWS, LSTM_WIDTH), F32),
                        pltpu.VMEM((2 * LSTM_HEADS, 1, LANES), F32)],
        compiler_params=_cparams(("arbitrary", "arbitrary")), name="mlstm",
    )(qk, qk, vt, vt, g, g, qkc, vtc, gc, gate_bias)


def _rope(x, cos, sin_signed):
    w = x.shape[1]
    lane = lax.broadcasted_iota(I32, x.shape, 1)
    first = (lane & (ATTN_DIM - 1)) < (ATTN_DIM // 2)
    partner = jnp.where(first, pltpu.roll(x, w - ATTN_DIM // 2, 1), pltpu.roll(x, ATTN_DIM // 2, 1))
    return x * cos + partner * sin_signed


def _attn_kernel(q_ref, kp_ref, kc_ref, kn_ref, vtp_ref, vtc_ref, vtn_ref, kctx_ref, vtctx_ref,
                 tp_ref, tc_ref, tn_ref, sink_ref, bias_ref, o_ref):
    def table(t_ref):
        t = t_ref[...]
        return t[:, :LANES], t[:, LANES:]

    cos_c, sin_c = table(tc_ref)
    q = _rope(q_ref[...].astype(F32), jnp.concatenate([cos_c] * 4, axis=1), jnp.concatenate([sin_c] * 4, axis=1))
    q = q * (ATTN_DIM ** -0.5)
    ks = []
    for k_ref, t_ref in ((kp_ref, tp_ref), (kc_ref, tc_ref), (kn_ref, tn_ref)):
        cos_t, sin_t = table(t_ref)
        ks.append(_rope(k_ref[...].astype(F32), cos_t, sin_t).astype(BF16))
    k_all = jnp.concatenate(ks + [kctx_ref[...]], axis=0)
    vt_all = jnp.concatenate([vtp_ref[...], vtc_ref[...], vtn_ref[...], vtctx_ref[...]], axis=1)

    per = ATTN_HEADS_PER_DOT
    bias = jnp.concatenate([bias_ref[...]] * per, axis=1)
    lane = lax.broadcasted_iota(I32, (CHUNK, LANES), 1)
    sink = sink_ref[...]

    for h0 in range(0, ATTN_HEADS, per):
        qs, snk = [], []
        for h in range(h0, h0 + per):
            g = h // ATTN_GROUP
            t = q[:, (h // 2) * LANES:(h // 2 + 1) * LANES]
            if h % 2 != g:
                t = pltpu.roll(t, ATTN_DIM, 1)
            half_g = (lane >= ATTN_DIM) if g == 1 else (lane < ATTN_DIM)
            qs.append(jnp.where(half_g, t, 0.0).astype(BF16))
            snk.append(jnp.broadcast_to(sink[:, h:h + 1], (1, CHUNK)))
        snk = jnp.concatenate(snk, axis=1)
        st = _dot_t(k_all, jnp.concatenate(qs, axis=0)) + bias
        m = jnp.maximum(jnp.max(st, axis=0, keepdims=True), snk)
        e = jnp.exp(st - m)
        den = jnp.sum(e, axis=0, keepdims=True) + jnp.exp(snk - m)
        pv = jnp.dot(vt_all, e.astype(BF16), preferred_element_type=F32)
        o = (pv / den).astype(o_ref.dtype)
        for r, h in enumerate(range(h0, h0 + per)):
            g = h // ATTN_GROUP
            o_ref[h * ATTN_DIM:(h + 1) * ATTN_DIM, :] = o[g * ATTN_DIM:(g + 1) * ATTN_DIM, r * CHUNK:(r + 1) * CHUNK]


def _attn_bias(nctx):
    i = jnp.arange(CHUNK)[None, :]
    j = jnp.arange(3 * CHUNK)[:, None]
    band = (j >= i) & (j <= i + 2 * CHUNK)
    local = jnp.stack([band & (j >= CHUNK), band, band & (j < 2 * CHUNK)])
    return jnp.concatenate([jnp.where(local, 0.0, -jnp.inf).astype(F32), jnp.zeros((3, nctx, CHUNK), F32)], axis=1)


def _attn(p, avt, pc, avtc, table, sink):
    b, n, _ = p.shape
    lc = pc.shape[1]
    nb = n // CHUNK
    assert nb >= 2
    qw = ATTN_HEADS * ATTN_DIM
    bias = _attn_bias(lc)

    def blk(col, off):
        return pl.BlockSpec((None, CHUNK, LANES), lambda bi, i: (bi, jnp.clip(i + off, 0, nb - 1), col))

    def vblk(off):
        return pl.BlockSpec((None, LANES, CHUNK), lambda bi, i: (bi, 0, jnp.clip(i + off, 0, nb - 1)))

    def tab(off):
        return pl.BlockSpec((CHUNK, 2 * LANES), lambda bi, i: (jnp.clip(i + off, 0, nb - 1), 0))

    in_specs = [pl.BlockSpec((None, CHUNK, qw), lambda bi, i: (bi, i, PB_AQ)),
                blk(PB_AK, -1), blk(PB_AK, 0), blk(PB_AK, 1), vblk(-1), vblk(0), vblk(1),
                pl.BlockSpec((None, lc, LANES), lambda bi, i: (bi, 0, PB_AK)),
                pl.BlockSpec((None, LANES, lc), lambda bi, i: (bi, 0, 0)),
                tab(-1), tab(0), tab(1),
                pl.BlockSpec((1, LANES), lambda bi, i: (0, 0)),
                pl.BlockSpec((None,) + bias.shape[1:],
                             lambda bi, i: (jnp.where(i == 0, 0, jnp.where(i == nb - 1, 2, 1)), 0, 0))]
    return pl.pallas_call(
        _attn_kernel,
        out_shape=jax.ShapeDtypeStruct((b, qw, n), BF16),
        grid=(b, nb), in_specs=in_specs,
        out_specs=pl.BlockSpec((None, qw, CHUNK), lambda bi, i: (bi, 0, i)),
        compiler_params=_cparams(("parallel", "parallel")), name="window_attn",
    )(p, p, p, p, avt, avt, avt, pc, avtc, table, table, table, sink, bias)


def _router_tail(x_new, g2, sc2, sh2, wr_ref, x_out_ref, h2_ref, aff_ref):
    x_out_ref[...] = x_new
    h2 = _rms_mod(x_new, g2, sc2, sh2)
    h_hi = h2.astype(BF16)
    h2_ref[...] = h_hi
    logits = jnp.dot(_hi_lo_lhs(h2, h_hi), wr_ref[...], preferred_element_type=F32)
    lane = lax.broadcasted_iota(I32, logits.shape, 1)
    logits = jnp.where(lane < N_EXPERTS, logits, -jnp.inf)
    e = jnp.exp(logits - jnp.max(logits, axis=1, keepdims=True))
    aff_ref[...] = e / jnp.sum(e, axis=1, keepdims=True)


def _ab_out_kernel(hf_ref, hb_ref, o_ref, at_ref, x_ref, hg_ref, wo_ref, gt1_ref, g2_ref, sc2_ref, sh2_ref, wr_ref,
                   x_out_ref, h2_ref, aff_ref):
    hsum = (hf_ref[...] + hb_ref[...]).T
    og = jax.nn.sigmoid(o_ref[...].astype(F32))
    hg = hg_ref[...]
    parts = []
    for h in range(LSTM_HEADS):
        sl = slice(h * LSTM_DIM, (h + 1) * LSTM_DIM)
        seg = hsum[:, sl]
        seg = seg * lax.rsqrt(jnp.mean(seg * seg, axis=-1, keepdims=True) + EPS)
        parts.append((seg * hg[:, sl] * og[:, sl]).astype(BF16))
    cat = jnp.concatenate(parts + [at_ref[...].astype(F32).T.astype(BF16)], axis=1)
    y = jnp.dot(cat, wo_ref[...], preferred_element_type=F32)
    _router_tail(x_ref[...] + gt1_ref[...] * y, g2_ref[...], sc2_ref[...], sh2_ref[...], wr_ref,
                 x_out_ref, h2_ref, aff_ref)


def _tail_out(b, n, d, tm):
    shapes = [jax.ShapeDtypeStruct((b, n, d), F32), jax.ShapeDtypeStruct((b, n, d), BF16),
              jax.ShapeDtypeStruct((b, n, LANES), F32)]
    specs = [pl.BlockSpec((None, tm, d), lambda bi, i: (bi, i, 0)),
             pl.BlockSpec((None, tm, d), lambda bi, i: (bi, i, 0)),
             pl.BlockSpec((None, tm, LANES), lambda bi, i: (bi, i, 0))]
    return shapes, specs


def _ab_out(hf, hb, p, at, x, head_g, w_out, gt1, g2, sc2, sh2, wr, tm=256):
    b, n, d = x.shape
    row = lambda w: pl.BlockSpec((None, tm, w), lambda bi, i: (bi, i, 0))
    vec = pl.BlockSpec((None, 1, d), lambda bi, i: (bi, 0, 0))
    const = lambda s: pl.BlockSpec(s, lambda bi, i: (0, 0))
    scan_out = pl.BlockSpec((None, LSTM_WIDTH, tm), lambda bi, i: (bi, 0, i))
    in_specs = [scan_out, scan_out,
                pl.BlockSpec((None, tm, LSTM_WIDTH), lambda bi, i: (bi, i, PB_O)),
                pl.BlockSpec((None, ATTN_HEADS * ATTN_DIM, tm), lambda bi, i: (bi, 0, i)),
                row(d), const((1, LSTM_WIDTH)), const(w_out.shape),
                vec, const((1, d)), vec, vec, const((3 * d, LANES))]
    shapes, specs = _tail_out(b, n, d, tm)
    return pl.pallas_call(
        _ab_out_kernel, out_shape=shapes, grid=(b, n // tm), in_specs=in_specs, out_specs=specs,
        compiler_params=_cparams(("parallel", "parallel")), name="ab_out",
    )(hf, hb, p, at, x, head_g, w_out, gt1, g2, sc2, sh2, wr)


def _gm_out_kernel(uv_ref, x_ref, lng_ref, lnb_ref, ws_ref, bs_ref, wo_ref, gt1_ref, g2_ref, sc2_ref, sh2_ref, wr_ref,
                   x_out_ref, h2_ref, aff_ref):
    tm = uv_ref.shape[0]
    gw = GM_HALF // GM_GROUPS
    v = uv_ref[:, GM_HALF:].astype(F32)
    mu = jnp.mean(v, axis=-1, keepdims=True)
    vc = v - mu
    var = jnp.mean(vc * vc, axis=-1, keepdims=True)
    vn = (vc * lax.rsqrt(var + EPS) * lng_ref[...] + lnb_ref[...]).astype(BF16)
    zs = []
    for ch in range(tm // CHUNK):
        rows = slice(ch * CHUNK, (ch + 1) * CHUNK)
        cols = []
        for g in range(GM_GROUPS):
            sv = jnp.dot(ws_ref[g], vn[rows, g * gw:(g + 1) * gw], preferred_element_type=F32)
            cols.append(sv + bs_ref[:, g:g + 1])
        sv = jnp.concatenate(cols, axis=1)
        zs.append((uv_ref[rows, :GM_HALF].astype(F32) * sv).astype(BF16))
    z = jnp.concatenate(zs, axis=0)
    y = jnp.dot(z, wo_ref[...], preferred_element_type=F32)
    _router_tail(x_ref[...] + gt1_ref[...] * y, g2_ref[...], sc2_ref[...], sh2_ref[...], wr_ref,
                 x_out_ref, h2_ref, aff_ref)


def _gm_out(uv, x, ln_g, ln_b, w_s, b_s_t, w_out, gt1, g2, sc2, sh2, wr, tm=256):
    b, n, d = x.shape
    vec = pl.BlockSpec((None, 1, d), lambda bi, i: (bi, 0, 0))
    const = lambda s: pl.BlockSpec(s, lambda *_: (0,) * len(s))
    in_specs = [pl.BlockSpec((None, tm, 2 * GM_HALF), lambda bi, i: (bi, i, 0)),
                pl.BlockSpec((None, tm, d), lambda bi, i: (bi, i, 0)),
                const((1, GM_HALF)), const((1, GM_HALF)), const(w_s.shape), const(b_s_t.shape), const(w_out.shape),
                vec, const((1, d)), vec, vec, const((3 * d, LANES))]
    shapes, specs = _tail_out(b, n, d, tm)
    return pl.pallas_call(
        _gm_out_kernel, out_shape=shapes, grid=(b, n // tm), in_specs=in_specs, out_specs=specs,
        compiler_params=_cparams(("parallel", "parallel")), name="gm_out",
    )(uv, x, ln_g, ln_b, w_s, b_s_t, w_out, gt1, g2, sc2, sh2, wr)


def _route_kernel(aff_ref, pos_ref, post_ref, offs_ref, afft_ref, *, cap):
    n = aff_ref.shape[0]
    nblk = n // CHUNK

    def to_expert_major(k, _):
        rows = pl.ds(pl.multiple_of(k * CHUNK, CHUNK), CHUNK)
        afft_ref[k] = aff_ref[rows, :].T[:N_EXPERTS, :]
        return 0

    lax.fori_loop(0, nblk, to_expert_major, 0, unroll=4)

    def count(pred):
        per_lane = jnp.sum(jnp.where(pred, 1.0, 0.0), axis=0)
        return jnp.sum(per_lane, axis=1, keepdims=True)

    def search(i, prefix):
        cand = prefix | jnp.left_shift(jnp.int32(1), 30 - i)
        cand_f = lax.bitcast_convert_type(cand, F32)
        return jnp.where(count(afft_ref[...] >= cand_f[None]) >= cap, cand, prefix)

    thr_col = lax.bitcast_convert_type(lax.fori_loop(0, 31, search, jnp.zeros((N_EXPERTS, 1), I32)), F32)
    need_col = cap - count(afft_ref[...] > thr_col[None])

    def to_row(col):
        full = jnp.concatenate([jnp.broadcast_to(col, (N_EXPERTS, LANES)),
                                jnp.zeros((LANES - N_EXPERTS, LANES), F32)], axis=0)
        return full.T[0:1, :]

    thr, need = to_row(thr_col), to_row(need_col)
    tril = jnp.where(_tri(CHUNK, True), 1.0, 0.0).astype(BF16)

    def block(k, carry):
        run_tie, run_sel = carry
        rows = pl.ds(pl.multiple_of(k * CHUNK, CHUNK), CHUNK)
        a = aff_ref[rows, :]
        gt = a > thr
        tie = jnp.where(a == thr, 1.0, 0.0)
        tie_incl = jnp.dot(tril, tie.astype(BF16), preferred_element_type=F32)
        sel = jnp.where(gt | ((tie > 0.0) & (tie_incl - tie + run_tie < need)), 1.0, 0.0)
        sel_incl = jnp.dot(tril, sel.astype(BF16), preferred_element_type=F32)
        pos = jnp.where(sel > 0.0, sel_incl - sel + run_sel, -1.0)
        pos_ref[rows, :] = pos.astype(I32)
        post_ref[k] = pos.T[:N_EXPERTS, :].astype(I32)
        offs_ref[k] = run_sel.astype(I32)
        return (run_tie + tie_incl[CHUNK - 1:CHUNK, :], run_sel + sel_incl[CHUNK - 1:CHUNK, :])

    zero = jnp.zeros((1, LANES), F32)
    lax.fori_loop(0, nblk, block, (zero, zero), unroll=4)


def _route(aff, cap):
    b, n, _ = aff.shape
    nblk = n // CHUNK
    return pl.pallas_call(
        functools.partial(_route_kernel, cap=cap),
        out_shape=[jax.ShapeDtypeStruct((b, n, LANES), I32),
                   jax.ShapeDtypeStruct((b, nblk, N_EXPERTS, CHUNK), I32),
                   jax.ShapeDtypeStruct((b, nblk, 1, LANES), I32),
                   jax.ShapeDtypeStruct((b, nblk, N_EXPERTS, CHUNK), F32)],
        grid=(b,),
        in_specs=[pl.BlockSpec((None, n, LANES), lambda bi: (bi, 0, 0))],
        out_specs=[pl.BlockSpec((None, n, LANES), lambda bi: (bi, 0, 0)),
                   pl.BlockSpec((None, nblk, N_EXPERTS, CHUNK), lambda bi: (bi, 0, 0, 0)),
                   pl.BlockSpec((None, nblk, 1, LANES), lambda bi: (bi, 0, 0, 0)),
                   pl.BlockSpec((None, nblk, N_EXPERTS, CHUNK), lambda bi: (bi, 0, 0, 0))],
        compiler_params=_cparams(("parallel",)), name="route",
    )(aff)


def _window_start(s0, w, cap):
    lo = ((s0 >> 4) << 4) + w * SLOT_CHUNK
    return lo, pl.multiple_of(jnp.minimum(lo, cap - SLOT_CHUNK), BF16_ROWS)


def _num_windows(s0, s1):
    return (s1 - ((s0 >> 4) << 4) + SLOT_CHUNK - 1) >> (SLOT_CHUNK.bit_length() - 1)


def _moe_gather_kernel(cnt_ref, post_ref, afft_ref, h_ref, xe_ref, gate_ref, *, ntile, cap, group, tiles_per_step):
    bi, eg, ts = pl.program_id(0), pl.program_id(1), pl.program_id(2)

    @pl.when(ts == 0)
    def _():
        xe_ref[...] = jnp.zeros_like(xe_ref)
        gate_ref[...] = jnp.zeros_like(gate_ref)

    slot = lax.broadcasted_iota(I32, (SLOT_CHUNK, MOE_TILE), 0)
    for u in range(tiles_per_step):
        htile = h_ref[u * MOE_TILE:(u + 1) * MOE_TILE, :]
        t = ts * tiles_per_step + u
        bases = [(bi * N_EXPERTS + eg * group + g) * (ntile + 1) + t for g in range(group)]

        def windows(w, _, u=u, htile=htile, bases=bases):
            starts, onehots = [], []
            for g in range(group):
                lo, start = _window_start(cnt_ref[bases[g]], w, cap)
                posrow = post_ref[g, u]
                hit = (posrow - start == slot) & (posrow >= lo)
                onehots.append(jnp.where(hit, 1.0, 0.0).astype(BF16))
                gates = jnp.sum(jnp.where(hit, afft_ref[g, u], 0.0), axis=1, keepdims=True)
                dst = pl.ds(start, SLOT_CHUNK)
                gate_ref[g, dst, :] = gate_ref[g, dst, :] + jnp.broadcast_to(gates, (SLOT_CHUNK, LANES))
                starts.append(start)
            rows = jnp.dot(jnp.concatenate(onehots, axis=0), htile, preferred_element_type=F32)
            for g in range(group):
                dst = pl.ds(starts[g], SLOT_CHUNK)
                xe_ref[g, dst, :] = xe_ref[g, dst, :] + rows[g * SLOT_CHUNK:(g + 1) * SLOT_CHUNK].astype(BF16)
            return 0

        windows(0, 0)
        nwin = 1
        for g in range(group):
            nwin = jnp.maximum(nwin, _num_windows(cnt_ref[bases[g]], cnt_ref[bases[g] + 1]))
        lax.fori_loop(1, nwin, windows, 0)


def _moe_gather(cnt, post, afft, h2, cap, group=8, tiles_per_step=4):
    b, n, d = h2.shape
    ntile = n // MOE_TILE
    per_tile = pl.BlockSpec((None, group, tiles_per_step, 1, MOE_TILE), lambda bi, eg, ts, c: (bi, eg, ts, 0, 0))
    grid_spec = pltpu.PrefetchScalarGridSpec(
        num_scalar_prefetch=1, grid=(b, N_EXPERTS // group, ntile // tiles_per_step),
        in_specs=[per_tile, per_tile,
                  pl.BlockSpec((None, tiles_per_step * MOE_TILE, d), lambda bi, eg, ts, c: (bi, ts, 0))],
        out_specs=[pl.BlockSpec((None, group, cap, d), lambda bi, eg, ts, c: (bi, eg, 0, 0)),
                   pl.BlockSpec((None, group, cap, LANES), lambda bi, eg, ts, c: (bi, eg, 0, 0))])
    return pl.pallas_call(
        functools.partial(_moe_gather_kernel, ntile=ntile, cap=cap, group=group, tiles_per_step=tiles_per_step),
        out_shape=[jax.ShapeDtypeStruct((b, N_EXPERTS, cap, d), BF16),
                   jax.ShapeDtypeStruct((b, N_EXPERTS, cap, LANES), F32)],
        grid_spec=grid_spec, compiler_params=_cparams(("arbitrary", "arbitrary", "arbitrary")), name="moe_gather",
    )(cnt, post, afft, h2)


def _moe_ffn_kernel(xe_ref, gate_ref, wg_ref, wu_ref, wd_ref, y_ref, acc_scr, *, hid_tile):
    j = pl.program_id(2)
    xe = xe_ref[...]
    for k in range(wg_ref.shape[1] // hid_tile):
        cols = slice(k * hid_tile, (k + 1) * hid_tile)
        gate = jnp.dot(xe, wg_ref[:, cols].astype(BF16), preferred_element_type=F32)
        up = jnp.dot(xe, wu_ref[:, cols].astype(BF16), preferred_element_type=F32)
        hid = (_silu(gate) * up).astype(BF16)
        part = jnp.dot(hid, wd_ref[cols, :].astype(BF16), preferred_element_type=F32)
        if k == 0:
            @pl.when(j == 0)
            def _():
                acc_scr[...] = part

            @pl.when(j != 0)
            def _():
                acc_scr[...] += part
        else:
            acc_scr[...] += part

    @pl.when(j == pl.num_programs(2) - 1)
    def _():
        y_ref[...] = (acc_scr[...] * gate_ref[:, 0:1]).astype(y_ref.dtype)


def _moe_ffn(xe, gate, wg, wu, wd, layer, hid_split=1):
    b, ne, cap, d = xe.shape
    dh = wg.shape[3] // hid_split
    return pl.pallas_call(
        functools.partial(_moe_ffn_kernel, hid_tile=256),
        out_shape=jax.ShapeDtypeStruct((b, ne, cap, d), BF16),
        grid=(b, ne, hid_split),
        in_specs=[pl.BlockSpec((None, None, cap, d), lambda bi, e, j: (bi, e, 0, 0)),
                  pl.BlockSpec((None, None, cap, LANES), lambda bi, e, j: (bi, e, 0, 0)),
                  pl.BlockSpec((None, None, d, dh), lambda bi, e, j: (layer, e, 0, j)),
                  pl.BlockSpec((None, None, d, dh), lambda bi, e, j: (layer, e, 0, j)),
                  pl.BlockSpec((None, None, dh, d), lambda bi, e, j: (layer, e, j, 0))],
        out_specs=pl.BlockSpec((None, None, cap, d), lambda bi, e, j: (bi, e, 0, 0)),
        scratch_shapes=[pltpu.VMEM((cap, d), F32)],
        compiler_params=_cparams(("parallel", "parallel", "arbitrary")), name="moe_ffn",
    )(xe, gate, wg, wu, wd)


def _moe_combine_kernel(cnt_ref, pos_ref, x_ref, gt2_ref, y_ref, fg_ref, o_ref, acc_scr, ycat_scr, *, ntile, cap, final):
    bi, t = pl.program_id(0), pl.program_id(1)
    kw = N_EXPERTS * SLOT_CHUNK
    shift = SLOT_CHUNK.bit_length() - 1
    sel = jnp.where(lax.broadcasted_iota(I32, (LANES, kw), 0) == (lax.broadcasted_iota(I32, (LANES, kw), 1) >> shift),
                    1.0, 0.0).astype(BF16)

    def spread(v):
        hi = (v >> 5).astype(F32).astype(BF16)
        lo = (v & 31).astype(F32).astype(BF16)
        return 32.0 * jnp.dot(hi, sel, preferred_element_type=F32) + jnp.dot(lo, sel, preferred_element_type=F32)

    pos_w = spread(pos_ref[...] + 1) - 1.0
    within = (lax.broadcasted_iota(I32, (1, kw), 1) & (SLOT_CHUNK - 1)).astype(F32)
    lane = lax.broadcasted_iota(I32, (SUBLANES, LANES), 1)

    def windows(w):
        starts = jnp.zeros((SUBLANES, LANES), I32)
        los = jnp.zeros((SUBLANES, LANES), I32)
        for e in range(N_EXPERTS):
            lo, start = _window_start(cnt_ref[(bi * N_EXPERTS + e) * (ntile + 1) + t], w, cap)
            starts = jnp.where(lane == e, start, starts)
            los = jnp.where(lane == e, lo, los)
            ycat_scr[e * SLOT_CHUNK:(e + 1) * SLOT_CHUNK, :] = y_ref[e, pl.ds(start, SLOT_CHUNK), :]
        want = spread(starts)[0:1] + within
        onehot = jnp.where((pos_w == want) & (pos_w >= spread(los)[0:1]), 1.0, 0.0).astype(BF16)
        return jnp.dot(onehot, ycat_scr[...], preferred_element_type=F32)

    acc_scr[...] = windows(0)
    nwin = 1
    for e in range(N_EXPERTS):
        base = (bi * N_EXPERTS + e) * (ntile + 1) + t
        nwin = jnp.maximum(nwin, _num_windows(cnt_ref[base], cnt_ref[base + 1]))

    def overflow(w, _):
        acc_scr[...] += windows(w)
        return 0

    lax.fori_loop(1, nwin, overflow, 0)
    out = x_ref[...] + gt2_ref[...] * acc_scr[...]
    if final:
        out = out * lax.rsqrt(jnp.mean(out * out, axis=-1, keepdims=True) + EPS) * fg_ref[...]
    o_ref[...] = out


def _moe_combine(cnt, pos, x, gt2, y, final_g, final):
    b, n, d = x.shape
    ntile = n // MOE_TILE
    cap = y.shape[2]
    grid_spec = pltpu.PrefetchScalarGridSpec(
        num_scalar_prefetch=1, grid=(b, ntile),
        in_specs=[pl.BlockSpec((None, MOE_TILE, LANES), lambda bi, t, c: (bi, t, 0)),
                  pl.BlockSpec((None, MOE_TILE, d), lambda bi, t, c: (bi, t, 0)),
                  pl.BlockSpec((None, 1, d), lambda bi, t, c: (bi, 0, 0)),
                  pl.BlockSpec((None, N_EXPERTS, cap, d), lambda bi, t, c: (bi, 0, 0, 0),
                               pipeline_mode=pl.Buffered(1)),
                  pl.BlockSpec((1, d), lambda bi, t, c: (0, 0))],
        out_specs=pl.BlockSpec((None, MOE_TILE, d), lambda bi, t, c: (bi, t, 0)),
        scratch_shapes=[pltpu.VMEM((MOE_TILE, d), F32), pltpu.VMEM((N_EXPERTS * SLOT_CHUNK, d), BF16)])
    return pl.pallas_call(
        functools.partial(_moe_combine_kernel, ntile=ntile, cap=cap, final=final),
        out_shape=jax.ShapeDtypeStruct((b, n, d), F32),
        grid_spec=grid_spec, compiler_params=_cparams(("arbitrary", "arbitrary")), name="moe_combine",
    )(cnt, pos, x, gt2, y, final_g)


def _ec_moe(x_mid, h2, aff, gt2, wg, wu, wd, layer, final_g, final):
    b, n, _ = x_mid.shape
    cap = max(1, EC_FACTOR * n // N_EXPERTS)
    ntile = n // MOE_TILE
    pos, post, offs, afft = _route(aff, cap)
    per_tile = lambda a: a.transpose(0, 2, 1, 3).reshape(b, N_EXPERTS, ntile, 1, MOE_TILE)
    starts = offs[:, ::MOE_TILE // CHUNK, 0, :N_EXPERTS].transpose(0, 2, 1)
    cnt = jnp.concatenate([starts, jnp.full((b, N_EXPERTS, 1), cap, I32)], axis=2).reshape(-1)
    xe, gate = _moe_gather(cnt, per_tile(post), per_tile(afft), h2, cap)
    y = _moe_ffn(xe, gate, wg, wu, wd, layer)
    return _moe_combine(cnt, pos, x_mid, gt2, y, final_g, final)


def _rope_table(n):
    rows = n // GRID_W
    row = jnp.repeat(jnp.arange(rows), GRID_W).astype(F32)
    col = jnp.tile(jnp.arange(GRID_W), rows).astype(F32)
    nf = ATTN_DIM // 4
    inv = ROPE_BASE ** (-jnp.arange(nf, dtype=F32) / nf)
    ang = jnp.concatenate([row[:, None] * inv, col[:, None] * inv], axis=-1)
    cos, sin = jnp.cos(ang), jnp.sin(ang)
    reps = LANES // ATTN_DIM
    return jnp.concatenate([jnp.tile(jnp.concatenate([cos, cos], -1), (1, reps)),
                            jnp.tile(jnp.concatenate([-sin, sin], -1), (1, reps))], axis=-1)


def _pad_lanes(a):
    return jnp.pad(a, ((0, 0), (0, LANES - a.shape[1])))


def kernel(x, c, ctx, c_ctx, w_mod, b_mod, norm_mix_g, norm_ffn_g, final_norm_g, ab_w_in, ab_conv_w, ab_gate_b,
           ab_head_g, ab_sink, ab_w_out, gm_w_in, gm_ln_g, gm_ln_b, gm_w_s, gm_b_s, gm_w_out, moe_w_router,
           moe_w_gate, moe_w_up, moe_w_down):
    b, n, d = x.shape
    depth = w_mod.shape[0]
    assert depth <= 2, "context stream is only advanced for deeper stacks; not supported here"
    cond = jnp.zeros((SUBLANES, d), F32).at[:b].set(c).at[b].set(c_ctx)
    mods = _adaln(cond, w_mod, b_mod)

    def mod_rows(layer, rows):
        m = mods[layer, rows].reshape(-1, 6, 1, d)
        return [m[:, i] for i in range(6)]

    row = lambda v: v.reshape(1, -1)
    for layer in range(depth):
        sh1, sc1, gt1, sh2, sc2, gt2 = mod_rows(layer, slice(0, b))
        g1, g2 = row(norm_mix_g[layer]), row(norm_ffn_g[layer])
        wr = _hi_lo_rhs(_pad_lanes(moe_w_router[layer]))
        if layer % 2 == 0:
            e = layer // 2
            csh1, csc1 = (jnp.broadcast_to(v, (b, 1, d)) for v in mod_rows(layer, slice(b, b + 1))[:2])
            w_in = ab_w_in[e]
            g_lo = 4 * LSTM_WIDTH
            w_main = jnp.concatenate([w_in[:, :g_lo], w_in[:, g_lo + N_GATES:]], axis=1).astype(BF16)
            w_gate = _hi_lo_rhs(_pad_lanes(w_in[:, g_lo:g_lo + N_GATES]))
            p, gts, vt, avt = _modmm(x, g1, sc1, sh1, w_main, w_gate, tm=1024, name="ab_in")
            pc, gtc, vtc, avtc = _modmm(ctx, g1, csc1, csh1, w_main, w_gate, name="ab_in_ctx")
            hf, hb = _mlstm(_conv_silu(p, ab_conv_w[e]), vt, gts, _conv_silu(pc, ab_conv_w[e]), vtc, gtc,
                            _pad_lanes(row(ab_gate_b[e])))
            at = _attn(p, avt, pc, avtc, _rope_table(n), _pad_lanes(row(ab_sink[e])))
            x_mid, h2, aff = _ab_out(hf, hb, p, at, x, row(ab_head_g[e]), ab_w_out[e].astype(BF16),
                                     gt1, g2, sc2, sh2, wr)
        else:
            o = layer // 2
            uv = _modmm(x, g1, sc1, sh1, gm_w_in[o].astype(BF16), act="gelu", tm=1024, name="gm_in")
            x_mid, h2, aff = _gm_out(uv, x, row(gm_ln_g[o]), row(gm_ln_b[o]), gm_w_s[o].astype(BF16),
                                     _pad_lanes(gm_b_s[o].T), gm_w_out[o].astype(BF16), gt1, g2, sc2, sh2, wr)
        x = _ec_moe(x_mid, h2, aff, gt2, moe_w_gate, moe_w_up, moe_w_down, layer,
                    row(final_norm_g), layer == depth - 1)
    return x
```

```python
import functools

import jax
import jax.numpy as jnp
from jax import lax
from jax.experimental import pallas as pl
from jax.experimental.pallas import tpu as pltpu

F32 = jnp.float32
BF16 = jnp.bfloat16
I32 = jnp.int32
HI = lax.Precision.HIGHEST

D_MODEL = 1024
GRID_W = 64
EPS = 1e-6
LSTM_HEADS = 4
LSTM_DIM = 128
LSTM_WIDTH = LSTM_HEADS * LSTM_DIM
LSTM_CONV = 5
CHUNK = 128
ATTN_HEADS = 8
ATTN_KV_HEADS = 2
ATTN_GROUP = ATTN_HEADS // ATTN_KV_HEADS
ATTN_DIM = 64
ROPE_BASE = 10000.0
GM_GROUPS = 8
GM_HALF = 2 * D_MODEL
N_EXPERTS = 16
EC_FACTOR = 2
N_GATES = 4 * LSTM_HEADS

LANES = 128
SUBLANES = 8
BF16_ROWS = 16
VMEM_LIMIT_BYTES = 56 * 1024 * 1024

P_COLS = 4 * LSTM_WIDTH + ATTN_HEADS * ATTN_DIM + 2 * ATTN_KV_HEADS * ATTN_DIM
PB_V, PB_O, PB_AQ = 2, 3, 4
PB_AK, PB_AV = 20, 21
ATTN_HEADS_PER_DOT = 8
MOE_TILE = 256
SLOT_CHUNK = 64


def _cparams(sem, vmem=VMEM_LIMIT_BYTES):
    return pltpu.CompilerParams(dimension_semantics=sem, vmem_limit_bytes=vmem)


def _rms_mod(x, g, sc, sh):
    y = x * lax.rsqrt(jnp.mean(x * x, axis=-1, keepdims=True) + EPS)
    return y * g * (1.0 + sc) + sh


def _silu(x):
    return x * jax.nn.sigmoid(x)


def _gelu_tanh(x):
    return 0.5 * x * (1.0 + jnp.tanh(0.7978845608028654 * (x + 0.044715 * (x * x * x))))


def _log_sigmoid(x):
    return jnp.minimum(x, 0.0) - jnp.log(1.0 + jnp.exp(-jnp.abs(x)))


def _dot_t(a, b):
    return lax.dot_general(a, b, (((1,), (1,)), ((), ())), preferred_element_type=F32)


def _adaln_kernel(c_ref, w_ref, b_ref, o_ref):
    s = _silu(c_ref[...])
    o_ref[...] = jnp.dot(s, w_ref[...], precision=HI, preferred_element_type=F32) + b_ref[...]


def _adaln(cond, w_mod, b_mod):
    depth, d, six_d = w_mod.shape
    tn = six_d // 4
    return pl.pallas_call(
        _adaln_kernel,
        out_shape=jax.ShapeDtypeStruct((depth, SUBLANES, six_d), F32),
        grid=(depth, six_d // tn),
        in_specs=[pl.BlockSpec((SUBLANES, d), lambda l, j: (0, 0)),
                  pl.BlockSpec((None, d, tn), lambda l, j: (l, 0, j)),
                  pl.BlockSpec((None, 1, tn), lambda l, j: (l, 0, j))],
        out_specs=pl.BlockSpec((None, SUBLANES, tn), lambda l, j: (l, 0, j)),
        compiler_params=_cparams(("arbitrary", "arbitrary")),
        name="adaln",
    )(cond, w_mod, b_mod.reshape(depth, 1, six_d))


def _hi_lo_lhs(h, h_hi):
    return jnp.concatenate([h_hi, (h - h_hi.astype(F32)).astype(BF16), h_hi], axis=1)


def _hi_lo_rhs(w):
    w_hi = w.astype(BF16)
    return jnp.concatenate([w_hi, w_hi, (w - w_hi.astype(F32)).astype(BF16)], axis=0)


def _modmm_kernel(x_ref, g_ref, sc_ref, sh_ref, w_ref, *rest, chunks, act, with_gates):
    if with_gates:
        wg_ref, o_ref, og_ref, vt_ref, avt_ref = rest
    else:
        (o_ref,) = rest
    h = _rms_mod(x_ref[...], g_ref[...], sc_ref[...], sh_ref[...])
    hb = h.astype(BF16)
    for lo, hi in chunks:
        y = jnp.dot(hb, w_ref[:, lo:hi], preferred_element_type=F32)
        if act == "gelu":
            y = _gelu_tanh(y)
        o_ref[:, lo:hi] = y.astype(o_ref.dtype)
        if with_gates and lo == PB_V * LSTM_WIDTH:
            vt_ref[...] = y.T.astype(vt_ref.dtype)
        if with_gates and lo <= PB_AV * LANES < hi:
            av = y[:, PB_AV * LANES - lo:(PB_AV + 1) * LANES - lo]
            avt_ref[...] = av.T.astype(avt_ref.dtype)
    if with_gates:
        og_ref[...] = jnp.dot(_hi_lo_lhs(h, hb), wg_ref[...], preferred_element_type=F32)


def _modmm(x, g, sc, sh, w, wg=None, *, act=None, tm=512, chunk=512, name="modmm"):
    b, n, d = x.shape
    no = w.shape[1]
    tm = min(tm, n)
    assert chunk == LSTM_WIDTH
    chunks = tuple((lo, min(lo + chunk, no)) for lo in range(0, no, chunk))
    in_specs = [pl.BlockSpec((None, tm, d), lambda bi, i: (bi, i, 0)),
                pl.BlockSpec((1, d), lambda bi, i: (0, 0)),
                pl.BlockSpec((None, 1, d), lambda bi, i: (bi, 0, 0)),
                pl.BlockSpec((None, 1, d), lambda bi, i: (bi, 0, 0)),
                pl.BlockSpec((d, no), lambda bi, i: (0, 0))]
    out_shape = [jax.ShapeDtypeStruct((b, n, no), BF16)]
    out_specs = [pl.BlockSpec((None, tm, no), lambda bi, i: (bi, i, 0))]
    args = [x, g, sc, sh, w]
    if wg is not None:
        in_specs.append(pl.BlockSpec(wg.shape, lambda bi, i: (0, 0)))
        out_shape += [jax.ShapeDtypeStruct((b, n, LANES), F32), jax.ShapeDtypeStruct((b, LSTM_WIDTH, n), BF16),
                      jax.ShapeDtypeStruct((b, LANES, n), BF16)]
        out_specs += [pl.BlockSpec((None, tm, LANES), lambda bi, i: (bi, i, 0)),
                      pl.BlockSpec((None, LSTM_WIDTH, tm), lambda bi, i: (bi, 0, i)),
                      pl.BlockSpec((None, LANES, tm), lambda bi, i: (bi, 0, i))]
        args.append(wg)
    res = pl.pallas_call(
        functools.partial(_modmm_kernel, chunks=chunks, act=act, with_gates=wg is not None),
        out_shape=out_shape, grid=(b, n // tm), in_specs=in_specs, out_specs=out_specs,
        compiler_params=_cparams(("parallel", "parallel")), name=name,
    )(*args)
    return res if wg is not None else res[0]


def _conv_silu_kernel(x_ref, xp_ref, xn_ref, w_ref, o_ref, pad_scr):
    i = pl.program_id(1)
    rows = x_ref.shape[0]
    halo = BF16_ROWS
    has_prev = jnp.where(i > 0, 1.0, 0.0)
    has_next = jnp.where(i < pl.num_programs(1) - 1, 1.0, 0.0)
    pad_scr[pl.ds(0, halo), :] = (xp_ref[...].astype(F32) * has_prev).astype(BF16)
    pad_scr[pl.ds(halo, rows), :] = x_ref[...]
    pad_scr[pl.ds(halo + rows, halo), :] = (xn_ref[...].astype(F32) * has_next).astype(BF16)
    w = w_ref[...]
    win = CHUNK + 2 * halo
    r = lax.broadcasted_iota(I32, (CHUNK, win), 0)
    c = lax.broadcasted_iota(I32, (CHUNK, win), 1)
    half = LSTM_CONV // 2
    shifts = {t: jnp.where(c == r + halo + t - half, 1.0, 0.0).astype(BF16) for t in range(LSTM_CONV) if t != half}
    scale = LSTM_DIM ** -0.5
    for blk in range(rows // CHUNK):
        xw = pad_scr[pl.ds(blk * CHUNK, win), :]
        acc = xw[halo:halo + CHUNK].astype(F32) * w[half:half + 1, :]
        for t, s in shifts.items():
            acc = acc + jnp.dot(s, xw, preferred_element_type=F32) * w[t:t + 1, :]
        y = _silu(acc)
        out = pl.ds(blk * CHUNK, CHUNK)
        o_ref[out, :LSTM_WIDTH] = y[:, :LSTM_WIDTH].astype(o_ref.dtype)
        o_ref[out, LSTM_WIDTH:] = (y[:, LSTM_WIDTH:] * scale).astype(o_ref.dtype)


def _conv_silu(p, conv_w, tm=512):
    b, n, _ = p.shape
    tm = min(tm, n)
    qkw = 2 * LSTM_WIDTH
    hpt = tm // BF16_ROWS
    nhb = n // BF16_ROWS
    return pl.pallas_call(
        _conv_silu_kernel,
        out_shape=jax.ShapeDtypeStruct((b, n, qkw), BF16),
        grid=(b, n // tm),
        in_specs=[pl.BlockSpec((None, tm, qkw), lambda bi, i: (bi, i, 0)),
                  pl.BlockSpec((None, BF16_ROWS, qkw), lambda bi, i: (bi, jnp.maximum(i * hpt - 1, 0), 0)),
                  pl.BlockSpec((None, BF16_ROWS, qkw), lambda bi, i: (bi, jnp.minimum((i + 1) * hpt, nhb - 1), 0)),
                  pl.BlockSpec((LSTM_CONV, qkw), lambda bi, i: (0, 0))],
        out_specs=pl.BlockSpec((None, tm, qkw), lambda bi, i: (bi, i, 0)),
        scratch_shapes=[pltpu.VMEM((tm + 2 * BF16_ROWS, qkw), BF16)],
        compiler_params=_cparams(("parallel", "parallel")), name="conv_silu",
    )(p, p, p, conv_w)


def _tri(n, lower):
    r = lax.broadcasted_iota(I32, (n, n), 0)
    c = lax.broadcasted_iota(I32, (n, n), 1)
    return (c <= r) if lower else (c >= r)


STATE_ROWS = LSTM_DIM + BF16_ROWS
MLSTM_STEP_CHUNKS = 8


def _mlstm_segment(d, gates, k_all, vt_all, q_all, state, ht_ref, cols):
    first = state is None
    nh = LSTM_HEADS
    seg = gates.shape[0]

    def side_by_side(pieces):
        return jnp.concatenate(pieces, axis=1)

    def block_diag(x):
        w = x.shape[1] // nh
        lane = lax.broadcasted_iota(I32, x.shape, 1)
        return jnp.concatenate([jnp.where((lane >= h * w) & (lane < (h + 1) * w), x, jnp.zeros_like(x))
                                for h in range(nh)], axis=0)

    r = lax.broadcasted_iota(I32, (seg, seg), 0)
    c = lax.broadcasted_iota(I32, (seg, seg), 1)
    before = (r <= c) if d == 0 else (r >= c)
    gates_t = gates.T[:N_GATES]
    ls = _log_sigmoid(gates_t)
    ls_hi = ls.astype(BF16)
    rest = ls - ls_hi.astype(F32)
    ls_mid = rest.astype(BF16)
    ls_lo = (rest - ls_mid.astype(F32)).astype(BF16)
    parts = jnp.dot(jnp.concatenate([ls_hi, ls_mid, ls_lo], axis=0), jnp.where(before, 1.0, 0.0).astype(BF16),
                    preferred_element_type=F32)
    bcum_t = parts[:N_GATES] + parts[N_GATES:2 * N_GATES] + parts[2 * N_GATES:]
    last = seg - 1 if d == 0 else 0
    ci = [2 * d * nh + h for h in range(nh)]
    cf = [(2 * d + 1) * nh + h for h in range(nh)]
    b_rows = side_by_side([bcum_t[c:c + 1, :] for c in cf])
    li_rows = side_by_side([gates_t[c:c + 1, :] for c in ci])
    g_heads = [bcum_t[c:c + 1, last:last + 1] for c in cf]
    cn, m_heads = (None, [jnp.zeros((1, 1), F32)] * nh) if first else state
    over_seg = lambda xs: side_by_side([jnp.broadcast_to(x, (1, seg)) for x in xs])
    m_rows, g_rows = over_seg(m_heads), over_seg(g_heads)
    vt_cat = side_by_side([vt_all[h * LSTM_DIM:(h + 1) * LSTM_DIM, :] for h in range(nh)])
    if q_all is not None:
        q_blk = block_diag(q_all)
        per_key = side_by_side([jnp.broadcast_to(gates_t[i:i + 1, :] - bcum_t[f:f + 1, :], (seg, seg)).T
                                for i, f in zip(ci, cf)])
        log_d = jnp.where(side_by_side([before] * nh), b_rows + per_key, -jnp.inf)
        m_row = jnp.maximum(b_rows + m_rows, jnp.max(log_d, axis=0, keepdims=True))
        sm = _dot_t(k_all, q_blk) * jnp.exp(log_d - m_row)
        a = jnp.exp(b_rows + m_rows - m_row)
        qc = _dot_t(cn.astype(BF16), q_blk)
        num = jnp.dot(vt_cat, block_diag(sm.astype(BF16)), preferred_element_type=F32) + a * qc[:LSTM_DIM]
        den = jnp.sum(sm, axis=0, keepdims=True) + a * qc[LSTM_DIM:LSTM_DIM + 1]
        out = num / jnp.maximum(jnp.abs(den), jnp.exp(-m_row))
        for h in range(nh):
            ht_ref[h * LSTM_DIM:(h + 1) * LSTM_DIM, cols] = out[:, h * seg:(h + 1) * seg]
    w = g_rows - b_rows + li_rows
    m_new = [jnp.maximum(g_heads[h] + m_heads[h], jnp.max(w[:, h * seg:(h + 1) * seg], axis=1, keepdims=True))
             for h in range(nh)]
    wt = jnp.exp(w - over_seg(m_new))
    aug = jnp.concatenate([(vt_cat.astype(F32) * wt).astype(BF16),
                           jnp.broadcast_to(wt, (BF16_ROWS, nh * seg)).astype(BF16)], axis=0)
    upd = jnp.dot(aug, block_diag(k_all), preferred_element_type=F32)
    if not first:
        decay = side_by_side([jnp.broadcast_to(jnp.exp(g_heads[h] + m_heads[h] - m_new[h]), (1, LSTM_DIM))
                              for h in range(nh)])
        upd = decay * cn + upd
    return upd, m_new


def _mlstm_kernel(qkf_ref, qkb_ref, vtf_ref, vtb_ref, gf_ref, gb_ref, kc_ref, vtc_ref, gc_ref, gbias_ref,
                  hf_ref, hb_ref, cn_scr, m_scr):
    gbias = gbias_ref[...]

    def save(d, state):
        cn, m_heads = state
        cn_scr[d] = cn
        for h, m in enumerate(m_heads):
            m_scr[d * LSTM_HEADS + h] = jnp.broadcast_to(m, (1, LANES))

    @pl.when(pl.program_id(1) == 0)
    def _():
        gates = gc_ref[...] + gbias
        for d in range(2):
            save(d, _mlstm_segment(d, gates, kc_ref[...], vtc_ref[...], None, None, None, None))

    states = [(cn_scr[d], [m_scr[d * LSTM_HEADS + h][:, 0:1] for h in range(LSTM_HEADS)]) for d in range(2)]
    per_step = qkf_ref.shape[0] // CHUNK
    for u in range(per_step):
        for d, (qk_ref, vt_ref, g_ref, h_ref) in enumerate(((qkf_ref, vtf_ref, gf_ref, hf_ref),
                                                            (qkb_ref, vtb_ref, gb_ref, hb_ref))):
            j = u if d == 0 else per_step - 1 - u
            rows = slice(j * CHUNK, (j + 1) * CHUNK)
            qk = qk_ref[rows, :]
            states[d] = _mlstm_segment(d, g_ref[rows, :] + gbias, qk[:, LSTM_WIDTH:], vt_ref[:, rows],
                                       qk[:, :LSTM_WIDTH], states[d], h_ref, rows)
    for d in range(2):
        save(d, states[d])


def _mlstm(qk, vt, g, qkc, vtc, gc, gate_bias):
    b, n, _ = qk.shape
    lc = qkc.shape[1]
    rows = MLSTM_STEP_CHUNKS * CHUNK
    nc = n // rows
    qkw = 2 * LSTM_WIDTH
    in_specs = [
        pl.BlockSpec((None, rows, qkw), lambda bi, c: (bi, c, 0)),
        pl.BlockSpec((None, rows, qkw), lambda bi, c: (bi, nc - 1 - c, 0)),
        pl.BlockSpec((None, LSTM_WIDTH, rows), lambda bi, c: (bi, 0, c)),
        pl.BlockSpec((None, LSTM_WIDTH, rows), lambda bi, c: (bi, 0, nc - 1 - c)),
        pl.BlockSpec((None, rows, LANES), lambda bi, c: (bi, c, 0)),
        pl.BlockSpec((None, rows, LANES), lambda bi, c: (bi, nc - 1 - c, 0)),
        pl.BlockSpec((None, lc, LSTM_WIDTH), lambda bi, c: (bi, 0, 1)),
        pl.BlockSpec((None, LSTM_WIDTH, lc), lambda bi, c: (bi, 0, 0)),
        pl.BlockSpec((None, lc, LANES), lambda bi, c: (bi, 0, 0)),
        pl.BlockSpec((1, LANES), lambda bi, c: (0, 0))]
    out_specs = [pl.BlockSpec((None, LSTM_WIDTH, rows), lambda bi, c: (bi, 0, c)),
                 pl.BlockSpec((None, LSTM_WIDTH, rows), lambda bi, c: (bi, 0, nc - 1 - c))]
    return pl.pallas_call(
        _mlstm_kernel,
        out_shape=[jax.ShapeDtypeStruct((b, LSTM_WIDTH, n), F32)] * 2,
        grid=(b, nc), in_specs=in_specs, out_specs=out_specs,
        scratch_shapes=[pltpu.VMEM((2, STATE_ROWS, LSTM_WIDTH), F32),
                        pltpu.VMEM((2 * LSTM_HEADS, 1, LANES), F32)],
        compiler_params=_cparams(("arbitrary", "arbitrary")), name="mlstm",
    )(qk, qk, vt, vt, g, g, qkc, vtc, gc, gate_bias)


def _rope(x, cos, sin_signed):
    w = x.shape[1]
    lane = lax.broadcasted_iota(I32, x.shape, 1)
    first = (lane & (ATTN_DIM - 1)) < (ATTN_DIM // 2)
    partner = jnp.where(first, pltpu.roll(x, w - ATTN_DIM // 2, 1), pltpu.roll(x, ATTN_DIM // 2, 1))
    return x * cos + partner * sin_signed


def _attn_kernel(q_ref, kp_ref, kc_ref, kn_ref, vtp_ref, vtc_ref, vtn_ref, kctx_ref, vtctx_ref,
                 tp_ref, tc_ref, tn_ref, sink_ref, bias_ref, o_ref):
    def table(t_ref):
        t = t_ref[...]
        return t[:, :LANES], t[:, LANES:]

    cos_c, sin_c = table(tc_ref)
    q = _rope(q_ref[...].astype(F32), jnp.concatenate([cos_c] * 4, axis=1), jnp.concatenate([sin_c] * 4, axis=1))
    q = q * (ATTN_DIM ** -0.5)
    ks = []
    for k_ref, t_ref in ((kp_ref, tp_ref), (kc_ref, tc_ref), (kn_ref, tn_ref)):
        cos_t, sin_t = table(t_ref)
        ks.append(_rope(k_ref[...].astype(F32), cos_t, sin_t).astype(BF16))
    k_all = jnp.concatenate(ks + [kctx_ref[...]], axis=0)
    vt_all = jnp.concatenate([vtp_ref[...], vtc_ref[...], vtn_ref[...], vtctx_ref[...]], axis=1)

    per = ATTN_HEADS_PER_DOT
    bias = jnp.concatenate([bias_ref[...]] * per, axis=1)
    lane = lax.broadcasted_iota(I32, (CHUNK, LANES), 1)
    sink = sink_ref[...]

    for h0 in range(0, ATTN_HEADS, per):
        qs, snk = [], []
        for h in range(h0, h0 + per):
            g = h // ATTN_GROUP
            t = q[:, (h // 2) * LANES:(h // 2 + 1) * LANES]
            if h % 2 != g:
                t = pltpu.roll(t, ATTN_DIM, 1)
            half_g = (lane >= ATTN_DIM) if g == 1 else (lane < ATTN_DIM)
            qs.append(jnp.where(half_g, t, 0.0).astype(BF16))
            snk.append(jnp.broadcast_to(sink[:, h:h + 1], (1, CHUNK)))
        snk = jnp.concatenate(snk, axis=1)
        st = _dot_t(k_all, jnp.concatenate(qs, axis=0)) + bias
        m = jnp.maximum(jnp.max(st, axis=0, keepdims=True), snk)
        e = jnp.exp(st - m)
        den = jnp.sum(e, axis=0, keepdims=True) + jnp.exp(snk - m)
        pv = jnp.dot(vt_all, e.astype(BF16), preferred_element_type=F32)
        o = (pv / den).astype(o_ref.dtype)
        for r, h in enumerate(range(h0, h0 + per)):
            g = h // ATTN_GROUP
            o_ref[h * ATTN_DIM:(h + 1) * ATTN_DIM, :] = o[g * ATTN_DIM:(g + 1) * ATTN_DIM, r * CHUNK:(r + 1) * CHUNK]


def _attn_bias(nctx):
    i = jnp.arange(CHUNK)[None, :]
    j = jnp.arange(3 * CHUNK)[:, None]
    band = (j >= i) & (j <= i + 2 * CHUNK)
    local = jnp.stack([band & (j >= CHUNK), band, band & (j < 2 * CHUNK)])
    return jnp.concatenate([jnp.where(local, 0.0, -jnp.inf).astype(F32), jnp.zeros((3, nctx, CHUNK), F32)], axis=1)


def _attn(p, avt, pc, avtc, table, sink):
    b, n, _ = p.shape
    lc = pc.shape[1]
    nb = n // CHUNK
    assert nb >= 2
    qw = ATTN_HEADS * ATTN_DIM
    bias = _attn_bias(lc)

    def blk(col, off):
        return pl.BlockSpec((None, CHUNK, LANES), lambda bi, i: (bi, jnp.clip(i + off, 0, nb - 1), col))

    def vblk(off):
        return pl.BlockSpec((None, LANES, CHUNK), lambda bi, i: (bi, 0, jnp.clip(i + off, 0, nb - 1)))

    def tab(off):
        return pl.BlockSpec((CHUNK, 2 * LANES), lambda bi, i: (jnp.clip(i + off, 0, nb - 1), 0))

    in_specs = [pl.BlockSpec((None, CHUNK, qw), lambda bi, i: (bi, i, PB_AQ)),
                blk(PB_AK, -1), blk(PB_AK, 0), blk(PB_AK, 1), vblk(-1), vblk(0), vblk(1),
                pl.BlockSpec((None, lc, LANES), lambda bi, i: (bi, 0, PB_AK)),
                pl.BlockSpec((None, LANES, lc), lambda bi, i: (bi, 0, 0)),
                tab(-1), tab(0), tab(1),
                pl.BlockSpec((1, LANES), lambda bi, i: (0, 0)),
                pl.BlockSpec((None,) + bias.shape[1:],
                             lambda bi, i: (jnp.where(i == 0, 0, jnp.where(i == nb - 1, 2, 1)), 0, 0))]
    return pl.pallas_call(
        _attn_kernel,
        out_shape=jax.ShapeDtypeStruct((b, qw, n), BF16),
        grid=(b, nb), in_specs=in_specs,
        out_specs=pl.BlockSpec((None, qw, CHUNK), lambda bi, i: (bi, 0, i)),
        compiler_params=_cparams(("parallel", "parallel")), name="window_attn",
    )(p, p, p, p, avt, avt, avt, pc, avtc, table, table, table, sink, bias)


def _router_tail(x_new, g2, sc2, sh2, wr_ref, x_out_ref, h2_ref, aff_ref):
    x_out_ref[...] = x_new
    h2 = _rms_mod(x_new, g2, sc2, sh2)
    h_hi = h2.astype(BF16)
    h2_ref[...] = h_hi
    logits = jnp.dot(_hi_lo_lhs(h2, h_hi), wr_ref[...], preferred_element_type=F32)
    lane = lax.broadcasted_iota(I32, logits.shape, 1)
    logits = jnp.where(lane < N_EXPERTS, logits, -jnp.inf)
    e = jnp.exp(logits - jnp.max(logits, axis=1, keepdims=True))
    aff_ref[...] = e / jnp.sum(e, axis=1, keepdims=True)


def _ab_out_kernel(hf_ref, hb_ref, o_ref, at_ref, x_ref, hg_ref, wo_ref, gt1_ref, g2_ref, sc2_ref, sh2_ref, wr_ref,
                   x_out_ref, h2_ref, aff_ref):
    hsum = (hf_ref[...] + hb_ref[...]).T
    og = jax.nn.sigmoid(o_ref[...].astype(F32))
    hg = hg_ref[...]
    parts = []
    for h in range(LSTM_HEADS):
        sl = slice(h * LSTM_DIM, (h + 1) * LSTM_DIM)
        seg = hsum[:, sl]
        seg = seg * lax.rsqrt(jnp.mean(seg * seg, axis=-1, keepdims=True) + EPS)
        parts.append((seg * hg[:, sl] * og[:, sl]).astype(BF16))
    cat = jnp.concatenate(parts + [at_ref[...].astype(F32).T.astype(BF16)], axis=1)
    y = jnp.dot(cat, wo_ref[...], preferred_element_type=F32)
    _router_tail(x_ref[...] + gt1_ref[...] * y, g2_ref[...], sc2_ref[...], sh2_ref[...], wr_ref,
                 x_out_ref, h2_ref, aff_ref)


def _tail_out(b, n, d, tm):
    shapes = [jax.ShapeDtypeStruct((b, n, d), F32), jax.ShapeDtypeStruct((b, n, d), BF16),
              jax.ShapeDtypeStruct((b, n, LANES), F32)]
    specs = [pl.BlockSpec((None, tm, d), lambda bi, i: (bi, i, 0)),
             pl.BlockSpec((None, tm, d), lambda bi, i: (bi, i, 0)),
             pl.BlockSpec((None, tm, LANES), lambda bi, i: (bi, i, 0))]
    return shapes, specs


def _ab_out(hf, hb, p, at, x, head_g, w_out, gt1, g2, sc2, sh2, wr, tm=256):
    b, n, d = x.shape
    row = lambda w: pl.BlockSpec((None, tm, w), lambda bi, i: (bi, i, 0))
    vec = pl.BlockSpec((None, 1, d), lambda bi, i: (bi, 0, 0))
    const = lambda s: pl.BlockSpec(s, lambda bi, i: (0, 0))
    scan_out = pl.BlockSpec((None, LSTM_WIDTH, tm), lambda bi, i: (bi, 0, i))
    in_specs = [scan_out, scan_out,
                pl.BlockSpec((None, tm, LSTM_WIDTH), lambda bi, i: (bi, i, PB_O)),
                pl.BlockSpec((None, ATTN_HEADS * ATTN_DIM, tm), lambda bi, i: (bi, 0, i)),
                row(d), const((1, LSTM_WIDTH)), const(w_out.shape),
                vec, const((1, d)), vec, vec, const((3 * d, LANES))]
    shapes, specs = _tail_out(b, n, d, tm)
    return pl.pallas_call(
        _ab_out_kernel, out_shape=shapes, grid=(b, n // tm), in_specs=in_specs, out_specs=specs,
        compiler_params=_cparams(("parallel", "parallel")), name="ab_out",
    )(hf, hb, p, at, x, head_g, w_out, gt1, g2, sc2, sh2, wr)


def _gm_out_kernel(uv_ref, x_ref, lng_ref, lnb_ref, ws_ref, bs_ref, wo_ref, gt1_ref, g2_ref, sc2_ref, sh2_ref, wr_ref,
                   x_out_ref, h2_ref, aff_ref):
    tm = uv_ref.shape[0]
    gw = GM_HALF // GM_GROUPS
    v = uv_ref[:, GM_HALF:].astype(F32)
    mu = jnp.mean(v, axis=-1, keepdims=True)
    vc = v - mu
    var = jnp.mean(vc * vc, axis=-1, keepdims=True)
    vn = (vc * lax.rsqrt(var + EPS) * lng_ref[...] + lnb_ref[...]).astype(BF16)
    zs = []
    for ch in range(tm // CHUNK):
        rows = slice(ch * CHUNK, (ch + 1) * CHUNK)
        cols = []
        for g in range(GM_GROUPS):
            sv = jnp.dot(ws_ref[g], vn[rows, g * gw:(g + 1) * gw], preferred_element_type=F32)
            cols.append(sv + bs_ref[:, g:g + 1])
        sv = jnp.concatenate(cols, axis=1)
        zs.append((uv_ref[rows, :GM_HALF].astype(F32) * sv).astype(BF16))
    z = jnp.concatenate(zs, axis=0)
    y = jnp.dot(z, wo_ref[...], preferred_element_type=F32)
    _router_tail(x_ref[...] + gt1_ref[...] * y, g2_ref[...], sc2_ref[...], sh2_ref[...], wr_ref,
                 x_out_ref, h2_ref, aff_ref)


def _gm_out(uv, x, ln_g, ln_b, w_s, b_s_t, w_out, gt1, g2, sc2, sh2, wr, tm=256):
    b, n, d = x.shape
    vec = pl.BlockSpec((None, 1, d), lambda bi, i: (bi, 0, 0))
    const = lambda s: pl.BlockSpec(s, lambda *_: (0,) * len(s))
    in_specs = [pl.BlockSpec((None, tm, 2 * GM_HALF), lambda bi, i: (bi, i, 0)),
                pl.BlockSpec((None, tm, d), lambda bi, i: (bi, i, 0)),
                const((1, GM_HALF)), const((1, GM_HALF)), const(w_s.shape), const(b_s_t.shape), const(w_out.shape),
                vec, const((1, d)), vec, vec, const((3 * d, LANES))]
    shapes, specs = _tail_out(b, n, d, tm)
    return pl.pallas_call(
        _gm_out_kernel, out_shape=shapes, grid=(b, n // tm), in_specs=in_specs, out_specs=specs,
        compiler_params=_cparams(("parallel", "parallel")), name="gm_out",
    )(uv, x, ln_g, ln_b, w_s, b_s_t, w_out, gt1, g2, sc2, sh2, wr)


def _route_kernel(aff_ref, pos_ref, post_ref, offs_ref, afft_ref, *, cap):
    n = aff_ref.shape[0]
    nblk = n // CHUNK

    def to_expert_major(k, _):
        rows = pl.ds(pl.multiple_of(k * CHUNK, CHUNK), CHUNK)
        afft_ref[k] = aff_ref[rows, :].T[:N_EXPERTS, :]
        return 0

    lax.fori_loop(0, nblk, to_expert_major, 0, unroll=4)

    def count(pred):
        per_lane = jnp.sum(jnp.where(pred, 1.0, 0.0), axis=0)
        return jnp.sum(per_lane, axis=1, keepdims=True)

    def search(i, prefix):
        cand = prefix | jnp.left_shift(jnp.int32(1), 30 - i)
        cand_f = lax.bitcast_convert_type(cand, F32)
        return jnp.where(count(afft_ref[...] >= cand_f[None]) >= cap, cand, prefix)

    thr_col = lax.bitcast_convert_type(lax.fori_loop(0, 31, search, jnp.zeros((N_EXPERTS, 1), I32)), F32)
    need_col = cap - count(afft_ref[...] > thr_col[None])

    def to_row(col):
        full = jnp.concatenate([jnp.broadcast_to(col, (N_EXPERTS, LANES)),
                                jnp.zeros((LANES - N_EXPERTS, LANES), F32)], axis=0)
        return full.T[0:1, :]

    thr, need = to_row(thr_col), to_row(need_col)
    tril = jnp.where(_tri(CHUNK, True), 1.0, 0.0).astype(BF16)

    def block(k, carry):
        run_tie, run_sel = carry
        rows = pl.ds(pl.multiple_of(k * CHUNK, CHUNK), CHUNK)
        a = aff_ref[rows, :]
        gt = a > thr
        tie = jnp.where(a == thr, 1.0, 0.0)
        tie_incl = jnp.dot(tril, tie.astype(BF16), preferred_element_type=F32)
        sel = jnp.where(gt | ((tie > 0.0) & (tie_incl - tie + run_tie < need)), 1.0, 0.0)
        sel_incl = jnp.dot(tril, sel.astype(BF16), preferred_element_type=F32)
        pos = jnp.where(sel > 0.0, sel_incl - sel + run_sel, -1.0)
        pos_ref[rows, :] = pos.astype(I32)
        post_ref[k] = pos.T[:N_EXPERTS, :].astype(I32)
        offs_ref[k] = run_sel.astype(I32)
        return (run_tie + tie_incl[CHUNK - 1:CHUNK, :], run_sel + sel_incl[CHUNK - 1:CHUNK, :])

    zero = jnp.zeros((1, LANES), F32)
    lax.fori_loop(0, nblk, block, (zero, zero), unroll=4)


def _route(aff, cap):
    b, n, _ = aff.shape
    nblk = n // CHUNK
    return pl.pallas_call(
        functools.partial(_route_kernel, cap=cap),
        out_shape=[jax.ShapeDtypeStruct((b, n, LANES), I32),
                   jax.ShapeDtypeStruct((b, nblk, N_EXPERTS, CHUNK), I32),
                   jax.ShapeDtypeStruct((b, nblk, 1, LANES), I32),
                   jax.ShapeDtypeStruct((b, nblk, N_EXPERTS, CHUNK), F32)],
        grid=(b,),
        in_specs=[pl.BlockSpec((None, n, LANES), lambda bi: (bi, 0, 0))],
        out_specs=[pl.BlockSpec((None, n, LANES), lambda bi: (bi, 0, 0)),
                   pl.BlockSpec((None, nblk, N_EXPERTS, CHUNK), lambda bi: (bi, 0, 0, 0)),
                   pl.BlockSpec((None, nblk, 1, LANES), lambda bi: (bi, 0, 0, 0)),
                   pl.BlockSpec((None, nblk, N_EXPERTS, CHUNK), lambda bi: (bi, 0, 0, 0))],
        compiler_params=_cparams(("parallel",)), name="route",
    )(aff)


def _window_start(s0, w, cap):
    lo = ((s0 >> 4) << 4) + w * SLOT_CHUNK
    return lo, pl.multiple_of(jnp.minimum(lo, cap - SLOT_CHUNK), BF16_ROWS)


def _num_windows(s0, s1):
    return (s1 - ((s0 >> 4) << 4) + SLOT_CHUNK - 1) >> (SLOT_CHUNK.bit_length() - 1)


def _moe_gather_kernel(cnt_ref, post_ref, afft_ref, h_ref, xe_ref, gate_ref, *, ntile, cap, group, tiles_per_step):
    bi, eg, ts = pl.program_id(0), pl.program_id(1), pl.program_id(2)

    @pl.when(ts == 0)
    def _():
        xe_ref[...] = jnp.zeros_like(xe_ref)
        gate_ref[...] = jnp.zeros_like(gate_ref)

    slot = lax.broadcasted_iota(I32, (SLOT_CHUNK, MOE_TILE), 0)
    for u in range(tiles_per_step):
        htile = h_ref[u * MOE_TILE:(u + 1) * MOE_TILE, :]
        t = ts * tiles_per_step + u
        bases = [(bi * N_EXPERTS + eg * group + g) * (ntile + 1) + t for g in range(group)]

        def windows(w, _, u=u, htile=htile, bases=bases):
            starts, onehots = [], []
            for g in range(group):
                lo, start = _window_start(cnt_ref[bases[g]], w, cap)
                posrow = post_ref[g, u]
                hit = (posrow - start == slot) & (posrow >= lo)
                onehots.append(jnp.where(hit, 1.0, 0.0).astype(BF16))
                gates = jnp.sum(jnp.where(hit, afft_ref[g, u], 0.0), axis=1, keepdims=True)
                dst = pl.ds(start, SLOT_CHUNK)
                gate_ref[g, dst, :] = gate_ref[g, dst, :] + jnp.broadcast_to(gates, (SLOT_CHUNK, LANES))
                starts.append(start)
            rows = jnp.dot(jnp.concatenate(onehots, axis=0), htile, preferred_element_type=F32)
            for g in range(group):
                dst = pl.ds(starts[g], SLOT_CHUNK)
                xe_ref[g, dst, :] = xe_ref[g, dst, :] + rows[g * SLOT_CHUNK:(g + 1) * SLOT_CHUNK].astype(BF16)
            return 0

        windows(0, 0)
        nwin = 1
        for g in range(group):
            nwin = jnp.maximum(nwin, _num_windows(cnt_ref[bases[g]], cnt_ref[bases[g] + 1]))
        lax.fori_loop(1, nwin, windows, 0)


def _moe_gather(cnt, post, afft, h2, cap, group=8, tiles_per_step=8):
    b, n, d = h2.shape
    ntile = n // MOE_TILE
    per_tile = pl.BlockSpec((None, group, tiles_per_step, 1, MOE_TILE), lambda bi, eg, ts, c: (bi, eg, ts, 0, 0))
    grid_spec = pltpu.PrefetchScalarGridSpec(
        num_scalar_prefetch=1, grid=(b, N_EXPERTS // group, ntile // tiles_per_step),
        in_specs=[per_tile, per_tile,
                  pl.BlockSpec((None, tiles_per_step * MOE_TILE, d), lambda bi, eg, ts, c: (bi, ts, 0))],
        out_specs=[pl.BlockSpec((None, group, cap, d), lambda bi, eg, ts, c: (bi, eg, 0, 0)),
                   pl.BlockSpec((None, group, cap, LANES), lambda bi, eg, ts, c: (bi, eg, 0, 0))])
    return pl.pallas_call(
        functools.partial(_moe_gather_kernel, ntile=ntile, cap=cap, group=group, tiles_per_step=tiles_per_step),
        out_shape=[jax.ShapeDtypeStruct((b, N_EXPERTS, cap, d), BF16),
                   jax.ShapeDtypeStruct((b, N_EXPERTS, cap, LANES), F32)],
        grid_spec=grid_spec, compiler_params=_cparams(("arbitrary", "arbitrary", "arbitrary")), name="moe_gather",
    )(cnt, post, afft, h2)


def _moe_ffn_kernel(xe_ref, gate_ref, wg_ref, wu_ref, wd_ref, y_ref, acc_scr, *, hid_tile):
    j = pl.program_id(2)
    xe = xe_ref[...]
    for k in range(wg_ref.shape[1] // hid_tile):
        cols = slice(k * hid_tile, (k + 1) * hid_tile)
        gate = jnp.dot(xe, wg_ref[:, cols].astype(BF16), preferred_element_type=F32)
        up = jnp.dot(xe, wu_ref[:, cols].astype(BF16), preferred_element_type=F32)
        hid = (_silu(gate) * up).astype(BF16)
        part = jnp.dot(hid, wd_ref[cols, :].astype(BF16), preferred_element_type=F32)
        if k == 0:
            @pl.when(j == 0)
            def _():
                acc_scr[...] = part

            @pl.when(j != 0)
            def _():
                acc_scr[...] += part
        else:
            acc_scr[...] += part

    @pl.when(j == pl.num_programs(2) - 1)
    def _():
        y_ref[...] = (acc_scr[...] * gate_ref[:, 0:1]).astype(y_ref.dtype)


def _moe_ffn(xe, gate, wg, wu, wd, layer, hid_split=1):
    b, ne, cap, d = xe.shape
    dh = wg.shape[3] // hid_split
    return pl.pallas_call(
        functools.partial(_moe_ffn_kernel, hid_tile=256),
        out_shape=jax.ShapeDtypeStruct((b, ne, cap, d), BF16),
        grid=(b, ne, hid_split),
        in_specs=[pl.BlockSpec((None, None, cap, d), lambda bi, e, j: (bi, e, 0, 0)),
                  pl.BlockSpec((None, None, cap, LANES), lambda bi, e, j: (bi, e, 0, 0)),
                  pl.BlockSpec((None, None, d, dh), lambda bi, e, j: (layer, e, 0, j)),
                  pl.BlockSpec((None, None, d, dh), lambda bi, e, j: (layer, e, 0, j)),
                  pl.BlockSpec((None, None, dh, d), lambda bi, e, j: (layer, e, j, 0))],
        out_specs=pl.BlockSpec((None, None, cap, d), lambda bi, e, j: (bi, e, 0, 0)),
        scratch_shapes=[pltpu.VMEM((cap, d), F32)],
        compiler_params=_cparams(("parallel", "parallel", "arbitrary")), name="moe_ffn",
    )(xe, gate, wg, wu, wd)


def _moe_combine_kernel(cnt_ref, pos_ref, x_ref, gt2_ref, y_ref, fg_ref, o_ref, acc_scr, ycat_scr, *, ntile, cap, final):
    for u in range(pos_ref.shape[0] // MOE_TILE):
        rows = slice(u * MOE_TILE, (u + 1) * MOE_TILE)
        _moe_combine_tile(cnt_ref, pos_ref.at[rows], x_ref.at[rows], gt2_ref, y_ref, fg_ref, o_ref.at[rows],
                          acc_scr.at[u], ycat_scr.at[u], pl.program_id(1) * (pos_ref.shape[0] // MOE_TILE) + u,
                          ntile=ntile, cap=cap, final=final)


def _moe_combine_tile(cnt_ref, pos_ref, x_ref, gt2_ref, y_ref, fg_ref, o_ref, acc_scr, ycat_scr, t, *, ntile, cap, final):
    bi = pl.program_id(0)
    kw = N_EXPERTS * SLOT_CHUNK
    shift = SLOT_CHUNK.bit_length() - 1
    sel = jnp.where(lax.broadcasted_iota(I32, (LANES, kw), 0) == (lax.broadcasted_iota(I32, (LANES, kw), 1) >> shift),
                    1.0, 0.0).astype(BF16)

    def spread(v):
        hi = (v >> 5).astype(F32).astype(BF16)
        lo = (v & 31).astype(F32).astype(BF16)
        return 32.0 * jnp.dot(hi, sel, preferred_element_type=F32) + jnp.dot(lo, sel, preferred_element_type=F32)

    pos_w = spread(pos_ref[...] + 1) - 1.0
    within = (lax.broadcasted_iota(I32, (1, kw), 1) & (SLOT_CHUNK - 1)).astype(F32)
    lane = lax.broadcasted_iota(I32, (SUBLANES, LANES), 1)

    def windows(w):
        starts = jnp.zeros((SUBLANES, LANES), I32)
        los = jnp.zeros((SUBLANES, LANES), I32)
        for e in range(N_EXPERTS):
            lo, start = _window_start(cnt_ref[(bi * N_EXPERTS + e) * (ntile + 1) + t], w, cap)
            starts = jnp.where(lane == e, start, starts)
            los = jnp.where(lane == e, lo, los)
            ycat_scr[e * SLOT_CHUNK:(e + 1) * SLOT_CHUNK, :] = y_ref[e, pl.ds(start, SLOT_CHUNK), :]
        want = spread(starts)[0:1] + within
        onehot = jnp.where((pos_w == want) & (pos_w >= spread(los)[0:1]), 1.0, 0.0).astype(BF16)
        return jnp.dot(onehot, ycat_scr[...], preferred_element_type=F32)

    acc_scr[...] = windows(0)
    nwin = 1
    for e in range(N_EXPERTS):
        base = (bi * N_EXPERTS + e) * (ntile + 1) + t
        nwin = jnp.maximum(nwin, _num_windows(cnt_ref[base], cnt_ref[base + 1]))

    def overflow(w, _):
        acc_scr[...] += windows(w)
        return 0

    lax.fori_loop(1, nwin, overflow, 0)
    out = x_ref[...] + gt2_ref[...] * acc_scr[...]
    if final:
        out = out * lax.rsqrt(jnp.mean(out * out, axis=-1, keepdims=True) + EPS) * fg_ref[...]
    o_ref[...] = out


def _moe_combine(cnt, pos, x, gt2, y, final_g, final, tiles_per_step=2):
    b, n, d = x.shape
    ntile = n // MOE_TILE
    cap = y.shape[2]
    rows = tiles_per_step * MOE_TILE
    grid_spec = pltpu.PrefetchScalarGridSpec(
        num_scalar_prefetch=1, grid=(b, ntile // tiles_per_step),
        in_specs=[pl.BlockSpec((None, rows, LANES), lambda bi, t, c: (bi, t, 0)),
                  pl.BlockSpec((None, rows, d), lambda bi, t, c: (bi, t, 0)),
                  pl.BlockSpec((None, 1, d), lambda bi, t, c: (bi, 0, 0)),
                  pl.BlockSpec((None, N_EXPERTS, cap, d), lambda bi, t, c: (bi, 0, 0, 0),
                               pipeline_mode=pl.Buffered(1)),
                  pl.BlockSpec((1, d), lambda bi, t, c: (0, 0))],
        out_specs=pl.BlockSpec((None, rows, d), lambda bi, t, c: (bi, t, 0)),
        scratch_shapes=[pltpu.VMEM((tiles_per_step, MOE_TILE, d), F32),
                        pltpu.VMEM((tiles_per_step, N_EXPERTS * SLOT_CHUNK, d), BF16)])
    return pl.pallas_call(
        functools.partial(_moe_combine_kernel, ntile=ntile, cap=cap, final=final),
        out_shape=jax.ShapeDtypeStruct((b, n, d), F32),
        grid_spec=grid_spec, compiler_params=_cparams(("arbitrary", "arbitrary")), name="moe_combine",
    )(cnt, pos, x, gt2, y, final_g)


def _ec_moe(x_mid, h2, aff, gt2, wg, wu, wd, layer, final_g, final):
    b, n, _ = x_mid.shape
    cap = max(1, EC_FACTOR * n // N_EXPERTS)
    ntile = n // MOE_TILE
    pos, post, offs, afft = _route(aff, cap)
    per_tile = lambda a: a.transpose(0, 2, 1, 3).reshape(b, N_EXPERTS, ntile, 1, MOE_TILE)
    starts = offs[:, ::MOE_TILE // CHUNK, 0, :N_EXPERTS].transpose(0, 2, 1)
    cnt = jnp.concatenate([starts, jnp.full((b, N_EXPERTS, 1), cap, I32)], axis=2).reshape(-1)
    xe, gate = _moe_gather(cnt, per_tile(post), per_tile(afft), h2, cap)
    y = _moe_ffn(xe, gate, wg, wu, wd, layer)
    return _moe_combine(cnt, pos, x_mid, gt2, y, final_g, final)


def _rope_table(n):
    rows = n // GRID_W
    row = jnp.repeat(jnp.arange(rows), GRID_W).astype(F32)
    col = jnp.tile(jnp.arange(GRID_W), rows).astype(F32)
    nf = ATTN_DIM // 4
    inv = ROPE_BASE ** (-jnp.arange(nf, dtype=F32) / nf)
    ang = jnp.concatenate([row[:, None] * inv, col[:, None] * inv], axis=-1)
    cos, sin = jnp.cos(ang), jnp.sin(ang)
    reps = LANES // ATTN_DIM
    return jnp.concatenate([jnp.tile(jnp.concatenate([cos, cos], -1), (1, reps)),
                            jnp.tile(jnp.concatenate([-sin, sin], -1), (1, reps))], axis=-1)


def _pad_lanes(a):
    return jnp.pad(a, ((0, 0), (0, LANES - a.shape[1])))


def kernel(x, c, ctx, c_ctx, w_mod, b_mod, norm_mix_g, norm_ffn_g, final_norm_g, ab_w_in, ab_conv_w, ab_gate_b,
           ab_head_g, ab_sink, ab_w_out, gm_w_in, gm_ln_g, gm_ln_b, gm_w_s, gm_b_s, gm_w_out, moe_w_router,
           moe_w_gate, moe_w_up, moe_w_down):
    b, n, d = x.shape
    depth = w_mod.shape[0]
    assert depth <= 2, "context stream is only advanced for deeper stacks; not supported here"
    cond = jnp.zeros((SUBLANES, d), F32).at[:b].set(c).at[b].set(c_ctx)
    mods = _adaln(cond, w_mod, b_mod)

    def mod_rows(layer, rows):
        m = mods[layer, rows].reshape(-1, 6, 1, d)
        return [m[:, i] for i in range(6)]

    row = lambda v: v.reshape(1, -1)
    for layer in range(depth):
        sh1, sc1, gt1, sh2, sc2, gt2 = mod_rows(layer, slice(0, b))
        g1, g2 = row(norm_mix_g[layer]), row(norm_ffn_g[layer])
        wr = _hi_lo_rhs(_pad_lanes(moe_w_router[layer]))
        if layer % 2 == 0:
            e = layer // 2
            csh1, csc1 = (jnp.broadcast_to(v, (b, 1, d)) for v in mod_rows(layer, slice(b, b + 1))[:2])
            w_in = ab_w_in[e]
            g_lo = 4 * LSTM_WIDTH
            w_main = jnp.concatenate([w_in[:, :g_lo], w_in[:, g_lo + N_GATES:]], axis=1).astype(BF16)
            w_gate = _hi_lo_rhs(_pad_lanes(w_in[:, g_lo:g_lo + N_GATES]))
            p, gts, vt, avt = _modmm(x, g1, sc1, sh1, w_main, w_gate, tm=1024, name="ab_in")
            pc, gtc, vtc, avtc = _modmm(ctx, g1, csc1, csh1, w_main, w_gate, name="ab_in_ctx")
            hf, hb = _mlstm(_conv_silu(p, ab_conv_w[e]), vt, gts, _conv_silu(pc, ab_conv_w[e]), vtc, gtc,
                            _pad_lanes(row(ab_gate_b[e])))
            at = _attn(p, avt, pc, avtc, _rope_table(n), _pad_lanes(row(ab_sink[e])))
            x_mid, h2, aff = _ab_out(hf, hb, p, at, x, row(ab_head_g[e]), ab_w_out[e].astype(BF16),
                                     gt1, g2, sc2, sh2, wr)
        else:
            o = layer // 2
            uv = _modmm(x, g1, sc1, sh1, gm_w_in[o].astype(BF16), act="gelu", tm=1024, name="gm_in")
            x_mid, h2, aff = _gm_out(uv, x, row(gm_ln_g[o]), row(gm_ln_b[o]), gm_w_s[o].astype(BF16),
                                     _pad_lanes(gm_b_s[o].T), gm_w_out[o].astype(BF16), gt1, g2, sc2, sh2, wr)
        x = _ec_moe(x_mid, h2, aff, gt2, moe_w_gate, moe_w_up, moe_w_down, layer,
                    row(final_norm_g), layer == depth - 1)
    return x
```

```python
import functools

import jax
import jax.numpy as jnp
from jax import lax
from jax.experimental import pallas as pl
from jax.experimental.pallas import tpu as pltpu

F32 = jnp.float32
BF16 = jnp.bfloat16
I32 = jnp.int32
HI = lax.Precision.HIGHEST

D_MODEL = 1024
GRID_W = 64
EPS = 1e-6
LSTM_HEADS = 4
LSTM_DIM = 128
LSTM_WIDTH = LSTM_HEADS * LSTM_DIM
LSTM_CONV = 5
CHUNK = 128
ATTN_HEADS = 8
ATTN_KV_HEADS = 2
ATTN_GROUP = ATTN_HEADS // ATTN_KV_HEADS
ATTN_DIM = 64
ROPE_BASE = 10000.0
GM_GROUPS = 8
GM_HALF = 2 * D_MODEL
N_EXPERTS = 16
EC_FACTOR = 2
N_GATES = 4 * LSTM_HEADS

LANES = 128
SUBLANES = 8
BF16_ROWS = 16
VMEM_LIMIT_BYTES = 56 * 1024 * 1024

P_COLS = 4 * LSTM_WIDTH + ATTN_HEADS * ATTN_DIM + 2 * ATTN_KV_HEADS * ATTN_DIM
PB_V, PB_O, PB_AQ = 2, 3, 4
PB_AK, PB_AV = 20, 21
ATTN_HEADS_PER_DOT = 8
ATTN_STEP_BLOCKS = 4
TAIL_CHAIN_ROWS = 256
MOE_TILE = 256
SLOT_CHUNK = 64


def _cparams(sem, vmem=VMEM_LIMIT_BYTES):
    return pltpu.CompilerParams(dimension_semantics=sem, vmem_limit_bytes=vmem)


def _rms_mod(x, g, sc, sh):
    y = x * lax.rsqrt(jnp.mean(x * x, axis=-1, keepdims=True) + EPS)
    return y * g * (1.0 + sc) + sh


def _silu(x):
    return x * jax.nn.sigmoid(x)


def _gelu_tanh(x):
    return 0.5 * x * (1.0 + jnp.tanh(0.7978845608028654 * (x + 0.044715 * (x * x * x))))


def _log_sigmoid(x):
    return jnp.minimum(x, 0.0) - jnp.log(1.0 + jnp.exp(-jnp.abs(x)))


def _dot_t(a, b):
    return lax.dot_general(a, b, (((1,), (1,)), ((), ())), preferred_element_type=F32)


def _adaln_kernel(c_ref, w_ref, b_ref, o_ref):
    s = _silu(c_ref[...])
    o_ref[...] = jnp.dot(s, w_ref[...], precision=HI, preferred_element_type=F32) + b_ref[...]


def _adaln(cond, w_mod, b_mod):
    depth, d, six_d = w_mod.shape
    tn = six_d // 4
    return pl.pallas_call(
        _adaln_kernel,
        out_shape=jax.ShapeDtypeStruct((depth, SUBLANES, six_d), F32),
        grid=(depth, six_d // tn),
        in_specs=[pl.BlockSpec((SUBLANES, d), lambda l, j: (0, 0)),
                  pl.BlockSpec((None, d, tn), lambda l, j: (l, 0, j)),
                  pl.BlockSpec((None, 1, tn), lambda l, j: (l, 0, j))],
        out_specs=pl.BlockSpec((None, SUBLANES, tn), lambda l, j: (l, 0, j)),
        compiler_params=_cparams(("arbitrary", "arbitrary")),
        name="adaln",
    )(cond, w_mod, b_mod.reshape(depth, 1, six_d))


def _hi_lo_lhs(h, h_hi):
    return jnp.concatenate([h_hi, (h - h_hi.astype(F32)).astype(BF16), h_hi], axis=1)


def _hi_lo_rhs(w):
    w_hi = w.astype(BF16)
    return jnp.concatenate([w_hi, w_hi, (w - w_hi.astype(F32)).astype(BF16)], axis=0)


def _modmm_kernel(x_ref, g_ref, sc_ref, sh_ref, w_ref, *rest, chunks, act, with_gates):
    if with_gates:
        wg_ref, o_ref, og_ref, vt_ref, avt_ref = rest
    else:
        (o_ref,) = rest
    h = _rms_mod(x_ref[...], g_ref[...], sc_ref[...], sh_ref[...])
    hb = h.astype(BF16)
    for lo, hi in chunks:
        y = jnp.dot(hb, w_ref[:, lo:hi], preferred_element_type=F32)
        if act == "gelu":
            y = _gelu_tanh(y)
        o_ref[:, lo:hi] = y.astype(o_ref.dtype)
        if with_gates and lo == PB_V * LSTM_WIDTH:
            vt_ref[...] = y.T.astype(vt_ref.dtype)
        if with_gates and lo <= PB_AV * LANES < hi:
            av = y[:, PB_AV * LANES - lo:(PB_AV + 1) * LANES - lo]
            avt_ref[...] = av.T.astype(avt_ref.dtype)
    if with_gates:
        og_ref[...] = jnp.dot(_hi_lo_lhs(h, hb), wg_ref[...], preferred_element_type=F32)


def _modmm(x, g, sc, sh, w, wg=None, *, act=None, tm=512, chunk=512, name="modmm"):
    b, n, d = x.shape
    no = w.shape[1]
    tm = min(tm, n)
    assert chunk == LSTM_WIDTH
    chunks = tuple((lo, min(lo + chunk, no)) for lo in range(0, no, chunk))
    in_specs = [pl.BlockSpec((None, tm, d), lambda bi, i: (bi, i, 0)),
                pl.BlockSpec((1, d), lambda bi, i: (0, 0)),
                pl.BlockSpec((None, 1, d), lambda bi, i: (bi, 0, 0)),
                pl.BlockSpec((None, 1, d), lambda bi, i: (bi, 0, 0)),
                pl.BlockSpec((d, no), lambda bi, i: (0, 0))]
    out_shape = [jax.ShapeDtypeStruct((b, n, no), BF16)]
    out_specs = [pl.BlockSpec((None, tm, no), lambda bi, i: (bi, i, 0))]
    args = [x, g, sc, sh, w]
    if wg is not None:
        in_specs.append(pl.BlockSpec(wg.shape, lambda bi, i: (0, 0)))
        out_shape += [jax.ShapeDtypeStruct((b, n, LANES), F32), jax.ShapeDtypeStruct((b, LSTM_WIDTH, n), BF16),
                      jax.ShapeDtypeStruct((b, LANES, n), BF16)]
        out_specs += [pl.BlockSpec((None, tm, LANES), lambda bi, i: (bi, i, 0)),
                      pl.BlockSpec((None, LSTM_WIDTH, tm), lambda bi, i: (bi, 0, i)),
                      pl.BlockSpec((None, LANES, tm), lambda bi, i: (bi, 0, i))]
        args.append(wg)
    res = pl.pallas_call(
        functools.partial(_modmm_kernel, chunks=chunks, act=act, with_gates=wg is not None),
        out_shape=out_shape, grid=(b, n // tm), in_specs=in_specs, out_specs=out_specs,
        compiler_params=_cparams(("parallel", "parallel")), name=name,
    )(*args)
    return res if wg is not None else res[0]


def _conv_silu_kernel(x_ref, xp_ref, xn_ref, w_ref, o_ref, pad_scr):
    i = pl.program_id(1)
    rows = x_ref.shape[0]
    halo = BF16_ROWS
    has_prev = jnp.where(i > 0, 1.0, 0.0)
    has_next = jnp.where(i < pl.num_programs(1) - 1, 1.0, 0.0)
    pad_scr[pl.ds(0, halo), :] = (xp_ref[...].astype(F32) * has_prev).astype(BF16)
    pad_scr[pl.ds(halo, rows), :] = x_ref[...]
    pad_scr[pl.ds(halo + rows, halo), :] = (xn_ref[...].astype(F32) * has_next).astype(BF16)
    w = w_ref[...]
    win = CHUNK + 2 * halo
    r = lax.broadcasted_iota(I32, (CHUNK, win), 0)
    c = lax.broadcasted_iota(I32, (CHUNK, win), 1)
    half = LSTM_CONV // 2
    shifts = {t: jnp.where(c == r + halo + t - half, 1.0, 0.0).astype(BF16) for t in range(LSTM_CONV) if t != half}
    scale = LSTM_DIM ** -0.5
    for blk in range(rows // CHUNK):
        xw = pad_scr[pl.ds(blk * CHUNK, win), :]
        acc = xw[halo:halo + CHUNK].astype(F32) * w[half:half + 1, :]
        for t, s in shifts.items():
            acc = acc + jnp.dot(s, xw, preferred_element_type=F32) * w[t:t + 1, :]
        y = _silu(acc)
        out = pl.ds(blk * CHUNK, CHUNK)
        o_ref[out, :LSTM_WIDTH] = y[:, :LSTM_WIDTH].astype(o_ref.dtype)
        o_ref[out, LSTM_WIDTH:] = (y[:, LSTM_WIDTH:] * scale).astype(o_ref.dtype)


def _conv_silu(p, conv_w, tm=512):
    b, n, _ = p.shape
    tm = min(tm, n)
    qkw = 2 * LSTM_WIDTH
    hpt = tm // BF16_ROWS
    nhb = n // BF16_ROWS
    return pl.pallas_call(
        _conv_silu_kernel,
        out_shape=jax.ShapeDtypeStruct((b, n, qkw), BF16),
        grid=(b, n // tm),
        in_specs=[pl.BlockSpec((None, tm, qkw), lambda bi, i: (bi, i, 0)),
                  pl.BlockSpec((None, BF16_ROWS, qkw), lambda bi, i: (bi, jnp.maximum(i * hpt - 1, 0), 0)),
                  pl.BlockSpec((None, BF16_ROWS, qkw), lambda bi, i: (bi, jnp.minimum((i + 1) * hpt, nhb - 1), 0)),
                  pl.BlockSpec((LSTM_CONV, qkw), lambda bi, i: (0, 0))],
        out_specs=pl.BlockSpec((None, tm, qkw), lambda bi, i: (bi, i, 0)),
        scratch_shapes=[pltpu.VMEM((tm + 2 * BF16_ROWS, qkw), BF16)],
        compiler_params=_cparams(("parallel", "parallel")), name="conv_silu",
    )(p, p, p, conv_w)


def _tri(n, lower):
    r = lax.broadcasted_iota(I32, (n, n), 0)
    c = lax.broadcasted_iota(I32, (n, n), 1)
    return (c <= r) if lower else (c >= r)


STATE_ROWS = LSTM_DIM + BF16_ROWS
MLSTM_STEP_CHUNKS = 8


def _mlstm_segment(d, gates, k_all, vt_all, q_all, state, ht_ref, cols):
    first = state is None
    nh = LSTM_HEADS
    seg = gates.shape[0]

    def side_by_side(pieces):
        return jnp.concatenate(pieces, axis=1)

    def block_diag(x):
        w = x.shape[1] // nh
        lane = lax.broadcasted_iota(I32, x.shape, 1)
        return jnp.concatenate([jnp.where((lane >= h * w) & (lane < (h + 1) * w), x, jnp.zeros_like(x))
                                for h in range(nh)], axis=0)

    r = lax.broadcasted_iota(I32, (seg, seg), 0)
    c = lax.broadcasted_iota(I32, (seg, seg), 1)
    before = (r <= c) if d == 0 else (r >= c)
    gates_t = gates.T[:N_GATES]
    ls = _log_sigmoid(gates_t)
    ls_hi = ls.astype(BF16)
    rest = ls - ls_hi.astype(F32)
    ls_mid = rest.astype(BF16)
    ls_lo = (rest - ls_mid.astype(F32)).astype(BF16)
    parts = jnp.dot(jnp.concatenate([ls_hi, ls_mid, ls_lo], axis=0), jnp.where(before, 1.0, 0.0).astype(BF16),
                    preferred_element_type=F32)
    bcum_t = parts[:N_GATES] + parts[N_GATES:2 * N_GATES] + parts[2 * N_GATES:]
    last = seg - 1 if d == 0 else 0
    ci = [2 * d * nh + h for h in range(nh)]
    cf = [(2 * d + 1) * nh + h for h in range(nh)]
    b_rows = side_by_side([bcum_t[c:c + 1, :] for c in cf])
    li_rows = side_by_side([gates_t[c:c + 1, :] for c in ci])
    g_heads = [bcum_t[c:c + 1, last:last + 1] for c in cf]
    cn, m_heads = (None, [jnp.zeros((1, 1), F32)] * nh) if first else state
    over_seg = lambda xs: side_by_side([jnp.broadcast_to(x, (1, seg)) for x in xs])
    m_rows, g_rows = over_seg(m_heads), over_seg(g_heads)
    vt_cat = side_by_side([vt_all[h * LSTM_DIM:(h + 1) * LSTM_DIM, :] for h in range(nh)])
    if q_all is not None:
        q_blk = block_diag(q_all)
        per_key = side_by_side([jnp.broadcast_to(gates_t[i:i + 1, :] - bcum_t[f:f + 1, :], (seg, seg)).T
                                for i, f in zip(ci, cf)])
        log_d = jnp.where(side_by_side([before] * nh), b_rows + per_key, -jnp.inf)
        m_row = jnp.maximum(b_rows + m_rows, jnp.max(log_d, axis=0, keepdims=True))
        sm = _dot_t(k_all, q_blk) * jnp.exp(log_d - m_row)
        a = jnp.exp(b_rows + m_rows - m_row)
        qc = _dot_t(cn.astype(BF16), q_blk)
        num = jnp.dot(vt_cat, block_diag(sm.astype(BF16)), preferred_element_type=F32) + a * qc[:LSTM_DIM]
        den = jnp.sum(sm, axis=0, keepdims=True) + a * qc[LSTM_DIM:LSTM_DIM + 1]
        out = num / jnp.maximum(jnp.abs(den), jnp.exp(-m_row))
        for h in range(nh):
            ht_ref[h * LSTM_DIM:(h + 1) * LSTM_DIM, cols] = out[:, h * seg:(h + 1) * seg]
    w = g_rows - b_rows + li_rows
    m_new = [jnp.maximum(g_heads[h] + m_heads[h], jnp.max(w[:, h * seg:(h + 1) * seg], axis=1, keepdims=True))
             for h in range(nh)]
    wt = jnp.exp(w - over_seg(m_new))
    aug = jnp.concatenate([(vt_cat.astype(F32) * wt).astype(BF16),
                           jnp.broadcast_to(wt, (BF16_ROWS, nh * seg)).astype(BF16)], axis=0)
    upd = jnp.dot(aug, block_diag(k_all), preferred_element_type=F32)
    if not first:
        decay = side_by_side([jnp.broadcast_to(jnp.exp(g_heads[h] + m_heads[h] - m_new[h]), (1, LSTM_DIM))
                              for h in range(nh)])
        upd = decay * cn + upd
    return upd, m_new


def _mlstm_kernel(qkf_ref, qkb_ref, vtf_ref, vtb_ref, gf_ref, gb_ref, kc_ref, vtc_ref, gc_ref, gbias_ref,
                  hf_ref, hb_ref, cn_scr, m_scr):
    gbias = gbias_ref[...]

    def save(d, state):
        cn, m_heads = state
        cn_scr[d] = cn
        for h, m in enumerate(m_heads):
            m_scr[d * LSTM_HEADS + h] = jnp.broadcast_to(m, (1, LANES))

    @pl.when(pl.program_id(1) == 0)
    def _():
        gates = gc_ref[...] + gbias
        for d in range(2):
            save(d, _mlstm_segment(d, gates, kc_ref[...], vtc_ref[...], None, None, None, None))

    states = [(cn_scr[d], [m_scr[d * LSTM_HEADS + h][:, 0:1] for h in range(LSTM_HEADS)]) for d in range(2)]
    per_step = qkf_ref.shape[0] // CHUNK
    for u in range(per_step):
        for d, (qk_ref, vt_ref, g_ref, h_ref) in enumerate(((qkf_ref, vtf_ref, gf_ref, hf_ref),
                                                            (qkb_ref, vtb_ref, gb_ref, hb_ref))):
            j = u if d == 0 else per_step - 1 - u
            rows = slice(j * CHUNK, (j + 1) * CHUNK)
            qk = qk_ref[rows, :]
            states[d] = _mlstm_segment(d, g_ref[rows, :] + gbias, qk[:, LSTM_WIDTH:], vt_ref[:, rows],
                                       qk[:, :LSTM_WIDTH], states[d], h_ref, rows)
    for d in range(2):
        save(d, states[d])


def _mlstm(qk, vt, g, qkc, vtc, gc, gate_bias):
    b, n, _ = qk.shape
    lc = qkc.shape[1]
    rows = MLSTM_STEP_CHUNKS * CHUNK
    nc = n // rows
    qkw = 2 * LSTM_WIDTH
    in_specs = [
        pl.BlockSpec((None, rows, qkw), lambda bi, c: (bi, c, 0)),
        pl.BlockSpec((None, rows, qkw), lambda bi, c: (bi, nc - 1 - c, 0)),
        pl.BlockSpec((None, LSTM_WIDTH, rows), lambda bi, c: (bi, 0, c)),
        pl.BlockSpec((None, LSTM_WIDTH, rows), lambda bi, c: (bi, 0, nc - 1 - c)),
        pl.BlockSpec((None, rows, LANES), lambda bi, c: (bi, c, 0)),
        pl.BlockSpec((None, rows, LANES), lambda bi, c: (bi, nc - 1 - c, 0)),
        pl.BlockSpec((None, lc, LSTM_WIDTH), lambda bi, c: (bi, 0, 1)),
        pl.BlockSpec((None, LSTM_WIDTH, lc), lambda bi, c: (bi, 0, 0)),
        pl.BlockSpec((None, lc, LANES), lambda bi, c: (bi, 0, 0)),
        pl.BlockSpec((1, LANES), lambda bi, c: (0, 0))]
    out_specs = [pl.BlockSpec((None, LSTM_WIDTH, rows), lambda bi, c: (bi, 0, c)),
                 pl.BlockSpec((None, LSTM_WIDTH, rows), lambda bi, c: (bi, 0, nc - 1 - c))]
    return pl.pallas_call(
        _mlstm_kernel,
        out_shape=[jax.ShapeDtypeStruct((b, LSTM_WIDTH, n), F32)] * 2,
        grid=(b, nc), in_specs=in_specs, out_specs=out_specs,
        scratch_shapes=[pltpu.VMEM((2, STATE_ROWS, LSTM_WIDTH), F32),
                        pltpu.VMEM((2 * LSTM_HEADS, 1, LANES), F32)],
        compiler_params=_cparams(("arbitrary", "arbitrary")), name="mlstm",
    )(qk, qk, vt, vt, g, g, qkc, vtc, gc, gate_bias)


def _rope(x, cos, sin_signed):
    w = x.shape[1]
    lane = lax.broadcasted_iota(I32, x.shape, 1)
    first = (lane & (ATTN_DIM - 1)) < (ATTN_DIM // 2)
    partner = jnp.where(first, pltpu.roll(x, w - ATTN_DIM // 2, 1), pltpu.roll(x, ATTN_DIM // 2, 1))
    return x * cos + partner * sin_signed


def _attn_kernel(q_ref, *refs):
    nblk = q_ref.shape[0] // CHUNK
    nwin = nblk + 2
    k_refs, vt_refs = refs[:nwin], refs[nwin:2 * nwin]
    kctx_ref, vtctx_ref = refs[2 * nwin:2 * nwin + 2]
    t_refs = refs[2 * nwin + 2:3 * nwin + 2]
    sink_ref, bias_ref, o_ref = refs[3 * nwin + 2:]
    step, last_step = pl.program_id(1), pl.num_programs(1) - 1

    def table(t_ref):
        t = t_ref[...]
        return t[:, :LANES], t[:, LANES:]

    ks = []
    for k_ref, t_ref in zip(k_refs, t_refs):
        cos_t, sin_t = table(t_ref)
        ks.append(_rope(k_ref[...].astype(F32), cos_t, sin_t).astype(BF16))
    for u in range(nblk):
        which = jnp.where(step == 0, 0, 1) if u == 0 else 1
        if u == nblk - 1:
            which = jnp.where(step == last_step, 2, which)
        _attn_block(q_ref[u * CHUNK:(u + 1) * CHUNK, :], table(t_refs[u + 1]),
                    jnp.concatenate(ks[u:u + 3] + [kctx_ref[...]], axis=0),
                    jnp.concatenate([r[...] for r in vt_refs[u:u + 3]] + [vtctx_ref[...]], axis=1),
                    sink_ref[...], bias_ref[which], o_ref.at[:, u * CHUNK:(u + 1) * CHUNK])


def _attn_block(q_in, q_table, k_all, vt_all, sink, bias1, o_ref):
    cos_c, sin_c = q_table
    q = _rope(q_in.astype(F32), jnp.concatenate([cos_c] * 4, axis=1), jnp.concatenate([sin_c] * 4, axis=1))
    q = q * (ATTN_DIM ** -0.5)
    per = ATTN_HEADS_PER_DOT
    bias = jnp.concatenate([bias1] * per, axis=1)
    lane = lax.broadcasted_iota(I32, (CHUNK, LANES), 1)

    for h0 in range(0, ATTN_HEADS, per):
        qs, snk = [], []
        for h in range(h0, h0 + per):
            g = h // ATTN_GROUP
            t = q[:, (h // 2) * LANES:(h // 2 + 1) * LANES]
            if h % 2 != g:
                t = pltpu.roll(t, ATTN_DIM, 1)
            half_g = (lane >= ATTN_DIM) if g == 1 else (lane < ATTN_DIM)
            qs.append(jnp.where(half_g, t, 0.0).astype(BF16))
            snk.append(jnp.broadcast_to(sink[:, h:h + 1], (1, CHUNK)))
        snk = jnp.concatenate(snk, axis=1)
        st = _dot_t(k_all, jnp.concatenate(qs, axis=0)) + bias
        m = jnp.maximum(jnp.max(st, axis=0, keepdims=True), snk)
        e = jnp.exp(st - m)
        den = jnp.sum(e, axis=0, keepdims=True) + jnp.exp(snk - m)
        pv = jnp.dot(vt_all, e.astype(BF16), preferred_element_type=F32)
        o = (pv / den).astype(o_ref.dtype)
        for r, h in enumerate(range(h0, h0 + per)):
            g = h // ATTN_GROUP
            o_ref[h * ATTN_DIM:(h + 1) * ATTN_DIM, :] = o[g * ATTN_DIM:(g + 1) * ATTN_DIM, r * CHUNK:(r + 1) * CHUNK]


def _attn_bias(nctx):
    i = jnp.arange(CHUNK)[None, :]
    j = jnp.arange(3 * CHUNK)[:, None]
    band = (j >= i) & (j <= i + 2 * CHUNK)
    local = jnp.stack([band & (j >= CHUNK), band, band & (j < 2 * CHUNK)])
    return jnp.concatenate([jnp.where(local, 0.0, -jnp.inf).astype(F32), jnp.zeros((3, nctx, CHUNK), F32)], axis=1)


def _attn(p, avt, pc, avtc, table, sink):
    b, n, _ = p.shape
    lc = pc.shape[1]
    nb = n // CHUNK
    assert nb >= 2
    qw = ATTN_HEADS * ATTN_DIM
    bias = _attn_bias(lc)

    s = ATTN_STEP_BLOCKS
    offs = range(-1, s + 1)

    def blk(col, off):
        return pl.BlockSpec((None, CHUNK, LANES), lambda bi, i: (bi, jnp.clip(i * s + off, 0, nb - 1), col))

    def vblk(off):
        return pl.BlockSpec((None, LANES, CHUNK), lambda bi, i: (bi, 0, jnp.clip(i * s + off, 0, nb - 1)))

    def tab(off):
        return pl.BlockSpec((CHUNK, 2 * LANES), lambda bi, i: (jnp.clip(i * s + off, 0, nb - 1), 0))

    in_specs = ([pl.BlockSpec((None, s * CHUNK, qw), lambda bi, i: (bi, i, PB_AQ))]
                + [blk(PB_AK, o) for o in offs] + [vblk(o) for o in offs]
                + [pl.BlockSpec((None, lc, LANES), lambda bi, i: (bi, 0, PB_AK)),
                   pl.BlockSpec((None, LANES, lc), lambda bi, i: (bi, 0, 0))]
                + [tab(o) for o in offs]
                + [pl.BlockSpec((1, LANES), lambda bi, i: (0, 0)),
                   pl.BlockSpec(bias.shape, lambda bi, i: (0, 0, 0))])
    nw = len(offs)
    return pl.pallas_call(
        _attn_kernel,
        out_shape=jax.ShapeDtypeStruct((b, qw, n), BF16),
        grid=(b, nb // s), in_specs=in_specs,
        out_specs=pl.BlockSpec((None, qw, s * CHUNK), lambda bi, i: (bi, 0, i)),
        compiler_params=_cparams(("parallel", "parallel")), name="window_attn",
    )(*([p] * (1 + nw) + [avt] * nw + [pc, avtc] + [table] * nw + [sink, bias]))


def _router_tail(x_new, g2, sc2, sh2, wr_ref, x_out_ref, h2_ref, aff_ref):
    x_out_ref[...] = x_new
    h2 = _rms_mod(x_new, g2, sc2, sh2)
    h_hi = h2.astype(BF16)
    h2_ref[...] = h_hi
    logits = jnp.dot(_hi_lo_lhs(h2, h_hi), wr_ref[...], preferred_element_type=F32)
    lane = lax.broadcasted_iota(I32, logits.shape, 1)
    logits = jnp.where(lane < N_EXPERTS, logits, -jnp.inf)
    e = jnp.exp(logits - jnp.max(logits, axis=1, keepdims=True))
    aff_ref[...] = e / jnp.sum(e, axis=1, keepdims=True)


def _ab_out_kernel(hf_ref, hb_ref, o_ref, at_ref, x_ref, *refs):
    for lo in range(0, x_ref.shape[0], TAIL_CHAIN_ROWS):
        rows = slice(lo, lo + TAIL_CHAIN_ROWS)
        _ab_out_rows(hf_ref.at[:, rows], hb_ref.at[:, rows], o_ref.at[rows], at_ref.at[:, rows], x_ref.at[rows],
                     *refs[:-3], *[out.at[rows] for out in refs[-3:]])


def _ab_out_rows(hf_ref, hb_ref, o_ref, at_ref, x_ref, hg_ref, wo_ref, gt1_ref, g2_ref, sc2_ref, sh2_ref, wr_ref,
                 x_out_ref, h2_ref, aff_ref):
    hsum = (hf_ref[...] + hb_ref[...]).T
    og = jax.nn.sigmoid(o_ref[...].astype(F32))
    hg = hg_ref[...]
    parts = []
    for h in range(LSTM_HEADS):
        sl = slice(h * LSTM_DIM, (h + 1) * LSTM_DIM)
        seg = hsum[:, sl]
        seg = seg * lax.rsqrt(jnp.mean(seg * seg, axis=-1, keepdims=True) + EPS)
        parts.append((seg * hg[:, sl] * og[:, sl]).astype(BF16))
    cat = jnp.concatenate(parts + [at_ref[...].astype(F32).T.astype(BF16)], axis=1)
    y = jnp.dot(cat, wo_ref[...], preferred_element_type=F32)
    _router_tail(x_ref[...] + gt1_ref[...] * y, g2_ref[...], sc2_ref[...], sh2_ref[...], wr_ref,
                 x_out_ref, h2_ref, aff_ref)


def _tail_out(b, n, d, tm):
    shapes = [jax.ShapeDtypeStruct((b, n, d), F32), jax.ShapeDtypeStruct((b, n, d), BF16),
              jax.ShapeDtypeStruct((b, n, LANES), F32)]
    specs = [pl.BlockSpec((None, tm, d), lambda bi, i: (bi, i, 0)),
             pl.BlockSpec((None, tm, d), lambda bi, i: (bi, i, 0)),
             pl.BlockSpec((None, tm, LANES), lambda bi, i: (bi, i, 0))]
    return shapes, specs


def _ab_out(hf, hb, p, at, x, head_g, w_out, gt1, g2, sc2, sh2, wr, tm=1024):
    b, n, d = x.shape
    row = lambda w: pl.BlockSpec((None, tm, w), lambda bi, i: (bi, i, 0))
    vec = pl.BlockSpec((None, 1, d), lambda bi, i: (bi, 0, 0))
    const = lambda s: pl.BlockSpec(s, lambda bi, i: (0, 0))
    scan_out = pl.BlockSpec((None, LSTM_WIDTH, tm), lambda bi, i: (bi, 0, i))
    in_specs = [scan_out, scan_out,
                pl.BlockSpec((None, tm, LSTM_WIDTH), lambda bi, i: (bi, i, PB_O)),
                pl.BlockSpec((None, ATTN_HEADS * ATTN_DIM, tm), lambda bi, i: (bi, 0, i)),
                row(d), const((1, LSTM_WIDTH)), const(w_out.shape),
                vec, const((1, d)), vec, vec, const((3 * d, LANES))]
    shapes, specs = _tail_out(b, n, d, tm)
    return pl.pallas_call(
        _ab_out_kernel, out_shape=shapes, grid=(b, n // tm), in_specs=in_specs, out_specs=specs,
        compiler_params=_cparams(("parallel", "parallel")), name="ab_out",
    )(hf, hb, p, at, x, head_g, w_out, gt1, g2, sc2, sh2, wr)


def _gm_out_kernel(uv_ref, x_ref, *refs):
    for lo in range(0, x_ref.shape[0], TAIL_CHAIN_ROWS):
        rows = slice(lo, lo + TAIL_CHAIN_ROWS)
        _gm_out_rows(uv_ref.at[rows], x_ref.at[rows], *refs[:-3], *[out.at[rows] for out in refs[-3:]])


def _gm_out_rows(uv_ref, x_ref, lng_ref, lnb_ref, ws_ref, bs_ref, wo_ref, gt1_ref, g2_ref, sc2_ref, sh2_ref, wr_ref,
                 x_out_ref, h2_ref, aff_ref):
    tm = uv_ref.shape[0]
    gw = GM_HALF // GM_GROUPS
    v = uv_ref[:, GM_HALF:].astype(F32)
    mu = jnp.mean(v, axis=-1, keepdims=True)
    vc = v - mu
    var = jnp.mean(vc * vc, axis=-1, keepdims=True)
    vn = (vc * lax.rsqrt(var + EPS) * lng_ref[...] + lnb_ref[...]).astype(BF16)
    zs = []
    for ch in range(tm // CHUNK):
        rows = slice(ch * CHUNK, (ch + 1) * CHUNK)
        cols = []
        for g in range(GM_GROUPS):
            sv = jnp.dot(ws_ref[g], vn[rows, g * gw:(g + 1) * gw], preferred_element_type=F32)
            cols.append(sv + bs_ref[:, g:g + 1])
        sv = jnp.concatenate(cols, axis=1)
        zs.append((uv_ref[rows, :GM_HALF].astype(F32) * sv).astype(BF16))
    z = jnp.concatenate(zs, axis=0)
    y = jnp.dot(z, wo_ref[...], preferred_element_type=F32)
    _router_tail(x_ref[...] + gt1_ref[...] * y, g2_ref[...], sc2_ref[...], sh2_ref[...], wr_ref,
                 x_out_ref, h2_ref, aff_ref)


def _gm_out(uv, x, ln_g, ln_b, w_s, b_s_t, w_out, gt1, g2, sc2, sh2, wr, tm=1024):
    b, n, d = x.shape
    vec = pl.BlockSpec((None, 1, d), lambda bi, i: (bi, 0, 0))
    const = lambda s: pl.BlockSpec(s, lambda *_: (0,) * len(s))
    in_specs = [pl.BlockSpec((None, tm, 2 * GM_HALF), lambda bi, i: (bi, i, 0)),
                pl.BlockSpec((None, tm, d), lambda bi, i: (bi, i, 0)),
                const((1, GM_HALF)), const((1, GM_HALF)), const(w_s.shape), const(b_s_t.shape), const(w_out.shape),
                vec, const((1, d)), vec, vec, const((3 * d, LANES))]
    shapes, specs = _tail_out(b, n, d, tm)
    return pl.pallas_call(
        _gm_out_kernel, out_shape=shapes, grid=(b, n // tm), in_specs=in_specs, out_specs=specs,
        compiler_params=_cparams(("parallel", "parallel")), name="gm_out",
    )(uv, x, ln_g, ln_b, w_s, b_s_t, w_out, gt1, g2, sc2, sh2, wr)


def _route_kernel(aff_ref, pos_ref, post_ref, offs_ref, afft_ref, *, cap):
    n = aff_ref.shape[0]
    nblk = n // CHUNK

    def to_expert_major(k, _):
        rows = pl.ds(pl.multiple_of(k * CHUNK, CHUNK), CHUNK)
        afft_ref[k] = aff_ref[rows, :].T[:N_EXPERTS, :]
        return 0

    lax.fori_loop(0, nblk, to_expert_major, 0, unroll=4)

    def count(pred):
        per_lane = jnp.sum(jnp.where(pred, 1.0, 0.0), axis=0)
        return jnp.sum(per_lane, axis=1, keepdims=True)

    def search(i, prefix):
        cand = prefix | jnp.left_shift(jnp.int32(1), 30 - i)
        cand_f = lax.bitcast_convert_type(cand, F32)
        return jnp.where(count(afft_ref[...] >= cand_f[None]) >= cap, cand, prefix)

    thr_col = lax.bitcast_convert_type(lax.fori_loop(0, 31, search, jnp.zeros((N_EXPERTS, 1), I32)), F32)
    need_col = cap - count(afft_ref[...] > thr_col[None])

    def to_row(col):
        full = jnp.concatenate([jnp.broadcast_to(col, (N_EXPERTS, LANES)),
                                jnp.zeros((LANES - N_EXPERTS, LANES), F32)], axis=0)
        return full.T[0:1, :]

    thr, need = to_row(thr_col), to_row(need_col)
    tril = jnp.where(_tri(CHUNK, True), 1.0, 0.0).astype(BF16)

    def block(k, carry):
        run_tie, run_sel = carry
        rows = pl.ds(pl.multiple_of(k * CHUNK, CHUNK), CHUNK)
        a = aff_ref[rows, :]
        gt = a > thr
        tie = jnp.where(a == thr, 1.0, 0.0)
        tie_incl = jnp.dot(tril, tie.astype(BF16), preferred_element_type=F32)
        sel = jnp.where(gt | ((tie > 0.0) & (tie_incl - tie + run_tie < need)), 1.0, 0.0)
        sel_incl = jnp.dot(tril, sel.astype(BF16), preferred_element_type=F32)
        pos = jnp.where(sel > 0.0, sel_incl - sel + run_sel, -1.0)
        pos_ref[rows, :] = pos.astype(I32)
        post_ref[k] = pos.T[:N_EXPERTS, :].astype(I32)
        offs_ref[k] = run_sel.astype(I32)
        return (run_tie + tie_incl[CHUNK - 1:CHUNK, :], run_sel + sel_incl[CHUNK - 1:CHUNK, :])

    zero = jnp.zeros((1, LANES), F32)
    lax.fori_loop(0, nblk, block, (zero, zero), unroll=4)


def _route(aff, cap):
    b, n, _ = aff.shape
    nblk = n // CHUNK
    return pl.pallas_call(
        functools.partial(_route_kernel, cap=cap),
        out_shape=[jax.ShapeDtypeStruct((b, n, LANES), I32),
                   jax.ShapeDtypeStruct((b, nblk, N_EXPERTS, CHUNK), I32),
                   jax.ShapeDtypeStruct((b, nblk, 1, LANES), I32),
                   jax.ShapeDtypeStruct((b, nblk, N_EXPERTS, CHUNK), F32)],
        grid=(b,),
        in_specs=[pl.BlockSpec((None, n, LANES), lambda bi: (bi, 0, 0))],
        out_specs=[pl.BlockSpec((None, n, LANES), lambda bi: (bi, 0, 0)),
                   pl.BlockSpec((None, nblk, N_EXPERTS, CHUNK), lambda bi: (bi, 0, 0, 0)),
                   pl.BlockSpec((None, nblk, 1, LANES), lambda bi: (bi, 0, 0, 0)),
                   pl.BlockSpec((None, nblk, N_EXPERTS, CHUNK), lambda bi: (bi, 0, 0, 0))],
        compiler_params=_cparams(("parallel",)), name="route",
    )(aff)


def _window_start(s0, w, cap):
    lo = ((s0 >> 4) << 4) + w * SLOT_CHUNK
    return lo, pl.multiple_of(jnp.minimum(lo, cap - SLOT_CHUNK), BF16_ROWS)


def _num_windows(s0, s1):
    return (s1 - ((s0 >> 4) << 4) + SLOT_CHUNK - 1) >> (SLOT_CHUNK.bit_length() - 1)


def _moe_gather_kernel(cnt_ref, post_ref, afft_ref, h_ref, xe_ref, gate_ref, *, ntile, cap, group, tiles_per_step):
    bi, eg, ts = pl.program_id(0), pl.program_id(1), pl.program_id(2)

    @pl.when(ts == 0)
    def _():
        xe_ref[...] = jnp.zeros_like(xe_ref)
        gate_ref[...] = jnp.zeros_like(gate_ref)

    slot = lax.broadcasted_iota(I32, (SLOT_CHUNK, MOE_TILE), 0)

    def base(u, g):
        return (bi * N_EXPERTS + eg * group + g) * (ntile + 1) + ts * tiles_per_step + u

    def windows(u, w):
        htile = h_ref[u * MOE_TILE:(u + 1) * MOE_TILE, :]
        starts, onehots = [], []
        for g in range(group):
            lo, start = _window_start(cnt_ref[base(u, g)], w, cap)
            posrow = post_ref[g, u]
            hit = (posrow - start == slot) & (posrow >= lo)
            onehots.append(jnp.where(hit, 1.0, 0.0).astype(BF16))
            gates = jnp.sum(jnp.where(hit, afft_ref[g, u], 0.0), axis=1, keepdims=True)
            dst = pl.ds(start, SLOT_CHUNK)
            gate_ref[g, dst, :] = gate_ref[g, dst, :] + jnp.broadcast_to(gates, (SLOT_CHUNK, LANES))
            starts.append(start)
        rows = jnp.dot(jnp.concatenate(onehots, axis=0), htile, preferred_element_type=F32)
        for g in range(group):
            dst = pl.ds(starts[g], SLOT_CHUNK)
            xe_ref[g, dst, :] = xe_ref[g, dst, :] + rows[g * SLOT_CHUNK:(g + 1) * SLOT_CHUNK].astype(BF16)

    nwin = 1
    for u in range(tiles_per_step):
        windows(u, 0)
        for g in range(group):
            nwin = jnp.maximum(nwin, _num_windows(cnt_ref[base(u, g)], cnt_ref[base(u, g) + 1]))

    def overflow(w, _):
        for u in range(tiles_per_step):
            windows(u, w)
        return 0

    lax.fori_loop(1, nwin, overflow, 0)


def _moe_gather(cnt, post, afft, h2, cap, group=8, tiles_per_step=8):
    b, n, d = h2.shape
    ntile = n // MOE_TILE
    per_tile = pl.BlockSpec((None, group, tiles_per_step, 1, MOE_TILE), lambda bi, eg, ts, c: (bi, eg, ts, 0, 0))
    grid_spec = pltpu.PrefetchScalarGridSpec(
        num_scalar_prefetch=1, grid=(b, N_EXPERTS // group, ntile // tiles_per_step),
        in_specs=[per_tile, per_tile,
                  pl.BlockSpec((None, tiles_per_step * MOE_TILE, d), lambda bi, eg, ts, c: (bi, ts, 0))],
        out_specs=[pl.BlockSpec((None, group, cap, d), lambda bi, eg, ts, c: (bi, eg, 0, 0)),
                   pl.BlockSpec((None, group, cap, LANES), lambda bi, eg, ts, c: (bi, eg, 0, 0))])
    return pl.pallas_call(
        functools.partial(_moe_gather_kernel, ntile=ntile, cap=cap, group=group, tiles_per_step=tiles_per_step),
        out_shape=[jax.ShapeDtypeStruct((b, N_EXPERTS, cap, d), BF16),
                   jax.ShapeDtypeStruct((b, N_EXPERTS, cap, LANES), F32)],
        grid_spec=grid_spec, compiler_params=_cparams(("arbitrary", "arbitrary", "arbitrary")), name="moe_gather",
    )(cnt, post, afft, h2)


def _moe_ffn_kernel(xe_ref, gate_ref, wg_ref, wu_ref, wd_ref, y_ref, acc_scr, *, hid_tile):
    j = pl.program_id(2)
    xe = xe_ref[...]
    for k in range(wg_ref.shape[1] // hid_tile):
        cols = slice(k * hid_tile, (k + 1) * hid_tile)
        gate = jnp.dot(xe, wg_ref[:, cols].astype(BF16), preferred_element_type=F32)
        up = jnp.dot(xe, wu_ref[:, cols].astype(BF16), preferred_element_type=F32)
        hid = (_silu(gate) * up).astype(BF16)
        part = jnp.dot(hid, wd_ref[cols, :].astype(BF16), preferred_element_type=F32)
        if k == 0:
            @pl.when(j == 0)
            def _():
                acc_scr[...] = part

            @pl.when(j != 0)
            def _():
                acc_scr[...] += part
        else:
            acc_scr[...] += part

    @pl.when(j == pl.num_programs(2) - 1)
    def _():
        y_ref[...] = (acc_scr[...] * gate_ref[:, 0:1]).astype(y_ref.dtype)


def _moe_ffn(xe, gate, wg, wu, wd, layer, hid_split=1):
    b, ne, cap, d = xe.shape
    dh = wg.shape[3] // hid_split
    return pl.pallas_call(
        functools.partial(_moe_ffn_kernel, hid_tile=256),
        out_shape=jax.ShapeDtypeStruct((b, ne, cap, d), BF16),
        grid=(b, ne, hid_split),
        in_specs=[pl.BlockSpec((None, None, cap, d), lambda bi, e, j: (bi, e, 0, 0)),
                  pl.BlockSpec((None, None, cap, LANES), lambda bi, e, j: (bi, e, 0, 0)),
                  pl.BlockSpec((None, None, d, dh), lambda bi, e, j: (layer, e, 0, j)),
                  pl.BlockSpec((None, None, d, dh), lambda bi, e, j: (layer, e, 0, j)),
                  pl.BlockSpec((None, None, dh, d), lambda bi, e, j: (layer, e, j, 0))],
        out_specs=pl.BlockSpec((None, None, cap, d), lambda bi, e, j: (bi, e, 0, 0)),
        scratch_shapes=[pltpu.VMEM((cap, d), F32)],
        compiler_params=_cparams(("parallel", "parallel", "arbitrary")), name="moe_ffn",
    )(xe, gate, wg, wu, wd)


def _moe_combine_kernel(cnt_ref, pos_ref, x_ref, gt2_ref, y_ref, fg_ref, o_ref, acc_scr, ycat_scr, *, ntile, cap, final):
    bi = pl.program_id(0)
    tiles = pos_ref.shape[0] // MOE_TILE
    kw = N_EXPERTS * SLOT_CHUNK
    shift = SLOT_CHUNK.bit_length() - 1
    sel = jnp.where(lax.broadcasted_iota(I32, (LANES, kw), 0) == (lax.broadcasted_iota(I32, (LANES, kw), 1) >> shift),
                    1.0, 0.0).astype(BF16)

    def spread(v):
        hi = (v >> 5).astype(F32).astype(BF16)
        lo = (v & 31).astype(F32).astype(BF16)
        return 32.0 * jnp.dot(hi, sel, preferred_element_type=F32) + jnp.dot(lo, sel, preferred_element_type=F32)

    within = (lax.broadcasted_iota(I32, (1, kw), 1) & (SLOT_CHUNK - 1)).astype(F32)
    lane = lax.broadcasted_iota(I32, (SUBLANES, LANES), 1)

    def base(u, e):
        return (bi * N_EXPERTS + e) * (ntile + 1) + pl.program_id(1) * tiles + u

    def windows(u, w):
        pos_w = spread(pos_ref[u * MOE_TILE:(u + 1) * MOE_TILE, :] + 1) - 1.0
        starts = jnp.zeros((SUBLANES, LANES), I32)
        los = jnp.zeros((SUBLANES, LANES), I32)
        for e in range(N_EXPERTS):
            lo, start = _window_start(cnt_ref[base(u, e)], w, cap)
            starts = jnp.where(lane == e, start, starts)
            los = jnp.where(lane == e, lo, los)
            ycat_scr[u, e * SLOT_CHUNK:(e + 1) * SLOT_CHUNK, :] = y_ref[e, pl.ds(start, SLOT_CHUNK), :]
        want = spread(starts)[0:1] + within
        onehot = jnp.where((pos_w == want) & (pos_w >= spread(los)[0:1]), 1.0, 0.0).astype(BF16)
        return jnp.dot(onehot, ycat_scr[u], preferred_element_type=F32)

    nwin = 1
    for u in range(tiles):
        acc_scr[u] = windows(u, 0)
        for e in range(N_EXPERTS):
            nwin = jnp.maximum(nwin, _num_windows(cnt_ref[base(u, e)], cnt_ref[base(u, e) + 1]))

    def overflow(w, _):
        for u in range(tiles):
            acc_scr[u] += windows(u, w)
        return 0

    lax.fori_loop(1, nwin, overflow, 0)
    for u in range(tiles):
        rows = slice(u * MOE_TILE, (u + 1) * MOE_TILE)
        out = x_ref[rows, :] + gt2_ref[...] * acc_scr[u]
        if final:
            out = out * lax.rsqrt(jnp.mean(out * out, axis=-1, keepdims=True) + EPS) * fg_ref[...]
        o_ref[rows, :] = out


def _moe_combine(cnt, pos, x, gt2, y, final_g, final, tiles_per_step=2):
    b, n, d = x.shape
    ntile = n // MOE_TILE
    cap = y.shape[2]
    rows = tiles_per_step * MOE_TILE
    grid_spec = pltpu.PrefetchScalarGridSpec(
        num_scalar_prefetch=1, grid=(b, ntile // tiles_per_step),
        in_specs=[pl.BlockSpec((None, rows, LANES), lambda bi, t, c: (bi, t, 0)),
                  pl.BlockSpec((None, rows, d), lambda bi, t, c: (bi, t, 0)),
                  pl.BlockSpec((None, 1, d), lambda bi, t, c: (bi, 0, 0)),
                  pl.BlockSpec((None, N_EXPERTS, cap, d), lambda bi, t, c: (bi, 0, 0, 0),
                               pipeline_mode=pl.Buffered(1)),
                  pl.BlockSpec((1, d), lambda bi, t, c: (0, 0))],
        out_specs=pl.BlockSpec((None, rows, d), lambda bi, t, c: (bi, t, 0)),
        scratch_shapes=[pltpu.VMEM((tiles_per_step, MOE_TILE, d), F32),
                        pltpu.VMEM((tiles_per_step, N_EXPERTS * SLOT_CHUNK, d), BF16)])
    return pl.pallas_call(
        functools.partial(_moe_combine_kernel, ntile=ntile, cap=cap, final=final),
        out_shape=jax.ShapeDtypeStruct((b, n, d), F32),
        grid_spec=grid_spec, compiler_params=_cparams(("arbitrary", "arbitrary")), name="moe_combine",
    )(cnt, pos, x, gt2, y, final_g)


def _ec_moe(x_mid, h2, aff, gt2, wg, wu, wd, layer, final_g, final):
    b, n, _ = x_mid.shape
    cap = max(1, EC_FACTOR * n // N_EXPERTS)
    ntile = n // MOE_TILE
    pos, post, offs, afft = _route(aff, cap)
    per_tile = lambda a: a.transpose(0, 2, 1, 3).reshape(b, N_EXPERTS, ntile, 1, MOE_TILE)
    starts = offs[:, ::MOE_TILE // CHUNK, 0, :N_EXPERTS].transpose(0, 2, 1)
    cnt = jnp.concatenate([starts, jnp.full((b, N_EXPERTS, 1), cap, I32)], axis=2).reshape(-1)
    xe, gate = _moe_gather(cnt, per_tile(post), per_tile(afft), h2, cap)
    y = _moe_ffn(xe, gate, wg, wu, wd, layer)
    return _moe_combine(cnt, pos, x_mid, gt2, y, final_g, final)


def _rope_table(n):
    rows = n // GRID_W
    row = jnp.repeat(jnp.arange(rows), GRID_W).astype(F32)
    col = jnp.tile(jnp.arange(GRID_W), rows).astype(F32)
    nf = ATTN_DIM // 4
    inv = ROPE_BASE ** (-jnp.arange(nf, dtype=F32) / nf)
    ang = jnp.concatenate([row[:, None] * inv, col[:, None] * inv], axis=-1)
    cos, sin = jnp.cos(ang), jnp.sin(ang)
    reps = LANES // ATTN_DIM
    return jnp.concatenate([jnp.tile(jnp.concatenate([cos, cos], -1), (1, reps)),
                            jnp.tile(jnp.concatenate([-sin, sin], -1), (1, reps))], axis=-1)


def _pad_lanes(a):
    return jnp.pad(a, ((0, 0), (0, LANES - a.shape[1])))


def kernel(x, c, ctx, c_ctx, w_mod, b_mod, norm_mix_g, norm_ffn_g, final_norm_g, ab_w_in, ab_conv_w, ab_gate_b,
           ab_head_g, ab_sink, ab_w_out, gm_w_in, gm_ln_g, gm_ln_b, gm_w_s, gm_b_s, gm_w_out, moe_w_router,
           moe_w_gate, moe_w_up, moe_w_down):
    b, n, d = x.shape
    depth = w_mod.shape[0]
    assert depth <= 2, "context stream is only advanced for deeper stacks; not supported here"
    cond = jnp.zeros((SUBLANES, d), F32).at[:b].set(c).at[b].set(c_ctx)
    mods = _adaln(cond, w_mod, b_mod)

    def mod_rows(layer, rows):
        m = mods[layer, rows].reshape(-1, 6, 1, d)
        return [m[:, i] for i in range(6)]

    row = lambda v: v.reshape(1, -1)
    for layer in range(depth):
        sh1, sc1, gt1, sh2, sc2, gt2 = mod_rows(layer, slice(0, b))
        g1, g2 = row(norm_mix_g[layer]), row(norm_ffn_g[layer])
        wr = _hi_lo_rhs(_pad_lanes(moe_w_router[layer]))
        if layer % 2 == 0:
            e = layer // 2
            csh1, csc1 = (jnp.broadcast_to(v, (b, 1, d)) for v in mod_rows(layer, slice(b, b + 1))[:2])
            w_in = ab_w_in[e]
            g_lo = 4 * LSTM_WIDTH
            w_main = jnp.concatenate([w_in[:, :g_lo], w_in[:, g_lo + N_GATES:]], axis=1).astype(BF16)
            w_gate = _hi_lo_rhs(_pad_lanes(w_in[:, g_lo:g_lo + N_GATES]))
            p, gts, vt, avt = _modmm(x, g1, sc1, sh1, w_main, w_gate, tm=1024, name="ab_in")
            pc, gtc, vtc, avtc = _modmm(ctx, g1, csc1, csh1, w_main, w_gate, name="ab_in_ctx")
            hf, hb = _mlstm(_conv_silu(p, ab_conv_w[e]), vt, gts, _conv_silu(pc, ab_conv_w[e]), vtc, gtc,
                            _pad_lanes(row(ab_gate_b[e])))
            at = _attn(p, avt, pc, avtc, _rope_table(n), _pad_lanes(row(ab_sink[e])))
            x_mid, h2, aff = _ab_out(hf, hb, p, at, x, row(ab_head_g[e]), ab_w_out[e].astype(BF16),
                                     gt1, g2, sc2, sh2, wr)
        else:
            o = layer // 2
            uv = _modmm(x, g1, sc1, sh1, gm_w_in[o].astype(BF16), act="gelu", tm=1024, name="gm_in")
            x_mid, h2, aff = _gm_out(uv, x, row(gm_ln_g[o]), row(gm_ln_b[o]), gm_w_s[o].astype(BF16),
                                     _pad_lanes(gm_b_s[o].T), gm_w_out[o].astype(BF16), gt1, g2, sc2, sh2, wr)
        x = _ec_moe(x_mid, h2, aff, gt2, moe_w_gate, moe_w_up, moe_w_down, layer,
                    row(final_norm_g), layer == depth - 1)
    return x
```

```python
import functools
import math

import jax
import jax.numpy as jnp
from jax import lax
from jax.experimental import pallas as pl
from jax.experimental.pallas import tpu as pltpu

F32 = jnp.float32
BF16 = jnp.bfloat16
I32 = jnp.int32

D_MODEL = 1024
GRID_W = 64
EPS = 1e-6
LSTM_HEADS = 4
LSTM_DIM = 128
LSTM_WIDTH = LSTM_HEADS * LSTM_DIM
LSTM_CONV = 5
CHUNK = 128
ATTN_HEADS = 8
ATTN_KV_HEADS = 2
ATTN_GROUP = ATTN_HEADS // ATTN_KV_HEADS
ATTN_DIM = 64
ROPE_BASE = 10000.0
GM_GROUPS = 8
GM_HALF = 2 * D_MODEL
N_EXPERTS = 16
EC_FACTOR = 2
N_GATES = 4 * LSTM_HEADS

LANES = 128
SUBLANES = 8
BF16_ROWS = 16
V7X_VMEM_BYTES = 64 * 1024 * 1024
VMEM_COMPILER_RESERVE_BYTES = 8 * 1024 * 1024
VMEM_LIMIT_BYTES = V7X_VMEM_BYTES - VMEM_COMPILER_RESERVE_BYTES

P_COLS = 4 * LSTM_WIDTH + ATTN_HEADS * ATTN_DIM + 2 * ATTN_KV_HEADS * ATTN_DIM
PB_V, PB_O, PB_AQ = 2, 3, 4
PB_AK = (4 * LSTM_WIDTH + ATTN_HEADS * ATTN_DIM) // LANES
PB_AV = PB_AK + ATTN_KV_HEADS * ATTN_DIM // LANES
assert ATTN_HEADS * ATTN_DIM == LSTM_WIDTH and ATTN_KV_HEADS * ATTN_DIM == LANES
ATTN_HEADS_PER_DOT = 8
ATTN_STEP_BLOCKS = 4
TAIL_CHAIN_ROWS = 256
MOE_TILE = 256
SLOT_CHUNK = 64


def _cparams(sem, vmem=VMEM_LIMIT_BYTES):
    return pltpu.CompilerParams(dimension_semantics=sem, vmem_limit_bytes=vmem)


def _rms_mod(x, g, sc, sh):
    y = x * lax.rsqrt(jnp.mean(x * x, axis=-1, keepdims=True) + EPS)
    return y * g * (1.0 + sc) + sh


def _silu(x):
    return x * jax.nn.sigmoid(x)


def _gelu_tanh(x):
    return 0.5 * x * (1.0 + jnp.tanh((2.0 / math.pi) ** 0.5 * (x + 0.044715 * (x * x * x))))


def _log_sigmoid(x):
    return jnp.minimum(x, 0.0) - jnp.log(1.0 + jnp.exp(-jnp.abs(x)))


def _dot_t(a, b):
    return lax.dot_general(a, b, (((1,), (1,)), ((), ())), preferred_element_type=F32)


def _adaln_kernel(c_ref, w_ref, b_ref, o_ref):
    s = _silu(c_ref[...])
    s_hi = s.astype(BF16)
    s_lo = (s - s_hi.astype(F32)).astype(BF16)
    w = w_ref[...]
    w_hi = w.astype(BF16)
    w_lo = (w - w_hi.astype(F32)).astype(BF16)
    both = jnp.dot(jnp.concatenate([s_hi, s_lo], axis=0), w_hi, preferred_element_type=F32)
    rows = s.shape[0]
    o_ref[...] = both[:rows] + both[rows:] + jnp.dot(s_hi, w_lo, preferred_element_type=F32) + b_ref[...]


def _adaln(cond, w_mod, b_mod):
    depth, d, six_d = w_mod.shape
    rows = cond.shape[0]
    tn = six_d // 4
    return pl.pallas_call(
        _adaln_kernel,
        out_shape=jax.ShapeDtypeStruct((depth, rows, six_d), F32),
        grid=(depth, six_d // tn),
        in_specs=[pl.BlockSpec((rows, d), lambda l, j: (0, 0)),
                  pl.BlockSpec((None, d, tn), lambda l, j: (l, 0, j)),
                  pl.BlockSpec((None, 1, tn), lambda l, j: (l, 0, j))],
        out_specs=pl.BlockSpec((None, rows, tn), lambda l, j: (l, 0, j)),
        compiler_params=_cparams(("arbitrary", "arbitrary")),
        name="adaln",
    )(cond, w_mod, b_mod.reshape(depth, 1, six_d))


def _hi_lo_lhs(h, h_hi):
    return jnp.concatenate([h_hi, (h - h_hi.astype(F32)).astype(BF16), h_hi], axis=1)


def _hi_lo_rhs(w):
    w_hi = w.astype(BF16)
    return jnp.concatenate([w_hi, w_hi, (w - w_hi.astype(F32)).astype(BF16)], axis=0)


def _modmm_kernel(x_ref, g_ref, sc_ref, sh_ref, w_ref, *rest, chunks, act, with_gates):
    if with_gates:
        wg_ref, o_ref, og_ref, vt_ref, avt_ref = rest
    else:
        (o_ref,) = rest
    h = _rms_mod(x_ref[...], g_ref[...], sc_ref[...], sh_ref[...])
    hb = h.astype(BF16)
    for lo, hi in chunks:
        y = jnp.dot(hb, w_ref[:, lo:hi], preferred_element_type=F32)
        if act == "gelu":
            y = _gelu_tanh(y)
        o_ref[:, lo:hi] = y.astype(o_ref.dtype)
        if with_gates and lo == PB_V * LSTM_WIDTH:
            vt_ref[...] = y.T.astype(vt_ref.dtype)
        if with_gates and lo <= PB_AV * LANES < hi:
            av = y[:, PB_AV * LANES - lo:(PB_AV + 1) * LANES - lo]
            avt_ref[...] = av.T.astype(avt_ref.dtype)
    if with_gates:
        og_ref[...] = jnp.dot(_hi_lo_lhs(h, hb), wg_ref[...], preferred_element_type=F32)


def _modmm(x, g, sc, sh, w, wg=None, *, act=None, tm=512, chunk=512, name="modmm"):
    b, n, d = x.shape
    no = w.shape[1]
    tm = min(tm, n)
    assert chunk == LSTM_WIDTH
    chunks = tuple((lo, min(lo + chunk, no)) for lo in range(0, no, chunk))
    in_specs = [pl.BlockSpec((None, tm, d), lambda bi, i: (bi, i, 0)),
                pl.BlockSpec((1, d), lambda bi, i: (0, 0)),
                pl.BlockSpec((None, 1, d), lambda bi, i: (bi, 0, 0)),
                pl.BlockSpec((None, 1, d), lambda bi, i: (bi, 0, 0)),
                pl.BlockSpec((d, no), lambda bi, i: (0, 0))]
    out_shape = [jax.ShapeDtypeStruct((b, n, no), BF16)]
    out_specs = [pl.BlockSpec((None, tm, no), lambda bi, i: (bi, i, 0))]
    args = [x, g, sc, sh, w]
    if wg is not None:
        in_specs.append(pl.BlockSpec(wg.shape, lambda bi, i: (0, 0)))
        out_shape += [jax.ShapeDtypeStruct((b, n, LANES), F32), jax.ShapeDtypeStruct((b, LSTM_WIDTH, n), BF16),
                      jax.ShapeDtypeStruct((b, LANES, n), BF16)]
        out_specs += [pl.BlockSpec((None, tm, LANES), lambda bi, i: (bi, i, 0)),
                      pl.BlockSpec((None, LSTM_WIDTH, tm), lambda bi, i: (bi, 0, i)),
                      pl.BlockSpec((None, LANES, tm), lambda bi, i: (bi, 0, i))]
        args.append(wg)
    res = pl.pallas_call(
        functools.partial(_modmm_kernel, chunks=chunks, act=act, with_gates=wg is not None),
        out_shape=out_shape, grid=(b, n // tm), in_specs=in_specs, out_specs=out_specs,
        compiler_params=_cparams(("parallel", "parallel")), name=name,
    )(*args)
    return res if wg is not None else res[0]


def _conv_silu_kernel(x_ref, xp_ref, xn_ref, w_ref, o_ref, pad_scr):
    i = pl.program_id(1)
    rows = x_ref.shape[0]
    halo = BF16_ROWS
    has_prev = jnp.where(i > 0, 1.0, 0.0)
    has_next = jnp.where(i < pl.num_programs(1) - 1, 1.0, 0.0)
    pad_scr[pl.ds(0, halo), :] = (xp_ref[...].astype(F32) * has_prev).astype(BF16)
    pad_scr[pl.ds(halo, rows), :] = x_ref[...]
    pad_scr[pl.ds(halo + rows, halo), :] = (xn_ref[...].astype(F32) * has_next).astype(BF16)
    w = w_ref[...]
    win = CHUNK + 2 * halo
    r = lax.broadcasted_iota(I32, (CHUNK, win), 0)
    c = lax.broadcasted_iota(I32, (CHUNK, win), 1)
    half = LSTM_CONV // 2
    shifts = {t: jnp.where(c == r + halo + t - half, 1.0, 0.0).astype(BF16) for t in range(LSTM_CONV) if t != half}
    scale = LSTM_DIM ** -0.5
    for blk in range(rows // CHUNK):
        xw = pad_scr[pl.ds(blk * CHUNK, win), :]
        acc = xw[halo:halo + CHUNK].astype(F32) * w[half:half + 1, :]
        for t, s in shifts.items():
            acc = acc + jnp.dot(s, xw, preferred_element_type=F32) * w[t:t + 1, :]
        y = _silu(acc)
        out = pl.ds(blk * CHUNK, CHUNK)
        o_ref[out, :LSTM_WIDTH] = y[:, :LSTM_WIDTH].astype(o_ref.dtype)
        o_ref[out, LSTM_WIDTH:] = (y[:, LSTM_WIDTH:] * scale).astype(o_ref.dtype)


def _conv_silu(p, conv_w, tm=1024):
    b, n, _ = p.shape
    tm = min(tm, n)
    qkw = 2 * LSTM_WIDTH
    hpt = tm // BF16_ROWS
    nhb = n // BF16_ROWS
    return pl.pallas_call(
        _conv_silu_kernel,
        out_shape=jax.ShapeDtypeStruct((b, n, qkw), BF16),
        grid=(b, n // tm),
        in_specs=[pl.BlockSpec((None, tm, qkw), lambda bi, i: (bi, i, 0)),
                  pl.BlockSpec((None, BF16_ROWS, qkw), lambda bi, i: (bi, jnp.maximum(i * hpt - 1, 0), 0)),
                  pl.BlockSpec((None, BF16_ROWS, qkw), lambda bi, i: (bi, jnp.minimum((i + 1) * hpt, nhb - 1), 0)),
                  pl.BlockSpec((LSTM_CONV, qkw), lambda bi, i: (0, 0))],
        out_specs=pl.BlockSpec((None, tm, qkw), lambda bi, i: (bi, i, 0)),
        scratch_shapes=[pltpu.VMEM((tm + 2 * BF16_ROWS, qkw), BF16)],
        compiler_params=_cparams(("parallel", "parallel")), name="conv_silu",
    )(p, p, p, conv_w)


def _tri(n, lower):
    r = lax.broadcasted_iota(I32, (n, n), 0)
    c = lax.broadcasted_iota(I32, (n, n), 1)
    return (c <= r) if lower else (c >= r)


STATE_ROWS = LSTM_DIM + BF16_ROWS
MLSTM_STEP_CHUNKS = 8


def _mlstm_segment(d, gates, k_all, vt_all, q_all, state, ht_ref, cols):
    first = state is None
    nh = LSTM_HEADS
    seg = gates.shape[0]

    def side_by_side(pieces):
        return jnp.concatenate(pieces, axis=1)

    def block_diag(x):
        w = x.shape[1] // nh
        lane = lax.broadcasted_iota(I32, x.shape, 1)
        return jnp.concatenate([jnp.where((lane >= h * w) & (lane < (h + 1) * w), x, jnp.zeros_like(x))
                                for h in range(nh)], axis=0)

    r = lax.broadcasted_iota(I32, (seg, seg), 0)
    c = lax.broadcasted_iota(I32, (seg, seg), 1)
    before = (r <= c) if d == 0 else (r >= c)
    gates_t = gates.T[:N_GATES]
    ls = _log_sigmoid(gates_t)
    ls_hi = ls.astype(BF16)
    rest = ls - ls_hi.astype(F32)
    ls_mid = rest.astype(BF16)
    ls_lo = (rest - ls_mid.astype(F32)).astype(BF16)
    parts = jnp.dot(jnp.concatenate([ls_hi, ls_mid, ls_lo], axis=0), jnp.where(before, 1.0, 0.0).astype(BF16),
                    preferred_element_type=F32)
    bcum_t = parts[:N_GATES] + parts[N_GATES:2 * N_GATES] + parts[2 * N_GATES:]
    last = seg - 1 if d == 0 else 0
    ci = [2 * d * nh + h for h in range(nh)]
    cf = [(2 * d + 1) * nh + h for h in range(nh)]
    b_rows = side_by_side([bcum_t[c:c + 1, :] for c in cf])
    li_rows = side_by_side([gates_t[c:c + 1, :] for c in ci])
    g_heads = [bcum_t[c:c + 1, last:last + 1] for c in cf]
    cn, m_heads = (None, [jnp.zeros((1, 1), F32)] * nh) if first else state
    over_seg = lambda xs: side_by_side([jnp.broadcast_to(x, (1, seg)) for x in xs])
    m_rows, g_rows = over_seg(m_heads), over_seg(g_heads)
    vt_cat = side_by_side([vt_all[h * LSTM_DIM:(h + 1) * LSTM_DIM, :] for h in range(nh)])
    if q_all is not None:
        q_blk = block_diag(q_all)
        per_key = side_by_side([jnp.broadcast_to(gates_t[i:i + 1, :] - bcum_t[f:f + 1, :], (seg, seg)).T
                                for i, f in zip(ci, cf)])
        log_d = jnp.where(side_by_side([before] * nh), b_rows + per_key, -jnp.inf)
        m_row = jnp.maximum(b_rows + m_rows, jnp.max(log_d, axis=0, keepdims=True))
        sm = _dot_t(k_all, q_blk) * jnp.exp(log_d - m_row)
        a = jnp.exp(b_rows + m_rows - m_row)
        qc = _dot_t(cn.astype(BF16), q_blk)
        num = jnp.dot(vt_cat, block_diag(sm.astype(BF16)), preferred_element_type=F32) + a * qc[:LSTM_DIM]
        den = jnp.sum(sm, axis=0, keepdims=True) + a * qc[LSTM_DIM:LSTM_DIM + 1]
        out = num / jnp.maximum(jnp.abs(den), jnp.exp(-m_row))
        for h in range(nh):
            ht_ref[h * LSTM_DIM:(h + 1) * LSTM_DIM, cols] = out[:, h * seg:(h + 1) * seg]
    w = g_rows - b_rows + li_rows
    m_new = [jnp.maximum(g_heads[h] + m_heads[h], jnp.max(w[:, h * seg:(h + 1) * seg], axis=1, keepdims=True))
             for h in range(nh)]
    wt = jnp.exp(w - over_seg(m_new))
    aug = jnp.concatenate([(vt_cat.astype(F32) * wt).astype(BF16),
                           jnp.broadcast_to(wt, (BF16_ROWS, nh * seg)).astype(BF16)], axis=0)
    upd = jnp.dot(aug, block_diag(k_all), preferred_element_type=F32)
    if not first:
        decay = side_by_side([jnp.broadcast_to(jnp.exp(g_heads[h] + m_heads[h] - m_new[h]), (1, LSTM_DIM))
                              for h in range(nh)])
        upd = decay * cn + upd
    return upd, m_new


def _mlstm_kernel(qkf_ref, qkb_ref, vtf_ref, vtb_ref, gf_ref, gb_ref, kc_ref, vtc_ref, gc_ref, gbias_ref,
                  hf_ref, hb_ref, cn_scr, m_scr):
    gbias = gbias_ref[...]

    def save(d, state):
        cn, m_heads = state
        cn_scr[d] = cn
        for h, m in enumerate(m_heads):
            m_scr[d * LSTM_HEADS + h] = jnp.broadcast_to(m, (1, LANES))

    @pl.when(pl.program_id(1) == 0)
    def _():
        gates = gc_ref[...] + gbias
        for d in range(2):
            save(d, _mlstm_segment(d, gates, kc_ref[...], vtc_ref[...], None, None, None, None))

    states = [(cn_scr[d], [m_scr[d * LSTM_HEADS + h][:, 0:1] for h in range(LSTM_HEADS)]) for d in range(2)]
    per_step = qkf_ref.shape[0] // CHUNK
    for u in range(per_step):
        for d, (qk_ref, vt_ref, g_ref, h_ref) in enumerate(((qkf_ref, vtf_ref, gf_ref, hf_ref),
                                                            (qkb_ref, vtb_ref, gb_ref, hb_ref))):
            j = u if d == 0 else per_step - 1 - u
            rows = slice(j * CHUNK, (j + 1) * CHUNK)
            qk = qk_ref[rows, :]
            states[d] = _mlstm_segment(d, g_ref[rows, :] + gbias, qk[:, LSTM_WIDTH:], vt_ref[:, rows],
                                       qk[:, :LSTM_WIDTH], states[d], h_ref, rows)
    for d in range(2):
        save(d, states[d])


def _mlstm(qk, vt, g, qkc, vtc, gc, gate_bias):
    b, n, _ = qk.shape
    lc = qkc.shape[1]
    rows = MLSTM_STEP_CHUNKS * CHUNK
    nc = n // rows
    qkw = 2 * LSTM_WIDTH
    in_specs = [
        pl.BlockSpec((None, rows, qkw), lambda bi, c: (bi, c, 0)),
        pl.BlockSpec((None, rows, qkw), lambda bi, c: (bi, nc - 1 - c, 0)),
        pl.BlockSpec((None, LSTM_WIDTH, rows), lambda bi, c: (bi, 0, c)),
        pl.BlockSpec((None, LSTM_WIDTH, rows), lambda bi, c: (bi, 0, nc - 1 - c)),
        pl.BlockSpec((None, rows, LANES), lambda bi, c: (bi, c, 0)),
        pl.BlockSpec((None, rows, LANES), lambda bi, c: (bi, nc - 1 - c, 0)),
        pl.BlockSpec((None, lc, LSTM_WIDTH), lambda bi, c: (bi, 0, 1)),
        pl.BlockSpec((None, LSTM_WIDTH, lc), lambda bi, c: (bi, 0, 0)),
        pl.BlockSpec((None, lc, LANES), lambda bi, c: (bi, 0, 0)),
        pl.BlockSpec((1, LANES), lambda bi, c: (0, 0))]
    out_specs = [pl.BlockSpec((None, LSTM_WIDTH, rows), lambda bi, c: (bi, 0, c)),
                 pl.BlockSpec((None, LSTM_WIDTH, rows), lambda bi, c: (bi, 0, nc - 1 - c))]
    return pl.pallas_call(
        _mlstm_kernel,
        out_shape=[jax.ShapeDtypeStruct((b, LSTM_WIDTH, n), F32)] * 2,
        grid=(b, nc), in_specs=in_specs, out_specs=out_specs,
        scratch_shapes=[pltpu.VMEM((2, STATE_ROWS, LSTM_WIDTH), F32),
                        pltpu.VMEM((2 * LSTM_HEADS, 1, LANES), F32)],
        compiler_params=_cparams(("arbitrary", "arbitrary")), name="mlstm",
    )(qk, qk, vt, vt, g, g, qkc, vtc, gc, gate_bias)


def _rope(x, cos, sin_signed):
    w = x.shape[1]
    lane = lax.broadcasted_iota(I32, x.shape, 1)
    first = (lane & (ATTN_DIM - 1)) < (ATTN_DIM // 2)
    partner = jnp.where(first, pltpu.roll(x, w - ATTN_DIM // 2, 1), pltpu.roll(x, ATTN_DIM // 2, 1))
    return x * cos + partner * sin_signed


def _attn_kernel(q_ref, *refs):
    nblk = q_ref.shape[0] // CHUNK
    nwin = nblk + 2
    k_refs, vt_refs = refs[:nwin], refs[nwin:2 * nwin]
    kctx_ref, vtctx_ref = refs[2 * nwin:2 * nwin + 2]
    t_refs = refs[2 * nwin + 2:3 * nwin + 2]
    sink_ref, bias_ref, o_ref = refs[3 * nwin + 2:]
    step, last_step = pl.program_id(1), pl.num_programs(1) - 1

    def table(t_ref):
        t = t_ref[...]
        return t[:, :LANES], t[:, LANES:]

    ks = []
    for k_ref, t_ref in zip(k_refs, t_refs):
        cos_t, sin_t = table(t_ref)
        ks.append(_rope(k_ref[...].astype(F32), cos_t, sin_t).astype(BF16))
    for u in range(nblk):
        which = jnp.where(step == 0, 0, 1) if u == 0 else 1
        if u == nblk - 1:
            which = jnp.where(step == last_step, 2, which)
        _attn_block(q_ref[u * CHUNK:(u + 1) * CHUNK, :], table(t_refs[u + 1]),
                    jnp.concatenate(ks[u:u + 3] + [kctx_ref[...]], axis=0),
                    jnp.concatenate([r[...] for r in vt_refs[u:u + 3]] + [vtctx_ref[...]], axis=1),
                    sink_ref[...], bias_ref[which], o_ref.at[:, u * CHUNK:(u + 1) * CHUNK])


def _attn_block(q_in, q_table, k_all, vt_all, sink, bias1, o_ref):
    cos_c, sin_c = q_table
    q = _rope(q_in.astype(F32), jnp.concatenate([cos_c] * 4, axis=1), jnp.concatenate([sin_c] * 4, axis=1))
    q = q * (ATTN_DIM ** -0.5)
    per = ATTN_HEADS_PER_DOT
    bias = jnp.concatenate([bias1] * per, axis=1)
    lane = lax.broadcasted_iota(I32, (CHUNK, LANES), 1)

    for h0 in range(0, ATTN_HEADS, per):
        qs, snk = [], []
        for h in range(h0, h0 + per):
            g = h // ATTN_GROUP
            t = q[:, (h // 2) * LANES:(h // 2 + 1) * LANES]
            if h % 2 != g:
                t = pltpu.roll(t, ATTN_DIM, 1)
            half_g = (lane >= ATTN_DIM) if g == 1 else (lane < ATTN_DIM)
            qs.append(jnp.where(half_g, t, 0.0).astype(BF16))
            snk.append(jnp.broadcast_to(sink[:, h:h + 1], (1, CHUNK)))
        snk = jnp.concatenate(snk, axis=1)
        st = _dot_t(k_all, jnp.concatenate(qs, axis=0)) + bias
        m = jnp.maximum(jnp.max(st, axis=0, keepdims=True), snk)
        e = jnp.exp(st - m)
        den = jnp.sum(e, axis=0, keepdims=True) + jnp.exp(snk - m)
        pv = jnp.dot(vt_all, e.astype(BF16), preferred_element_type=F32)
        o = (pv / den).astype(o_ref.dtype)
        for r, h in enumerate(range(h0, h0 + per)):
            g = h // ATTN_GROUP
            o_ref[h * ATTN_DIM:(h + 1) * ATTN_DIM, :] = o[g * ATTN_DIM:(g + 1) * ATTN_DIM, r * CHUNK:(r + 1) * CHUNK]


def _attn_bias(nctx):
    i = jnp.arange(CHUNK)[None, :]
    j = jnp.arange(3 * CHUNK)[:, None]
    band = (j >= i) & (j <= i + 2 * CHUNK)
    local = jnp.stack([band & (j >= CHUNK), band, band & (j < 2 * CHUNK)])
    return jnp.concatenate([jnp.where(local, 0.0, -jnp.inf).astype(F32), jnp.zeros((3, nctx, CHUNK), F32)], axis=1)


def _attn(p, avt, pc, avtc, table, sink):
    b, n, _ = p.shape
    lc = pc.shape[1]
    nb = n // CHUNK
    assert nb >= 2
    qw = ATTN_HEADS * ATTN_DIM
    bias = _attn_bias(lc)

    s = ATTN_STEP_BLOCKS
    offs = range(-1, s + 1)

    def blk(col, off):
        return pl.BlockSpec((None, CHUNK, LANES), lambda bi, i: (bi, jnp.clip(i * s + off, 0, nb - 1), col))

    def vblk(off):
        return pl.BlockSpec((None, LANES, CHUNK), lambda bi, i: (bi, 0, jnp.clip(i * s + off, 0, nb - 1)))

    def tab(off):
        return pl.BlockSpec((CHUNK, 2 * LANES), lambda bi, i: (jnp.clip(i * s + off, 0, nb - 1), 0))

    in_specs = ([pl.BlockSpec((None, s * CHUNK, qw), lambda bi, i: (bi, i, PB_AQ))]
                + [blk(PB_AK, o) for o in offs] + [vblk(o) for o in offs]
                + [pl.BlockSpec((None, lc, LANES), lambda bi, i: (bi, 0, PB_AK)),
                   pl.BlockSpec((None, LANES, lc), lambda bi, i: (bi, 0, 0))]
                + [tab(o) for o in offs]
                + [pl.BlockSpec((1, LANES), lambda bi, i: (0, 0)),
                   pl.BlockSpec(bias.shape, lambda bi, i: (0, 0, 0))])
    nw = len(offs)
    return pl.pallas_call(
        _attn_kernel,
        out_shape=jax.ShapeDtypeStruct((b, qw, n), BF16),
        grid=(b, nb // s), in_specs=in_specs,
        out_specs=pl.BlockSpec((None, qw, s * CHUNK), lambda bi, i: (bi, 0, i)),
        compiler_params=_cparams(("parallel", "parallel")), name="window_attn",
    )(*([p] * (1 + nw) + [avt] * nw + [pc, avtc] + [table] * nw + [sink, bias]))


def _router_tail(x_new, g2, sc2, sh2, wr_ref, x_out_ref, h2_ref, aff_ref):
    x_out_ref[...] = x_new
    h2 = _rms_mod(x_new, g2, sc2, sh2)
    h_hi = h2.astype(BF16)
    h2_ref[...] = h_hi
    logits = jnp.dot(_hi_lo_lhs(h2, h_hi), wr_ref[...], preferred_element_type=F32)
    lane = lax.broadcasted_iota(I32, logits.shape, 1)
    logits = jnp.where(lane < N_EXPERTS, logits, -jnp.inf)
    e = jnp.exp(logits - jnp.max(logits, axis=1, keepdims=True))
    aff_ref[...] = e / jnp.sum(e, axis=1, keepdims=True)


def _ab_out_kernel(hf_ref, hb_ref, o_ref, at_ref, x_ref, *refs):
    for lo in range(0, x_ref.shape[0], TAIL_CHAIN_ROWS):
        rows = slice(lo, lo + TAIL_CHAIN_ROWS)
        _ab_out_rows(hf_ref.at[:, rows], hb_ref.at[:, rows], o_ref.at[rows], at_ref.at[:, rows], x_ref.at[rows],
                     *refs[:-3], *[out.at[rows] for out in refs[-3:]])


def _ab_out_rows(hf_ref, hb_ref, o_ref, at_ref, x_ref, hg_ref, wo_ref, gt1_ref, g2_ref, sc2_ref, sh2_ref, wr_ref,
                 x_out_ref, h2_ref, aff_ref):
    hsum = (hf_ref[...] + hb_ref[...]).T
    og = jax.nn.sigmoid(o_ref[...].astype(F32))
    hg = hg_ref[...]
    parts = []
    for h in range(LSTM_HEADS):
        sl = slice(h * LSTM_DIM, (h + 1) * LSTM_DIM)
        seg = hsum[:, sl]
        seg = seg * lax.rsqrt(jnp.mean(seg * seg, axis=-1, keepdims=True) + EPS)
        parts.append((seg * hg[:, sl] * og[:, sl]).astype(BF16))
    cat = jnp.concatenate(parts + [at_ref[...].astype(F32).T.astype(BF16)], axis=1)
    y = jnp.dot(cat, wo_ref[...], preferred_element_type=F32)
    _router_tail(x_ref[...] + gt1_ref[...] * y, g2_ref[...], sc2_ref[...], sh2_ref[...], wr_ref,
                 x_out_ref, h2_ref, aff_ref)


def _tail_out(b, n, d, tm):
    shapes = [jax.ShapeDtypeStruct((b, n, d), F32), jax.ShapeDtypeStruct((b, n, d), BF16),
              jax.ShapeDtypeStruct((b, n, LANES), F32)]
    specs = [pl.BlockSpec((None, tm, d), lambda bi, i: (bi, i, 0)),
             pl.BlockSpec((None, tm, d), lambda bi, i: (bi, i, 0)),
             pl.BlockSpec((None, tm, LANES), lambda bi, i: (bi, i, 0))]
    return shapes, specs


def _ab_out(hf, hb, p, at, x, head_g, w_out, gt1, g2, sc2, sh2, wr, tm=1024):
    b, n, d = x.shape
    row = lambda w: pl.BlockSpec((None, tm, w), lambda bi, i: (bi, i, 0))
    vec = pl.BlockSpec((None, 1, d), lambda bi, i: (bi, 0, 0))
    const = lambda s: pl.BlockSpec(s, lambda bi, i: (0, 0))
    scan_out = pl.BlockSpec((None, LSTM_WIDTH, tm), lambda bi, i: (bi, 0, i))
    in_specs = [scan_out, scan_out,
                pl.BlockSpec((None, tm, LSTM_WIDTH), lambda bi, i: (bi, i, PB_O)),
                pl.BlockSpec((None, ATTN_HEADS * ATTN_DIM, tm), lambda bi, i: (bi, 0, i)),
                row(d), const((1, LSTM_WIDTH)), const(w_out.shape),
                vec, const((1, d)), vec, vec, const((3 * d, LANES))]
    shapes, specs = _tail_out(b, n, d, tm)
    return pl.pallas_call(
        _ab_out_kernel, out_shape=shapes, grid=(b, n // tm), in_specs=in_specs, out_specs=specs,
        compiler_params=_cparams(("parallel", "parallel")), name="ab_out",
    )(hf, hb, p, at, x, head_g, w_out, gt1, g2, sc2, sh2, wr)


def _gm_out_kernel(uv_ref, x_ref, *refs):
    for lo in range(0, x_ref.shape[0], TAIL_CHAIN_ROWS):
        rows = slice(lo, lo + TAIL_CHAIN_ROWS)
        _gm_out_rows(uv_ref.at[rows], x_ref.at[rows], *refs[:-3], *[out.at[rows] for out in refs[-3:]])


def _gm_out_rows(uv_ref, x_ref, lng_ref, lnb_ref, ws_ref, bs_ref, wo_ref, gt1_ref, g2_ref, sc2_ref, sh2_ref, wr_ref,
                 x_out_ref, h2_ref, aff_ref):
    tm = uv_ref.shape[0]
    gw = GM_HALF // GM_GROUPS
    v = uv_ref[:, GM_HALF:].astype(F32)
    mu = jnp.mean(v, axis=-1, keepdims=True)
    vc = v - mu
    var = jnp.mean(vc * vc, axis=-1, keepdims=True)
    vn = (vc * lax.rsqrt(var + EPS) * lng_ref[...] + lnb_ref[...]).astype(BF16)
    zs = []
    for ch in range(tm // CHUNK):
        rows = slice(ch * CHUNK, (ch + 1) * CHUNK)
        cols = []
        for g in range(GM_GROUPS):
            sv = jnp.dot(ws_ref[g], vn[rows, g * gw:(g + 1) * gw], preferred_element_type=F32)
            cols.append(sv + bs_ref[:, g:g + 1])
        sv = jnp.concatenate(cols, axis=1)
        zs.append((uv_ref[rows, :GM_HALF].astype(F32) * sv).astype(BF16))
    z = jnp.concatenate(zs, axis=0)
    y = jnp.dot(z, wo_ref[...], preferred_element_type=F32)
    _router_tail(x_ref[...] + gt1_ref[...] * y, g2_ref[...], sc2_ref[...], sh2_ref[...], wr_ref,
                 x_out_ref, h2_ref, aff_ref)


def _gm_out(uv, x, ln_g, ln_b, w_s, b_s_t, w_out, gt1, g2, sc2, sh2, wr, tm=1024):
    b, n, d = x.shape
    vec = pl.BlockSpec((None, 1, d), lambda bi, i: (bi, 0, 0))
    const = lambda s: pl.BlockSpec(s, lambda *_: (0,) * len(s))
    in_specs = [pl.BlockSpec((None, tm, 2 * GM_HALF), lambda bi, i: (bi, i, 0)),
                pl.BlockSpec((None, tm, d), lambda bi, i: (bi, i, 0)),
                const((1, GM_HALF)), const((1, GM_HALF)), const(w_s.shape), const(b_s_t.shape), const(w_out.shape),
                vec, const((1, d)), vec, vec, const((3 * d, LANES))]
    shapes, specs = _tail_out(b, n, d, tm)
    return pl.pallas_call(
        _gm_out_kernel, out_shape=shapes, grid=(b, n // tm), in_specs=in_specs, out_specs=specs,
        compiler_params=_cparams(("parallel", "parallel")), name="gm_out",
    )(uv, x, ln_g, ln_b, w_s, b_s_t, w_out, gt1, g2, sc2, sh2, wr)


def _route_kernel(aff_ref, pos_ref, post_ref, offs_ref, afft_ref, *, cap):
    n = aff_ref.shape[0]
    nblk = n // CHUNK

    def to_expert_major(k, _):
        rows = pl.ds(pl.multiple_of(k * CHUNK, CHUNK), CHUNK)
        afft_ref[k] = aff_ref[rows, :].T[:N_EXPERTS, :]
        return 0

    lax.fori_loop(0, nblk, to_expert_major, 0, unroll=4)

    def count(pred):
        per_lane = jnp.sum(jnp.where(pred, 1.0, 0.0), axis=0)
        return jnp.sum(per_lane, axis=1, keepdims=True)

    def search(i, prefix):
        cand = prefix | jnp.left_shift(jnp.int32(1), 30 - i)
        cand_f = lax.bitcast_convert_type(cand, F32)
        return jnp.where(count(afft_ref[...] >= cand_f[None]) >= cap, cand, prefix)

    thr_col = lax.bitcast_convert_type(lax.fori_loop(0, 31, search, jnp.zeros((N_EXPERTS, 1), I32)), F32)
    need_col = cap - count(afft_ref[...] > thr_col[None])

    def to_row(col):
        full = jnp.concatenate([jnp.broadcast_to(col, (N_EXPERTS, LANES)),
                                jnp.zeros((LANES - N_EXPERTS, LANES), F32)], axis=0)
        return full.T[0:1, :]

    thr, need = to_row(thr_col), to_row(need_col)
    tril = jnp.where(_tri(CHUNK, True), 1.0, 0.0).astype(BF16)

    def block(k, carry):
        run_tie, run_sel = carry
        rows = pl.ds(pl.multiple_of(k * CHUNK, CHUNK), CHUNK)
        a = aff_ref[rows, :]
        gt = a > thr
        tie = jnp.where(a == thr, 1.0, 0.0)
        tie_incl = jnp.dot(tril, tie.astype(BF16), preferred_element_type=F32)
        sel = jnp.where(gt | ((tie > 0.0) & (tie_incl - tie + run_tie < need)), 1.0, 0.0)
        sel_incl = jnp.dot(tril, sel.astype(BF16), preferred_element_type=F32)
        pos = jnp.where(sel > 0.0, sel_incl - sel + run_sel, -1.0)
        pos_ref[rows, :] = pos.astype(I32)
        post_ref[k] = pos.T[:N_EXPERTS, :].astype(I32)
        offs_ref[k] = run_sel.astype(I32)
        return (run_tie + tie_incl[CHUNK - 1:CHUNK, :], run_sel + sel_incl[CHUNK - 1:CHUNK, :])

    zero = jnp.zeros((1, LANES), F32)
    lax.fori_loop(0, nblk, block, (zero, zero), unroll=4)


def _route(aff, cap):
    b, n, _ = aff.shape
    nblk = n // CHUNK
    return pl.pallas_call(
        functools.partial(_route_kernel, cap=cap),
        out_shape=[jax.ShapeDtypeStruct((b, n, LANES), I32),
                   jax.ShapeDtypeStruct((b, nblk, N_EXPERTS, CHUNK), I32),
                   jax.ShapeDtypeStruct((b, nblk, 1, LANES), I32),
                   jax.ShapeDtypeStruct((b, nblk, N_EXPERTS, CHUNK), F32)],
        grid=(b,),
        in_specs=[pl.BlockSpec((None, n, LANES), lambda bi: (bi, 0, 0))],
        out_specs=[pl.BlockSpec((None, n, LANES), lambda bi: (bi, 0, 0)),
                   pl.BlockSpec((None, nblk, N_EXPERTS, CHUNK), lambda bi: (bi, 0, 0, 0)),
                   pl.BlockSpec((None, nblk, 1, LANES), lambda bi: (bi, 0, 0, 0)),
                   pl.BlockSpec((None, nblk, N_EXPERTS, CHUNK), lambda bi: (bi, 0, 0, 0))],
        compiler_params=_cparams(("parallel",)), name="route",
    )(aff)


def _align_rows(s):
    shift = BF16_ROWS.bit_length() - 1
    return (s >> shift) << shift


def _window_start(s0, w, cap):
    lo = _align_rows(s0) + w * SLOT_CHUNK
    return lo, pl.multiple_of(jnp.minimum(lo, cap - SLOT_CHUNK), BF16_ROWS)


def _num_windows(s0, s1):
    return (s1 - _align_rows(s0) + SLOT_CHUNK - 1) >> (SLOT_CHUNK.bit_length() - 1)


def _moe_gather_kernel(cnt_ref, post_ref, afft_ref, h_ref, xe_ref, gate_ref, *, ntile, cap, group, tiles_per_step):
    bi, eg, ts = pl.program_id(0), pl.program_id(1), pl.program_id(2)

    @pl.when(ts == 0)
    def _():
        xe_ref[...] = jnp.zeros_like(xe_ref)
        gate_ref[...] = jnp.zeros_like(gate_ref)

    slot = lax.broadcasted_iota(I32, (SLOT_CHUNK, MOE_TILE), 0)

    def base(u, g):
        return (bi * N_EXPERTS + eg * group + g) * (ntile + 1) + ts * tiles_per_step + u

    def windows(u, w):
        htile = h_ref[u * MOE_TILE:(u + 1) * MOE_TILE, :]
        starts, onehots = [], []
        for g in range(group):
            lo, start = _window_start(cnt_ref[base(u, g)], w, cap)
            posrow = post_ref[g, u]
            hit = (posrow - start == slot) & (posrow >= lo)
            onehots.append(jnp.where(hit, 1.0, 0.0).astype(BF16))
            gates = jnp.sum(jnp.where(hit, afft_ref[g, u], 0.0), axis=1, keepdims=True)
            dst = pl.ds(start, SLOT_CHUNK)
            gate_ref[g, dst, :] = gate_ref[g, dst, :] + jnp.broadcast_to(gates, (SLOT_CHUNK, LANES))
            starts.append(start)
        rows = jnp.dot(jnp.concatenate(onehots, axis=0), htile, preferred_element_type=F32)
        for g in range(group):
            dst = pl.ds(starts[g], SLOT_CHUNK)
            xe_ref[g, dst, :] = xe_ref[g, dst, :] + rows[g * SLOT_CHUNK:(g + 1) * SLOT_CHUNK].astype(BF16)

    nwin = 1
    for u in range(tiles_per_step):
        windows(u, 0)
        for g in range(group):
            nwin = jnp.maximum(nwin, _num_windows(cnt_ref[base(u, g)], cnt_ref[base(u, g) + 1]))

    def overflow(w, _):
        for u in range(tiles_per_step):
            windows(u, w)
        return 0

    lax.fori_loop(1, nwin, overflow, 0)


def _moe_gather(cnt, post, afft, h2, cap, group=8, tiles_per_step=8):
    b, n, d = h2.shape
    ntile = n // MOE_TILE
    per_tile = pl.BlockSpec((None, group, tiles_per_step, 1, MOE_TILE), lambda bi, eg, ts, c: (bi, eg, ts, 0, 0))
    grid_spec = pltpu.PrefetchScalarGridSpec(
        num_scalar_prefetch=1, grid=(b, N_EXPERTS // group, ntile // tiles_per_step),
        in_specs=[per_tile, per_tile,
                  pl.BlockSpec((None, tiles_per_step * MOE_TILE, d), lambda bi, eg, ts, c: (bi, ts, 0))],
        out_specs=[pl.BlockSpec((None, group, cap, d), lambda bi, eg, ts, c: (bi, eg, 0, 0)),
                   pl.BlockSpec((None, group, cap, LANES), lambda bi, eg, ts, c: (bi, eg, 0, 0))])
    return pl.pallas_call(
        functools.partial(_moe_gather_kernel, ntile=ntile, cap=cap, group=group, tiles_per_step=tiles_per_step),
        out_shape=[jax.ShapeDtypeStruct((b, N_EXPERTS, cap, d), BF16),
                   jax.ShapeDtypeStruct((b, N_EXPERTS, cap, LANES), F32)],
        grid_spec=grid_spec, compiler_params=_cparams(("arbitrary", "arbitrary", "arbitrary")), name="moe_gather",
    )(cnt, post, afft, h2)


def _moe_ffn_kernel(xe_ref, gate_ref, wg_ref, wu_ref, wd_ref, y_ref, acc_scr, *, hid_tile):
    j = pl.program_id(2)
    xe = xe_ref[...]
    for k in range(wg_ref.shape[1] // hid_tile):
        cols = slice(k * hid_tile, (k + 1) * hid_tile)
        gate = jnp.dot(xe, wg_ref[:, cols].astype(BF16), preferred_element_type=F32)
        up = jnp.dot(xe, wu_ref[:, cols].astype(BF16), preferred_element_type=F32)
        hid = (_silu(gate) * up).astype(BF16)
        part = jnp.dot(hid, wd_ref[cols, :].astype(BF16), preferred_element_type=F32)
        if k == 0:
            @pl.when(j == 0)
            def _():
                acc_scr[...] = part

            @pl.when(j != 0)
            def _():
                acc_scr[...] += part
        else:
            acc_scr[...] += part

    @pl.when(j == pl.num_programs(2) - 1)
    def _():
        y_ref[...] = (acc_scr[...] * gate_ref[:, 0:1]).astype(y_ref.dtype)


def _moe_ffn(xe, gate, wg, wu, wd, layer, hid_split=1):
    b, ne, cap, d = xe.shape
    dh = wg.shape[3] // hid_split
    return pl.pallas_call(
        functools.partial(_moe_ffn_kernel, hid_tile=256),
        out_shape=jax.ShapeDtypeStruct((b, ne, cap, d), BF16),
        grid=(b, ne, hid_split),
        in_specs=[pl.BlockSpec((None, None, cap, d), lambda bi, e, j: (bi, e, 0, 0)),
                  pl.BlockSpec((None, None, cap, LANES), lambda bi, e, j: (bi, e, 0, 0)),
                  pl.BlockSpec((None, None, d, dh), lambda bi, e, j: (layer, e, 0, j)),
                  pl.BlockSpec((None, None, d, dh), lambda bi, e, j: (layer, e, 0, j)),
                  pl.BlockSpec((None, None, dh, d), lambda bi, e, j: (layer, e, j, 0))],
        out_specs=pl.BlockSpec((None, None, cap, d), lambda bi, e, j: (bi, e, 0, 0)),
        scratch_shapes=[pltpu.VMEM((cap, d), F32)],
        compiler_params=_cparams(("parallel", "parallel", "arbitrary")), name="moe_ffn",
    )(xe, gate, wg, wu, wd)


def _moe_combine_kernel(cnt_ref, pos_ref, x_ref, gt2_ref, y_ref, fg_ref, o_ref, acc_scr, ycat_scr, *, ntile, cap, final):
    bi = pl.program_id(0)
    tiles = pos_ref.shape[0] // MOE_TILE
    kw = N_EXPERTS * SLOT_CHUNK
    shift = SLOT_CHUNK.bit_length() - 1
    sel = jnp.where(lax.broadcasted_iota(I32, (LANES, kw), 0) == (lax.broadcasted_iota(I32, (LANES, kw), 1) >> shift),
                    1.0, 0.0).astype(BF16)

    def spread(v):
        low_bits = 5
        hi = (v >> low_bits).astype(F32).astype(BF16)
        lo = (v & ((1 << low_bits) - 1)).astype(F32).astype(BF16)
        return (float(1 << low_bits) * jnp.dot(hi, sel, preferred_element_type=F32)
                + jnp.dot(lo, sel, preferred_element_type=F32))

    within = (lax.broadcasted_iota(I32, (1, kw), 1) & (SLOT_CHUNK - 1)).astype(F32)
    lane = lax.broadcasted_iota(I32, (SUBLANES, LANES), 1)

    def base(u, e):
        return (bi * N_EXPERTS + e) * (ntile + 1) + pl.program_id(1) * tiles + u

    def windows(u, w):
        pos_w = spread(pos_ref[u * MOE_TILE:(u + 1) * MOE_TILE, :] + 1) - 1.0
        starts = jnp.zeros((SUBLANES, LANES), I32)
        los = jnp.zeros((SUBLANES, LANES), I32)
        for e in range(N_EXPERTS):
            lo, start = _window_start(cnt_ref[base(u, e)], w, cap)
            starts = jnp.where(lane == e, start, starts)
            los = jnp.where(lane == e, lo, los)
            ycat_scr[u, e * SLOT_CHUNK:(e + 1) * SLOT_CHUNK, :] = y_ref[e, pl.ds(start, SLOT_CHUNK), :]
        want = spread(starts)[0:1] + within
        onehot = jnp.where((pos_w == want) & (pos_w >= spread(los)[0:1]), 1.0, 0.0).astype(BF16)
        return jnp.dot(onehot, ycat_scr[u], preferred_element_type=F32)

    nwin = 1
    for u in range(tiles):
        acc_scr[u] = windows(u, 0)
        for e in range(N_EXPERTS):
            nwin = jnp.maximum(nwin, _num_windows(cnt_ref[base(u, e)], cnt_ref[base(u, e) + 1]))

    def overflow(w, _):
        for u in range(tiles):
            acc_scr[u] += windows(u, w)
        return 0

    lax.fori_loop(1, nwin, overflow, 0)
    for u in range(tiles):
        rows = slice(u * MOE_TILE, (u + 1) * MOE_TILE)
        out = x_ref[rows, :] + gt2_ref[...] * acc_scr[u]
        if final:
            out = out * lax.rsqrt(jnp.mean(out * out, axis=-1, keepdims=True) + EPS) * fg_ref[...]
        o_ref[rows, :] = out


def _moe_combine(cnt, pos, x, gt2, y, final_g, final, tiles_per_step=2):
    b, n, d = x.shape
    ntile = n // MOE_TILE
    cap = y.shape[2]
    rows = tiles_per_step * MOE_TILE
    grid_spec = pltpu.PrefetchScalarGridSpec(
        num_scalar_prefetch=1, grid=(b, ntile // tiles_per_step),
        in_specs=[pl.BlockSpec((None, rows, LANES), lambda bi, t, c: (bi, t, 0)),
                  pl.BlockSpec((None, rows, d), lambda bi, t, c: (bi, t, 0)),
                  pl.BlockSpec((None, 1, d), lambda bi, t, c: (bi, 0, 0)),
                  pl.BlockSpec((None, N_EXPERTS, cap, d), lambda bi, t, c: (bi, 0, 0, 0),
                               pipeline_mode=pl.Buffered(1)),
                  pl.BlockSpec((1, d), lambda bi, t, c: (0, 0))],
        out_specs=pl.BlockSpec((None, rows, d), lambda bi, t, c: (bi, t, 0)),
        scratch_shapes=[pltpu.VMEM((tiles_per_step, MOE_TILE, d), F32),
                        pltpu.VMEM((tiles_per_step, N_EXPERTS * SLOT_CHUNK, d), BF16)])
    return pl.pallas_call(
        functools.partial(_moe_combine_kernel, ntile=ntile, cap=cap, final=final),
        out_shape=jax.ShapeDtypeStruct((b, n, d), F32),
        grid_spec=grid_spec, compiler_params=_cparams(("arbitrary", "arbitrary")), name="moe_combine",
    )(cnt, pos, x, gt2, y, final_g)


def _ec_moe(x_mid, h2, aff, gt2, wg, wu, wd, layer, final_g, final):
    b, n, _ = x_mid.shape
    cap = max(1, EC_FACTOR * n // N_EXPERTS)
    ntile = n // MOE_TILE
    pos, post, offs, afft = _route(aff, cap)
    per_tile = lambda a: a.transpose(0, 2, 1, 3).reshape(b, N_EXPERTS, ntile, 1, MOE_TILE)
    starts = offs[:, ::MOE_TILE // CHUNK, 0, :N_EXPERTS].transpose(0, 2, 1)
    cnt = jnp.concatenate([starts, jnp.full((b, N_EXPERTS, 1), cap, I32)], axis=2).reshape(-1)
    xe, gate = _moe_gather(cnt, per_tile(post), per_tile(afft), h2, cap)
    y = _moe_ffn(xe, gate, wg, wu, wd, layer)
    return _moe_combine(cnt, pos, x_mid, gt2, y, final_g, final)


def _rope_table(n):
    rows = n // GRID_W
    row = jnp.repeat(jnp.arange(rows), GRID_W).astype(F32)
    col = jnp.tile(jnp.arange(GRID_W), rows).astype(F32)
    nf = ATTN_DIM // 4
    inv = ROPE_BASE ** (-jnp.arange(nf, dtype=F32) / nf)
    ang = jnp.concatenate([row[:, None] * inv, col[:, None] * inv], axis=-1)
    cos, sin = jnp.cos(ang), jnp.sin(ang)
    reps = LANES // ATTN_DIM
    return jnp.concatenate([jnp.tile(jnp.concatenate([cos, cos], -1), (1, reps)),
                            jnp.tile(jnp.concatenate([-sin, sin], -1), (1, reps))], axis=-1)


def _pad_lanes(a):
    return jnp.pad(a, ((0, 0), (0, LANES - a.shape[1])))


def kernel(x, c, ctx, c_ctx, w_mod, b_mod, norm_mix_g, norm_ffn_g, final_norm_g, ab_w_in, ab_conv_w, ab_gate_b,
           ab_head_g, ab_sink, ab_w_out, gm_w_in, gm_ln_g, gm_ln_b, gm_w_s, gm_b_s, gm_w_out, moe_w_router,
           moe_w_gate, moe_w_up, moe_w_down):
    b, n, d = x.shape
    depth = w_mod.shape[0]
    assert depth <= 2, "context stream is only advanced for deeper stacks; not supported here"
    cond = jnp.zeros((BF16_ROWS, d), F32).at[:b].set(c).at[b].set(c_ctx)
    mods = _adaln(cond, w_mod, b_mod)

    def mod_rows(layer, rows):
        m = mods[layer, rows].reshape(-1, 6, 1, d)
        return [m[:, i] for i in range(6)]

    row = lambda v: v.reshape(1, -1)
    for layer in range(depth):
        sh1, sc1, gt1, sh2, sc2, gt2 = mod_rows(layer, slice(0, b))
        g1, g2 = row(norm_mix_g[layer]), row(norm_ffn_g[layer])
        wr = _hi_lo_rhs(_pad_lanes(moe_w_router[layer]))
        if layer % 2 == 0:
            e = layer // 2
            csh1, csc1 = (jnp.broadcast_to(v, (b, 1, d)) for v in mod_rows(layer, slice(b, b + 1))[:2])
            w_in = ab_w_in[e]
            g_lo = 4 * LSTM_WIDTH
            w_main = jnp.concatenate([w_in[:, :g_lo], w_in[:, g_lo + N_GATES:]], axis=1).astype(BF16)
            w_gate = _hi_lo_rhs(_pad_lanes(w_in[:, g_lo:g_lo + N_GATES]))
            p, gts, vt, avt = _modmm(x, g1, sc1, sh1, w_main, w_gate, tm=1024, name="ab_in")
            pc, gtc, vtc, avtc = _modmm(ctx, g1, csc1, csh1, w_main, w_gate, name="ab_in_ctx")
            hf, hb = _mlstm(_conv_silu(p, ab_conv_w[e]), vt, gts, _conv_silu(pc, ab_conv_w[e]), vtc, gtc,
                            _pad_lanes(row(ab_gate_b[e])))
            at = _attn(p, avt, pc, avtc, _rope_table(n), _pad_lanes(row(ab_sink[e])))
            x_mid, h2, aff = _ab_out(hf, hb, p, at, x, row(ab_head_g[e]), ab_w_out[e].astype(BF16),
                                     gt1, g2, sc2, sh2, wr)
        else:
            o = layer // 2
            uv = _modmm(x, g1, sc1, sh1, gm_w_in[o].astype(BF16), act="gelu", tm=1024, name="gm_in")
            x_mid, h2, aff = _gm_out(uv, x, row(gm_ln_g[o]), row(gm_ln_b[o]), gm_w_s[o].astype(BF16),
                                     _pad_lanes(gm_b_s[o].T), gm_w_out[o].astype(BF16), gt1, g2, sc2, sh2, wr)
        x = _ec_moe(x_mid, h2, aff, gt2, moe_w_gate, moe_w_up, moe_w_down, layer,
                    row(final_norm_g), layer == depth - 1)
    return x
```

```python
import functools
import math

import jax
import jax.numpy as jnp
from jax import lax
from jax.experimental import pallas as pl
from jax.experimental.pallas import tpu as pltpu

F32 = jnp.float32
BF16 = jnp.bfloat16
I32 = jnp.int32

D_MODEL = 1024
GRID_W = 64
EPS = 1e-6
LSTM_HEADS = 4
LSTM_DIM = 128
LSTM_WIDTH = LSTM_HEADS * LSTM_DIM
LSTM_CONV = 5
CHUNK = 128
ATTN_HEADS = 8
ATTN_KV_HEADS = 2
ATTN_GROUP = ATTN_HEADS // ATTN_KV_HEADS
ATTN_DIM = 64
ROPE_BASE = 10000.0
GM_GROUPS = 8
GM_HALF = 2 * D_MODEL
N_EXPERTS = 16
EC_FACTOR = 2
N_GATES = 4 * LSTM_HEADS

LANES = 128
SUBLANES = 8
BF16_ROWS = 16
V7X_VMEM_BYTES = 64 * 1024 * 1024
VMEM_COMPILER_RESERVE_BYTES = 8 * 1024 * 1024
VMEM_LIMIT_BYTES = V7X_VMEM_BYTES - VMEM_COMPILER_RESERVE_BYTES

P_COLS = 4 * LSTM_WIDTH + ATTN_HEADS * ATTN_DIM + 2 * ATTN_KV_HEADS * ATTN_DIM
PB_V, PB_O, PB_AQ = 2, 3, 4
PB_AK = (4 * LSTM_WIDTH + ATTN_HEADS * ATTN_DIM) // LANES
PB_AV = PB_AK + ATTN_KV_HEADS * ATTN_DIM // LANES
assert ATTN_HEADS * ATTN_DIM == LSTM_WIDTH and ATTN_KV_HEADS * ATTN_DIM == LANES
ATTN_HEADS_PER_DOT = 8
ATTN_STEP_BLOCKS = 8
TAIL_CHAIN_ROWS = 256
MOE_TILE = 256
SLOT_CHUNK = 64


def _cparams(sem, vmem=VMEM_LIMIT_BYTES):
    return pltpu.CompilerParams(dimension_semantics=sem, vmem_limit_bytes=vmem)


def _rms_mod(x, g, sc, sh):
    y = x * lax.rsqrt(jnp.mean(x * x, axis=-1, keepdims=True) + EPS)
    return y * g * (1.0 + sc) + sh


def _silu(x):
    return x * jax.nn.sigmoid(x)


def _gelu_tanh(x):
    return 0.5 * x * (1.0 + jnp.tanh((2.0 / math.pi) ** 0.5 * (x + 0.044715 * (x * x * x))))


def _log_sigmoid(x):
    return jnp.minimum(x, 0.0) - jnp.log(1.0 + jnp.exp(-jnp.abs(x)))


def _dot_t(a, b):
    return lax.dot_general(a, b, (((1,), (1,)), ((), ())), preferred_element_type=F32)


def _adaln_kernel(c_ref, w_ref, b_ref, o_ref):
    s = _silu(c_ref[...])
    s_hi = s.astype(BF16)
    s_lo = (s - s_hi.astype(F32)).astype(BF16)
    w = w_ref[...]
    w_hi = w.astype(BF16)
    w_lo = (w - w_hi.astype(F32)).astype(BF16)
    both = jnp.dot(jnp.concatenate([s_hi, s_lo], axis=0), w_hi, preferred_element_type=F32)
    rows = s.shape[0]
    o_ref[...] = both[:rows] + both[rows:] + jnp.dot(s_hi, w_lo, preferred_element_type=F32) + b_ref[...]


def _adaln(cond, w_mod, b_mod):
    depth, d, six_d = w_mod.shape
    rows = cond.shape[0]
    tn = six_d // 4
    return pl.pallas_call(
        _adaln_kernel,
        out_shape=jax.ShapeDtypeStruct((depth, rows, six_d), F32),
        grid=(depth, six_d // tn),
        in_specs=[pl.BlockSpec((rows, d), lambda l, j: (0, 0)),
                  pl.BlockSpec((None, d, tn), lambda l, j: (l, 0, j)),
                  pl.BlockSpec((None, 1, tn), lambda l, j: (l, 0, j))],
        out_specs=pl.BlockSpec((None, rows, tn), lambda l, j: (l, 0, j)),
        compiler_params=_cparams(("arbitrary", "arbitrary")),
        name="adaln",
    )(cond, w_mod, b_mod.reshape(depth, 1, six_d))


def _hi_lo_rhs(w):
    w_hi = w.astype(BF16)
    return jnp.concatenate([w_hi, (w - w_hi.astype(F32)).astype(BF16)], axis=1)


def _hi_lo_dot(h, h_hi, w2):
    n = w2.shape[1] // 2
    both = jnp.dot(h_hi, w2, preferred_element_type=F32)
    h_lo = (h - h_hi.astype(F32)).astype(BF16)
    return both[:, :n] + both[:, n:] + jnp.dot(h_lo, w2[:, :n], preferred_element_type=F32)


def _modmm_kernel(x_ref, g_ref, sc_ref, sh_ref, w_ref, *rest, chunks, act, with_gates):
    if with_gates:
        wg_ref, o_ref, og_ref, vt_ref, avt_ref = rest
    else:
        (o_ref,) = rest
    h = _rms_mod(x_ref[...], g_ref[...], sc_ref[...], sh_ref[...])
    hb = h.astype(BF16)
    for lo, hi in chunks:
        y = jnp.dot(hb, w_ref[:, lo:hi], preferred_element_type=F32)
        if act == "gelu":
            y = _gelu_tanh(y)
        o_ref[:, lo:hi] = y.astype(o_ref.dtype)
        if with_gates and lo == PB_V * LSTM_WIDTH:
            vt_ref[...] = y.T.astype(vt_ref.dtype)
        if with_gates and lo <= PB_AV * LANES < hi:
            av = y[:, PB_AV * LANES - lo:(PB_AV + 1) * LANES - lo]
            avt_ref[...] = av.T.astype(avt_ref.dtype)
    if with_gates:
        og_ref[...] = _hi_lo_dot(h, hb, wg_ref[...])


def _modmm(x, g, sc, sh, w, wg=None, *, act=None, tm=512, chunk=512, name="modmm"):
    b, n, d = x.shape
    no = w.shape[1]
    tm = min(tm, n)
    assert chunk == LSTM_WIDTH
    chunks = tuple((lo, min(lo + chunk, no)) for lo in range(0, no, chunk))
    in_specs = [pl.BlockSpec((None, tm, d), lambda bi, i: (bi, i, 0)),
                pl.BlockSpec((1, d), lambda bi, i: (0, 0)),
                pl.BlockSpec((None, 1, d), lambda bi, i: (bi, 0, 0)),
                pl.BlockSpec((None, 1, d), lambda bi, i: (bi, 0, 0)),
                pl.BlockSpec((d, no), lambda bi, i: (0, 0))]
    out_shape = [jax.ShapeDtypeStruct((b, n, no), BF16)]
    out_specs = [pl.BlockSpec((None, tm, no), lambda bi, i: (bi, i, 0))]
    args = [x, g, sc, sh, w]
    if wg is not None:
        in_specs.append(pl.BlockSpec(wg.shape, lambda bi, i: (0, 0)))
        out_shape += [jax.ShapeDtypeStruct((b, n, LANES), F32), jax.ShapeDtypeStruct((b, LSTM_WIDTH, n), BF16),
                      jax.ShapeDtypeStruct((b, LANES, n), BF16)]
        out_specs += [pl.BlockSpec((None, tm, LANES), lambda bi, i: (bi, i, 0)),
                      pl.BlockSpec((None, LSTM_WIDTH, tm), lambda bi, i: (bi, 0, i)),
                      pl.BlockSpec((None, LANES, tm), lambda bi, i: (bi, 0, i))]
        args.append(wg)
    res = pl.pallas_call(
        functools.partial(_modmm_kernel, chunks=chunks, act=act, with_gates=wg is not None),
        out_shape=out_shape, grid=(b, n // tm), in_specs=in_specs, out_specs=out_specs,
        compiler_params=_cparams(("parallel", "parallel")), name=name,
    )(*args)
    return res if wg is not None else res[0]


def _conv_silu_kernel(x_ref, xp_ref, xn_ref, w_ref, o_ref, pad_scr):
    i = pl.program_id(1)
    rows = x_ref.shape[0]
    halo = BF16_ROWS
    has_prev = jnp.where(i > 0, 1.0, 0.0)
    has_next = jnp.where(i < pl.num_programs(1) - 1, 1.0, 0.0)
    pad_scr[pl.ds(0, halo), :] = (xp_ref[...].astype(F32) * has_prev).astype(BF16)
    pad_scr[pl.ds(halo, rows), :] = x_ref[...]
    pad_scr[pl.ds(halo + rows, halo), :] = (xn_ref[...].astype(F32) * has_next).astype(BF16)
    w = w_ref[...]
    win = CHUNK + 2 * halo
    r = lax.broadcasted_iota(I32, (CHUNK, win), 0)
    c = lax.broadcasted_iota(I32, (CHUNK, win), 1)
    half = LSTM_CONV // 2
    shifts = {t: jnp.where(c == r + halo + t - half, 1.0, 0.0).astype(BF16) for t in range(LSTM_CONV) if t != half}
    scale = LSTM_DIM ** -0.5
    for blk in range(rows // CHUNK):
        xw = pad_scr[pl.ds(blk * CHUNK, win), :]
        acc = xw[halo:halo + CHUNK].astype(F32) * w[half:half + 1, :]
        for t, s in shifts.items():
            acc = acc + jnp.dot(s, xw, preferred_element_type=F32) * w[t:t + 1, :]
        y = _silu(acc)
        out = pl.ds(blk * CHUNK, CHUNK)
        o_ref[out, :LSTM_WIDTH] = y[:, :LSTM_WIDTH].astype(o_ref.dtype)
        o_ref[out, LSTM_WIDTH:] = (y[:, LSTM_WIDTH:] * scale).astype(o_ref.dtype)


def _conv_silu(p, conv_w, tm=1024):
    b, n, _ = p.shape
    tm = min(tm, n)
    qkw = 2 * LSTM_WIDTH
    hpt = tm // BF16_ROWS
    nhb = n // BF16_ROWS
    return pl.pallas_call(
        _conv_silu_kernel,
        out_shape=jax.ShapeDtypeStruct((b, n, qkw), BF16),
        grid=(b, n // tm),
        in_specs=[pl.BlockSpec((None, tm, qkw), lambda bi, i: (bi, i, 0)),
                  pl.BlockSpec((None, BF16_ROWS, qkw), lambda bi, i: (bi, jnp.maximum(i * hpt - 1, 0), 0)),
                  pl.BlockSpec((None, BF16_ROWS, qkw), lambda bi, i: (bi, jnp.minimum((i + 1) * hpt, nhb - 1), 0)),
                  pl.BlockSpec((LSTM_CONV, qkw), lambda bi, i: (0, 0))],
        out_specs=pl.BlockSpec((None, tm, qkw), lambda bi, i: (bi, i, 0)),
        scratch_shapes=[pltpu.VMEM((tm + 2 * BF16_ROWS, qkw), BF16)],
        compiler_params=_cparams(("parallel", "parallel")), name="conv_silu",
    )(p, p, p, conv_w)


def _tri(n, lower):
    r = lax.broadcasted_iota(I32, (n, n), 0)
    c = lax.broadcasted_iota(I32, (n, n), 1)
    return (c <= r) if lower else (c >= r)


STATE_ROWS = LSTM_DIM + BF16_ROWS
MLSTM_STEP_CHUNKS = 8


def _mlstm_segment(d, gates, k_all, vt_all, q_all, state, ht_ref, cols):
    first = state is None
    nh = LSTM_HEADS
    seg = gates.shape[0]

    def side_by_side(pieces):
        return jnp.concatenate(pieces, axis=1)

    def block_diag(x):
        w = x.shape[1] // nh
        lane = lax.broadcasted_iota(I32, x.shape, 1)
        return jnp.concatenate([jnp.where((lane >= h * w) & (lane < (h + 1) * w), x, jnp.zeros_like(x))
                                for h in range(nh)], axis=0)

    r = lax.broadcasted_iota(I32, (seg, seg), 0)
    c = lax.broadcasted_iota(I32, (seg, seg), 1)
    before = (r <= c) if d == 0 else (r >= c)
    gates_t = gates.T[:N_GATES]
    ls = _log_sigmoid(gates_t)
    ls_hi = ls.astype(BF16)
    rest = ls - ls_hi.astype(F32)
    ls_mid = rest.astype(BF16)
    ls_lo = (rest - ls_mid.astype(F32)).astype(BF16)
    parts = jnp.dot(jnp.concatenate([ls_hi, ls_mid, ls_lo], axis=0), jnp.where(before, 1.0, 0.0).astype(BF16),
                    preferred_element_type=F32)
    bcum_t = parts[:N_GATES] + parts[N_GATES:2 * N_GATES] + parts[2 * N_GATES:]
    last = seg - 1 if d == 0 else 0
    ci = [2 * d * nh + h for h in range(nh)]
    cf = [(2 * d + 1) * nh + h for h in range(nh)]
    b_rows = side_by_side([bcum_t[c:c + 1, :] for c in cf])
    li_rows = side_by_side([gates_t[c:c + 1, :] for c in ci])
    g_heads = [bcum_t[c:c + 1, last:last + 1] for c in cf]
    cn, m_heads = (None, [jnp.zeros((1, 1), F32)] * nh) if first else state
    over_seg = lambda xs: side_by_side([jnp.broadcast_to(x, (1, seg)) for x in xs])
    m_rows, g_rows = over_seg(m_heads), over_seg(g_heads)
    vt_cat = side_by_side([vt_all[h * LSTM_DIM:(h + 1) * LSTM_DIM, :] for h in range(nh)])
    if q_all is not None:
        q_blk = block_diag(q_all)
        per_key = side_by_side([jnp.broadcast_to(gates_t[i:i + 1, :] - bcum_t[f:f + 1, :], (seg, seg)).T
                                for i, f in zip(ci, cf)])
        log_d = jnp.where(side_by_side([before] * nh), b_rows + per_key, -jnp.inf)
        m_row = jnp.maximum(b_rows + m_rows, jnp.max(log_d, axis=0, keepdims=True))
        sm = _dot_t(k_all, q_blk) * jnp.exp(log_d - m_row)
        a = jnp.exp(b_rows + m_rows - m_row)
        qc = _dot_t(cn.astype(BF16), q_blk)
        num = jnp.dot(vt_cat, block_diag(sm.astype(BF16)), preferred_element_type=F32) + a * qc[:LSTM_DIM]
        den = jnp.sum(sm, axis=0, keepdims=True) + a * qc[LSTM_DIM:LSTM_DIM + 1]
        out = num / jnp.maximum(jnp.abs(den), jnp.exp(-m_row))
        for h in range(nh):
            ht_ref[h * LSTM_DIM:(h + 1) * LSTM_DIM, cols] = out[:, h * seg:(h + 1) * seg]
    w = g_rows - b_rows + li_rows
    m_new = [jnp.maximum(g_heads[h] + m_heads[h], jnp.max(w[:, h * seg:(h + 1) * seg], axis=1, keepdims=True))
             for h in range(nh)]
    wt = jnp.exp(w - over_seg(m_new))
    aug = jnp.concatenate([(vt_cat.astype(F32) * wt).astype(BF16),
                           jnp.broadcast_to(wt, (BF16_ROWS, nh * seg)).astype(BF16)], axis=0)
    upd = jnp.dot(aug, block_diag(k_all), preferred_element_type=F32)
    if not first:
        decay = side_by_side([jnp.broadcast_to(jnp.exp(g_heads[h] + m_heads[h] - m_new[h]), (1, LSTM_DIM))
                              for h in range(nh)])
        upd = decay * cn + upd
    return upd, m_new


def _mlstm_kernel(qkf_ref, qkb_ref, vtf_ref, vtb_ref, gf_ref, gb_ref, kc_ref, vtc_ref, gc_ref, gbias_ref,
                  hf_ref, hb_ref, cn_scr, m_scr):
    gbias = gbias_ref[...]

    def save(d, state):
        cn, m_heads = state
        cn_scr[d] = cn
        for h, m in enumerate(m_heads):
            m_scr[d * LSTM_HEADS + h] = jnp.broadcast_to(m, (1, LANES))

    @pl.when(pl.program_id(1) == 0)
    def _():
        gates = gc_ref[...] + gbias
        for d in range(2):
            save(d, _mlstm_segment(d, gates, kc_ref[...], vtc_ref[...], None, None, None, None))

    states = [(cn_scr[d], [m_scr[d * LSTM_HEADS + h][:, 0:1] for h in range(LSTM_HEADS)]) for d in range(2)]
    per_step = qkf_ref.shape[0] // CHUNK
    for u in range(per_step):
        for d, (qk_ref, vt_ref, g_ref, h_ref) in enumerate(((qkf_ref, vtf_ref, gf_ref, hf_ref),
                                                            (qkb_ref, vtb_ref, gb_ref, hb_ref))):
            j = u if d == 0 else per_step - 1 - u
            rows = slice(j * CHUNK, (j + 1) * CHUNK)
            qk = qk_ref[rows, :]
            states[d] = _mlstm_segment(d, g_ref[rows, :] + gbias, qk[:, LSTM_WIDTH:], vt_ref[:, rows],
                                       qk[:, :LSTM_WIDTH], states[d], h_ref, rows)
    for d in range(2):
        save(d, states[d])


def _mlstm(qk, vt, g, qkc, vtc, gc, gate_bias):
    b, n, _ = qk.shape
    lc = qkc.shape[1]
    rows = MLSTM_STEP_CHUNKS * CHUNK
    nc = n // rows
    qkw = 2 * LSTM_WIDTH
    in_specs = [
        pl.BlockSpec((None, rows, qkw), lambda bi, c: (bi, c, 0)),
        pl.BlockSpec((None, rows, qkw), lambda bi, c: (bi, nc - 1 - c, 0)),
        pl.BlockSpec((None, LSTM_WIDTH, rows), lambda bi, c: (bi, 0, c)),
        pl.BlockSpec((None, LSTM_WIDTH, rows), lambda bi, c: (bi, 0, nc - 1 - c)),
        pl.BlockSpec((None, rows, LANES), lambda bi, c: (bi, c, 0)),
        pl.BlockSpec((None, rows, LANES), lambda bi, c: (bi, nc - 1 - c, 0)),
        pl.BlockSpec((None, lc, LSTM_WIDTH), lambda bi, c: (bi, 0, 1)),
        pl.BlockSpec((None, LSTM_WIDTH, lc), lambda bi, c: (bi, 0, 0)),
        pl.BlockSpec((None, lc, LANES), lambda bi, c: (bi, 0, 0)),
        pl.BlockSpec((1, LANES), lambda bi, c: (0, 0))]
    out_specs = [pl.BlockSpec((None, LSTM_WIDTH, rows), lambda bi, c: (bi, 0, c)),
                 pl.BlockSpec((None, LSTM_WIDTH, rows), lambda bi, c: (bi, 0, nc - 1 - c))]
    return pl.pallas_call(
        _mlstm_kernel,
        out_shape=[jax.ShapeDtypeStruct((b, LSTM_WIDTH, n), F32)] * 2,
        grid=(b, nc), in_specs=in_specs, out_specs=out_specs,
        scratch_shapes=[pltpu.VMEM((2, STATE_ROWS, LSTM_WIDTH), F32),
                        pltpu.VMEM((2 * LSTM_HEADS, 1, LANES), F32)],
        compiler_params=_cparams(("arbitrary", "arbitrary")), name="mlstm",
    )(qk, qk, vt, vt, g, g, qkc, vtc, gc, gate_bias)


def _rope(x, cos, sin_signed):
    w = x.shape[1]
    lane = lax.broadcasted_iota(I32, x.shape, 1)
    first = (lane & (ATTN_DIM - 1)) < (ATTN_DIM // 2)
    partner = jnp.where(first, pltpu.roll(x, w - ATTN_DIM // 2, 1), pltpu.roll(x, ATTN_DIM // 2, 1))
    return x * cos + partner * sin_signed


def _attn_kernel(q_ref, *refs):
    nblk = q_ref.shape[0] // CHUNK
    nwin = nblk + 2
    k_refs, vt_refs = refs[:nwin], refs[nwin:2 * nwin]
    kctx_ref, vtctx_ref = refs[2 * nwin:2 * nwin + 2]
    t_refs = refs[2 * nwin + 2:3 * nwin + 2]
    sink_ref, bias_ref, o_ref = refs[3 * nwin + 2:]
    step, last_step = pl.program_id(1), pl.num_programs(1) - 1

    def table(t_ref):
        t = t_ref[...]
        return t[:, :LANES], t[:, LANES:]

    ks = []
    for k_ref, t_ref in zip(k_refs, t_refs):
        cos_t, sin_t = table(t_ref)
        ks.append(_rope(k_ref[...].astype(F32), cos_t, sin_t).astype(BF16))
    for u in range(nblk):
        which = jnp.where(step == 0, 0, 1) if u == 0 else 1
        if u == nblk - 1:
            which = jnp.where(step == last_step, 2, which)
        _attn_block(q_ref[u * CHUNK:(u + 1) * CHUNK, :], table(t_refs[u + 1]),
                    jnp.concatenate(ks[u:u + 3] + [kctx_ref[...]], axis=0),
                    jnp.concatenate([r[...] for r in vt_refs[u:u + 3]] + [vtctx_ref[...]], axis=1),
                    sink_ref[...], bias_ref[which], o_ref.at[:, u * CHUNK:(u + 1) * CHUNK])


def _attn_block(q_in, q_table, k_all, vt_all, sink, bias1, o_ref):
    cos_c, sin_c = q_table
    q = _rope(q_in.astype(F32), jnp.concatenate([cos_c] * 4, axis=1), jnp.concatenate([sin_c] * 4, axis=1))
    q = q * (ATTN_DIM ** -0.5)
    per = ATTN_HEADS_PER_DOT
    bias = jnp.concatenate([bias1] * per, axis=1)
    lane = lax.broadcasted_iota(I32, (CHUNK, LANES), 1)

    for h0 in range(0, ATTN_HEADS, per):
        qs, snk = [], []
        for h in range(h0, h0 + per):
            g = h // ATTN_GROUP
            t = q[:, (h // 2) * LANES:(h // 2 + 1) * LANES]
            if h % 2 != g:
                t = pltpu.roll(t, ATTN_DIM, 1)
            half_g = (lane >= ATTN_DIM) if g == 1 else (lane < ATTN_DIM)
            qs.append(jnp.where(half_g, t, 0.0).astype(BF16))
            snk.append(jnp.broadcast_to(sink[:, h:h + 1], (1, CHUNK)))
        snk = jnp.concatenate(snk, axis=1)
        st = _dot_t(k_all, jnp.concatenate(qs, axis=0)) + bias
        m = jnp.maximum(jnp.max(st, axis=0, keepdims=True), snk)
        e = jnp.exp(st - m)
        den = jnp.sum(e, axis=0, keepdims=True) + jnp.exp(snk - m)
        pv = jnp.dot(vt_all, e.astype(BF16), preferred_element_type=F32)
        o = (pv / den).astype(o_ref.dtype)
        for r, h in enumerate(range(h0, h0 + per)):
            g = h // ATTN_GROUP
            o_ref[h * ATTN_DIM:(h + 1) * ATTN_DIM, :] = o[g * ATTN_DIM:(g + 1) * ATTN_DIM, r * CHUNK:(r + 1) * CHUNK]


def _attn_bias(nctx):
    i = jnp.arange(CHUNK)[None, :]
    j = jnp.arange(3 * CHUNK)[:, None]
    band = (j >= i) & (j <= i + 2 * CHUNK)
    local = jnp.stack([band & (j >= CHUNK), band, band & (j < 2 * CHUNK)])
    return jnp.concatenate([jnp.where(local, 0.0, -jnp.inf).astype(F32), jnp.zeros((3, nctx, CHUNK), F32)], axis=1)


def _attn(p, avt, pc, avtc, table, sink):
    b, n, _ = p.shape
    lc = pc.shape[1]
    nb = n // CHUNK
    assert nb >= 2
    qw = ATTN_HEADS * ATTN_DIM
    bias = _attn_bias(lc)

    s = ATTN_STEP_BLOCKS
    offs = range(-1, s + 1)

    def blk(col, off):
        return pl.BlockSpec((None, CHUNK, LANES), lambda bi, i: (bi, jnp.clip(i * s + off, 0, nb - 1), col))

    def vblk(off):
        return pl.BlockSpec((None, LANES, CHUNK), lambda bi, i: (bi, 0, jnp.clip(i * s + off, 0, nb - 1)))

    def tab(off):
        return pl.BlockSpec((CHUNK, 2 * LANES), lambda bi, i: (jnp.clip(i * s + off, 0, nb - 1), 0))

    in_specs = ([pl.BlockSpec((None, s * CHUNK, qw), lambda bi, i: (bi, i, PB_AQ))]
                + [blk(PB_AK, o) for o in offs] + [vblk(o) for o in offs]
                + [pl.BlockSpec((None, lc, LANES), lambda bi, i: (bi, 0, PB_AK)),
                   pl.BlockSpec((None, LANES, lc), lambda bi, i: (bi, 0, 0))]
                + [tab(o) for o in offs]
                + [pl.BlockSpec((1, LANES), lambda bi, i: (0, 0)),
                   pl.BlockSpec(bias.shape, lambda bi, i: (0, 0, 0))])
    nw = len(offs)
    return pl.pallas_call(
        _attn_kernel,
        out_shape=jax.ShapeDtypeStruct((b, qw, n), BF16),
        grid=(b, nb // s), in_specs=in_specs,
        out_specs=pl.BlockSpec((None, qw, s * CHUNK), lambda bi, i: (bi, 0, i)),
        compiler_params=_cparams(("parallel", "parallel")), name="window_attn",
    )(*([p] * (1 + nw) + [avt] * nw + [pc, avtc] + [table] * nw + [sink, bias]))


def _router_tail(x_new, g2, sc2, sh2, wr_ref, x_out_ref, h2_ref, aff_ref):
    x_out_ref[...] = x_new
    h2 = _rms_mod(x_new, g2, sc2, sh2)
    h_hi = h2.astype(BF16)
    h2_ref[...] = h_hi
    logits = _hi_lo_dot(h2, h_hi, wr_ref[...])
    lane = lax.broadcasted_iota(I32, logits.shape, 1)
    logits = jnp.where(lane < N_EXPERTS, logits, -jnp.inf)
    e = jnp.exp(logits - jnp.max(logits, axis=1, keepdims=True))
    aff_ref[...] = e / jnp.sum(e, axis=1, keepdims=True)


def _ab_out_kernel(hf_ref, hb_ref, o_ref, at_ref, x_ref, *refs):
    for lo in range(0, x_ref.shape[0], TAIL_CHAIN_ROWS):
        rows = slice(lo, lo + TAIL_CHAIN_ROWS)
        _ab_out_rows(hf_ref.at[:, rows], hb_ref.at[:, rows], o_ref.at[rows], at_ref.at[:, rows], x_ref.at[rows],
                     *refs[:-3], *[out.at[rows] for out in refs[-3:]])


def _ab_out_rows(hf_ref, hb_ref, o_ref, at_ref, x_ref, hg_ref, wo_ref, gt1_ref, g2_ref, sc2_ref, sh2_ref, wr_ref,
                 x_out_ref, h2_ref, aff_ref):
    hsum = (hf_ref[...] + hb_ref[...]).T
    og = jax.nn.sigmoid(o_ref[...].astype(F32))
    hg = hg_ref[...]
    parts = []
    for h in range(LSTM_HEADS):
        sl = slice(h * LSTM_DIM, (h + 1) * LSTM_DIM)
        seg = hsum[:, sl]
        seg = seg * lax.rsqrt(jnp.mean(seg * seg, axis=-1, keepdims=True) + EPS)
        parts.append((seg * hg[:, sl] * og[:, sl]).astype(BF16))
    cat = jnp.concatenate(parts + [at_ref[...].astype(F32).T.astype(BF16)], axis=1)
    y = jnp.dot(cat, wo_ref[...], preferred_element_type=F32)
    _router_tail(x_ref[...] + gt1_ref[...] * y, g2_ref[...], sc2_ref[...], sh2_ref[...], wr_ref,
                 x_out_ref, h2_ref, aff_ref)


def _tail_out(b, n, d, tm):
    shapes = [jax.ShapeDtypeStruct((b, n, d), F32), jax.ShapeDtypeStruct((b, n, d), BF16),
              jax.ShapeDtypeStruct((b, n, LANES), F32)]
    specs = [pl.BlockSpec((None, tm, d), lambda bi, i: (bi, i, 0)),
             pl.BlockSpec((None, tm, d), lambda bi, i: (bi, i, 0)),
             pl.BlockSpec((None, tm, LANES), lambda bi, i: (bi, i, 0))]
    return shapes, specs


def _ab_out(hf, hb, p, at, x, head_g, w_out, gt1, g2, sc2, sh2, wr, tm=1024):
    b, n, d = x.shape
    row = lambda w: pl.BlockSpec((None, tm, w), lambda bi, i: (bi, i, 0))
    vec = pl.BlockSpec((None, 1, d), lambda bi, i: (bi, 0, 0))
    const = lambda s: pl.BlockSpec(s, lambda bi, i: (0, 0))
    scan_out = pl.BlockSpec((None, LSTM_WIDTH, tm), lambda bi, i: (bi, 0, i))
    in_specs = [scan_out, scan_out,
                pl.BlockSpec((None, tm, LSTM_WIDTH), lambda bi, i: (bi, i, PB_O)),
                pl.BlockSpec((None, ATTN_HEADS * ATTN_DIM, tm), lambda bi, i: (bi, 0, i)),
                row(d), const((1, LSTM_WIDTH)), const(w_out.shape),
                vec, const((1, d)), vec, vec, const((d, 2 * LANES))]
    shapes, specs = _tail_out(b, n, d, tm)
    return pl.pallas_call(
        _ab_out_kernel, out_shape=shapes, grid=(b, n // tm), in_specs=in_specs, out_specs=specs,
        compiler_params=_cparams(("parallel", "parallel")), name="ab_out",
    )(hf, hb, p, at, x, head_g, w_out, gt1, g2, sc2, sh2, wr)


def _gm_out_kernel(uv_ref, x_ref, *refs):
    for lo in range(0, x_ref.shape[0], TAIL_CHAIN_ROWS):
        rows = slice(lo, lo + TAIL_CHAIN_ROWS)
        _gm_out_rows(uv_ref.at[rows], x_ref.at[rows], *refs[:-3], *[out.at[rows] for out in refs[-3:]])


def _gm_out_rows(uv_ref, x_ref, lng_ref, lnb_ref, ws_ref, bs_ref, wo_ref, gt1_ref, g2_ref, sc2_ref, sh2_ref, wr_ref,
                 x_out_ref, h2_ref, aff_ref):
    tm = uv_ref.shape[0]
    gw = GM_HALF // GM_GROUPS
    v = uv_ref[:, GM_HALF:].astype(F32)
    mu = jnp.mean(v, axis=-1, keepdims=True)
    vc = v - mu
    var = jnp.mean(vc * vc, axis=-1, keepdims=True)
    vn = (vc * lax.rsqrt(var + EPS) * lng_ref[...] + lnb_ref[...]).astype(BF16)
    zs = []
    for ch in range(tm // CHUNK):
        rows = slice(ch * CHUNK, (ch + 1) * CHUNK)
        cols = []
        for g in range(GM_GROUPS):
            sv = jnp.dot(ws_ref[g], vn[rows, g * gw:(g + 1) * gw], preferred_element_type=F32)
            cols.append(sv + bs_ref[:, g:g + 1])
        sv = jnp.concatenate(cols, axis=1)
        zs.append((uv_ref[rows, :GM_HALF].astype(F32) * sv).astype(BF16))
    z = jnp.concatenate(zs, axis=0)
    y = jnp.dot(z, wo_ref[...], preferred_element_type=F32)
    _router_tail(x_ref[...] + gt1_ref[...] * y, g2_ref[...], sc2_ref[...], sh2_ref[...], wr_ref,
                 x_out_ref, h2_ref, aff_ref)


def _gm_out(uv, x, ln_g, ln_b, w_s, b_s_t, w_out, gt1, g2, sc2, sh2, wr, tm=1024):
    b, n, d = x.shape
    vec = pl.BlockSpec((None, 1, d), lambda bi, i: (bi, 0, 0))
    const = lambda s: pl.BlockSpec(s, lambda *_: (0,) * len(s))
    in_specs = [pl.BlockSpec((None, tm, 2 * GM_HALF), lambda bi, i: (bi, i, 0)),
                pl.BlockSpec((None, tm, d), lambda bi, i: (bi, i, 0)),
                const((1, GM_HALF)), const((1, GM_HALF)), const(w_s.shape), const(b_s_t.shape), const(w_out.shape),
                vec, const((1, d)), vec, vec, const((d, 2 * LANES))]
    shapes, specs = _tail_out(b, n, d, tm)
    return pl.pallas_call(
        _gm_out_kernel, out_shape=shapes, grid=(b, n // tm), in_specs=in_specs, out_specs=specs,
        compiler_params=_cparams(("parallel", "parallel")), name="gm_out",
    )(uv, x, ln_g, ln_b, w_s, b_s_t, w_out, gt1, g2, sc2, sh2, wr)


def _route_kernel(aff_ref, pos_ref, post_ref, offs_ref, afft_ref, *, cap):
    n = aff_ref.shape[0]
    nblk = n // CHUNK

    def to_expert_major(k, _):
        rows = pl.ds(pl.multiple_of(k * CHUNK, CHUNK), CHUNK)
        afft_ref[k] = aff_ref[rows, :].T[:N_EXPERTS, :]
        return 0

    lax.fori_loop(0, nblk, to_expert_major, 0, unroll=4)

    def count(pred):
        per_lane = jnp.sum(jnp.where(pred, 1.0, 0.0), axis=0)
        return jnp.sum(per_lane, axis=1, keepdims=True)

    def search(i, prefix):
        cand = prefix | jnp.left_shift(jnp.int32(1), 30 - i)
        cand_f = lax.bitcast_convert_type(cand, F32)
        return jnp.where(count(afft_ref[...] >= cand_f[None]) >= cap, cand, prefix)

    thr_col = lax.bitcast_convert_type(lax.fori_loop(0, 31, search, jnp.zeros((N_EXPERTS, 1), I32)), F32)
    need_col = cap - count(afft_ref[...] > thr_col[None])

    def to_row(col):
        full = jnp.concatenate([jnp.broadcast_to(col, (N_EXPERTS, LANES)),
                                jnp.zeros((LANES - N_EXPERTS, LANES), F32)], axis=0)
        return full.T[0:1, :]

    thr, need = to_row(thr_col), to_row(need_col)
    tril = jnp.where(_tri(CHUNK, True), 1.0, 0.0).astype(BF16)

    def block(k, carry):
        run_tie, run_sel = carry
        rows = pl.ds(pl.multiple_of(k * CHUNK, CHUNK), CHUNK)
        a = aff_ref[rows, :]
        gt = a > thr
        tie = jnp.where(a == thr, 1.0, 0.0)
        tie_incl = jnp.dot(tril, tie.astype(BF16), preferred_element_type=F32)
        sel = jnp.where(gt | ((tie > 0.0) & (tie_incl - tie + run_tie < need)), 1.0, 0.0)
        sel_incl = jnp.dot(tril, sel.astype(BF16), preferred_element_type=F32)
        pos = jnp.where(sel > 0.0, sel_incl - sel + run_sel, -1.0)
        pos_ref[rows, :] = pos.astype(I32)
        post_ref[k] = pos.T[:N_EXPERTS, :].astype(I32)
        offs_ref[k] = run_sel.astype(I32)
        return (run_tie + tie_incl[CHUNK - 1:CHUNK, :], run_sel + sel_incl[CHUNK - 1:CHUNK, :])

    zero = jnp.zeros((1, LANES), F32)
    lax.fori_loop(0, nblk, block, (zero, zero), unroll=4)


def _route(aff, cap):
    b, n, _ = aff.shape
    nblk = n // CHUNK
    return pl.pallas_call(
        functools.partial(_route_kernel, cap=cap),
        out_shape=[jax.ShapeDtypeStruct((b, n, LANES), I32),
                   jax.ShapeDtypeStruct((b, nblk, N_EXPERTS, CHUNK), I32),
                   jax.ShapeDtypeStruct((b, nblk, 1, LANES), I32),
                   jax.ShapeDtypeStruct((b, nblk, N_EXPERTS, CHUNK), F32)],
        grid=(b,),
        in_specs=[pl.BlockSpec((None, n, LANES), lambda bi: (bi, 0, 0))],
        out_specs=[pl.BlockSpec((None, n, LANES), lambda bi: (bi, 0, 0)),
                   pl.BlockSpec((None, nblk, N_EXPERTS, CHUNK), lambda bi: (bi, 0, 0, 0)),
                   pl.BlockSpec((None, nblk, 1, LANES), lambda bi: (bi, 0, 0, 0)),
                   pl.BlockSpec((None, nblk, N_EXPERTS, CHUNK), lambda bi: (bi, 0, 0, 0))],
        compiler_params=_cparams(("parallel",)), name="route",
    )(aff)


def _align_rows(s):
    shift = BF16_ROWS.bit_length() - 1
    return (s >> shift) << shift


def _window_start(s0, w, cap):
    lo = _align_rows(s0) + w * SLOT_CHUNK
    return lo, pl.multiple_of(jnp.minimum(lo, cap - SLOT_CHUNK), BF16_ROWS)


def _num_windows(s0, s1):
    return (s1 - _align_rows(s0) + SLOT_CHUNK - 1) >> (SLOT_CHUNK.bit_length() - 1)


def _moe_gather_kernel(cnt_ref, post_ref, afft_ref, h_ref, xe_ref, gate_ref, *, ntile, cap, group, tiles_per_step):
    bi, eg, ts = pl.program_id(0), pl.program_id(1), pl.program_id(2)

    @pl.when(ts == 0)
    def _():
        xe_ref[...] = jnp.zeros_like(xe_ref)
        gate_ref[...] = jnp.zeros_like(gate_ref)

    slot = lax.broadcasted_iota(I32, (SLOT_CHUNK, MOE_TILE), 0)

    def base(u, g):
        return (bi * N_EXPERTS + eg * group + g) * (ntile + 1) + ts * tiles_per_step + u

    def windows(u, w):
        htile = h_ref[u * MOE_TILE:(u + 1) * MOE_TILE, :]
        starts, onehots = [], []
        for g in range(group):
            lo, start = _window_start(cnt_ref[base(u, g)], w, cap)
            posrow = post_ref[g, u]
            hit = (posrow - start == slot) & (posrow >= lo)
            onehots.append(jnp.where(hit, 1.0, 0.0).astype(BF16))
            gates = jnp.sum(jnp.where(hit, afft_ref[g, u], 0.0), axis=1, keepdims=True)
            dst = pl.ds(start, SLOT_CHUNK)
            gate_ref[g, dst, :] = gate_ref[g, dst, :] + jnp.broadcast_to(gates, (SLOT_CHUNK, LANES))
            starts.append(start)
        rows = jnp.dot(jnp.concatenate(onehots, axis=0), htile, preferred_element_type=F32)
        for g in range(group):
            dst = pl.ds(starts[g], SLOT_CHUNK)
            xe_ref[g, dst, :] = xe_ref[g, dst, :] + rows[g * SLOT_CHUNK:(g + 1) * SLOT_CHUNK].astype(BF16)

    nwin = 1
    for u in range(tiles_per_step):
        windows(u, 0)
        for g in range(group):
            nwin = jnp.maximum(nwin, _num_windows(cnt_ref[base(u, g)], cnt_ref[base(u, g) + 1]))

    def overflow(w, _):
        for u in range(tiles_per_step):
            windows(u, w)
        return 0

    lax.fori_loop(1, nwin, overflow, 0)


def _moe_gather(cnt, post, afft, h2, cap, group=8, tiles_per_step=8):
    b, n, d = h2.shape
    ntile = n // MOE_TILE
    per_tile = pl.BlockSpec((None, group, tiles_per_step, 1, MOE_TILE), lambda bi, eg, ts, c: (bi, eg, ts, 0, 0))
    grid_spec = pltpu.PrefetchScalarGridSpec(
        num_scalar_prefetch=1, grid=(b, N_EXPERTS // group, ntile // tiles_per_step),
        in_specs=[per_tile, per_tile,
                  pl.BlockSpec((None, tiles_per_step * MOE_TILE, d), lambda bi, eg, ts, c: (bi, ts, 0))],
        out_specs=[pl.BlockSpec((None, group, cap, d), lambda bi, eg, ts, c: (bi, eg, 0, 0)),
                   pl.BlockSpec((None, group, cap, LANES), lambda bi, eg, ts, c: (bi, eg, 0, 0))])
    return pl.pallas_call(
        functools.partial(_moe_gather_kernel, ntile=ntile, cap=cap, group=group, tiles_per_step=tiles_per_step),
        out_shape=[jax.ShapeDtypeStruct((b, N_EXPERTS, cap, d), BF16),
                   jax.ShapeDtypeStruct((b, N_EXPERTS, cap, LANES), F32)],
        grid_spec=grid_spec, compiler_params=_cparams(("arbitrary", "arbitrary", "arbitrary")), name="moe_gather",
    )(cnt, post, afft, h2)


def _moe_ffn_kernel(xe_ref, gate_ref, wg_ref, wu_ref, wd_ref, y_ref, acc_scr, *, hid_tile):
    j = pl.program_id(2)
    xe = xe_ref[...]
    for k in range(wg_ref.shape[1] // hid_tile):
        cols = slice(k * hid_tile, (k + 1) * hid_tile)
        gate = jnp.dot(xe, wg_ref[:, cols].astype(BF16), preferred_element_type=F32)
        up = jnp.dot(xe, wu_ref[:, cols].astype(BF16), preferred_element_type=F32)
        hid = (_silu(gate) * up).astype(BF16)
        part = jnp.dot(hid, wd_ref[cols, :].astype(BF16), preferred_element_type=F32)
        if k == 0:
            @pl.when(j == 0)
            def _():
                acc_scr[...] = part

            @pl.when(j != 0)
            def _():
                acc_scr[...] += part
        else:
            acc_scr[...] += part

    @pl.when(j == pl.num_programs(2) - 1)
    def _():
        y_ref[...] = (acc_scr[...] * gate_ref[:, 0:1]).astype(y_ref.dtype)


def _moe_ffn(xe, gate, wg, wu, wd, layer, hid_split=1):
    b, ne, cap, d = xe.shape
    dh = wg.shape[3] // hid_split
    return pl.pallas_call(
        functools.partial(_moe_ffn_kernel, hid_tile=256),
        out_shape=jax.ShapeDtypeStruct((b, ne, cap, d), BF16),
        grid=(b, ne, hid_split),
        in_specs=[pl.BlockSpec((None, None, cap, d), lambda bi, e, j: (bi, e, 0, 0)),
                  pl.BlockSpec((None, None, cap, LANES), lambda bi, e, j: (bi, e, 0, 0)),
                  pl.BlockSpec((None, None, d, dh), lambda bi, e, j: (layer, e, 0, j)),
                  pl.BlockSpec((None, None, d, dh), lambda bi, e, j: (layer, e, 0, j)),
                  pl.BlockSpec((None, None, dh, d), lambda bi, e, j: (layer, e, j, 0))],
        out_specs=pl.BlockSpec((None, None, cap, d), lambda bi, e, j: (bi, e, 0, 0)),
        scratch_shapes=[pltpu.VMEM((cap, d), F32)],
        compiler_params=_cparams(("parallel", "parallel", "arbitrary")), name="moe_ffn",
    )(xe, gate, wg, wu, wd)


def _moe_combine_kernel(cnt_ref, pos_ref, x_ref, gt2_ref, y_ref, fg_ref, o_ref, acc_scr, ycat_scr, *, ntile, cap, final):
    bi = pl.program_id(0)
    tiles = pos_ref.shape[0] // MOE_TILE
    kw = N_EXPERTS * SLOT_CHUNK
    shift = SLOT_CHUNK.bit_length() - 1
    sel = jnp.where(lax.broadcasted_iota(I32, (LANES, kw), 0) == (lax.broadcasted_iota(I32, (LANES, kw), 1) >> shift),
                    1.0, 0.0).astype(BF16)

    def spread(v):
        low_bits = 5
        hi = (v >> low_bits).astype(F32).astype(BF16)
        lo = (v & ((1 << low_bits) - 1)).astype(F32).astype(BF16)
        return (float(1 << low_bits) * jnp.dot(hi, sel, preferred_element_type=F32)
                + jnp.dot(lo, sel, preferred_element_type=F32))

    within = (lax.broadcasted_iota(I32, (1, kw), 1) & (SLOT_CHUNK - 1)).astype(F32)
    lane = lax.broadcasted_iota(I32, (SUBLANES, LANES), 1)

    def base(u, e):
        return (bi * N_EXPERTS + e) * (ntile + 1) + pl.program_id(1) * tiles + u

    def windows(u, w):
        pos_w = spread(pos_ref[u * MOE_TILE:(u + 1) * MOE_TILE, :] + 1) - 1.0
        starts = jnp.zeros((SUBLANES, LANES), I32)
        los = jnp.zeros((SUBLANES, LANES), I32)
        for e in range(N_EXPERTS):
            lo, start = _window_start(cnt_ref[base(u, e)], w, cap)
            starts = jnp.where(lane == e, start, starts)
            los = jnp.where(lane == e, lo, los)
            ycat_scr[u, e * SLOT_CHUNK:(e + 1) * SLOT_CHUNK, :] = y_ref[e, pl.ds(start, SLOT_CHUNK), :]
        want = spread(starts)[0:1] + within
        onehot = jnp.where((pos_w == want) & (pos_w >= spread(los)[0:1]), 1.0, 0.0).astype(BF16)
        return jnp.dot(onehot, ycat_scr[u], preferred_element_type=F32)

    nwin = 1
    for u in range(tiles):
        acc_scr[u] = windows(u, 0)
        for e in range(N_EXPERTS):
            nwin = jnp.maximum(nwin, _num_windows(cnt_ref[base(u, e)], cnt_ref[base(u, e) + 1]))

    def overflow(w, _):
        for u in range(tiles):
            acc_scr[u] += windows(u, w)
        return 0

    lax.fori_loop(1, nwin, overflow, 0)
    for u in range(tiles):
        rows = slice(u * MOE_TILE, (u + 1) * MOE_TILE)
        out = x_ref[rows, :] + gt2_ref[...] * acc_scr[u]
        if final:
            out = out * lax.rsqrt(jnp.mean(out * out, axis=-1, keepdims=True) + EPS) * fg_ref[...]
        o_ref[rows, :] = out


def _moe_combine(cnt, pos, x, gt2, y, final_g, final, tiles_per_step=2):
    b, n, d = x.shape
    ntile = n // MOE_TILE
    cap = y.shape[2]
    rows = tiles_per_step * MOE_TILE
    grid_spec = pltpu.PrefetchScalarGridSpec(
        num_scalar_prefetch=1, grid=(b, ntile // tiles_per_step),
        in_specs=[pl.BlockSpec((None, rows, LANES), lambda bi, t, c: (bi, t, 0)),
                  pl.BlockSpec((None, rows, d), lambda bi, t, c: (bi, t, 0)),
                  pl.BlockSpec((None, 1, d), lambda bi, t, c: (bi, 0, 0)),
                  pl.BlockSpec((None, N_EXPERTS, cap, d), lambda bi, t, c: (bi, 0, 0, 0),
                               pipeline_mode=pl.Buffered(1)),
                  pl.BlockSpec((1, d), lambda bi, t, c: (0, 0))],
        out_specs=pl.BlockSpec((None, rows, d), lambda bi, t, c: (bi, t, 0)),
        scratch_shapes=[pltpu.VMEM((tiles_per_step, MOE_TILE, d), F32),
                        pltpu.VMEM((tiles_per_step, N_EXPERTS * SLOT_CHUNK, d), BF16)])
    return pl.pallas_call(
        functools.partial(_moe_combine_kernel, ntile=ntile, cap=cap, final=final),
        out_shape=jax.ShapeDtypeStruct((b, n, d), F32),
        grid_spec=grid_spec, compiler_params=_cparams(("arbitrary", "arbitrary")), name="moe_combine",
    )(cnt, pos, x, gt2, y, final_g)


def _ec_moe(x_mid, h2, aff, gt2, wg, wu, wd, layer, final_g, final):
    b, n, _ = x_mid.shape
    cap = max(1, EC_FACTOR * n // N_EXPERTS)
    ntile = n // MOE_TILE
    pos, post, offs, afft = _route(aff, cap)
    per_tile = lambda a: a.transpose(0, 2, 1, 3).reshape(b, N_EXPERTS, ntile, 1, MOE_TILE)
    starts = offs[:, ::MOE_TILE // CHUNK, 0, :N_EXPERTS].transpose(0, 2, 1)
    cnt = jnp.concatenate([starts, jnp.full((b, N_EXPERTS, 1), cap, I32)], axis=2).reshape(-1)
    xe, gate = _moe_gather(cnt, per_tile(post), per_tile(afft), h2, cap)
    y = _moe_ffn(xe, gate, wg, wu, wd, layer)
    return _moe_combine(cnt, pos, x_mid, gt2, y, final_g, final)


def _rope_table(n):
    rows = n // GRID_W
    row = jnp.repeat(jnp.arange(rows), GRID_W).astype(F32)
    col = jnp.tile(jnp.arange(GRID_W), rows).astype(F32)
    nf = ATTN_DIM // 4
    inv = ROPE_BASE ** (-jnp.arange(nf, dtype=F32) / nf)
    ang = jnp.concatenate([row[:, None] * inv, col[:, None] * inv], axis=-1)
    cos, sin = jnp.cos(ang), jnp.sin(ang)
    reps = LANES // ATTN_DIM
    return jnp.concatenate([jnp.tile(jnp.concatenate([cos, cos], -1), (1, reps)),
                            jnp.tile(jnp.concatenate([-sin, sin], -1), (1, reps))], axis=-1)


def _pad_lanes(a):
    return jnp.pad(a, ((0, 0), (0, LANES - a.shape[1])))


def kernel(x, c, ctx, c_ctx, w_mod, b_mod, norm_mix_g, norm_ffn_g, final_norm_g, ab_w_in, ab_conv_w, ab_gate_b,
           ab_head_g, ab_sink, ab_w_out, gm_w_in, gm_ln_g, gm_ln_b, gm_w_s, gm_b_s, gm_w_out, moe_w_router,
           moe_w_gate, moe_w_up, moe_w_down):
    b, n, d = x.shape
    depth = w_mod.shape[0]
    assert depth <= 2, "context stream is only advanced for deeper stacks; not supported here"
    cond = jnp.zeros((BF16_ROWS, d), F32).at[:b].set(c).at[b].set(c_ctx)
    mods = _adaln(cond, w_mod, b_mod)

    def mod_rows(layer, rows):
        m = mods[layer, rows].reshape(-1, 6, 1, d)
        return [m[:, i] for i in range(6)]

    row = lambda v: v.reshape(1, -1)
    for layer in range(depth):
        sh1, sc1, gt1, sh2, sc2, gt2 = mod_rows(layer, slice(0, b))
        g1, g2 = row(norm_mix_g[layer]), row(norm_ffn_g[layer])
        wr = _hi_lo_rhs(_pad_lanes(moe_w_router[layer]))
        if layer % 2 == 0:
            e = layer // 2
            csh1, csc1 = (jnp.broadcast_to(v, (b, 1, d)) for v in mod_rows(layer, slice(b, b + 1))[:2])
            w_in = ab_w_in[e]
            g_lo = 4 * LSTM_WIDTH
            w_main = jnp.concatenate([w_in[:, :g_lo], w_in[:, g_lo + N_GATES:]], axis=1).astype(BF16)
            w_gate = _hi_lo_rhs(_pad_lanes(w_in[:, g_lo:g_lo + N_GATES]))
            p, gts, vt, avt = _modmm(x, g1, sc1, sh1, w_main, w_gate, tm=1024, name="ab_in")
            pc, gtc, vtc, avtc = _modmm(ctx, g1, csc1, csh1, w_main, w_gate, name="ab_in_ctx")
            hf, hb = _mlstm(_conv_silu(p, ab_conv_w[e]), vt, gts, _conv_silu(pc, ab_conv_w[e]), vtc, gtc,
                            _pad_lanes(row(ab_gate_b[e])))
            at = _attn(p, avt, pc, avtc, _rope_table(n), _pad_lanes(row(ab_sink[e])))
            x_mid, h2, aff = _ab_out(hf, hb, p, at, x, row(ab_head_g[e]), ab_w_out[e].astype(BF16),
                                     gt1, g2, sc2, sh2, wr)
        else:
            o = layer // 2
            uv = _modmm(x, g1, sc1, sh1, gm_w_in[o].astype(BF16), act="gelu", tm=1024, name="gm_in")
            x_mid, h2, aff = _gm_out(uv, x, row(gm_ln_g[o]), row(gm_ln_b[o]), gm_w_s[o].astype(BF16),
                                     _pad_lanes(gm_b_s[o].T), gm_w_out[o].astype(BF16), gt1, g2, sc2, sh2, wr)
        x = _ec_moe(x_mid, h2, aff, gt2, moe_w_gate, moe_w_up, moe_w_down, layer,
                    row(final_norm_g), layer == depth - 1)
    return x
```

```python
import functools
import math

import jax
import jax.numpy as jnp
from jax import lax
from jax.experimental import pallas as pl
from jax.experimental.pallas import tpu as pltpu

F32 = jnp.float32
BF16 = jnp.bfloat16
I32 = jnp.int32

D_MODEL = 1024
GRID_W = 64
EPS = 1e-6
LSTM_HEADS = 4
LSTM_DIM = 128
LSTM_WIDTH = LSTM_HEADS * LSTM_DIM
LSTM_CONV = 5
CHUNK = 128
ATTN_HEADS = 8
ATTN_KV_HEADS = 2
ATTN_GROUP = ATTN_HEADS // ATTN_KV_HEADS
ATTN_DIM = 64
ROPE_BASE = 10000.0
GM_GROUPS = 8
GM_HALF = 2 * D_MODEL
N_EXPERTS = 16
EC_FACTOR = 2
N_GATES = 4 * LSTM_HEADS

LANES = 128
SUBLANES = 8
BF16_ROWS = 16
V7X_VMEM_BYTES = 64 * 1024 * 1024
VMEM_COMPILER_RESERVE_BYTES = 8 * 1024 * 1024
VMEM_LIMIT_BYTES = V7X_VMEM_BYTES - VMEM_COMPILER_RESERVE_BYTES

P_COLS = 4 * LSTM_WIDTH + ATTN_HEADS * ATTN_DIM + 2 * ATTN_KV_HEADS * ATTN_DIM
PB_V, PB_O, PB_AQ = 2, 3, 4
PB_AK = (4 * LSTM_WIDTH + ATTN_HEADS * ATTN_DIM) // LANES
PB_AV = PB_AK + ATTN_KV_HEADS * ATTN_DIM // LANES
assert ATTN_HEADS * ATTN_DIM == LSTM_WIDTH and ATTN_KV_HEADS * ATTN_DIM == LANES
ATTN_HEADS_PER_DOT = 8
ATTN_STEP_BLOCKS = 8
TAIL_CHAIN_ROWS = 256
MOE_TILE = 256
SLOT_CHUNK = 64


def _cparams(sem, vmem=VMEM_LIMIT_BYTES):
    return pltpu.CompilerParams(dimension_semantics=sem, vmem_limit_bytes=vmem)


SHIFT1, SCALE1, GATE1, SHIFT2, SCALE2, GATE2 = range(6)


def _mod_spec(layer, comp, row=None):
    def index(bi, *_):
        return (layer, bi if row is None else row, comp, 0, 0)
    return pl.BlockSpec((None, None, None, 1, D_MODEL), index)


def _rms_mod(x, g, sc, sh):
    y = x * lax.rsqrt(jnp.mean(x * x, axis=-1, keepdims=True) + EPS)
    return y * g * (1.0 + sc) + sh


def _silu(x):
    return x * jax.nn.sigmoid(x)


def _gelu_tanh(x):
    return 0.5 * x * (1.0 + jnp.tanh((2.0 / math.pi) ** 0.5 * (x + 0.044715 * (x * x * x))))


def _log_sigmoid(x):
    return jnp.minimum(x, 0.0) - jnp.log(1.0 + jnp.exp(-jnp.abs(x)))


def _dot_t(a, b):
    return lax.dot_general(a, b, (((1,), (1,)), ((), ())), preferred_element_type=F32)


def _adaln_kernel(c_ref, w_ref, b_ref, o_ref):
    s = _silu(c_ref[...])
    s_hi = s.astype(BF16)
    s_lo = (s - s_hi.astype(F32)).astype(BF16)
    w = w_ref[...]
    w_hi = w.astype(BF16)
    w_lo = (w - w_hi.astype(F32)).astype(BF16)
    both = jnp.dot(jnp.concatenate([s_hi, s_lo], axis=0), w_hi, preferred_element_type=F32)
    rows = s.shape[0]
    o_ref[...] = both[:rows] + both[rows:] + jnp.dot(s_hi, w_lo, preferred_element_type=F32) + b_ref[...]


def _adaln(cond, w_mod, b_mod):
    depth, d, six_d = w_mod.shape
    rows = cond.shape[0]
    tn = six_d // 4
    return pl.pallas_call(
        _adaln_kernel,
        out_shape=jax.ShapeDtypeStruct((depth, rows, six_d), F32),
        grid=(depth, six_d // tn),
        in_specs=[pl.BlockSpec((rows, d), lambda l, j: (0, 0)),
                  pl.BlockSpec((None, d, tn), lambda l, j: (l, 0, j)),
                  pl.BlockSpec((None, 1, tn), lambda l, j: (l, 0, j))],
        out_specs=pl.BlockSpec((None, rows, tn), lambda l, j: (l, 0, j)),
        compiler_params=_cparams(("arbitrary", "arbitrary")),
        name="adaln",
    )(cond, w_mod, b_mod.reshape(depth, 1, six_d))


def _hi_lo_rhs(w):
    w_hi = w.astype(BF16)
    return jnp.concatenate([w_hi, (w - w_hi.astype(F32)).astype(BF16)], axis=1)


def _hi_lo_dot(h, h_hi, w2):
    n = w2.shape[1] // 2
    both = jnp.dot(h_hi, w2, preferred_element_type=F32)
    h_lo = (h - h_hi.astype(F32)).astype(BF16)
    return both[:, :n] + both[:, n:] + jnp.dot(h_lo, w2[:, :n], preferred_element_type=F32)


def _modmm_kernel(x_ref, g_ref, sc_ref, sh_ref, w_ref, *rest, chunks, act, with_gates):
    if with_gates:
        wg_ref, o_ref, og_ref, vt_ref, avt_ref = rest
    else:
        (o_ref,) = rest
    h = _rms_mod(x_ref[...], g_ref[...], sc_ref[...], sh_ref[...])
    hb = h.astype(BF16)
    for lo, hi in chunks:
        y = jnp.dot(hb, w_ref[:, lo:hi], preferred_element_type=F32)
        if act == "gelu":
            y = _gelu_tanh(y)
        o_ref[:, lo:hi] = y.astype(o_ref.dtype)
        if with_gates and lo == PB_V * LSTM_WIDTH:
            vt_ref[...] = y.T.astype(vt_ref.dtype)
        if with_gates and lo <= PB_AV * LANES < hi:
            av = y[:, PB_AV * LANES - lo:(PB_AV + 1) * LANES - lo]
            avt_ref[...] = av.T.astype(avt_ref.dtype)
    if with_gates:
        og_ref[...] = _hi_lo_dot(h, hb, wg_ref[...])


def _modmm(x, g, mods, layer, w, wg=None, *, mod_row=None, act=None, tm=512, chunk=512, name="modmm"):
    b, n, d = x.shape
    no = w.shape[1]
    tm = min(tm, n)
    assert chunk == LSTM_WIDTH
    chunks = tuple((lo, min(lo + chunk, no)) for lo in range(0, no, chunk))
    in_specs = [pl.BlockSpec((None, tm, d), lambda bi, i: (bi, i, 0)),
                pl.BlockSpec((1, d), lambda bi, i: (0, 0)),
                _mod_spec(layer, SCALE1, mod_row), _mod_spec(layer, SHIFT1, mod_row),
                pl.BlockSpec((d, no), lambda bi, i: (0, 0))]
    out_shape = [jax.ShapeDtypeStruct((b, n, no), BF16)]
    out_specs = [pl.BlockSpec((None, tm, no), lambda bi, i: (bi, i, 0))]
    args = [x, g, mods, mods, w]
    if wg is not None:
        in_specs.append(pl.BlockSpec(wg.shape, lambda bi, i: (0, 0)))
        out_shape += [jax.ShapeDtypeStruct((b, n, LANES), F32), jax.ShapeDtypeStruct((b, LSTM_WIDTH, n), BF16),
                      jax.ShapeDtypeStruct((b, LANES, n), BF16)]
        out_specs += [pl.BlockSpec((None, tm, LANES), lambda bi, i: (bi, i, 0)),
                      pl.BlockSpec((None, LSTM_WIDTH, tm), lambda bi, i: (bi, 0, i)),
                      pl.BlockSpec((None, LANES, tm), lambda bi, i: (bi, 0, i))]
        args.append(wg)
    res = pl.pallas_call(
        functools.partial(_modmm_kernel, chunks=chunks, act=act, with_gates=wg is not None),
        out_shape=out_shape, grid=(b, n // tm), in_specs=in_specs, out_specs=out_specs,
        compiler_params=_cparams(("parallel", "parallel")), name=name,
    )(*args)
    return res if wg is not None else res[0]


def _conv_silu_kernel(x_ref, xp_ref, xn_ref, w_ref, o_ref, pad_scr):
    i = pl.program_id(1)
    rows = x_ref.shape[0]
    halo = BF16_ROWS
    has_prev = jnp.where(i > 0, 1.0, 0.0)
    has_next = jnp.where(i < pl.num_programs(1) - 1, 1.0, 0.0)
    pad_scr[pl.ds(0, halo), :] = (xp_ref[...].astype(F32) * has_prev).astype(BF16)
    pad_scr[pl.ds(halo, rows), :] = x_ref[...]
    pad_scr[pl.ds(halo + rows, halo), :] = (xn_ref[...].astype(F32) * has_next).astype(BF16)
    w = w_ref[...]
    win = CHUNK + 2 * halo
    r = lax.broadcasted_iota(I32, (CHUNK, win), 0)
    c = lax.broadcasted_iota(I32, (CHUNK, win), 1)
    half = LSTM_CONV // 2
    shifts = {t: jnp.where(c == r + halo + t - half, 1.0, 0.0).astype(BF16) for t in range(LSTM_CONV) if t != half}
    scale = LSTM_DIM ** -0.5
    for blk in range(rows // CHUNK):
        xw = pad_scr[pl.ds(blk * CHUNK, win), :]
        acc = xw[halo:halo + CHUNK].astype(F32) * w[half:half + 1, :]
        for t, s in shifts.items():
            acc = acc + jnp.dot(s, xw, preferred_element_type=F32) * w[t:t + 1, :]
        y = _silu(acc)
        out = pl.ds(blk * CHUNK, CHUNK)
        o_ref[out, :LSTM_WIDTH] = y[:, :LSTM_WIDTH].astype(o_ref.dtype)
        o_ref[out, LSTM_WIDTH:] = (y[:, LSTM_WIDTH:] * scale).astype(o_ref.dtype)


def _conv_silu(p, conv_w, tm=1024):
    b, n, _ = p.shape
    tm = min(tm, n)
    qkw = 2 * LSTM_WIDTH
    hpt = tm // BF16_ROWS
    nhb = n // BF16_ROWS
    return pl.pallas_call(
        _conv_silu_kernel,
        out_shape=jax.ShapeDtypeStruct((b, n, qkw), BF16),
        grid=(b, n // tm),
        in_specs=[pl.BlockSpec((None, tm, qkw), lambda bi, i: (bi, i, 0)),
                  pl.BlockSpec((None, BF16_ROWS, qkw), lambda bi, i: (bi, jnp.maximum(i * hpt - 1, 0), 0)),
                  pl.BlockSpec((None, BF16_ROWS, qkw), lambda bi, i: (bi, jnp.minimum((i + 1) * hpt, nhb - 1), 0)),
                  pl.BlockSpec((LSTM_CONV, qkw), lambda bi, i: (0, 0))],
        out_specs=pl.BlockSpec((None, tm, qkw), lambda bi, i: (bi, i, 0)),
        scratch_shapes=[pltpu.VMEM((tm + 2 * BF16_ROWS, qkw), BF16)],
        compiler_params=_cparams(("parallel", "parallel")), name="conv_silu",
    )(p, p, p, conv_w)


def _tri(n, lower):
    r = lax.broadcasted_iota(I32, (n, n), 0)
    c = lax.broadcasted_iota(I32, (n, n), 1)
    return (c <= r) if lower else (c >= r)


STATE_ROWS = LSTM_DIM + BF16_ROWS
MLSTM_STEP_CHUNKS = 8


def _mlstm_segment(d, gates, k_all, vt_all, q_all, state, ht_ref, cols):
    first = state is None
    nh = LSTM_HEADS
    seg = gates.shape[0]

    def side_by_side(pieces):
        return jnp.concatenate(pieces, axis=1)

    def block_diag(x):
        w = x.shape[1] // nh
        lane = lax.broadcasted_iota(I32, x.shape, 1)
        return jnp.concatenate([jnp.where((lane >= h * w) & (lane < (h + 1) * w), x, jnp.zeros_like(x))
                                for h in range(nh)], axis=0)

    r = lax.broadcasted_iota(I32, (seg, seg), 0)
    c = lax.broadcasted_iota(I32, (seg, seg), 1)
    before = (r <= c) if d == 0 else (r >= c)
    gates_t = gates.T[:N_GATES]
    ls = _log_sigmoid(gates_t)
    ls_hi = ls.astype(BF16)
    rest = ls - ls_hi.astype(F32)
    ls_mid = rest.astype(BF16)
    ls_lo = (rest - ls_mid.astype(F32)).astype(BF16)
    parts = jnp.dot(jnp.concatenate([ls_hi, ls_mid, ls_lo], axis=0), jnp.where(before, 1.0, 0.0).astype(BF16),
                    preferred_element_type=F32)
    bcum_t = parts[:N_GATES] + parts[N_GATES:2 * N_GATES] + parts[2 * N_GATES:]
    last = seg - 1 if d == 0 else 0
    ci = [2 * d * nh + h for h in range(nh)]
    cf = [(2 * d + 1) * nh + h for h in range(nh)]
    b_rows = side_by_side([bcum_t[c:c + 1, :] for c in cf])
    li_rows = side_by_side([gates_t[c:c + 1, :] for c in ci])
    g_heads = [bcum_t[c:c + 1, last:last + 1] for c in cf]
    cn, m_heads = (None, [jnp.zeros((1, 1), F32)] * nh) if first else state
    over_seg = lambda xs: side_by_side([jnp.broadcast_to(x, (1, seg)) for x in xs])
    m_rows, g_rows = over_seg(m_heads), over_seg(g_heads)
    vt_cat = side_by_side([vt_all[h * LSTM_DIM:(h + 1) * LSTM_DIM, :] for h in range(nh)])
    if q_all is not None:
        q_blk = block_diag(q_all)
        per_key = side_by_side([jnp.broadcast_to(gates_t[i:i + 1, :] - bcum_t[f:f + 1, :], (seg, seg)).T
                                for i, f in zip(ci, cf)])
        log_d = jnp.where(side_by_side([before] * nh), b_rows + per_key, -jnp.inf)
        m_row = jnp.maximum(b_rows + m_rows, jnp.max(log_d, axis=0, keepdims=True))
        sm = _dot_t(k_all, q_blk) * jnp.exp(log_d - m_row)
        a = jnp.exp(b_rows + m_rows - m_row)
        qc = _dot_t(cn.astype(BF16), q_blk)
        num = jnp.dot(vt_cat, block_diag(sm.astype(BF16)), preferred_element_type=F32) + a * qc[:LSTM_DIM]
        den = jnp.sum(sm, axis=0, keepdims=True) + a * qc[LSTM_DIM:LSTM_DIM + 1]
        out = num / jnp.maximum(jnp.abs(den), jnp.exp(-m_row))
        for h in range(nh):
            ht_ref[h * LSTM_DIM:(h + 1) * LSTM_DIM, cols] = out[:, h * seg:(h + 1) * seg]
    w = g_rows - b_rows + li_rows
    m_new = [jnp.maximum(g_heads[h] + m_heads[h], jnp.max(w[:, h * seg:(h + 1) * seg], axis=1, keepdims=True))
             for h in range(nh)]
    wt = jnp.exp(w - over_seg(m_new))
    aug = jnp.concatenate([(vt_cat.astype(F32) * wt).astype(BF16),
                           jnp.broadcast_to(wt, (BF16_ROWS, nh * seg)).astype(BF16)], axis=0)
    upd = jnp.dot(aug, block_diag(k_all), preferred_element_type=F32)
    if not first:
        decay = side_by_side([jnp.broadcast_to(jnp.exp(g_heads[h] + m_heads[h] - m_new[h]), (1, LSTM_DIM))
                              for h in range(nh)])
        upd = decay * cn + upd
    return upd, m_new


def _mlstm_kernel(qkf_ref, qkb_ref, vtf_ref, vtb_ref, gf_ref, gb_ref, kc_ref, vtc_ref, gc_ref, gbias_ref,
                  hf_ref, hb_ref, cn_scr, m_scr):
    gbias = gbias_ref[...]

    def save(d, state):
        cn, m_heads = state
        cn_scr[d] = cn
        for h, m in enumerate(m_heads):
            m_scr[d * LSTM_HEADS + h] = jnp.broadcast_to(m, (1, LANES))

    @pl.when(pl.program_id(1) == 0)
    def _():
        gates = gc_ref[...] + gbias
        for d in range(2):
            save(d, _mlstm_segment(d, gates, kc_ref[...], vtc_ref[...], None, None, None, None))

    states = [(cn_scr[d], [m_scr[d * LSTM_HEADS + h][:, 0:1] for h in range(LSTM_HEADS)]) for d in range(2)]
    per_step = qkf_ref.shape[0] // CHUNK
    for u in range(per_step):
        for d, (qk_ref, vt_ref, g_ref, h_ref) in enumerate(((qkf_ref, vtf_ref, gf_ref, hf_ref),
                                                            (qkb_ref, vtb_ref, gb_ref, hb_ref))):
            j = u if d == 0 else per_step - 1 - u
            rows = slice(j * CHUNK, (j + 1) * CHUNK)
            qk = qk_ref[rows, :]
            states[d] = _mlstm_segment(d, g_ref[rows, :] + gbias, qk[:, LSTM_WIDTH:], vt_ref[:, rows],
                                       qk[:, :LSTM_WIDTH], states[d], h_ref, rows)
    for d in range(2):
        save(d, states[d])


def _mlstm(qk, vt, g, qkc, vtc, gc, gate_bias):
    b, n, _ = qk.shape
    lc = qkc.shape[1]
    rows = MLSTM_STEP_CHUNKS * CHUNK
    nc = n // rows
    qkw = 2 * LSTM_WIDTH
    in_specs = [
        pl.BlockSpec((None, rows, qkw), lambda bi, c: (bi, c, 0)),
        pl.BlockSpec((None, rows, qkw), lambda bi, c: (bi, nc - 1 - c, 0)),
        pl.BlockSpec((None, LSTM_WIDTH, rows), lambda bi, c: (bi, 0, c)),
        pl.BlockSpec((None, LSTM_WIDTH, rows), lambda bi, c: (bi, 0, nc - 1 - c)),
        pl.BlockSpec((None, rows, LANES), lambda bi, c: (bi, c, 0)),
        pl.BlockSpec((None, rows, LANES), lambda bi, c: (bi, nc - 1 - c, 0)),
        pl.BlockSpec((None, lc, LSTM_WIDTH), lambda bi, c: (bi, 0, 1)),
        pl.BlockSpec((None, LSTM_WIDTH, lc), lambda bi, c: (bi, 0, 0)),
        pl.BlockSpec((None, lc, LANES), lambda bi, c: (bi, 0, 0)),
        pl.BlockSpec((1, LANES), lambda bi, c: (0, 0))]
    out_specs = [pl.BlockSpec((None, LSTM_WIDTH, rows), lambda bi, c: (bi, 0, c)),
                 pl.BlockSpec((None, LSTM_WIDTH, rows), lambda bi, c: (bi, 0, nc - 1 - c))]
    return pl.pallas_call(
        _mlstm_kernel,
        out_shape=[jax.ShapeDtypeStruct((b, LSTM_WIDTH, n), F32)] * 2,
        grid=(b, nc), in_specs=in_specs, out_specs=out_specs,
        scratch_shapes=[pltpu.VMEM((2, STATE_ROWS, LSTM_WIDTH), F32),
                        pltpu.VMEM((2 * LSTM_HEADS, 1, LANES), F32)],
        compiler_params=_cparams(("arbitrary", "arbitrary")), name="mlstm",
    )(qk, qk, vt, vt, g, g, qkc, vtc, gc, gate_bias)


def _rope(x, cos, sin_signed):
    w = x.shape[1]
    lane = lax.broadcasted_iota(I32, x.shape, 1)
    first = (lane & (ATTN_DIM - 1)) < (ATTN_DIM // 2)
    partner = jnp.where(first, pltpu.roll(x, w - ATTN_DIM // 2, 1), pltpu.roll(x, ATTN_DIM // 2, 1))
    return x * cos + partner * sin_signed


def _attn_kernel(q_ref, *refs):
    nblk = q_ref.shape[0] // CHUNK
    nwin = nblk + 2
    k_refs, vt_refs = refs[:nwin], refs[nwin:2 * nwin]
    kctx_ref, vtctx_ref = refs[2 * nwin:2 * nwin + 2]
    t_refs = refs[2 * nwin + 2:3 * nwin + 2]
    sink_ref, bias_ref, o_ref = refs[3 * nwin + 2:]
    step, last_step = pl.program_id(1), pl.num_programs(1) - 1

    def table(t_ref):
        t = t_ref[...]
        return t[:, :LANES], t[:, LANES:]

    ks = []
    for k_ref, t_ref in zip(k_refs, t_refs):
        cos_t, sin_t = table(t_ref)
        ks.append(_rope(k_ref[...].astype(F32), cos_t, sin_t).astype(BF16))
    for u in range(nblk):
        which = jnp.where(step == 0, 0, 1) if u == 0 else 1
        if u == nblk - 1:
            which = jnp.where(step == last_step, 2, which)
        _attn_block(q_ref[u * CHUNK:(u + 1) * CHUNK, :], table(t_refs[u + 1]),
                    jnp.concatenate(ks[u:u + 3] + [kctx_ref[...]], axis=0),
                    jnp.concatenate([r[...] for r in vt_refs[u:u + 3]] + [vtctx_ref[...]], axis=1),
                    sink_ref[...], bias_ref[which], o_ref.at[:, u * CHUNK:(u + 1) * CHUNK])


def _attn_block(q_in, q_table, k_all, vt_all, sink, bias1, o_ref):
    cos_c, sin_c = q_table
    q = _rope(q_in.astype(F32), jnp.concatenate([cos_c] * 4, axis=1), jnp.concatenate([sin_c] * 4, axis=1))
    q = q * (ATTN_DIM ** -0.5)
    per = ATTN_HEADS_PER_DOT
    bias = jnp.concatenate([bias1] * per, axis=1)
    lane = lax.broadcasted_iota(I32, (CHUNK, LANES), 1)

    for h0 in range(0, ATTN_HEADS, per):
        qs, snk = [], []
        for h in range(h0, h0 + per):
            g = h // ATTN_GROUP
            t = q[:, (h // 2) * LANES:(h // 2 + 1) * LANES]
            if h % 2 != g:
                t = pltpu.roll(t, ATTN_DIM, 1)
            half_g = (lane >= ATTN_DIM) if g == 1 else (lane < ATTN_DIM)
            qs.append(jnp.where(half_g, t, 0.0).astype(BF16))
            snk.append(jnp.broadcast_to(sink[:, h:h + 1], (1, CHUNK)))
        snk = jnp.concatenate(snk, axis=1)
        st = _dot_t(k_all, jnp.concatenate(qs, axis=0)) + bias
        m = jnp.maximum(jnp.max(st, axis=0, keepdims=True), snk)
        e = jnp.exp(st - m)
        den = jnp.sum(e, axis=0, keepdims=True) + jnp.exp(snk - m)
        pv = jnp.dot(vt_all, e.astype(BF16), preferred_element_type=F32)
        o = (pv / den).astype(o_ref.dtype)
        for r, h in enumerate(range(h0, h0 + per)):
            g = h // ATTN_GROUP
            o_ref[h * ATTN_DIM:(h + 1) * ATTN_DIM, :] = o[g * ATTN_DIM:(g + 1) * ATTN_DIM, r * CHUNK:(r + 1) * CHUNK]


def _attn_bias(nctx):
    i = jnp.arange(CHUNK)[None, :]
    j = jnp.arange(3 * CHUNK)[:, None]
    band = (j >= i) & (j <= i + 2 * CHUNK)
    local = jnp.stack([band & (j >= CHUNK), band, band & (j < 2 * CHUNK)])
    return jnp.concatenate([jnp.where(local, 0.0, -jnp.inf).astype(F32), jnp.zeros((3, nctx, CHUNK), F32)], axis=1)


def _attn(p, avt, pc, avtc, table, sink):
    b, n, _ = p.shape
    lc = pc.shape[1]
    nb = n // CHUNK
    assert nb >= 2
    qw = ATTN_HEADS * ATTN_DIM
    bias = _attn_bias(lc)

    s = ATTN_STEP_BLOCKS
    offs = range(-1, s + 1)

    def blk(col, off):
        return pl.BlockSpec((None, CHUNK, LANES), lambda bi, i: (bi, jnp.clip(i * s + off, 0, nb - 1), col))

    def vblk(off):
        return pl.BlockSpec((None, LANES, CHUNK), lambda bi, i: (bi, 0, jnp.clip(i * s + off, 0, nb - 1)))

    def tab(off):
        return pl.BlockSpec((CHUNK, 2 * LANES), lambda bi, i: (jnp.clip(i * s + off, 0, nb - 1), 0))

    in_specs = ([pl.BlockSpec((None, s * CHUNK, qw), lambda bi, i: (bi, i, PB_AQ))]
                + [blk(PB_AK, o) for o in offs] + [vblk(o) for o in offs]
                + [pl.BlockSpec((None, lc, LANES), lambda bi, i: (bi, 0, PB_AK)),
                   pl.BlockSpec((None, LANES, lc), lambda bi, i: (bi, 0, 0))]
                + [tab(o) for o in offs]
                + [pl.BlockSpec((1, LANES), lambda bi, i: (0, 0)),
                   pl.BlockSpec(bias.shape, lambda bi, i: (0, 0, 0))])
    nw = len(offs)
    return pl.pallas_call(
        _attn_kernel,
        out_shape=jax.ShapeDtypeStruct((b, qw, n), BF16),
        grid=(b, nb // s), in_specs=in_specs,
        out_specs=pl.BlockSpec((None, qw, s * CHUNK), lambda bi, i: (bi, 0, i)),
        compiler_params=_cparams(("parallel", "parallel")), name="window_attn",
    )(*([p] * (1 + nw) + [avt] * nw + [pc, avtc] + [table] * nw + [sink, bias]))


def _router_tail(x_new, g2, sc2, sh2, wr_ref, x_out_ref, h2_ref, aff_ref):
    x_out_ref[...] = x_new
    h2 = _rms_mod(x_new, g2, sc2, sh2)
    h_hi = h2.astype(BF16)
    h2_ref[...] = h_hi
    logits = _hi_lo_dot(h2, h_hi, wr_ref[...])
    lane = lax.broadcasted_iota(I32, logits.shape, 1)
    logits = jnp.where(lane < N_EXPERTS, logits, -jnp.inf)
    e = jnp.exp(logits - jnp.max(logits, axis=1, keepdims=True))
    aff_ref[...] = e / jnp.sum(e, axis=1, keepdims=True)


def _ab_out_kernel(hf_ref, hb_ref, o_ref, at_ref, x_ref, *refs):
    for lo in range(0, x_ref.shape[0], TAIL_CHAIN_ROWS):
        rows = slice(lo, lo + TAIL_CHAIN_ROWS)
        _ab_out_rows(hf_ref.at[:, rows], hb_ref.at[:, rows], o_ref.at[rows], at_ref.at[:, rows], x_ref.at[rows],
                     *refs[:-3], *[out.at[rows] for out in refs[-3:]])


def _ab_out_rows(hf_ref, hb_ref, o_ref, at_ref, x_ref, hg_ref, wo_ref, gt1_ref, g2_ref, sc2_ref, sh2_ref, wr_ref,
                 x_out_ref, h2_ref, aff_ref):
    hsum = (hf_ref[...] + hb_ref[...]).T
    og = jax.nn.sigmoid(o_ref[...].astype(F32))
    hg = hg_ref[...]
    parts = []
    for h in range(LSTM_HEADS):
        sl = slice(h * LSTM_DIM, (h + 1) * LSTM_DIM)
        seg = hsum[:, sl]
        seg = seg * lax.rsqrt(jnp.mean(seg * seg, axis=-1, keepdims=True) + EPS)
        parts.append((seg * hg[:, sl] * og[:, sl]).astype(BF16))
    cat = jnp.concatenate(parts + [at_ref[...].astype(F32).T.astype(BF16)], axis=1)
    y = jnp.dot(cat, wo_ref[...], preferred_element_type=F32)
    _router_tail(x_ref[...] + gt1_ref[...] * y, g2_ref[...], sc2_ref[...], sh2_ref[...], wr_ref,
                 x_out_ref, h2_ref, aff_ref)


def _tail_out(b, n, d, tm):
    shapes = [jax.ShapeDtypeStruct((b, n, d), F32), jax.ShapeDtypeStruct((b, n, d), BF16),
              jax.ShapeDtypeStruct((b, n, LANES), F32)]
    specs = [pl.BlockSpec((None, tm, d), lambda bi, i: (bi, i, 0)),
             pl.BlockSpec((None, tm, d), lambda bi, i: (bi, i, 0)),
             pl.BlockSpec((None, tm, LANES), lambda bi, i: (bi, i, 0))]
    return shapes, specs


def _ab_out(hf, hb, p, at, x, head_g, w_out, mods, layer, g2, wr, tm=1024):
    b, n, d = x.shape
    row = lambda w: pl.BlockSpec((None, tm, w), lambda bi, i: (bi, i, 0))
    const = lambda s: pl.BlockSpec(s, lambda bi, i: (0, 0))
    scan_out = pl.BlockSpec((None, LSTM_WIDTH, tm), lambda bi, i: (bi, 0, i))
    in_specs = [scan_out, scan_out,
                pl.BlockSpec((None, tm, LSTM_WIDTH), lambda bi, i: (bi, i, PB_O)),
                pl.BlockSpec((None, ATTN_HEADS * ATTN_DIM, tm), lambda bi, i: (bi, 0, i)),
                row(d), const((1, LSTM_WIDTH)), const(w_out.shape),
                _mod_spec(layer, GATE1), const((1, d)), _mod_spec(layer, SCALE2), _mod_spec(layer, SHIFT2),
                const((d, 2 * LANES))]
    shapes, specs = _tail_out(b, n, d, tm)
    return pl.pallas_call(
        _ab_out_kernel, out_shape=shapes, grid=(b, n // tm), in_specs=in_specs, out_specs=specs,
        compiler_params=_cparams(("parallel", "parallel")), name="ab_out",
    )(hf, hb, p, at, x, head_g, w_out, mods, g2, mods, mods, wr)


def _gm_out_kernel(uv_ref, x_ref, *refs):
    for lo in range(0, x_ref.shape[0], TAIL_CHAIN_ROWS):
        rows = slice(lo, lo + TAIL_CHAIN_ROWS)
        _gm_out_rows(uv_ref.at[rows], x_ref.at[rows], *refs[:-3], *[out.at[rows] for out in refs[-3:]])


def _gm_out_rows(uv_ref, x_ref, lng_ref, lnb_ref, ws_ref, bs_ref, wo_ref, gt1_ref, g2_ref, sc2_ref, sh2_ref, wr_ref,
                 x_out_ref, h2_ref, aff_ref):
    tm = uv_ref.shape[0]
    gw = GM_HALF // GM_GROUPS
    v = uv_ref[:, GM_HALF:].astype(F32)
    mu = jnp.mean(v, axis=-1, keepdims=True)
    vc = v - mu
    var = jnp.mean(vc * vc, axis=-1, keepdims=True)
    vn = (vc * lax.rsqrt(var + EPS) * lng_ref[...] + lnb_ref[...]).astype(BF16)
    zs = []
    for ch in range(tm // CHUNK):
        rows = slice(ch * CHUNK, (ch + 1) * CHUNK)
        cols = []
        for g in range(GM_GROUPS):
            sv = jnp.dot(ws_ref[g], vn[rows, g * gw:(g + 1) * gw], preferred_element_type=F32)
            cols.append(sv + bs_ref[:, g:g + 1])
        sv = jnp.concatenate(cols, axis=1)
        zs.append((uv_ref[rows, :GM_HALF].astype(F32) * sv).astype(BF16))
    z = jnp.concatenate(zs, axis=0)
    y = jnp.dot(z, wo_ref[...], preferred_element_type=F32)
    _router_tail(x_ref[...] + gt1_ref[...] * y, g2_ref[...], sc2_ref[...], sh2_ref[...], wr_ref,
                 x_out_ref, h2_ref, aff_ref)


def _gm_out(uv, x, ln_g, ln_b, w_s, b_s_t, w_out, mods, layer, g2, wr, tm=1024):
    b, n, d = x.shape
    const = lambda s: pl.BlockSpec(s, lambda *_: (0,) * len(s))
    in_specs = [pl.BlockSpec((None, tm, 2 * GM_HALF), lambda bi, i: (bi, i, 0)),
                pl.BlockSpec((None, tm, d), lambda bi, i: (bi, i, 0)),
                const((1, GM_HALF)), const((1, GM_HALF)), const(w_s.shape), const(b_s_t.shape), const(w_out.shape),
                _mod_spec(layer, GATE1), const((1, d)), _mod_spec(layer, SCALE2), _mod_spec(layer, SHIFT2),
                const((d, 2 * LANES))]
    shapes, specs = _tail_out(b, n, d, tm)
    return pl.pallas_call(
        _gm_out_kernel, out_shape=shapes, grid=(b, n // tm), in_specs=in_specs, out_specs=specs,
        compiler_params=_cparams(("parallel", "parallel")), name="gm_out",
    )(uv, x, ln_g, ln_b, w_s, b_s_t, w_out, mods, g2, mods, mods, wr)


def _route_kernel(aff_ref, pos_ref, post_ref, offs_ref, afft_ref, *, cap):
    n = aff_ref.shape[0]
    nblk = n // CHUNK

    def to_expert_major(k, _):
        rows = pl.ds(pl.multiple_of(k * CHUNK, CHUNK), CHUNK)
        afft_ref[k] = aff_ref[rows, :].T[:N_EXPERTS, :]
        return 0

    lax.fori_loop(0, nblk, to_expert_major, 0, unroll=4)

    def count(pred):
        per_lane = jnp.sum(jnp.where(pred, 1.0, 0.0), axis=0)
        return jnp.sum(per_lane, axis=1, keepdims=True)

    def search(i, prefix):
        cand = prefix | jnp.left_shift(jnp.int32(1), 30 - i)
        cand_f = lax.bitcast_convert_type(cand, F32)
        return jnp.where(count(afft_ref[...] >= cand_f[None]) >= cap, cand, prefix)

    thr_col = lax.bitcast_convert_type(lax.fori_loop(0, 31, search, jnp.zeros((N_EXPERTS, 1), I32)), F32)
    need_col = cap - count(afft_ref[...] > thr_col[None])

    def to_row(col):
        full = jnp.concatenate([jnp.broadcast_to(col, (N_EXPERTS, LANES)),
                                jnp.zeros((LANES - N_EXPERTS, LANES), F32)], axis=0)
        return full.T[0:1, :]

    thr, need = to_row(thr_col), to_row(need_col)
    tril = jnp.where(_tri(CHUNK, True), 1.0, 0.0).astype(BF16)

    def block(k, carry):
        run_tie, run_sel = carry
        rows = pl.ds(pl.multiple_of(k * CHUNK, CHUNK), CHUNK)
        a = aff_ref[rows, :]
        gt = a > thr
        tie = jnp.where(a == thr, 1.0, 0.0)
        tie_incl = jnp.dot(tril, tie.astype(BF16), preferred_element_type=F32)
        sel = jnp.where(gt | ((tie > 0.0) & (tie_incl - tie + run_tie < need)), 1.0, 0.0)
        sel_incl = jnp.dot(tril, sel.astype(BF16), preferred_element_type=F32)
        pos = jnp.where(sel > 0.0, sel_incl - sel + run_sel, -1.0)
        pos_ref[rows, :] = pos.astype(I32)
        post_ref[k] = pos.T[:N_EXPERTS, :].astype(I32)
        offs_ref[k] = run_sel.astype(I32)
        return (run_tie + tie_incl[CHUNK - 1:CHUNK, :], run_sel + sel_incl[CHUNK - 1:CHUNK, :])

    zero = jnp.zeros((1, LANES), F32)
    lax.fori_loop(0, nblk, block, (zero, zero), unroll=4)


def _route(aff, cap):
    b, n, _ = aff.shape
    nblk = n // CHUNK
    return pl.pallas_call(
        functools.partial(_route_kernel, cap=cap),
        out_shape=[jax.ShapeDtypeStruct((b, n, LANES), I32),
                   jax.ShapeDtypeStruct((b, nblk, N_EXPERTS, CHUNK), I32),
                   jax.ShapeDtypeStruct((b, nblk, 1, LANES), I32),
                   jax.ShapeDtypeStruct((b, nblk, N_EXPERTS, CHUNK), F32)],
        grid=(b,),
        in_specs=[pl.BlockSpec((None, n, LANES), lambda bi: (bi, 0, 0))],
        out_specs=[pl.BlockSpec((None, n, LANES), lambda bi: (bi, 0, 0)),
                   pl.BlockSpec((None, nblk, N_EXPERTS, CHUNK), lambda bi: (bi, 0, 0, 0)),
                   pl.BlockSpec((None, nblk, 1, LANES), lambda bi: (bi, 0, 0, 0)),
                   pl.BlockSpec((None, nblk, N_EXPERTS, CHUNK), lambda bi: (bi, 0, 0, 0))],
        compiler_params=_cparams(("parallel",)), name="route",
    )(aff)


def _align_rows(s):
    shift = BF16_ROWS.bit_length() - 1
    return (s >> shift) << shift


def _window_start(s0, w, cap):
    lo = _align_rows(s0) + w * SLOT_CHUNK
    return lo, pl.multiple_of(jnp.minimum(lo, cap - SLOT_CHUNK), BF16_ROWS)


def _num_windows(s0, s1):
    return (s1 - _align_rows(s0) + SLOT_CHUNK - 1) >> (SLOT_CHUNK.bit_length() - 1)


def _moe_gather_kernel(cnt_ref, post_ref, afft_ref, h_ref, xe_ref, gate_ref, *, ntile, cap, group, tiles_per_step):
    bi, eg, ts = pl.program_id(0), pl.program_id(1), pl.program_id(2)

    @pl.when(ts == 0)
    def _():
        xe_ref[...] = jnp.zeros_like(xe_ref)
        gate_ref[...] = jnp.zeros_like(gate_ref)

    slot = lax.broadcasted_iota(I32, (SLOT_CHUNK, MOE_TILE), 0)

    def base(u, g):
        return (bi * N_EXPERTS + eg * group + g) * (ntile + 1) + ts * tiles_per_step + u

    def windows(u, w):
        htile = h_ref[u * MOE_TILE:(u + 1) * MOE_TILE, :]
        starts, onehots = [], []
        for g in range(group):
            lo, start = _window_start(cnt_ref[base(u, g)], w, cap)
            posrow = post_ref[g, u]
            hit = (posrow - start == slot) & (posrow >= lo)
            onehots.append(jnp.where(hit, 1.0, 0.0).astype(BF16))
            gates = jnp.sum(jnp.where(hit, afft_ref[g, u], 0.0), axis=1, keepdims=True)
            dst = pl.ds(start, SLOT_CHUNK)
            gate_ref[g, dst, :] = gate_ref[g, dst, :] + jnp.broadcast_to(gates, (SLOT_CHUNK, LANES))
            starts.append(start)
        rows = jnp.dot(jnp.concatenate(onehots, axis=0), htile, preferred_element_type=F32)
        for g in range(group):
            dst = pl.ds(starts[g], SLOT_CHUNK)
            xe_ref[g, dst, :] = xe_ref[g, dst, :] + rows[g * SLOT_CHUNK:(g + 1) * SLOT_CHUNK].astype(BF16)

    nwin = 1
    for u in range(tiles_per_step):
        windows(u, 0)
        for g in range(group):
            nwin = jnp.maximum(nwin, _num_windows(cnt_ref[base(u, g)], cnt_ref[base(u, g) + 1]))

    def overflow(w, _):
        for u in range(tiles_per_step):
            windows(u, w)
        return 0

    lax.fori_loop(1, nwin, overflow, 0)


def _moe_gather(cnt, post, afft, h2, cap, group=8, tiles_per_step=8):
    b, n, d = h2.shape
    ntile = n // MOE_TILE
    per_tile = pl.BlockSpec((None, group, tiles_per_step, 1, MOE_TILE), lambda bi, eg, ts, c: (bi, eg, ts, 0, 0))
    grid_spec = pltpu.PrefetchScalarGridSpec(
        num_scalar_prefetch=1, grid=(b, N_EXPERTS // group, ntile // tiles_per_step),
        in_specs=[per_tile, per_tile,
                  pl.BlockSpec((None, tiles_per_step * MOE_TILE, d), lambda bi, eg, ts, c: (bi, ts, 0))],
        out_specs=[pl.BlockSpec((None, group, cap, d), lambda bi, eg, ts, c: (bi, eg, 0, 0)),
                   pl.BlockSpec((None, group, cap, LANES), lambda bi, eg, ts, c: (bi, eg, 0, 0))])
    return pl.pallas_call(
        functools.partial(_moe_gather_kernel, ntile=ntile, cap=cap, group=group, tiles_per_step=tiles_per_step),
        out_shape=[jax.ShapeDtypeStruct((b, N_EXPERTS, cap, d), BF16),
                   jax.ShapeDtypeStruct((b, N_EXPERTS, cap, LANES), F32)],
        grid_spec=grid_spec, compiler_params=_cparams(("arbitrary", "arbitrary", "arbitrary")), name="moe_gather",
    )(cnt, post, afft, h2)


def _moe_ffn_kernel(xe_ref, gate_ref, wg_ref, wu_ref, wd_ref, y_ref, acc_scr, *, hid_tile):
    j = pl.program_id(2)
    xe = xe_ref[...]
    for k in range(wg_ref.shape[1] // hid_tile):
        cols = slice(k * hid_tile, (k + 1) * hid_tile)
        gate = jnp.dot(xe, wg_ref[:, cols].astype(BF16), preferred_element_type=F32)
        up = jnp.dot(xe, wu_ref[:, cols].astype(BF16), preferred_element_type=F32)
        hid = (_silu(gate) * up).astype(BF16)
        part = jnp.dot(hid, wd_ref[cols, :].astype(BF16), preferred_element_type=F32)
        if k == 0:
            @pl.when(j == 0)
            def _():
                acc_scr[...] = part

            @pl.when(j != 0)
            def _():
                acc_scr[...] += part
        else:
            acc_scr[...] += part

    @pl.when(j == pl.num_programs(2) - 1)
    def _():
        y_ref[...] = (acc_scr[...] * gate_ref[:, 0:1]).astype(y_ref.dtype)


def _moe_ffn(xe, gate, wg, wu, wd, layer, hid_split=1):
    b, ne, cap, d = xe.shape
    dh = wg.shape[3] // hid_split
    return pl.pallas_call(
        functools.partial(_moe_ffn_kernel, hid_tile=256),
        out_shape=jax.ShapeDtypeStruct((b, ne, cap, d), BF16),
        grid=(b, ne, hid_split),
        in_specs=[pl.BlockSpec((None, None, cap, d), lambda bi, e, j: (bi, e, 0, 0)),
                  pl.BlockSpec((None, None, cap, LANES), lambda bi, e, j: (bi, e, 0, 0)),
                  pl.BlockSpec((None, None, d, dh), lambda bi, e, j: (layer, e, 0, j)),
                  pl.BlockSpec((None, None, d, dh), lambda bi, e, j: (layer, e, 0, j)),
                  pl.BlockSpec((None, None, dh, d), lambda bi, e, j: (layer, e, j, 0))],
        out_specs=pl.BlockSpec((None, None, cap, d), lambda bi, e, j: (bi, e, 0, 0)),
        scratch_shapes=[pltpu.VMEM((cap, d), F32)],
        compiler_params=_cparams(("parallel", "parallel", "arbitrary")), name="moe_ffn",
    )(xe, gate, wg, wu, wd)


def _moe_combine_kernel(cnt_ref, pos_ref, x_ref, gt2_ref, y_ref, fg_ref, o_ref, acc_scr, ycat_scr, *, ntile, cap, final):
    bi = pl.program_id(0)
    tiles = pos_ref.shape[0] // MOE_TILE
    kw = N_EXPERTS * SLOT_CHUNK
    shift = SLOT_CHUNK.bit_length() - 1
    sel = jnp.where(lax.broadcasted_iota(I32, (LANES, kw), 0) == (lax.broadcasted_iota(I32, (LANES, kw), 1) >> shift),
                    1.0, 0.0).astype(BF16)

    def spread(v):
        low_bits = 5
        hi = (v >> low_bits).astype(F32).astype(BF16)
        lo = (v & ((1 << low_bits) - 1)).astype(F32).astype(BF16)
        return (float(1 << low_bits) * jnp.dot(hi, sel, preferred_element_type=F32)
                + jnp.dot(lo, sel, preferred_element_type=F32))

    within = (lax.broadcasted_iota(I32, (1, kw), 1) & (SLOT_CHUNK - 1)).astype(F32)
    lane = lax.broadcasted_iota(I32, (SUBLANES, LANES), 1)

    def base(u, e):
        return (bi * N_EXPERTS + e) * (ntile + 1) + pl.program_id(1) * tiles + u

    def windows(u, w):
        pos_w = spread(pos_ref[u * MOE_TILE:(u + 1) * MOE_TILE, :] + 1) - 1.0
        starts = jnp.zeros((SUBLANES, LANES), I32)
        los = jnp.zeros((SUBLANES, LANES), I32)
        for e in range(N_EXPERTS):
            lo, start = _window_start(cnt_ref[base(u, e)], w, cap)
            starts = jnp.where(lane == e, start, starts)
            los = jnp.where(lane == e, lo, los)
            ycat_scr[u, e * SLOT_CHUNK:(e + 1) * SLOT_CHUNK, :] = y_ref[e, pl.ds(start, SLOT_CHUNK), :]
        want = spread(starts)[0:1] + within
        onehot = jnp.where((pos_w == want) & (pos_w >= spread(los)[0:1]), 1.0, 0.0).astype(BF16)
        return jnp.dot(onehot, ycat_scr[u], preferred_element_type=F32)

    nwin = 1
    for u in range(tiles):
        acc_scr[u] = windows(u, 0)
        for e in range(N_EXPERTS):
            nwin = jnp.maximum(nwin, _num_windows(cnt_ref[base(u, e)], cnt_ref[base(u, e) + 1]))

    def overflow(w, _):
        for u in range(tiles):
            acc_scr[u] += windows(u, w)
        return 0

    lax.fori_loop(1, nwin, overflow, 0)
    for u in range(tiles):
        rows = slice(u * MOE_TILE, (u + 1) * MOE_TILE)
        out = x_ref[rows, :] + gt2_ref[...] * acc_scr[u]
        if final:
            out = out * lax.rsqrt(jnp.mean(out * out, axis=-1, keepdims=True) + EPS) * fg_ref[...]
        o_ref[rows, :] = out


def _moe_combine(cnt, pos, x, mods, layer, y, final_g, final, tiles_per_step=2):
    b, n, d = x.shape
    ntile = n // MOE_TILE
    cap = y.shape[2]
    rows = tiles_per_step * MOE_TILE
    grid_spec = pltpu.PrefetchScalarGridSpec(
        num_scalar_prefetch=1, grid=(b, ntile // tiles_per_step),
        in_specs=[pl.BlockSpec((None, rows, LANES), lambda bi, t, c: (bi, t, 0)),
                  pl.BlockSpec((None, rows, d), lambda bi, t, c: (bi, t, 0)),
                  _mod_spec(layer, GATE2),
                  pl.BlockSpec((None, N_EXPERTS, cap, d), lambda bi, t, c: (bi, 0, 0, 0),
                               pipeline_mode=pl.Buffered(1)),
                  pl.BlockSpec((1, d), lambda bi, t, c: (0, 0))],
        out_specs=pl.BlockSpec((None, rows, d), lambda bi, t, c: (bi, t, 0)),
        scratch_shapes=[pltpu.VMEM((tiles_per_step, MOE_TILE, d), F32),
                        pltpu.VMEM((tiles_per_step, N_EXPERTS * SLOT_CHUNK, d), BF16)])
    return pl.pallas_call(
        functools.partial(_moe_combine_kernel, ntile=ntile, cap=cap, final=final),
        out_shape=jax.ShapeDtypeStruct((b, n, d), F32),
        grid_spec=grid_spec, compiler_params=_cparams(("arbitrary", "arbitrary")), name="moe_combine",
    )(cnt, pos, x, mods, y, final_g)


def _ec_moe(x_mid, h2, aff, mods, wg, wu, wd, layer, final_g, final):
    b, n, _ = x_mid.shape
    cap = max(1, EC_FACTOR * n // N_EXPERTS)
    ntile = n // MOE_TILE
    pos, post, offs, afft = _route(aff, cap)
    per_tile = lambda a: a.transpose(0, 2, 1, 3).reshape(b, N_EXPERTS, ntile, 1, MOE_TILE)
    starts = offs[:, ::MOE_TILE // CHUNK, 0, :N_EXPERTS].transpose(0, 2, 1)
    cnt = jnp.concatenate([starts, jnp.full((b, N_EXPERTS, 1), cap, I32)], axis=2).reshape(-1)
    xe, gate = _moe_gather(cnt, per_tile(post), per_tile(afft), h2, cap)
    y = _moe_ffn(xe, gate, wg, wu, wd, layer)
    return _moe_combine(cnt, pos, x_mid, mods, layer, y, final_g, final)


def _rope_table(n):
    rows = n // GRID_W
    row = jnp.repeat(jnp.arange(rows), GRID_W).astype(F32)
    col = jnp.tile(jnp.arange(GRID_W), rows).astype(F32)
    nf = ATTN_DIM // 4
    inv = ROPE_BASE ** (-jnp.arange(nf, dtype=F32) / nf)
    ang = jnp.concatenate([row[:, None] * inv, col[:, None] * inv], axis=-1)
    cos, sin = jnp.cos(ang), jnp.sin(ang)
    reps = LANES // ATTN_DIM
    return jnp.concatenate([jnp.tile(jnp.concatenate([cos, cos], -1), (1, reps)),
                            jnp.tile(jnp.concatenate([-sin, sin], -1), (1, reps))], axis=-1)


def _pad_lanes(a):
    return jnp.pad(a, ((0, 0), (0, LANES - a.shape[1])))


def kernel(x, c, ctx, c_ctx, w_mod, b_mod, norm_mix_g, norm_ffn_g, final_norm_g, ab_w_in, ab_conv_w, ab_gate_b,
           ab_head_g, ab_sink, ab_w_out, gm_w_in, gm_ln_g, gm_ln_b, gm_w_s, gm_b_s, gm_w_out, moe_w_router,
           moe_w_gate, moe_w_up, moe_w_down):
    b, n, d = x.shape
    depth = w_mod.shape[0]
    assert depth <= 2, "context stream is only advanced for deeper stacks; not supported here"
    cond = jnp.zeros((BF16_ROWS, d), F32).at[:b].set(c).at[b].set(c_ctx)
    mods = _adaln(cond, w_mod, b_mod).reshape(depth, BF16_ROWS, 6, 1, d)

    row = lambda v: v.reshape(1, -1)
    for layer in range(depth):
        g1, g2 = row(norm_mix_g[layer]), row(norm_ffn_g[layer])
        wr = _hi_lo_rhs(_pad_lanes(moe_w_router[layer]))
        if layer % 2 == 0:
            e = layer // 2
            w_in = ab_w_in[e]
            g_lo = 4 * LSTM_WIDTH
            w_main = jnp.concatenate([w_in[:, :g_lo], w_in[:, g_lo + N_GATES:]], axis=1).astype(BF16)
            w_gate = _hi_lo_rhs(_pad_lanes(w_in[:, g_lo:g_lo + N_GATES]))
            p, gts, vt, avt = _modmm(x, g1, mods, layer, w_main, w_gate, tm=1024, name="ab_in")
            pc, gtc, vtc, avtc = _modmm(ctx, g1, mods, layer, w_main, w_gate, mod_row=b, name="ab_in_ctx")
            hf, hb = _mlstm(_conv_silu(p, ab_conv_w[e]), vt, gts, _conv_silu(pc, ab_conv_w[e]), vtc, gtc,
                            _pad_lanes(row(ab_gate_b[e])))
            at = _attn(p, avt, pc, avtc, _rope_table(n), _pad_lanes(row(ab_sink[e])))
            x_mid, h2, aff = _ab_out(hf, hb, p, at, x, row(ab_head_g[e]), ab_w_out[e].astype(BF16),
                                     mods, layer, g2, wr)
        else:
            o = layer // 2
            uv = _modmm(x, g1, mods, layer, gm_w_in[o].astype(BF16), act="gelu", tm=1024, name="gm_in")
            x_mid, h2, aff = _gm_out(uv, x, row(gm_ln_g[o]), row(gm_ln_b[o]), gm_w_s[o].astype(BF16),
                                     _pad_lanes(gm_b_s[o].T), gm_w_out[o].astype(BF16), mods, layer, g2, wr)
        x = _ec_moe(x_mid, h2, aff, mods, moe_w_gate, moe_w_up, moe_w_down, layer,
                    row(final_norm_g), layer == depth - 1)
    return x
```

```python
import functools
import math

import jax
import jax.numpy as jnp
from jax import lax
from jax.experimental import pallas as pl
from jax.experimental.pallas import tpu as pltpu

F32 = jnp.float32
BF16 = jnp.bfloat16
I32 = jnp.int32

D_MODEL = 1024
GRID_W = 64
EPS = 1e-6
LSTM_HEADS = 4
LSTM_DIM = 128
LSTM_WIDTH = LSTM_HEADS * LSTM_DIM
LSTM_CONV = 5
CHUNK = 128
ATTN_HEADS = 8
ATTN_KV_HEADS = 2
ATTN_GROUP = ATTN_HEADS // ATTN_KV_HEADS
ATTN_DIM = 64
ROPE_BASE = 10000.0
GM_GROUPS = 8
GM_HALF = 2 * D_MODEL
N_EXPERTS = 16
EC_FACTOR = 2
N_GATES = 4 * LSTM_HEADS

LANES = 128
SUBLANES = 8
BF16_ROWS = 16
V7X_VMEM_BYTES = 64 * 1024 * 1024
VMEM_COMPILER_RESERVE_BYTES = 8 * 1024 * 1024
VMEM_LIMIT_BYTES = V7X_VMEM_BYTES - VMEM_COMPILER_RESERVE_BYTES

P_COLS = 4 * LSTM_WIDTH + ATTN_HEADS * ATTN_DIM + 2 * ATTN_KV_HEADS * ATTN_DIM
PB_V, PB_O, PB_AQ = 2, 3, 4
PB_AK = (4 * LSTM_WIDTH + ATTN_HEADS * ATTN_DIM) // LANES
PB_AV = PB_AK + ATTN_KV_HEADS * ATTN_DIM // LANES
assert ATTN_HEADS * ATTN_DIM == LSTM_WIDTH and ATTN_KV_HEADS * ATTN_DIM == LANES
ATTN_HEADS_PER_DOT = 8
ATTN_STEP_BLOCKS = 8
TAIL_CHAIN_ROWS = 256
MOE_TILE = 256
SLOT_CHUNK = 64


def _cparams(sem, vmem=VMEM_LIMIT_BYTES):
    return pltpu.CompilerParams(dimension_semantics=sem, vmem_limit_bytes=vmem)


SHIFT1, SCALE1, GATE1, SHIFT2, SCALE2, GATE2 = range(6)


def _mod_spec(layer, comp, row=None):
    def index(bi, *_):
        return (layer, bi if row is None else row, comp, 0, 0)
    return pl.BlockSpec((None, None, None, 1, D_MODEL), index)


def _rms_mod(x, g, sc, sh):
    y = x * lax.rsqrt(jnp.mean(x * x, axis=-1, keepdims=True) + EPS)
    return y * g * (1.0 + sc) + sh


def _silu(x):
    return x * jax.nn.sigmoid(x)


def _gelu_tanh(x):
    return 0.5 * x * (1.0 + jnp.tanh((2.0 / math.pi) ** 0.5 * (x + 0.044715 * (x * x * x))))


def _log_sigmoid(x):
    return jnp.minimum(x, 0.0) - jnp.log(1.0 + jnp.exp(-jnp.abs(x)))


def _dot_t(a, b):
    return lax.dot_general(a, b, (((1,), (1,)), ((), ())), preferred_element_type=F32)


def _adaln_kernel(c_ref, w_ref, b_ref, o_ref):
    s = _silu(c_ref[...])
    s_hi = s.astype(BF16)
    s_lo = (s - s_hi.astype(F32)).astype(BF16)
    w = w_ref[...]
    w_hi = w.astype(BF16)
    w_lo = (w - w_hi.astype(F32)).astype(BF16)
    both = jnp.dot(jnp.concatenate([s_hi, s_lo], axis=0), w_hi, preferred_element_type=F32)
    rows = s.shape[0]
    o_ref[...] = both[:rows] + both[rows:] + jnp.dot(s_hi, w_lo, preferred_element_type=F32) + b_ref[...]


def _adaln(cond, w_mod, b_mod):
    depth, d, six_d = w_mod.shape
    rows = cond.shape[0]
    tn = six_d // 4
    return pl.pallas_call(
        _adaln_kernel,
        out_shape=jax.ShapeDtypeStruct((depth, rows, six_d), F32),
        grid=(depth, six_d // tn),
        in_specs=[pl.BlockSpec((rows, d), lambda l, j: (0, 0)),
                  pl.BlockSpec((None, d, tn), lambda l, j: (l, 0, j)),
                  pl.BlockSpec((None, 1, tn), lambda l, j: (l, 0, j))],
        out_specs=pl.BlockSpec((None, rows, tn), lambda l, j: (l, 0, j)),
        compiler_params=_cparams(("arbitrary", "arbitrary")),
        name="adaln",
    )(cond, w_mod, b_mod.reshape(depth, 1, six_d))


def _hi_lo_rhs(w):
    w_hi = w.astype(BF16)
    return jnp.concatenate([w_hi, (w - w_hi.astype(F32)).astype(BF16)], axis=1)


def _hi_lo_dot(h, h_hi, w2):
    n = w2.shape[1] // 2
    both = jnp.dot(h_hi, w2, preferred_element_type=F32)
    h_lo = (h - h_hi.astype(F32)).astype(BF16)
    return both[:, :n] + both[:, n:] + jnp.dot(h_lo, w2[:, :n], preferred_element_type=F32)


def _modmm_kernel(x_ref, g_ref, sc_ref, sh_ref, w_ref, *rest, chunks, act, with_gates):
    if with_gates:
        wg_ref, o_ref, og_ref, vt_ref, avt_ref = rest
    else:
        (o_ref,) = rest
    h = _rms_mod(x_ref[...], g_ref[...], sc_ref[...], sh_ref[...])
    hb = h.astype(BF16)
    for lo, hi in chunks:
        y = jnp.dot(hb, w_ref[:, lo:hi], preferred_element_type=F32)
        if act == "gelu":
            y = _gelu_tanh(y)
        o_ref[:, lo:hi] = y.astype(o_ref.dtype)
        if with_gates and lo == PB_V * LSTM_WIDTH:
            vt_ref[...] = y.T.astype(vt_ref.dtype)
        if with_gates and lo <= PB_AV * LANES < hi:
            av = y[:, PB_AV * LANES - lo:(PB_AV + 1) * LANES - lo]
            avt_ref[...] = av.T.astype(avt_ref.dtype)
    if with_gates:
        og_ref[...] = _hi_lo_dot(h, hb, wg_ref[...])


def _modmm(x, g, mods, layer, w, wg=None, *, mod_row=None, act=None, tm=512, chunk=512, name="modmm"):
    b, n, d = x.shape
    no = w.shape[1]
    tm = min(tm, n)
    assert chunk == LSTM_WIDTH
    chunks = tuple((lo, min(lo + chunk, no)) for lo in range(0, no, chunk))
    in_specs = [pl.BlockSpec((None, tm, d), lambda bi, i: (bi, i, 0)),
                pl.BlockSpec((1, d), lambda bi, i: (0, 0)),
                _mod_spec(layer, SCALE1, mod_row), _mod_spec(layer, SHIFT1, mod_row),
                pl.BlockSpec((d, no), lambda bi, i: (0, 0))]
    out_shape = [jax.ShapeDtypeStruct((b, n, no), BF16)]
    out_specs = [pl.BlockSpec((None, tm, no), lambda bi, i: (bi, i, 0))]
    args = [x, g, mods, mods, w]
    if wg is not None:
        in_specs.append(pl.BlockSpec(wg.shape, lambda bi, i: (0, 0)))
        out_shape += [jax.ShapeDtypeStruct((b, n, LANES), F32), jax.ShapeDtypeStruct((b, LSTM_WIDTH, n), BF16),
                      jax.ShapeDtypeStruct((b, LANES, n), BF16)]
        out_specs += [pl.BlockSpec((None, tm, LANES), lambda bi, i: (bi, i, 0)),
                      pl.BlockSpec((None, LSTM_WIDTH, tm), lambda bi, i: (bi, 0, i)),
                      pl.BlockSpec((None, LANES, tm), lambda bi, i: (bi, 0, i))]
        args.append(wg)
    res = pl.pallas_call(
        functools.partial(_modmm_kernel, chunks=chunks, act=act, with_gates=wg is not None),
        out_shape=out_shape, grid=(b, n // tm), in_specs=in_specs, out_specs=out_specs,
        compiler_params=_cparams(("parallel", "parallel")), name=name,
    )(*args)
    return res if wg is not None else res[0]


def _conv_silu_kernel(x_ref, xp_ref, xn_ref, w_ref, o_ref, pad_scr):
    i = pl.program_id(1)
    rows = x_ref.shape[0]
    halo = BF16_ROWS
    has_prev = jnp.where(i > 0, 1.0, 0.0)
    has_next = jnp.where(i < pl.num_programs(1) - 1, 1.0, 0.0)
    pad_scr[pl.ds(0, halo), :] = (xp_ref[...].astype(F32) * has_prev).astype(BF16)
    pad_scr[pl.ds(halo, rows), :] = x_ref[...]
    pad_scr[pl.ds(halo + rows, halo), :] = (xn_ref[...].astype(F32) * has_next).astype(BF16)
    w = w_ref[...]
    win = CHUNK + 2 * halo
    r = lax.broadcasted_iota(I32, (CHUNK, win), 0)
    c = lax.broadcasted_iota(I32, (CHUNK, win), 1)
    half = LSTM_CONV // 2
    shifts = {t: jnp.where(c == r + halo + t - half, 1.0, 0.0).astype(BF16) for t in range(LSTM_CONV) if t != half}
    scale = LSTM_DIM ** -0.5
    for blk in range(rows // CHUNK):
        xw = pad_scr[pl.ds(blk * CHUNK, win), :]
        acc = xw[halo:halo + CHUNK].astype(F32) * w[half:half + 1, :]
        for t, s in shifts.items():
            acc = acc + jnp.dot(s, xw, preferred_element_type=F32) * w[t:t + 1, :]
        y = _silu(acc)
        out = pl.ds(blk * CHUNK, CHUNK)
        o_ref[out, :LSTM_WIDTH] = y[:, :LSTM_WIDTH].astype(o_ref.dtype)
        o_ref[out, LSTM_WIDTH:] = (y[:, LSTM_WIDTH:] * scale).astype(o_ref.dtype)


def _conv_silu(p, conv_w, tm=1024):
    b, n, _ = p.shape
    tm = min(tm, n)
    qkw = 2 * LSTM_WIDTH
    hpt = tm // BF16_ROWS
    nhb = n // BF16_ROWS
    return pl.pallas_call(
        _conv_silu_kernel,
        out_shape=jax.ShapeDtypeStruct((b, n, qkw), BF16),
        grid=(b, n // tm),
        in_specs=[pl.BlockSpec((None, tm, qkw), lambda bi, i: (bi, i, 0)),
                  pl.BlockSpec((None, BF16_ROWS, qkw), lambda bi, i: (bi, jnp.maximum(i * hpt - 1, 0), 0)),
                  pl.BlockSpec((None, BF16_ROWS, qkw), lambda bi, i: (bi, jnp.minimum((i + 1) * hpt, nhb - 1), 0)),
                  pl.BlockSpec((LSTM_CONV, qkw), lambda bi, i: (0, 0))],
        out_specs=pl.BlockSpec((None, tm, qkw), lambda bi, i: (bi, i, 0)),
        scratch_shapes=[pltpu.VMEM((tm + 2 * BF16_ROWS, qkw), BF16)],
        compiler_params=_cparams(("parallel", "parallel")), name="conv_silu",
    )(p, p, p, conv_w)


def _tri(n, lower):
    r = lax.broadcasted_iota(I32, (n, n), 0)
    c = lax.broadcasted_iota(I32, (n, n), 1)
    return (c <= r) if lower else (c >= r)


STATE_ROWS = LSTM_DIM + BF16_ROWS
MLSTM_STEP_CHUNKS = 8


def _mlstm_segment(d, gates, k_all, vt_all, q_all, state, ht_ref, cols):
    first = state is None
    nh = LSTM_HEADS
    seg = gates.shape[0]

    def side_by_side(pieces):
        return jnp.concatenate(pieces, axis=1)

    def block_diag(x):
        w = x.shape[1] // nh
        lane = lax.broadcasted_iota(I32, x.shape, 1)
        return jnp.concatenate([jnp.where((lane >= h * w) & (lane < (h + 1) * w), x, jnp.zeros_like(x))
                                for h in range(nh)], axis=0)

    r = lax.broadcasted_iota(I32, (seg, seg), 0)
    c = lax.broadcasted_iota(I32, (seg, seg), 1)
    before = (r <= c) if d == 0 else (r >= c)
    gates_t = gates.T[:N_GATES]
    ls = _log_sigmoid(gates_t)
    ls_hi = ls.astype(BF16)
    rest = ls - ls_hi.astype(F32)
    ls_mid = rest.astype(BF16)
    ls_lo = (rest - ls_mid.astype(F32)).astype(BF16)
    parts = jnp.dot(jnp.concatenate([ls_hi, ls_mid, ls_lo], axis=0), jnp.where(before, 1.0, 0.0).astype(BF16),
                    preferred_element_type=F32)
    bcum_t = parts[:N_GATES] + parts[N_GATES:2 * N_GATES] + parts[2 * N_GATES:]
    last = seg - 1 if d == 0 else 0
    ci = [2 * d * nh + h for h in range(nh)]
    cf = [(2 * d + 1) * nh + h for h in range(nh)]
    b_rows = side_by_side([bcum_t[c:c + 1, :] for c in cf])
    li_rows = side_by_side([gates_t[c:c + 1, :] for c in ci])
    g_heads = [bcum_t[c:c + 1, last:last + 1] for c in cf]
    cn, m_heads = (None, [jnp.zeros((1, 1), F32)] * nh) if first else state
    over_seg = lambda xs: side_by_side([jnp.broadcast_to(x, (1, seg)) for x in xs])
    m_rows, g_rows = over_seg(m_heads), over_seg(g_heads)
    vt_cat = side_by_side([vt_all[h * LSTM_DIM:(h + 1) * LSTM_DIM, :] for h in range(nh)])
    if q_all is not None:
        q_blk = block_diag(q_all)
        per_key = side_by_side([jnp.broadcast_to(gates_t[i:i + 1, :] - bcum_t[f:f + 1, :], (seg, seg)).T
                                for i, f in zip(ci, cf)])
        log_d = jnp.where(side_by_side([before] * nh), b_rows + per_key, -jnp.inf)
        m_row = jnp.maximum(b_rows + m_rows, jnp.max(log_d, axis=0, keepdims=True))
        sm = _dot_t(k_all, q_blk) * jnp.exp(log_d - m_row)
        a = jnp.exp(b_rows + m_rows - m_row)
        qc = _dot_t(cn.astype(BF16), q_blk)
        num = jnp.dot(vt_cat, block_diag(sm.astype(BF16)), preferred_element_type=F32) + a * qc[:LSTM_DIM]
        den = jnp.sum(sm, axis=0, keepdims=True) + a * qc[LSTM_DIM:LSTM_DIM + 1]
        out = num / jnp.maximum(jnp.abs(den), jnp.exp(-m_row))
        for h in range(nh):
            ht_ref[h * LSTM_DIM:(h + 1) * LSTM_DIM, cols] = out[:, h * seg:(h + 1) * seg]
    w = g_rows - b_rows + li_rows
    m_new = [jnp.maximum(g_heads[h] + m_heads[h], jnp.max(w[:, h * seg:(h + 1) * seg], axis=1, keepdims=True))
             for h in range(nh)]
    wt = jnp.exp(w - over_seg(m_new))
    aug = jnp.concatenate([(vt_cat.astype(F32) * wt).astype(BF16),
                           jnp.broadcast_to(wt, (BF16_ROWS, nh * seg)).astype(BF16)], axis=0)
    upd = jnp.dot(aug, block_diag(k_all), preferred_element_type=F32)
    if not first:
        decay = side_by_side([jnp.broadcast_to(jnp.exp(g_heads[h] + m_heads[h] - m_new[h]), (1, LSTM_DIM))
                              for h in range(nh)])
        upd = decay * cn + upd
    return upd, m_new


def _mlstm_kernel(qkf_ref, qkb_ref, vtf_ref, vtb_ref, gf_ref, gb_ref, kc_ref, vtc_ref, gc_ref, gbias_ref,
                  hf_ref, hb_ref, cn_scr, m_scr):
    gbias = gbias_ref[...]

    def save(d, state):
        cn, m_heads = state
        cn_scr[d] = cn
        for h, m in enumerate(m_heads):
            m_scr[d * LSTM_HEADS + h] = jnp.broadcast_to(m, (1, LANES))

    @pl.when(pl.program_id(1) == 0)
    def _():
        gates = gc_ref[...] + gbias
        for d in range(2):
            save(d, _mlstm_segment(d, gates, kc_ref[...], vtc_ref[...], None, None, None, None))

    states = [(cn_scr[d], [m_scr[d * LSTM_HEADS + h][:, 0:1] for h in range(LSTM_HEADS)]) for d in range(2)]
    per_step = qkf_ref.shape[0] // CHUNK
    for u in range(per_step):
        for d, (qk_ref, vt_ref, g_ref, h_ref) in enumerate(((qkf_ref, vtf_ref, gf_ref, hf_ref),
                                                            (qkb_ref, vtb_ref, gb_ref, hb_ref))):
            j = u if d == 0 else per_step - 1 - u
            rows = slice(j * CHUNK, (j + 1) * CHUNK)
            qk = qk_ref[rows, :]
            states[d] = _mlstm_segment(d, g_ref[rows, :] + gbias, qk[:, LSTM_WIDTH:], vt_ref[:, rows],
                                       qk[:, :LSTM_WIDTH], states[d], h_ref, rows)
    for d in range(2):
        save(d, states[d])


def _mlstm(qk, vt, g, qkc, vtc, gc, gate_bias):
    b, n, _ = qk.shape
    lc = qkc.shape[1]
    rows = MLSTM_STEP_CHUNKS * CHUNK
    nc = n // rows
    qkw = 2 * LSTM_WIDTH
    in_specs = [
        pl.BlockSpec((None, rows, qkw), lambda bi, c: (bi, c, 0)),
        pl.BlockSpec((None, rows, qkw), lambda bi, c: (bi, nc - 1 - c, 0)),
        pl.BlockSpec((None, LSTM_WIDTH, rows), lambda bi, c: (bi, 0, c)),
        pl.BlockSpec((None, LSTM_WIDTH, rows), lambda bi, c: (bi, 0, nc - 1 - c)),
        pl.BlockSpec((None, rows, LANES), lambda bi, c: (bi, c, 0)),
        pl.BlockSpec((None, rows, LANES), lambda bi, c: (bi, nc - 1 - c, 0)),
        pl.BlockSpec((None, lc, LSTM_WIDTH), lambda bi, c: (bi, 0, 1)),
        pl.BlockSpec((None, LSTM_WIDTH, lc), lambda bi, c: (bi, 0, 0)),
        pl.BlockSpec((None, lc, LANES), lambda bi, c: (bi, 0, 0)),
        pl.BlockSpec((1, LANES), lambda bi, c: (0, 0))]
    out_specs = [pl.BlockSpec((None, LSTM_WIDTH, rows), lambda bi, c: (bi, 0, c)),
                 pl.BlockSpec((None, LSTM_WIDTH, rows), lambda bi, c: (bi, 0, nc - 1 - c))]
    return pl.pallas_call(
        _mlstm_kernel,
        out_shape=[jax.ShapeDtypeStruct((b, LSTM_WIDTH, n), F32)] * 2,
        grid=(b, nc), in_specs=in_specs, out_specs=out_specs,
        scratch_shapes=[pltpu.VMEM((2, STATE_ROWS, LSTM_WIDTH), F32),
                        pltpu.VMEM((2 * LSTM_HEADS, 1, LANES), F32)],
        compiler_params=_cparams(("arbitrary", "arbitrary")), name="mlstm",
    )(qk, qk, vt, vt, g, g, qkc, vtc, gc, gate_bias)


def _rope(x, cos, sin_signed):
    w = x.shape[1]
    lane = lax.broadcasted_iota(I32, x.shape, 1)
    first = (lane & (ATTN_DIM - 1)) < (ATTN_DIM // 2)
    partner = jnp.where(first, pltpu.roll(x, w - ATTN_DIM // 2, 1), pltpu.roll(x, ATTN_DIM // 2, 1))
    return x * cos + partner * sin_signed


def _attn_kernel(q_ref, *refs):
    nblk = q_ref.shape[0] // CHUNK
    nwin = nblk + 2
    k_refs, vt_refs = refs[:nwin], refs[nwin:2 * nwin]
    kctx_ref, vtctx_ref = refs[2 * nwin:2 * nwin + 2]
    t_refs = refs[2 * nwin + 2:3 * nwin + 2]
    sink_ref, bias_ref, o_ref = refs[3 * nwin + 2:]
    step, last_step = pl.program_id(1), pl.num_programs(1) - 1

    def table(t_ref):
        t = t_ref[...]
        return t[:, :LANES], t[:, LANES:]

    ks = []
    for k_ref, t_ref in zip(k_refs, t_refs):
        cos_t, sin_t = table(t_ref)
        ks.append(_rope(k_ref[...].astype(F32), cos_t, sin_t).astype(BF16))
    for u in range(nblk):
        which = jnp.where(step == 0, 0, 1) if u == 0 else 1
        if u == nblk - 1:
            which = jnp.where(step == last_step, 2, which)
        _attn_block(q_ref[u * CHUNK:(u + 1) * CHUNK, :], table(t_refs[u + 1]),
                    jnp.concatenate(ks[u:u + 3] + [kctx_ref[...]], axis=0),
                    jnp.concatenate([r[...] for r in vt_refs[u:u + 3]] + [vtctx_ref[...]], axis=1),
                    sink_ref[...], bias_ref[which], o_ref.at[:, u * CHUNK:(u + 1) * CHUNK])


def _attn_block(q_in, q_table, k_all, vt_all, sink, bias1, o_ref):
    cos_c, sin_c = q_table
    q = _rope(q_in.astype(F32), jnp.concatenate([cos_c] * 4, axis=1), jnp.concatenate([sin_c] * 4, axis=1))
    q = q * (ATTN_DIM ** -0.5)
    per = ATTN_HEADS_PER_DOT
    bias = jnp.concatenate([bias1] * per, axis=1)
    lane = lax.broadcasted_iota(I32, (CHUNK, LANES), 1)

    for h0 in range(0, ATTN_HEADS, per):
        qs, snk = [], []
        for h in range(h0, h0 + per):
            g = h // ATTN_GROUP
            t = q[:, (h // 2) * LANES:(h // 2 + 1) * LANES]
            if h % 2 != g:
                t = pltpu.roll(t, ATTN_DIM, 1)
            half_g = (lane >= ATTN_DIM) if g == 1 else (lane < ATTN_DIM)
            qs.append(jnp.where(half_g, t, 0.0).astype(BF16))
            snk.append(jnp.broadcast_to(sink[:, h:h + 1], (1, CHUNK)))
        snk = jnp.concatenate(snk, axis=1)
        st = _dot_t(k_all, jnp.concatenate(qs, axis=0)) + bias
        m = jnp.maximum(jnp.max(st, axis=0, keepdims=True), snk)
        e = jnp.exp(st - m)
        den = jnp.sum(e, axis=0, keepdims=True) + jnp.exp(snk - m)
        pv = jnp.dot(vt_all, e.astype(BF16), preferred_element_type=F32)
        o = (pv / den).astype(o_ref.dtype)
        for r, h in enumerate(range(h0, h0 + per)):
            g = h // ATTN_GROUP
            o_ref[h * ATTN_DIM:(h + 1) * ATTN_DIM, :] = o[g * ATTN_DIM:(g + 1) * ATTN_DIM, r * CHUNK:(r + 1) * CHUNK]


def _attn_bias(nctx):
    i = jnp.arange(CHUNK)[None, :]
    j = jnp.arange(3 * CHUNK)[:, None]
    band = (j >= i) & (j <= i + 2 * CHUNK)
    local = jnp.stack([band & (j >= CHUNK), band, band & (j < 2 * CHUNK)])
    return jnp.concatenate([jnp.where(local, 0.0, -jnp.inf).astype(F32), jnp.zeros((3, nctx, CHUNK), F32)], axis=1)


def _attn(p, avt, pc, avtc, table, sink):
    b, n, _ = p.shape
    lc = pc.shape[1]
    nb = n // CHUNK
    assert nb >= 2
    qw = ATTN_HEADS * ATTN_DIM
    bias = _attn_bias(lc)

    s = ATTN_STEP_BLOCKS
    offs = range(-1, s + 1)

    def blk(col, off):
        return pl.BlockSpec((None, CHUNK, LANES), lambda bi, i: (bi, jnp.clip(i * s + off, 0, nb - 1), col))

    def vblk(off):
        return pl.BlockSpec((None, LANES, CHUNK), lambda bi, i: (bi, 0, jnp.clip(i * s + off, 0, nb - 1)))

    def tab(off):
        return pl.BlockSpec((CHUNK, 2 * LANES), lambda bi, i: (jnp.clip(i * s + off, 0, nb - 1), 0))

    in_specs = ([pl.BlockSpec((None, s * CHUNK, qw), lambda bi, i: (bi, i, PB_AQ))]
                + [blk(PB_AK, o) for o in offs] + [vblk(o) for o in offs]
                + [pl.BlockSpec((None, lc, LANES), lambda bi, i: (bi, 0, PB_AK)),
                   pl.BlockSpec((None, LANES, lc), lambda bi, i: (bi, 0, 0))]
                + [tab(o) for o in offs]
                + [pl.BlockSpec((1, LANES), lambda bi, i: (0, 0)),
                   pl.BlockSpec(bias.shape, lambda bi, i: (0, 0, 0))])
    nw = len(offs)
    return pl.pallas_call(
        _attn_kernel,
        out_shape=jax.ShapeDtypeStruct((b, qw, n), BF16),
        grid=(b, nb // s), in_specs=in_specs,
        out_specs=pl.BlockSpec((None, qw, s * CHUNK), lambda bi, i: (bi, 0, i)),
        compiler_params=_cparams(("parallel", "parallel")), name="window_attn",
    )(*([p] * (1 + nw) + [avt] * nw + [pc, avtc] + [table] * nw + [sink, bias]))


def _router_tail(x_new, g2, sc2, sh2, wr_ref, x_out_ref, h2_ref, aff_ref):
    x_out_ref[...] = x_new
    h2 = _rms_mod(x_new, g2, sc2, sh2)
    h_hi = h2.astype(BF16)
    h2_ref[...] = h_hi
    logits = _hi_lo_dot(h2, h_hi, wr_ref[...])
    lane = lax.broadcasted_iota(I32, logits.shape, 1)
    logits = jnp.where(lane < N_EXPERTS, logits, -jnp.inf)
    e = jnp.exp(logits - jnp.max(logits, axis=1, keepdims=True))
    aff_ref[...] = e / jnp.sum(e, axis=1, keepdims=True)


def _ab_out_kernel(hf_ref, hb_ref, o_ref, at_ref, x_ref, *refs):
    for lo in range(0, x_ref.shape[0], TAIL_CHAIN_ROWS):
        rows = slice(lo, lo + TAIL_CHAIN_ROWS)
        _ab_out_rows(hf_ref.at[:, rows], hb_ref.at[:, rows], o_ref.at[rows], at_ref.at[:, rows], x_ref.at[rows],
                     *refs[:-3], *[out.at[rows] for out in refs[-3:]])


def _ab_out_rows(hf_ref, hb_ref, o_ref, at_ref, x_ref, hg_ref, wo_ref, gt1_ref, g2_ref, sc2_ref, sh2_ref, wr_ref,
                 x_out_ref, h2_ref, aff_ref):
    hsum = (hf_ref[...] + hb_ref[...]).T
    og = jax.nn.sigmoid(o_ref[...].astype(F32))
    hg = hg_ref[...]
    parts = []
    for h in range(LSTM_HEADS):
        sl = slice(h * LSTM_DIM, (h + 1) * LSTM_DIM)
        seg = hsum[:, sl]
        seg = seg * lax.rsqrt(jnp.mean(seg * seg, axis=-1, keepdims=True) + EPS)
        parts.append((seg * hg[:, sl] * og[:, sl]).astype(BF16))
    cat = jnp.concatenate(parts + [at_ref[...].astype(F32).T.astype(BF16)], axis=1)
    y = jnp.dot(cat, wo_ref[...], preferred_element_type=F32)
    _router_tail(x_ref[...] + gt1_ref[...] * y, g2_ref[...], sc2_ref[...], sh2_ref[...], wr_ref,
                 x_out_ref, h2_ref, aff_ref)


def _tail_out(b, n, d, tm):
    shapes = [jax.ShapeDtypeStruct((b, n, d), F32), jax.ShapeDtypeStruct((b, n, d), BF16),
              jax.ShapeDtypeStruct((b, n, LANES), F32)]
    specs = [pl.BlockSpec((None, tm, d), lambda bi, i: (bi, i, 0)),
             pl.BlockSpec((None, tm, d), lambda bi, i: (bi, i, 0)),
             pl.BlockSpec((None, tm, LANES), lambda bi, i: (bi, i, 0))]
    return shapes, specs


def _ab_out(hf, hb, p, at, x, head_g, w_out, mods, layer, g2, wr, tm=1024):
    b, n, d = x.shape
    row = lambda w: pl.BlockSpec((None, tm, w), lambda bi, i: (bi, i, 0))
    const = lambda s: pl.BlockSpec(s, lambda bi, i: (0, 0))
    scan_out = pl.BlockSpec((None, LSTM_WIDTH, tm), lambda bi, i: (bi, 0, i))
    in_specs = [scan_out, scan_out,
                pl.BlockSpec((None, tm, LSTM_WIDTH), lambda bi, i: (bi, i, PB_O)),
                pl.BlockSpec((None, ATTN_HEADS * ATTN_DIM, tm), lambda bi, i: (bi, 0, i)),
                row(d), const((1, LSTM_WIDTH)), const(w_out.shape),
                _mod_spec(layer, GATE1), const((1, d)), _mod_spec(layer, SCALE2), _mod_spec(layer, SHIFT2),
                const((d, 2 * LANES))]
    shapes, specs = _tail_out(b, n, d, tm)
    return pl.pallas_call(
        _ab_out_kernel, out_shape=shapes, grid=(b, n // tm), in_specs=in_specs, out_specs=specs,
        compiler_params=_cparams(("parallel", "parallel")), name="ab_out",
    )(hf, hb, p, at, x, head_g, w_out, mods, g2, mods, mods, wr)


def _gm_out_kernel(uv_ref, x_ref, *refs):
    for lo in range(0, x_ref.shape[0], TAIL_CHAIN_ROWS):
        rows = slice(lo, lo + TAIL_CHAIN_ROWS)
        _gm_out_rows(uv_ref.at[rows], x_ref.at[rows], *refs[:-3], *[out.at[rows] for out in refs[-3:]])


def _gm_out_rows(uv_ref, x_ref, lng_ref, lnb_ref, ws_ref, bs_ref, wo_ref, gt1_ref, g2_ref, sc2_ref, sh2_ref, wr_ref,
                 x_out_ref, h2_ref, aff_ref):
    tm = uv_ref.shape[0]
    gw = GM_HALF // GM_GROUPS
    v = uv_ref[:, GM_HALF:].astype(F32)
    mu = jnp.mean(v, axis=-1, keepdims=True)
    vc = v - mu
    var = jnp.mean(vc * vc, axis=-1, keepdims=True)
    vn = (vc * lax.rsqrt(var + EPS) * lng_ref[...] + lnb_ref[...]).astype(BF16)
    zs = []
    for ch in range(tm // CHUNK):
        rows = slice(ch * CHUNK, (ch + 1) * CHUNK)
        cols = []
        for g in range(GM_GROUPS):
            sv = jnp.dot(ws_ref[g], vn[rows, g * gw:(g + 1) * gw], preferred_element_type=F32)
            cols.append(sv + bs_ref[:, g:g + 1])
        sv = jnp.concatenate(cols, axis=1)
        zs.append((uv_ref[rows, :GM_HALF].astype(F32) * sv).astype(BF16))
    z = jnp.concatenate(zs, axis=0)
    y = jnp.dot(z, wo_ref[...], preferred_element_type=F32)
    _router_tail(x_ref[...] + gt1_ref[...] * y, g2_ref[...], sc2_ref[...], sh2_ref[...], wr_ref,
                 x_out_ref, h2_ref, aff_ref)


def _gm_out(uv, x, ln_g, ln_b, w_s, b_s_t, w_out, mods, layer, g2, wr, tm=1024):
    b, n, d = x.shape
    const = lambda s: pl.BlockSpec(s, lambda *_: (0,) * len(s))
    in_specs = [pl.BlockSpec((None, tm, 2 * GM_HALF), lambda bi, i: (bi, i, 0)),
                pl.BlockSpec((None, tm, d), lambda bi, i: (bi, i, 0)),
                const((1, GM_HALF)), const((1, GM_HALF)), const(w_s.shape), const(b_s_t.shape), const(w_out.shape),
                _mod_spec(layer, GATE1), const((1, d)), _mod_spec(layer, SCALE2), _mod_spec(layer, SHIFT2),
                const((d, 2 * LANES))]
    shapes, specs = _tail_out(b, n, d, tm)
    return pl.pallas_call(
        _gm_out_kernel, out_shape=shapes, grid=(b, n // tm), in_specs=in_specs, out_specs=specs,
        compiler_params=_cparams(("parallel", "parallel")), name="gm_out",
    )(uv, x, ln_g, ln_b, w_s, b_s_t, w_out, mods, g2, mods, mods, wr)


def _route_kernel(aff_ref, pos_ref, post_ref, offs_ref, afft_ref, *, cap):
    n = aff_ref.shape[0]
    nblk = n // CHUNK

    def to_expert_major(k, _):
        rows = pl.ds(pl.multiple_of(k * CHUNK, CHUNK), CHUNK)
        afft_ref[k] = aff_ref[rows, :].T[:N_EXPERTS, :]
        return 0

    lax.fori_loop(0, nblk, to_expert_major, 0, unroll=4)

    def count(pred):
        per_lane = jnp.sum(jnp.where(pred, 1.0, 0.0), axis=0)
        return jnp.sum(per_lane, axis=1, keepdims=True)

    def search(i, prefix):
        cand = prefix | jnp.left_shift(jnp.int32(1), 30 - i)
        cand_f = lax.bitcast_convert_type(cand, F32)
        return jnp.where(count(afft_ref[...] >= cand_f[None]) >= cap, cand, prefix)

    thr_col = lax.bitcast_convert_type(lax.fori_loop(0, 31, search, jnp.zeros((N_EXPERTS, 1), I32)), F32)
    need_col = cap - count(afft_ref[...] > thr_col[None])

    def to_row(col):
        full = jnp.concatenate([jnp.broadcast_to(col, (N_EXPERTS, LANES)),
                                jnp.zeros((LANES - N_EXPERTS, LANES), F32)], axis=0)
        return full.T[0:1, :]

    thr, need = to_row(thr_col), to_row(need_col)
    tril = jnp.where(_tri(CHUNK, True), 1.0, 0.0).astype(BF16)

    def block(k, carry):
        run_tie, run_sel = carry
        rows = pl.ds(pl.multiple_of(k * CHUNK, CHUNK), CHUNK)
        a = aff_ref[rows, :]
        gt = a > thr
        tie = jnp.where(a == thr, 1.0, 0.0)
        tie_incl = jnp.dot(tril, tie.astype(BF16), preferred_element_type=F32)
        sel = jnp.where(gt | ((tie > 0.0) & (tie_incl - tie + run_tie < need)), 1.0, 0.0)
        sel_incl = jnp.dot(tril, sel.astype(BF16), preferred_element_type=F32)
        pos = jnp.where(sel > 0.0, sel_incl - sel + run_sel, -1.0)
        pos_ref[rows, :] = pos.astype(I32)
        post_ref[k] = pos.T[:N_EXPERTS, :].astype(I32)
        offs_ref[k] = run_sel.astype(I32)
        return (run_tie + tie_incl[CHUNK - 1:CHUNK, :], run_sel + sel_incl[CHUNK - 1:CHUNK, :])

    zero = jnp.zeros((1, LANES), F32)
    lax.fori_loop(0, nblk, block, (zero, zero), unroll=4)


def _route(aff, cap):
    b, n, _ = aff.shape
    nblk = n // CHUNK
    return pl.pallas_call(
        functools.partial(_route_kernel, cap=cap),
        out_shape=[jax.ShapeDtypeStruct((b, n, LANES), I32),
                   jax.ShapeDtypeStruct((b, nblk, N_EXPERTS, CHUNK), I32),
                   jax.ShapeDtypeStruct((b, nblk, 1, LANES), I32),
                   jax.ShapeDtypeStruct((b, nblk, N_EXPERTS, CHUNK), F32)],
        grid=(b,),
        in_specs=[pl.BlockSpec((None, n, LANES), lambda bi: (bi, 0, 0))],
        out_specs=[pl.BlockSpec((None, n, LANES), lambda bi: (bi, 0, 0)),
                   pl.BlockSpec((None, nblk, N_EXPERTS, CHUNK), lambda bi: (bi, 0, 0, 0)),
                   pl.BlockSpec((None, nblk, 1, LANES), lambda bi: (bi, 0, 0, 0)),
                   pl.BlockSpec((None, nblk, N_EXPERTS, CHUNK), lambda bi: (bi, 0, 0, 0))],
        compiler_params=_cparams(("parallel",)), name="route",
    )(aff)


def _align_rows(s):
    shift = BF16_ROWS.bit_length() - 1
    return (s >> shift) << shift


def _window_start(s0, w, cap):
    lo = _align_rows(s0) + w * SLOT_CHUNK
    return lo, pl.multiple_of(jnp.minimum(lo, cap - SLOT_CHUNK), BF16_ROWS)


def _num_windows(s0, s1):
    return (s1 - _align_rows(s0) + SLOT_CHUNK - 1) >> (SLOT_CHUNK.bit_length() - 1)


def _moe_gather_kernel(cnt_ref, post_ref, afft_ref, h_ref, xe_ref, gate_ref, *, ntile, cap, group, tiles_per_step):
    bi, eg, ts = pl.program_id(0), pl.program_id(1), pl.program_id(2)

    @pl.when(ts == 0)
    def _():
        xe_ref[...] = jnp.zeros_like(xe_ref)
        gate_ref[...] = jnp.zeros_like(gate_ref)

    slot = lax.broadcasted_iota(I32, (SLOT_CHUNK, MOE_TILE), 0)

    def base(u, g):
        return (bi * N_EXPERTS + eg * group + g) * (ntile + 1) + ts * tiles_per_step + u

    def windows(u, w):
        htile = h_ref[u * MOE_TILE:(u + 1) * MOE_TILE, :]
        starts, onehots = [], []
        for g in range(group):
            lo, start = _window_start(cnt_ref[base(u, g)], w, cap)
            posrow = post_ref[g, u]
            hit = (posrow - start == slot) & (posrow >= lo)
            onehots.append(jnp.where(hit, 1.0, 0.0).astype(BF16))
            gates = jnp.sum(jnp.where(hit, afft_ref[g, u], 0.0), axis=1, keepdims=True)
            dst = pl.ds(start, SLOT_CHUNK)
            gate_ref[g, dst, :] = gate_ref[g, dst, :] + jnp.broadcast_to(gates, (SLOT_CHUNK, LANES))
            starts.append(start)
        rows = jnp.dot(jnp.concatenate(onehots, axis=0), htile, preferred_element_type=F32)
        for g in range(group):
            dst = pl.ds(starts[g], SLOT_CHUNK)
            xe_ref[g, dst, :] = xe_ref[g, dst, :] + rows[g * SLOT_CHUNK:(g + 1) * SLOT_CHUNK].astype(BF16)

    nwin = 1
    for u in range(tiles_per_step):
        windows(u, 0)
        for g in range(group):
            nwin = jnp.maximum(nwin, _num_windows(cnt_ref[base(u, g)], cnt_ref[base(u, g) + 1]))

    def overflow(w, _):
        for u in range(tiles_per_step):
            windows(u, w)
        return 0

    lax.fori_loop(1, nwin, overflow, 0)


def _moe_gather(cnt, post, afft, h2, cap, group=8, tiles_per_step=8):
    b, n, d = h2.shape
    ntile = n // MOE_TILE
    per_tile = pl.BlockSpec((None, group, tiles_per_step, 1, MOE_TILE), lambda bi, eg, ts, c: (bi, eg, ts, 0, 0))
    grid_spec = pltpu.PrefetchScalarGridSpec(
        num_scalar_prefetch=1, grid=(b, N_EXPERTS // group, ntile // tiles_per_step),
        in_specs=[per_tile, per_tile,
                  pl.BlockSpec((None, tiles_per_step * MOE_TILE, d), lambda bi, eg, ts, c: (bi, ts, 0))],
        out_specs=[pl.BlockSpec((None, group, cap, d), lambda bi, eg, ts, c: (bi, eg, 0, 0)),
                   pl.BlockSpec((None, group, cap, LANES), lambda bi, eg, ts, c: (bi, eg, 0, 0))])
    return pl.pallas_call(
        functools.partial(_moe_gather_kernel, ntile=ntile, cap=cap, group=group, tiles_per_step=tiles_per_step),
        out_shape=[jax.ShapeDtypeStruct((b, N_EXPERTS, cap, d), BF16),
                   jax.ShapeDtypeStruct((b, N_EXPERTS, cap, LANES), F32)],
        grid_spec=grid_spec, compiler_params=_cparams(("arbitrary", "arbitrary", "arbitrary")), name="moe_gather",
    )(cnt, post, afft, h2)


def _moe_ffn_kernel(xe_ref, gate_ref, wg_ref, wu_ref, wd_ref, y_ref, acc_scr, *, hid_tile):
    j = pl.program_id(2)
    xe = xe_ref[...]
    for k in range(wg_ref.shape[1] // hid_tile):
        cols = slice(k * hid_tile, (k + 1) * hid_tile)
        gate = jnp.dot(xe, wg_ref[:, cols].astype(BF16), preferred_element_type=F32)
        up = jnp.dot(xe, wu_ref[:, cols].astype(BF16), preferred_element_type=F32)
        hid = (_silu(gate) * up).astype(BF16)
        part = jnp.dot(hid, wd_ref[cols, :].astype(BF16), preferred_element_type=F32)
        if k == 0:
            @pl.when(j == 0)
            def _():
                acc_scr[...] = part

            @pl.when(j != 0)
            def _():
                acc_scr[...] += part
        else:
            acc_scr[...] += part

    @pl.when(j == pl.num_programs(2) - 1)
    def _():
        y_ref[...] = (acc_scr[...] * gate_ref[:, 0:1]).astype(y_ref.dtype)


def _moe_ffn(xe, gate, wg, wu, wd, layer, hid_split=1):
    b, ne, cap, d = xe.shape
    dh = wg.shape[3] // hid_split
    return pl.pallas_call(
        functools.partial(_moe_ffn_kernel, hid_tile=256),
        out_shape=jax.ShapeDtypeStruct((b, ne, cap, d), BF16),
        grid=(b, ne, hid_split),
        in_specs=[pl.BlockSpec((None, None, cap, d), lambda bi, e, j: (bi, e, 0, 0)),
                  pl.BlockSpec((None, None, cap, LANES), lambda bi, e, j: (bi, e, 0, 0)),
                  pl.BlockSpec((None, None, d, dh), lambda bi, e, j: (layer, e, 0, j)),
                  pl.BlockSpec((None, None, d, dh), lambda bi, e, j: (layer, e, 0, j)),
                  pl.BlockSpec((None, None, dh, d), lambda bi, e, j: (layer, e, j, 0))],
        out_specs=pl.BlockSpec((None, None, cap, d), lambda bi, e, j: (bi, e, 0, 0)),
        scratch_shapes=[pltpu.VMEM((cap, d), F32)],
        compiler_params=_cparams(("parallel", "parallel", "arbitrary")), name="moe_ffn",
    )(xe, gate, wg, wu, wd)


def _moe_combine_kernel(cnt_ref, pos_ref, x_ref, gt2_ref, y_hbm, fg_ref, o_ref, acc_scr, ybuf, yovf, ring_sem, ovf_sem,
                        *, ntile, cap, final, nbatch, nsteps):
    bi, ts = pl.program_id(0), pl.program_id(1)
    tiles = pos_ref.shape[0] // MOE_TILE
    kw = N_EXPERTS * SLOT_CHUNK
    step = bi * nsteps + ts
    slot = lax.rem(step, 2)

    def window_copies(b_, ts_, w, dst, sem):
        copies = []
        for u in range(tiles):
            for e in range(N_EXPERTS):
                s0 = cnt_ref[(b_ * N_EXPERTS + e) * (ntile + 1) + ts_ * tiles + u]
                start = _window_start(s0, w, cap)[1]
                copies.append(pltpu.make_async_copy(y_hbm.at[b_, e, pl.ds(start, SLOT_CHUNK), :],
                                                    dst(u).at[pl.ds(e * SLOT_CHUNK, SLOT_CHUNK), :], sem))
        return copies

    def ring(b_, ts_, s_):
        return window_copies(b_, ts_, 0, lambda u: ybuf.at[s_, u], ring_sem.at[s_])

    @pl.when(step == 0)
    def _():
        for cp in ring(bi, ts, slot):
            cp.start()

    @pl.when(step + 1 < nbatch * nsteps)
    def _():
        wrap = ts == nsteps - 1
        for cp in ring(jnp.where(wrap, bi + 1, bi), jnp.where(wrap, 0, ts + 1), 1 - slot):
            cp.start()

    for cp in ring(bi, ts, slot):
        cp.wait()

    shift = SLOT_CHUNK.bit_length() - 1
    sel = jnp.where(lax.broadcasted_iota(I32, (LANES, kw), 0) == (lax.broadcasted_iota(I32, (LANES, kw), 1) >> shift),
                    1.0, 0.0).astype(BF16)

    def spread(v):
        low_bits = 5
        hi = (v >> low_bits).astype(F32).astype(BF16)
        lo = (v & ((1 << low_bits) - 1)).astype(F32).astype(BF16)
        return (float(1 << low_bits) * jnp.dot(hi, sel, preferred_element_type=F32)
                + jnp.dot(lo, sel, preferred_element_type=F32))

    within = (lax.broadcasted_iota(I32, (1, kw), 1) & (SLOT_CHUNK - 1)).astype(F32)
    lane = lax.broadcasted_iota(I32, (SUBLANES, LANES), 1)

    def base(u, e):
        return (bi * N_EXPERTS + e) * (ntile + 1) + ts * tiles + u

    def windows(u, w, rows_of_y):
        pos_w = spread(pos_ref[u * MOE_TILE:(u + 1) * MOE_TILE, :] + 1) - 1.0
        starts = jnp.zeros((SUBLANES, LANES), I32)
        los = jnp.zeros((SUBLANES, LANES), I32)
        for e in range(N_EXPERTS):
            lo, start = _window_start(cnt_ref[base(u, e)], w, cap)
            starts = jnp.where(lane == e, start, starts)
            los = jnp.where(lane == e, lo, los)
        want = spread(starts)[0:1] + within
        onehot = jnp.where((pos_w == want) & (pos_w >= spread(los)[0:1]), 1.0, 0.0).astype(BF16)
        return jnp.dot(onehot, rows_of_y, preferred_element_type=F32)

    nwin = 1
    for u in range(tiles):
        acc_scr[u] = windows(u, 0, ybuf[slot, u])
        for e in range(N_EXPERTS):
            nwin = jnp.maximum(nwin, _num_windows(cnt_ref[base(u, e)], cnt_ref[base(u, e) + 1]))

    def overflow(w, _):
        extra = window_copies(bi, ts, w, lambda u: yovf.at[u], ovf_sem.at[0])
        for cp in extra:
            cp.start()
        for cp in extra:
            cp.wait()
        for u in range(tiles):
            acc_scr[u] += windows(u, w, yovf[u])
        return 0

    lax.fori_loop(1, nwin, overflow, 0)
    for u in range(tiles):
        rows = slice(u * MOE_TILE, (u + 1) * MOE_TILE)
        out = x_ref[rows, :] + gt2_ref[...] * acc_scr[u]
        if final:
            out = out * lax.rsqrt(jnp.mean(out * out, axis=-1, keepdims=True) + EPS) * fg_ref[...]
        o_ref[rows, :] = out


def _moe_combine(cnt, pos, x, mods, layer, y, final_g, final, tiles_per_step=2):
    b, n, d = x.shape
    ntile = n // MOE_TILE
    cap = y.shape[2]
    rows = tiles_per_step * MOE_TILE
    grid_spec = pltpu.PrefetchScalarGridSpec(
        num_scalar_prefetch=1, grid=(b, ntile // tiles_per_step),
        in_specs=[pl.BlockSpec((None, rows, LANES), lambda bi, t, c: (bi, t, 0)),
                  pl.BlockSpec((None, rows, d), lambda bi, t, c: (bi, t, 0)),
                  _mod_spec(layer, GATE2),
                  pl.BlockSpec(memory_space=pl.ANY),
                  pl.BlockSpec((1, d), lambda bi, t, c: (0, 0))],
        out_specs=pl.BlockSpec((None, rows, d), lambda bi, t, c: (bi, t, 0)),
        scratch_shapes=[pltpu.VMEM((tiles_per_step, MOE_TILE, d), F32),
                        pltpu.VMEM((2, tiles_per_step, N_EXPERTS * SLOT_CHUNK, d), BF16),
                        pltpu.VMEM((tiles_per_step, N_EXPERTS * SLOT_CHUNK, d), BF16),
                        pltpu.SemaphoreType.DMA((2,)), pltpu.SemaphoreType.DMA((1,))])
    return pl.pallas_call(
        functools.partial(_moe_combine_kernel, ntile=ntile, cap=cap, final=final, nbatch=b,
                          nsteps=ntile // tiles_per_step),
        out_shape=jax.ShapeDtypeStruct((b, n, d), F32),
        grid_spec=grid_spec, compiler_params=_cparams(("arbitrary", "arbitrary")), name="moe_combine",
    )(cnt, pos, x, mods, y, final_g)


def _ec_moe(x_mid, h2, aff, mods, wg, wu, wd, layer, final_g, final):
    b, n, _ = x_mid.shape
    cap = max(1, EC_FACTOR * n // N_EXPERTS)
    ntile = n // MOE_TILE
    pos, post, offs, afft = _route(aff, cap)
    per_tile = lambda a: a.transpose(0, 2, 1, 3).reshape(b, N_EXPERTS, ntile, 1, MOE_TILE)
    starts = offs[:, ::MOE_TILE // CHUNK, 0, :N_EXPERTS].transpose(0, 2, 1)
    cnt = jnp.concatenate([starts, jnp.full((b, N_EXPERTS, 1), cap, I32)], axis=2).reshape(-1)
    xe, gate = _moe_gather(cnt, per_tile(post), per_tile(afft), h2, cap)
    y = _moe_ffn(xe, gate, wg, wu, wd, layer)
    return _moe_combine(cnt, pos, x_mid, mods, layer, y, final_g, final)


def _rope_table(n):
    rows = n // GRID_W
    row = jnp.repeat(jnp.arange(rows), GRID_W).astype(F32)
    col = jnp.tile(jnp.arange(GRID_W), rows).astype(F32)
    nf = ATTN_DIM // 4
    inv = ROPE_BASE ** (-jnp.arange(nf, dtype=F32) / nf)
    ang = jnp.concatenate([row[:, None] * inv, col[:, None] * inv], axis=-1)
    cos, sin = jnp.cos(ang), jnp.sin(ang)
    reps = LANES // ATTN_DIM
    return jnp.concatenate([jnp.tile(jnp.concatenate([cos, cos], -1), (1, reps)),
                            jnp.tile(jnp.concatenate([-sin, sin], -1), (1, reps))], axis=-1)


def _pad_lanes(a):
    return jnp.pad(a, ((0, 0), (0, LANES - a.shape[1])))


def kernel(x, c, ctx, c_ctx, w_mod, b_mod, norm_mix_g, norm_ffn_g, final_norm_g, ab_w_in, ab_conv_w, ab_gate_b,
           ab_head_g, ab_sink, ab_w_out, gm_w_in, gm_ln_g, gm_ln_b, gm_w_s, gm_b_s, gm_w_out, moe_w_router,
           moe_w_gate, moe_w_up, moe_w_down):
    b, n, d = x.shape
    depth = w_mod.shape[0]
    assert depth <= 2, "context stream is only advanced for deeper stacks; not supported here"
    cond = jnp.zeros((BF16_ROWS, d), F32).at[:b].set(c).at[b].set(c_ctx)
    mods = _adaln(cond, w_mod, b_mod).reshape(depth, BF16_ROWS, 6, 1, d)

    row = lambda v: v.reshape(1, -1)
    for layer in range(depth):
        g1, g2 = row(norm_mix_g[layer]), row(norm_ffn_g[layer])
        wr = _hi_lo_rhs(_pad_lanes(moe_w_router[layer]))
        if layer % 2 == 0:
            e = layer // 2
            w_in = ab_w_in[e]
            g_lo = 4 * LSTM_WIDTH
            w_main = jnp.concatenate([w_in[:, :g_lo], w_in[:, g_lo + N_GATES:]], axis=1).astype(BF16)
            w_gate = _hi_lo_rhs(_pad_lanes(w_in[:, g_lo:g_lo + N_GATES]))
            p, gts, vt, avt = _modmm(x, g1, mods, layer, w_main, w_gate, tm=1024, name="ab_in")
            pc, gtc, vtc, avtc = _modmm(ctx, g1, mods, layer, w_main, w_gate, mod_row=b, name="ab_in_ctx")
            hf, hb = _mlstm(_conv_silu(p, ab_conv_w[e]), vt, gts, _conv_silu(pc, ab_conv_w[e]), vtc, gtc,
                            _pad_lanes(row(ab_gate_b[e])))
            at = _attn(p, avt, pc, avtc, _rope_table(n), _pad_lanes(row(ab_sink[e])))
            x_mid, h2, aff = _ab_out(hf, hb, p, at, x, row(ab_head_g[e]), ab_w_out[e].astype(BF16),
                                     mods, layer, g2, wr)
        else:
            o = layer // 2
            uv = _modmm(x, g1, mods, layer, gm_w_in[o].astype(BF16), act="gelu", tm=1024, name="gm_in")
            x_mid, h2, aff = _gm_out(uv, x, row(gm_ln_g[o]), row(gm_ln_b[o]), gm_w_s[o].astype(BF16),
                                     _pad_lanes(gm_b_s[o].T), gm_w_out[o].astype(BF16), mods, layer, g2, wr)
        x = _ec_moe(x_mid, h2, aff, mods, moe_w_gate, moe_w_up, moe_w_down, layer,
                    row(final_norm_g), layer == depth - 1)
    return x
```

```python
import functools
import math

import jax
import jax.numpy as jnp
from jax import lax
from jax.experimental import pallas as pl
from jax.experimental.pallas import tpu as pltpu

F32 = jnp.float32
BF16 = jnp.bfloat16
I32 = jnp.int32

D_MODEL = 1024
GRID_W = 64
EPS = 1e-6
LSTM_HEADS = 4
LSTM_DIM = 128
LSTM_WIDTH = LSTM_HEADS * LSTM_DIM
LSTM_CONV = 5
CHUNK = 128
ATTN_HEADS = 8
ATTN_KV_HEADS = 2
ATTN_GROUP = ATTN_HEADS // ATTN_KV_HEADS
ATTN_DIM = 64
ROPE_BASE = 10000.0
GM_GROUPS = 8
GM_HALF = 2 * D_MODEL
N_EXPERTS = 16
EC_FACTOR = 2
N_GATES = 4 * LSTM_HEADS

LANES = 128
SUBLANES = 8
BF16_ROWS = 16
V7X_VMEM_BYTES = 64 * 1024 * 1024
VMEM_COMPILER_RESERVE_BYTES = 8 * 1024 * 1024
VMEM_LIMIT_BYTES = V7X_VMEM_BYTES - VMEM_COMPILER_RESERVE_BYTES

P_COLS = 4 * LSTM_WIDTH + ATTN_HEADS * ATTN_DIM + 2 * ATTN_KV_HEADS * ATTN_DIM
PB_V, PB_O, PB_AQ = 2, 3, 4
PB_AK = (4 * LSTM_WIDTH + ATTN_HEADS * ATTN_DIM) // LANES
PB_AV = PB_AK + ATTN_KV_HEADS * ATTN_DIM // LANES
assert ATTN_HEADS * ATTN_DIM == LSTM_WIDTH and ATTN_KV_HEADS * ATTN_DIM == LANES
ATTN_HEADS_PER_DOT = 8
ATTN_STEP_BLOCKS = 8
TAIL_CHAIN_ROWS = 256
MOE_TILE = 256
SLOT_CHUNK = 64


def _cparams(sem, vmem=VMEM_LIMIT_BYTES):
    return pltpu.CompilerParams(dimension_semantics=sem, vmem_limit_bytes=vmem)


SHIFT1, SCALE1, GATE1, SHIFT2, SCALE2, GATE2 = range(6)


def _mod_spec(layer, comp, row=None):
    def index(bi, *_):
        return (layer, bi if row is None else row, comp, 0, 0)
    return pl.BlockSpec((None, None, None, 1, D_MODEL), index)


def _rms_mod(x, g, sc, sh):
    y = x * lax.rsqrt(jnp.mean(x * x, axis=-1, keepdims=True) + EPS)
    return y * g * (1.0 + sc) + sh


def _silu(x):
    return x * jax.nn.sigmoid(x)


def _gelu_tanh(x):
    return 0.5 * x * (1.0 + jnp.tanh((2.0 / math.pi) ** 0.5 * (x + 0.044715 * (x * x * x))))


def _log_sigmoid(x):
    return jnp.minimum(x, 0.0) - jnp.log(1.0 + jnp.exp(-jnp.abs(x)))


def _dot_t(a, b):
    return lax.dot_general(a, b, (((1,), (1,)), ((), ())), preferred_element_type=F32)


def _adaln_kernel(c_ref, w_ref, b_ref, o_ref):
    s = _silu(c_ref[...])
    s_hi = s.astype(BF16)
    s_lo = (s - s_hi.astype(F32)).astype(BF16)
    w = w_ref[...]
    w_hi = w.astype(BF16)
    w_lo = (w - w_hi.astype(F32)).astype(BF16)
    both = jnp.dot(jnp.concatenate([s_hi, s_lo], axis=0), w_hi, preferred_element_type=F32)
    rows = s.shape[0]
    o_ref[...] = both[:rows] + both[rows:] + jnp.dot(s_hi, w_lo, preferred_element_type=F32) + b_ref[...]


def _adaln(cond, w_mod, b_mod):
    depth, d, six_d = w_mod.shape
    rows = cond.shape[0]
    tn = six_d // 4
    return pl.pallas_call(
        _adaln_kernel,
        out_shape=jax.ShapeDtypeStruct((depth, rows, six_d), F32),
        grid=(depth, six_d // tn),
        in_specs=[pl.BlockSpec((rows, d), lambda l, j: (0, 0)),
                  pl.BlockSpec((None, d, tn), lambda l, j: (l, 0, j)),
                  pl.BlockSpec((None, 1, tn), lambda l, j: (l, 0, j))],
        out_specs=pl.BlockSpec((None, rows, tn), lambda l, j: (l, 0, j)),
        compiler_params=_cparams(("arbitrary", "arbitrary")),
        name="adaln",
    )(cond, w_mod, b_mod.reshape(depth, 1, six_d))


def _hi_lo_rhs(w):
    w_hi = w.astype(BF16)
    return jnp.concatenate([w_hi, (w - w_hi.astype(F32)).astype(BF16)], axis=1)


def _hi_lo_dot(h, h_hi, w2):
    n = w2.shape[1] // 2
    both = jnp.dot(h_hi, w2, preferred_element_type=F32)
    h_lo = (h - h_hi.astype(F32)).astype(BF16)
    return both[:, :n] + both[:, n:] + jnp.dot(h_lo, w2[:, :n], preferred_element_type=F32)


def _modmm_kernel(x_ref, g_ref, sc_ref, sh_ref, w_ref, *rest, chunks, act, with_gates):
    if with_gates:
        wg_ref, o_ref, og_ref, vt_ref, avt_ref = rest
    else:
        (o_ref,) = rest
    h = _rms_mod(x_ref[...], g_ref[...], sc_ref[...], sh_ref[...])
    hb = h.astype(BF16)
    for lo, hi in chunks:
        y = jnp.dot(hb, w_ref[:, lo:hi], preferred_element_type=F32)
        if act == "gelu":
            y = _gelu_tanh(y)
        o_ref[:, lo:hi] = y.astype(o_ref.dtype)
        if with_gates and lo == PB_V * LSTM_WIDTH:
            vt_ref[...] = y.T.astype(vt_ref.dtype)
        if with_gates and lo <= PB_AV * LANES < hi:
            av = y[:, PB_AV * LANES - lo:(PB_AV + 1) * LANES - lo]
            avt_ref[...] = av.T.astype(avt_ref.dtype)
    if with_gates:
        og_ref[...] = _hi_lo_dot(h, hb, wg_ref[...])


def _modmm(x, g, mods, layer, w, wg=None, *, mod_row=None, act=None, tm=512, chunk=512, name="modmm"):
    b, n, d = x.shape
    no = w.shape[1]
    tm = min(tm, n)
    assert chunk == LSTM_WIDTH
    chunks = tuple((lo, min(lo + chunk, no)) for lo in range(0, no, chunk))
    in_specs = [pl.BlockSpec((None, tm, d), lambda bi, i: (bi, i, 0)),
                pl.BlockSpec((1, d), lambda bi, i: (0, 0)),
                _mod_spec(layer, SCALE1, mod_row), _mod_spec(layer, SHIFT1, mod_row),
                pl.BlockSpec((d, no), lambda bi, i: (0, 0))]
    out_shape = [jax.ShapeDtypeStruct((b, n, no), BF16)]
    out_specs = [pl.BlockSpec((None, tm, no), lambda bi, i: (bi, i, 0))]
    args = [x, g, mods, mods, w]
    if wg is not None:
        in_specs.append(pl.BlockSpec(wg.shape, lambda bi, i: (0, 0)))
        out_shape += [jax.ShapeDtypeStruct((b, n, LANES), F32), jax.ShapeDtypeStruct((b, LSTM_WIDTH, n), BF16),
                      jax.ShapeDtypeStruct((b, LANES, n), BF16)]
        out_specs += [pl.BlockSpec((None, tm, LANES), lambda bi, i: (bi, i, 0)),
                      pl.BlockSpec((None, LSTM_WIDTH, tm), lambda bi, i: (bi, 0, i)),
                      pl.BlockSpec((None, LANES, tm), lambda bi, i: (bi, 0, i))]
        args.append(wg)
    res = pl.pallas_call(
        functools.partial(_modmm_kernel, chunks=chunks, act=act, with_gates=wg is not None),
        out_shape=out_shape, grid=(b, n // tm), in_specs=in_specs, out_specs=out_specs,
        compiler_params=_cparams(("parallel", "parallel")), name=name,
    )(*args)
    return res if wg is not None else res[0]


def _conv_silu_kernel(x_ref, xp_ref, xn_ref, w_ref, o_ref, pad_scr):
    i = pl.program_id(1)
    rows = x_ref.shape[0]
    halo = BF16_ROWS
    has_prev = jnp.where(i > 0, 1.0, 0.0)
    has_next = jnp.where(i < pl.num_programs(1) - 1, 1.0, 0.0)
    pad_scr[pl.ds(0, halo), :] = (xp_ref[...].astype(F32) * has_prev).astype(BF16)
    pad_scr[pl.ds(halo, rows), :] = x_ref[...]
    pad_scr[pl.ds(halo + rows, halo), :] = (xn_ref[...].astype(F32) * has_next).astype(BF16)
    w = w_ref[...]
    win = CHUNK + 2 * halo
    r = lax.broadcasted_iota(I32, (CHUNK, win), 0)
    c = lax.broadcasted_iota(I32, (CHUNK, win), 1)
    half = LSTM_CONV // 2
    shifts = {t: jnp.where(c == r + halo + t - half, 1.0, 0.0).astype(BF16) for t in range(LSTM_CONV) if t != half}
    scale = LSTM_DIM ** -0.5
    for blk in range(rows // CHUNK):
        xw = pad_scr[pl.ds(blk * CHUNK, win), :]
        acc = xw[halo:halo + CHUNK].astype(F32) * w[half:half + 1, :]
        for t, s in shifts.items():
            acc = acc + jnp.dot(s, xw, preferred_element_type=F32) * w[t:t + 1, :]
        y = _silu(acc)
        out = pl.ds(blk * CHUNK, CHUNK)
        o_ref[out, :LSTM_WIDTH] = y[:, :LSTM_WIDTH].astype(o_ref.dtype)
        o_ref[out, LSTM_WIDTH:] = (y[:, LSTM_WIDTH:] * scale).astype(o_ref.dtype)


def _conv_silu(p, conv_w, tm=1024):
    b, n, _ = p.shape
    tm = min(tm, n)
    qkw = 2 * LSTM_WIDTH
    hpt = tm // BF16_ROWS
    nhb = n // BF16_ROWS
    return pl.pallas_call(
        _conv_silu_kernel,
        out_shape=jax.ShapeDtypeStruct((b, n, qkw), BF16),
        grid=(b, n // tm),
        in_specs=[pl.BlockSpec((None, tm, qkw), lambda bi, i: (bi, i, 0)),
                  pl.BlockSpec((None, BF16_ROWS, qkw), lambda bi, i: (bi, jnp.maximum(i * hpt - 1, 0), 0)),
                  pl.BlockSpec((None, BF16_ROWS, qkw), lambda bi, i: (bi, jnp.minimum((i + 1) * hpt, nhb - 1), 0)),
                  pl.BlockSpec((LSTM_CONV, qkw), lambda bi, i: (0, 0))],
        out_specs=pl.BlockSpec((None, tm, qkw), lambda bi, i: (bi, i, 0)),
        scratch_shapes=[pltpu.VMEM((tm + 2 * BF16_ROWS, qkw), BF16)],
        compiler_params=_cparams(("parallel", "parallel")), name="conv_silu",
    )(p, p, p, conv_w)


def _tri(n, lower):
    r = lax.broadcasted_iota(I32, (n, n), 0)
    c = lax.broadcasted_iota(I32, (n, n), 1)
    return (c <= r) if lower else (c >= r)


STATE_ROWS = LSTM_DIM + BF16_ROWS
MLSTM_STEP_CHUNKS = 8


def _mlstm_segment(d, gates, k_all, vt_all, q_all, state, ht_ref, cols):
    first = state is None
    nh = LSTM_HEADS
    seg = gates.shape[0]

    def side_by_side(pieces):
        return jnp.concatenate(pieces, axis=1)

    def block_diag(x):
        w = x.shape[1] // nh
        lane = lax.broadcasted_iota(I32, x.shape, 1)
        return jnp.concatenate([jnp.where((lane >= h * w) & (lane < (h + 1) * w), x, jnp.zeros_like(x))
                                for h in range(nh)], axis=0)

    r = lax.broadcasted_iota(I32, (seg, seg), 0)
    c = lax.broadcasted_iota(I32, (seg, seg), 1)
    before = (r <= c) if d == 0 else (r >= c)
    gates_t = gates.T[:N_GATES]
    ls = _log_sigmoid(gates_t)
    ls_hi = ls.astype(BF16)
    rest = ls - ls_hi.astype(F32)
    ls_mid = rest.astype(BF16)
    ls_lo = (rest - ls_mid.astype(F32)).astype(BF16)
    parts = jnp.dot(jnp.concatenate([ls_hi, ls_mid, ls_lo], axis=0), jnp.where(before, 1.0, 0.0).astype(BF16),
                    preferred_element_type=F32)
    bcum_t = parts[:N_GATES] + parts[N_GATES:2 * N_GATES] + parts[2 * N_GATES:]
    last = seg - 1 if d == 0 else 0
    ci = [2 * d * nh + h for h in range(nh)]
    cf = [(2 * d + 1) * nh + h for h in range(nh)]
    b_rows = side_by_side([bcum_t[c:c + 1, :] for c in cf])
    li_rows = side_by_side([gates_t[c:c + 1, :] for c in ci])
    g_heads = [bcum_t[c:c + 1, last:last + 1] for c in cf]
    cn, m_heads = (None, [jnp.zeros((1, 1), F32)] * nh) if first else state
    over_seg = lambda xs: side_by_side([jnp.broadcast_to(x, (1, seg)) for x in xs])
    m_rows, g_rows = over_seg(m_heads), over_seg(g_heads)
    vt_cat = side_by_side([vt_all[h * LSTM_DIM:(h + 1) * LSTM_DIM, :] for h in range(nh)])
    if q_all is not None:
        q_blk = block_diag(q_all)
        per_key = side_by_side([jnp.broadcast_to(gates_t[i:i + 1, :] - bcum_t[f:f + 1, :], (seg, seg)).T
                                for i, f in zip(ci, cf)])
        log_d = jnp.where(side_by_side([before] * nh), b_rows + per_key, -jnp.inf)
        m_row = jnp.maximum(b_rows + m_rows, jnp.max(log_d, axis=0, keepdims=True))
        sm = _dot_t(k_all, q_blk) * jnp.exp(log_d - m_row)
        a = jnp.exp(b_rows + m_rows - m_row)
        qc = _dot_t(cn.astype(BF16), q_blk)
        num = jnp.dot(vt_cat, block_diag(sm.astype(BF16)), preferred_element_type=F32) + a * qc[:LSTM_DIM]
        den = jnp.sum(sm, axis=0, keepdims=True) + a * qc[LSTM_DIM:LSTM_DIM + 1]
        out = num / jnp.maximum(jnp.abs(den), jnp.exp(-m_row))
        for h in range(nh):
            ht_ref[h * LSTM_DIM:(h + 1) * LSTM_DIM, cols] = out[:, h * seg:(h + 1) * seg]
    w = g_rows - b_rows + li_rows
    m_new = [jnp.maximum(g_heads[h] + m_heads[h], jnp.max(w[:, h * seg:(h + 1) * seg], axis=1, keepdims=True))
             for h in range(nh)]
    wt = jnp.exp(w - over_seg(m_new))
    aug = jnp.concatenate([(vt_cat.astype(F32) * wt).astype(BF16),
                           jnp.broadcast_to(wt, (BF16_ROWS, nh * seg)).astype(BF16)], axis=0)
    upd = jnp.dot(aug, block_diag(k_all), preferred_element_type=F32)
    if not first:
        decay = side_by_side([jnp.broadcast_to(jnp.exp(g_heads[h] + m_heads[h] - m_new[h]), (1, LSTM_DIM))
                              for h in range(nh)])
        upd = decay * cn + upd
    return upd, m_new


def _mlstm_kernel(qkf_ref, qkb_ref, vtf_ref, vtb_ref, gf_ref, gb_ref, kc_ref, vtc_ref, gc_ref, gbias_ref,
                  hf_ref, hb_ref, cn_scr, m_scr):
    gbias = gbias_ref[...]

    def save(d, state):
        cn, m_heads = state
        cn_scr[d] = cn
        for h, m in enumerate(m_heads):
            m_scr[d * LSTM_HEADS + h] = jnp.broadcast_to(m, (1, LANES))

    @pl.when(pl.program_id(1) == 0)
    def _():
        gates = gc_ref[...] + gbias
        for d in range(2):
            save(d, _mlstm_segment(d, gates, kc_ref[...], vtc_ref[...], None, None, None, None))

    states = [(cn_scr[d], [m_scr[d * LSTM_HEADS + h][:, 0:1] for h in range(LSTM_HEADS)]) for d in range(2)]
    per_step = qkf_ref.shape[0] // CHUNK
    for u in range(per_step):
        for d, (qk_ref, vt_ref, g_ref, h_ref) in enumerate(((qkf_ref, vtf_ref, gf_ref, hf_ref),
                                                            (qkb_ref, vtb_ref, gb_ref, hb_ref))):
            j = u if d == 0 else per_step - 1 - u
            rows = slice(j * CHUNK, (j + 1) * CHUNK)
            qk = qk_ref[rows, :]
            states[d] = _mlstm_segment(d, g_ref[rows, :] + gbias, qk[:, LSTM_WIDTH:], vt_ref[:, rows],
                                       qk[:, :LSTM_WIDTH], states[d], h_ref, rows)
    for d in range(2):
        save(d, states[d])


def _mlstm(qk, vt, g, qkc, vtc, gc, gate_bias):
    b, n, _ = qk.shape
    lc = qkc.shape[1]
    rows = MLSTM_STEP_CHUNKS * CHUNK
    nc = n // rows
    qkw = 2 * LSTM_WIDTH
    in_specs = [
        pl.BlockSpec((None, rows, qkw), lambda bi, c: (bi, c, 0)),
        pl.BlockSpec((None, rows, qkw), lambda bi, c: (bi, nc - 1 - c, 0)),
        pl.BlockSpec((None, LSTM_WIDTH, rows), lambda bi, c: (bi, 0, c)),
        pl.BlockSpec((None, LSTM_WIDTH, rows), lambda bi, c: (bi, 0, nc - 1 - c)),
        pl.BlockSpec((None, rows, LANES), lambda bi, c: (bi, c, 0)),
        pl.BlockSpec((None, rows, LANES), lambda bi, c: (bi, nc - 1 - c, 0)),
        pl.BlockSpec((None, lc, LSTM_WIDTH), lambda bi, c: (bi, 0, 1)),
        pl.BlockSpec((None, LSTM_WIDTH, lc), lambda bi, c: (bi, 0, 0)),
        pl.BlockSpec((None, lc, LANES), lambda bi, c: (bi, 0, 0)),
        pl.BlockSpec((1, LANES), lambda bi, c: (0, 0))]
    out_specs = [pl.BlockSpec((None, LSTM_WIDTH, rows), lambda bi, c: (bi, 0, c)),
                 pl.BlockSpec((None, LSTM_WIDTH, rows), lambda bi, c: (bi, 0, nc - 1 - c))]
    return pl.pallas_call(
        _mlstm_kernel,
        out_shape=[jax.ShapeDtypeStruct((b, LSTM_WIDTH, n), F32)] * 2,
        grid=(b, nc), in_specs=in_specs, out_specs=out_specs,
        scratch_shapes=[pltpu.VMEM((2, STATE_ROWS, LSTM_WIDTH), F32),
                        pltpu.VMEM((2 * LSTM_HEADS, 1, LANES), F32)],
        compiler_params=_cparams(("arbitrary", "arbitrary")), name="mlstm",
    )(qk, qk, vt, vt, g, g, qkc, vtc, gc, gate_bias)


def _rope(x, cos, sin_signed):
    w = x.shape[1]
    lane = lax.broadcasted_iota(I32, x.shape, 1)
    first = (lane & (ATTN_DIM - 1)) < (ATTN_DIM // 2)
    partner = jnp.where(first, pltpu.roll(x, w - ATTN_DIM // 2, 1), pltpu.roll(x, ATTN_DIM // 2, 1))
    return x * cos + partner * sin_signed


def _attn_kernel(q_ref, *refs):
    nblk = q_ref.shape[0] // CHUNK
    nwin = nblk + 2
    k_refs, vt_refs = refs[:nwin], refs[nwin:2 * nwin]
    kctx_ref, vtctx_ref = refs[2 * nwin:2 * nwin + 2]
    t_refs = refs[2 * nwin + 2:3 * nwin + 2]
    sink_ref, bias_ref, o_ref = refs[3 * nwin + 2:]
    step, last_step = pl.program_id(1), pl.num_programs(1) - 1

    def table(t_ref):
        t = t_ref[...]
        return t[:, :LANES], t[:, LANES:]

    ks = []
    for k_ref, t_ref in zip(k_refs, t_refs):
        cos_t, sin_t = table(t_ref)
        ks.append(_rope(k_ref[...].astype(F32), cos_t, sin_t).astype(BF16))
    for u in range(nblk):
        which = jnp.where(step == 0, 0, 1) if u == 0 else 1
        if u == nblk - 1:
            which = jnp.where(step == last_step, 2, which)
        _attn_block(q_ref[u * CHUNK:(u + 1) * CHUNK, :], table(t_refs[u + 1]),
                    jnp.concatenate(ks[u:u + 3] + [kctx_ref[...]], axis=0),
                    jnp.concatenate([r[...] for r in vt_refs[u:u + 3]] + [vtctx_ref[...]], axis=1),
                    sink_ref[...], bias_ref[which], o_ref.at[:, u * CHUNK:(u + 1) * CHUNK])


def _attn_block(q_in, q_table, k_all, vt_all, sink, bias1, o_ref):
    cos_c, sin_c = q_table
    q = _rope(q_in.astype(F32), jnp.concatenate([cos_c] * 4, axis=1), jnp.concatenate([sin_c] * 4, axis=1))
    q = q * (ATTN_DIM ** -0.5)
    per = ATTN_HEADS_PER_DOT
    bias = jnp.concatenate([bias1] * per, axis=1)
    lane = lax.broadcasted_iota(I32, (CHUNK, LANES), 1)

    for h0 in range(0, ATTN_HEADS, per):
        qs, snk = [], []
        for h in range(h0, h0 + per):
            g = h // ATTN_GROUP
            t = q[:, (h // 2) * LANES:(h // 2 + 1) * LANES]
            if h % 2 != g:
                t = pltpu.roll(t, ATTN_DIM, 1)
            half_g = (lane >= ATTN_DIM) if g == 1 else (lane < ATTN_DIM)
            qs.append(jnp.where(half_g, t, 0.0).astype(BF16))
            snk.append(jnp.broadcast_to(sink[:, h:h + 1], (1, CHUNK)))
        snk = jnp.concatenate(snk, axis=1)
        st = _dot_t(k_all, jnp.concatenate(qs, axis=0)) + bias
        m = jnp.maximum(jnp.max(st, axis=0, keepdims=True), snk)
        e = jnp.exp(st - m)
        den = jnp.sum(e, axis=0, keepdims=True) + jnp.exp(snk - m)
        pv = jnp.dot(vt_all, e.astype(BF16), preferred_element_type=F32)
        o = (pv / den).astype(o_ref.dtype)
        for r, h in enumerate(range(h0, h0 + per)):
            g = h // ATTN_GROUP
            o_ref[h * ATTN_DIM:(h + 1) * ATTN_DIM, :] = o[g * ATTN_DIM:(g + 1) * ATTN_DIM, r * CHUNK:(r + 1) * CHUNK]


def _attn_bias(nctx):
    i = jnp.arange(CHUNK)[None, :]
    j = jnp.arange(3 * CHUNK)[:, None]
    band = (j >= i) & (j <= i + 2 * CHUNK)
    local = jnp.stack([band & (j >= CHUNK), band, band & (j < 2 * CHUNK)])
    return jnp.concatenate([jnp.where(local, 0.0, -jnp.inf).astype(F32), jnp.zeros((3, nctx, CHUNK), F32)], axis=1)


def _attn(p, avt, pc, avtc, table, sink):
    b, n, _ = p.shape
    lc = pc.shape[1]
    nb = n // CHUNK
    assert nb >= 2
    qw = ATTN_HEADS * ATTN_DIM
    bias = _attn_bias(lc)

    s = ATTN_STEP_BLOCKS
    offs = range(-1, s + 1)

    def blk(col, off):
        return pl.BlockSpec((None, CHUNK, LANES), lambda bi, i: (bi, jnp.clip(i * s + off, 0, nb - 1), col))

    def vblk(off):
        return pl.BlockSpec((None, LANES, CHUNK), lambda bi, i: (bi, 0, jnp.clip(i * s + off, 0, nb - 1)))

    def tab(off):
        return pl.BlockSpec((CHUNK, 2 * LANES), lambda bi, i: (jnp.clip(i * s + off, 0, nb - 1), 0))

    in_specs = ([pl.BlockSpec((None, s * CHUNK, qw), lambda bi, i: (bi, i, PB_AQ))]
                + [blk(PB_AK, o) for o in offs] + [vblk(o) for o in offs]
                + [pl.BlockSpec((None, lc, LANES), lambda bi, i: (bi, 0, PB_AK)),
                   pl.BlockSpec((None, LANES, lc), lambda bi, i: (bi, 0, 0))]
                + [tab(o) for o in offs]
                + [pl.BlockSpec((1, LANES), lambda bi, i: (0, 0)),
                   pl.BlockSpec(bias.shape, lambda bi, i: (0, 0, 0))])
    nw = len(offs)
    return pl.pallas_call(
        _attn_kernel,
        out_shape=jax.ShapeDtypeStruct((b, qw, n), BF16),
        grid=(b, nb // s), in_specs=in_specs,
        out_specs=pl.BlockSpec((None, qw, s * CHUNK), lambda bi, i: (bi, 0, i)),
        compiler_params=_cparams(("parallel", "parallel")), name="window_attn",
    )(*([p] * (1 + nw) + [avt] * nw + [pc, avtc] + [table] * nw + [sink, bias]))


def _router_tail(x_new, g2, sc2, sh2, wr_ref, x_out_ref, h2_ref, aff_ref):
    x_out_ref[...] = x_new
    h2 = _rms_mod(x_new, g2, sc2, sh2)
    h_hi = h2.astype(BF16)
    h2_ref[...] = h_hi
    logits = _hi_lo_dot(h2, h_hi, wr_ref[...])
    lane = lax.broadcasted_iota(I32, logits.shape, 1)
    logits = jnp.where(lane < N_EXPERTS, logits, -jnp.inf)
    e = jnp.exp(logits - jnp.max(logits, axis=1, keepdims=True))
    aff_ref[...] = e / jnp.sum(e, axis=1, keepdims=True)


def _ab_out_kernel(hf_ref, hb_ref, o_ref, at_ref, x_ref, *refs):
    for lo in range(0, x_ref.shape[0], TAIL_CHAIN_ROWS):
        rows = slice(lo, lo + TAIL_CHAIN_ROWS)
        _ab_out_rows(hf_ref.at[:, rows], hb_ref.at[:, rows], o_ref.at[rows], at_ref.at[:, rows], x_ref.at[rows],
                     *refs[:-3], *[out.at[rows] for out in refs[-3:]])


def _ab_out_rows(hf_ref, hb_ref, o_ref, at_ref, x_ref, hg_ref, wo_ref, gt1_ref, g2_ref, sc2_ref, sh2_ref, wr_ref,
                 x_out_ref, h2_ref, aff_ref):
    hsum = (hf_ref[...] + hb_ref[...]).T
    og = jax.nn.sigmoid(o_ref[...].astype(F32))
    hg = hg_ref[...]
    parts = []
    for h in range(LSTM_HEADS):
        sl = slice(h * LSTM_DIM, (h + 1) * LSTM_DIM)
        seg = hsum[:, sl]
        seg = seg * lax.rsqrt(jnp.mean(seg * seg, axis=-1, keepdims=True) + EPS)
        parts.append((seg * hg[:, sl] * og[:, sl]).astype(BF16))
    cat = jnp.concatenate(parts + [at_ref[...].astype(F32).T.astype(BF16)], axis=1)
    y = jnp.dot(cat, wo_ref[...], preferred_element_type=F32)
    _router_tail(x_ref[...] + gt1_ref[...] * y, g2_ref[...], sc2_ref[...], sh2_ref[...], wr_ref,
                 x_out_ref, h2_ref, aff_ref)


def _tail_out(b, n, d, tm):
    shapes = [jax.ShapeDtypeStruct((b, n, d), F32), jax.ShapeDtypeStruct((b, n, d), BF16),
              jax.ShapeDtypeStruct((b, n, LANES), F32)]
    specs = [pl.BlockSpec((None, tm, d), lambda bi, i: (bi, i, 0)),
             pl.BlockSpec((None, tm, d), lambda bi, i: (bi, i, 0)),
             pl.BlockSpec((None, tm, LANES), lambda bi, i: (bi, i, 0))]
    return shapes, specs


def _ab_out(hf, hb, p, at, x, head_g, w_out, mods, layer, g2, wr, tm=1024):
    b, n, d = x.shape
    row = lambda w: pl.BlockSpec((None, tm, w), lambda bi, i: (bi, i, 0))
    const = lambda s: pl.BlockSpec(s, lambda bi, i: (0, 0))
    scan_out = pl.BlockSpec((None, LSTM_WIDTH, tm), lambda bi, i: (bi, 0, i))
    in_specs = [scan_out, scan_out,
                pl.BlockSpec((None, tm, LSTM_WIDTH), lambda bi, i: (bi, i, PB_O)),
                pl.BlockSpec((None, ATTN_HEADS * ATTN_DIM, tm), lambda bi, i: (bi, 0, i)),
                row(d), const((1, LSTM_WIDTH)), const(w_out.shape),
                _mod_spec(layer, GATE1), const((1, d)), _mod_spec(layer, SCALE2), _mod_spec(layer, SHIFT2),
                const((d, 2 * LANES))]
    shapes, specs = _tail_out(b, n, d, tm)
    return pl.pallas_call(
        _ab_out_kernel, out_shape=shapes, grid=(b, n // tm), in_specs=in_specs, out_specs=specs,
        compiler_params=_cparams(("parallel", "parallel")), name="ab_out",
    )(hf, hb, p, at, x, head_g, w_out, mods, g2, mods, mods, wr)


def _gm_out_kernel(uv_ref, x_ref, *refs):
    for lo in range(0, x_ref.shape[0], TAIL_CHAIN_ROWS):
        rows = slice(lo, lo + TAIL_CHAIN_ROWS)
        _gm_out_rows(uv_ref.at[rows], x_ref.at[rows], *refs[:-3], *[out.at[rows] for out in refs[-3:]])


def _gm_out_rows(uv_ref, x_ref, lng_ref, lnb_ref, ws_ref, bs_ref, wo_ref, gt1_ref, g2_ref, sc2_ref, sh2_ref, wr_ref,
                 x_out_ref, h2_ref, aff_ref):
    tm = uv_ref.shape[0]
    gw = GM_HALF // GM_GROUPS
    v = uv_ref[:, GM_HALF:].astype(F32)
    mu = jnp.mean(v, axis=-1, keepdims=True)
    vc = v - mu
    var = jnp.mean(vc * vc, axis=-1, keepdims=True)
    vn = (vc * lax.rsqrt(var + EPS) * lng_ref[...] + lnb_ref[...]).astype(BF16)
    zs = []
    for ch in range(tm // CHUNK):
        rows = slice(ch * CHUNK, (ch + 1) * CHUNK)
        cols = []
        for g in range(GM_GROUPS):
            sv = jnp.dot(ws_ref[g], vn[rows, g * gw:(g + 1) * gw], preferred_element_type=F32)
            cols.append(sv + bs_ref[:, g:g + 1])
        sv = jnp.concatenate(cols, axis=1)
        zs.append((uv_ref[rows, :GM_HALF].astype(F32) * sv).astype(BF16))
    z = jnp.concatenate(zs, axis=0)
    y = jnp.dot(z, wo_ref[...], preferred_element_type=F32)
    _router_tail(x_ref[...] + gt1_ref[...] * y, g2_ref[...], sc2_ref[...], sh2_ref[...], wr_ref,
                 x_out_ref, h2_ref, aff_ref)


def _gm_out(uv, x, ln_g, ln_b, w_s, b_s_t, w_out, mods, layer, g2, wr, tm=1024):
    b, n, d = x.shape
    const = lambda s: pl.BlockSpec(s, lambda *_: (0,) * len(s))
    in_specs = [pl.BlockSpec((None, tm, 2 * GM_HALF), lambda bi, i: (bi, i, 0)),
                pl.BlockSpec((None, tm, d), lambda bi, i: (bi, i, 0)),
                const((1, GM_HALF)), const((1, GM_HALF)), const(w_s.shape), const(b_s_t.shape), const(w_out.shape),
                _mod_spec(layer, GATE1), const((1, d)), _mod_spec(layer, SCALE2), _mod_spec(layer, SHIFT2),
                const((d, 2 * LANES))]
    shapes, specs = _tail_out(b, n, d, tm)
    return pl.pallas_call(
        _gm_out_kernel, out_shape=shapes, grid=(b, n // tm), in_specs=in_specs, out_specs=specs,
        compiler_params=_cparams(("parallel", "parallel")), name="gm_out",
    )(uv, x, ln_g, ln_b, w_s, b_s_t, w_out, mods, g2, mods, mods, wr)


def _route_kernel(aff_ref, pos_ref, post_ref, offs_ref, afft_ref, *, cap):
    n = aff_ref.shape[0]
    nblk = n // CHUNK

    def to_expert_major(k, _):
        rows = pl.ds(pl.multiple_of(k * CHUNK, CHUNK), CHUNK)
        afft_ref[k] = aff_ref[rows, :].T[:N_EXPERTS, :]
        return 0

    lax.fori_loop(0, nblk, to_expert_major, 0, unroll=4)

    def count(pred):
        per_lane = jnp.sum(jnp.where(pred, 1.0, 0.0), axis=0)
        return jnp.sum(per_lane, axis=1, keepdims=True)

    def search(i, prefix):
        cand = prefix | jnp.left_shift(jnp.int32(1), 30 - i)
        cand_f = lax.bitcast_convert_type(cand, F32)
        return jnp.where(count(afft_ref[...] >= cand_f[None]) >= cap, cand, prefix)

    thr_col = lax.bitcast_convert_type(lax.fori_loop(0, 31, search, jnp.zeros((N_EXPERTS, 1), I32)), F32)
    need_col = cap - count(afft_ref[...] > thr_col[None])

    def to_row(col):
        full = jnp.concatenate([jnp.broadcast_to(col, (N_EXPERTS, LANES)),
                                jnp.zeros((LANES - N_EXPERTS, LANES), F32)], axis=0)
        return full.T[0:1, :]

    thr, need = to_row(thr_col), to_row(need_col)
    tril = jnp.where(_tri(CHUNK, True), 1.0, 0.0).astype(BF16)

    def block(k, carry):
        run_tie, run_sel = carry
        rows = pl.ds(pl.multiple_of(k * CHUNK, CHUNK), CHUNK)
        a = aff_ref[rows, :]
        gt = a > thr
        tie = jnp.where(a == thr, 1.0, 0.0)
        tie_incl = jnp.dot(tril, tie.astype(BF16), preferred_element_type=F32)
        sel = jnp.where(gt | ((tie > 0.0) & (tie_incl - tie + run_tie < need)), 1.0, 0.0)
        sel_incl = jnp.dot(tril, sel.astype(BF16), preferred_element_type=F32)
        pos = jnp.where(sel > 0.0, sel_incl - sel + run_sel, -1.0)
        pos_ref[rows, :] = pos.astype(I32)
        post_ref[k] = pos.T[:N_EXPERTS, :].astype(I32)
        offs_ref[k] = run_sel.astype(I32)
        return (run_tie + tie_incl[CHUNK - 1:CHUNK, :], run_sel + sel_incl[CHUNK - 1:CHUNK, :])

    zero = jnp.zeros((1, LANES), F32)
    lax.fori_loop(0, nblk, block, (zero, zero), unroll=4)


def _route(aff, cap):
    b, n, _ = aff.shape
    nblk = n // CHUNK
    return pl.pallas_call(
        functools.partial(_route_kernel, cap=cap),
        out_shape=[jax.ShapeDtypeStruct((b, n, LANES), I32),
                   jax.ShapeDtypeStruct((b, nblk, N_EXPERTS, CHUNK), I32),
                   jax.ShapeDtypeStruct((b, nblk, 1, LANES), I32),
                   jax.ShapeDtypeStruct((b, nblk, N_EXPERTS, CHUNK), F32)],
        grid=(b,),
        in_specs=[pl.BlockSpec((None, n, LANES), lambda bi: (bi, 0, 0))],
        out_specs=[pl.BlockSpec((None, n, LANES), lambda bi: (bi, 0, 0)),
                   pl.BlockSpec((None, nblk, N_EXPERTS, CHUNK), lambda bi: (bi, 0, 0, 0)),
                   pl.BlockSpec((None, nblk, 1, LANES), lambda bi: (bi, 0, 0, 0)),
                   pl.BlockSpec((None, nblk, N_EXPERTS, CHUNK), lambda bi: (bi, 0, 0, 0))],
        compiler_params=_cparams(("parallel",)), name="route",
    )(aff)


def _align_rows(s):
    shift = BF16_ROWS.bit_length() - 1
    return (s >> shift) << shift


def _window_start(s0, w, cap):
    lo = _align_rows(s0) + w * SLOT_CHUNK
    return lo, pl.multiple_of(jnp.minimum(lo, cap - SLOT_CHUNK), BF16_ROWS)


def _num_windows(s0, s1):
    return (s1 - _align_rows(s0) + SLOT_CHUNK - 1) >> (SLOT_CHUNK.bit_length() - 1)


def _moe_gather_kernel(cnt_ref, post_ref, afft_ref, h_ref, xe_ref, gate_ref, *, ntile, cap, group, tiles_per_step):
    bi, eg, ts = pl.program_id(0), pl.program_id(1), pl.program_id(2)

    @pl.when(ts == 0)
    def _():
        xe_ref[...] = jnp.zeros_like(xe_ref)
        gate_ref[...] = jnp.zeros_like(gate_ref)

    slot = lax.broadcasted_iota(I32, (SLOT_CHUNK, MOE_TILE), 0)

    def base(u, g):
        return (bi * N_EXPERTS + eg * group + g) * (ntile + 1) + ts * tiles_per_step + u

    def windows(u, w):
        htile = h_ref[u * MOE_TILE:(u + 1) * MOE_TILE, :]
        starts, onehots = [], []
        for g in range(group):
            lo, start = _window_start(cnt_ref[base(u, g)], w, cap)
            posrow = post_ref[g, u]
            hit = (posrow - start == slot) & (posrow >= lo)
            onehots.append(jnp.where(hit, 1.0, 0.0).astype(BF16))
            gates = jnp.sum(jnp.where(hit, afft_ref[g, u], 0.0), axis=1, keepdims=True)
            dst = pl.ds(start, SLOT_CHUNK)
            gate_ref[g, dst, :] = gate_ref[g, dst, :] + jnp.broadcast_to(gates, (SLOT_CHUNK, LANES))
            starts.append(start)
        rows = jnp.dot(jnp.concatenate(onehots, axis=0), htile, preferred_element_type=F32)
        for g in range(group):
            dst = pl.ds(starts[g], SLOT_CHUNK)
            xe_ref[g, dst, :] = xe_ref[g, dst, :] + rows[g * SLOT_CHUNK:(g + 1) * SLOT_CHUNK].astype(BF16)

    nwin = 1
    for u in range(tiles_per_step):
        windows(u, 0)
        for g in range(group):
            nwin = jnp.maximum(nwin, _num_windows(cnt_ref[base(u, g)], cnt_ref[base(u, g) + 1]))

    def overflow(w, _):
        for u in range(tiles_per_step):
            windows(u, w)
        return 0

    lax.fori_loop(1, nwin, overflow, 0)


def _moe_gather(cnt, post, afft, h2, cap, group=8, tiles_per_step=8):
    b, n, d = h2.shape
    ntile = n // MOE_TILE
    per_tile = pl.BlockSpec((None, group, tiles_per_step, 1, MOE_TILE), lambda bi, eg, ts, c: (bi, eg, ts, 0, 0))
    grid_spec = pltpu.PrefetchScalarGridSpec(
        num_scalar_prefetch=1, grid=(b, N_EXPERTS // group, ntile // tiles_per_step),
        in_specs=[per_tile, per_tile,
                  pl.BlockSpec((None, tiles_per_step * MOE_TILE, d), lambda bi, eg, ts, c: (bi, ts, 0))],
        out_specs=[pl.BlockSpec((None, group, cap, d), lambda bi, eg, ts, c: (bi, eg, 0, 0)),
                   pl.BlockSpec((None, group, cap, LANES), lambda bi, eg, ts, c: (bi, eg, 0, 0))])
    return pl.pallas_call(
        functools.partial(_moe_gather_kernel, ntile=ntile, cap=cap, group=group, tiles_per_step=tiles_per_step),
        out_shape=[jax.ShapeDtypeStruct((b, N_EXPERTS, cap, d), BF16),
                   jax.ShapeDtypeStruct((b, N_EXPERTS, cap, LANES), F32)],
        grid_spec=grid_spec, compiler_params=_cparams(("arbitrary", "arbitrary", "arbitrary")), name="moe_gather",
    )(cnt, post, afft, h2)


def _moe_ffn_kernel(xe_ref, gate_ref, wg_ref, wu_ref, wd_ref, y_ref, acc_scr, *, hid_tile):
    j = pl.program_id(2)
    xe = xe_ref[...]
    for k in range(wg_ref.shape[1] // hid_tile):
        cols = slice(k * hid_tile, (k + 1) * hid_tile)
        gate = jnp.dot(xe, wg_ref[:, cols].astype(BF16), preferred_element_type=F32)
        up = jnp.dot(xe, wu_ref[:, cols].astype(BF16), preferred_element_type=F32)
        hid = (_silu(gate) * up).astype(BF16)
        part = jnp.dot(hid, wd_ref[cols, :].astype(BF16), preferred_element_type=F32)
        if k == 0:
            @pl.when(j == 0)
            def _():
                acc_scr[...] = part

            @pl.when(j != 0)
            def _():
                acc_scr[...] += part
        else:
            acc_scr[...] += part

    @pl.when(j == pl.num_programs(2) - 1)
    def _():
        y_ref[...] = (acc_scr[...] * gate_ref[:, 0:1]).astype(y_ref.dtype)


def _moe_ffn(xe, gate, wg, wu, wd, layer, hid_split=1):
    b, ne, cap, d = xe.shape
    dh = wg.shape[3] // hid_split
    return pl.pallas_call(
        functools.partial(_moe_ffn_kernel, hid_tile=256),
        out_shape=jax.ShapeDtypeStruct((b, ne, cap, d), BF16),
        grid=(b, ne, hid_split),
        in_specs=[pl.BlockSpec((None, None, cap, d), lambda bi, e, j: (bi, e, 0, 0)),
                  pl.BlockSpec((None, None, cap, LANES), lambda bi, e, j: (bi, e, 0, 0)),
                  pl.BlockSpec((None, None, d, dh), lambda bi, e, j: (layer, e, 0, j)),
                  pl.BlockSpec((None, None, d, dh), lambda bi, e, j: (layer, e, 0, j)),
                  pl.BlockSpec((None, None, dh, d), lambda bi, e, j: (layer, e, j, 0))],
        out_specs=pl.BlockSpec((None, None, cap, d), lambda bi, e, j: (bi, e, 0, 0)),
        scratch_shapes=[pltpu.VMEM((cap, d), F32)],
        compiler_params=_cparams(("parallel", "parallel", "arbitrary")), name="moe_ffn",
    )(xe, gate, wg, wu, wd)


def _moe_combine_kernel(cnt_ref, pos_ref, x_ref, gt2_ref, y_hbm, fg_ref, o_ref, acc_scr, ybuf, yovf, ring_sem, ovf_sem,
                        *, ntile, cap, final, nbatch, nsteps):
    bi, ts = pl.program_id(0), pl.program_id(1)
    tiles = pos_ref.shape[0] // MOE_TILE
    kw = N_EXPERTS * SLOT_CHUNK
    step = bi * nsteps + ts
    slot = lax.rem(step, 2)

    def window_copies(b_, ts_, w, dst, sem):
        copies = []
        for u in range(tiles):
            for e in range(N_EXPERTS):
                s0 = cnt_ref[(b_ * N_EXPERTS + e) * (ntile + 1) + ts_ * tiles + u]
                start = _window_start(s0, w, cap)[1]
                copies.append(pltpu.make_async_copy(y_hbm.at[b_, e, pl.ds(start, SLOT_CHUNK), :],
                                                    dst(u).at[pl.ds(e * SLOT_CHUNK, SLOT_CHUNK), :], sem))
        return copies

    def ring(b_, ts_, s_):
        return window_copies(b_, ts_, 0, lambda u: ybuf.at[s_, u], ring_sem.at[s_])

    @pl.when(step == 0)
    def _():
        for cp in ring(bi, ts, slot):
            cp.start()

    @pl.when(step + 1 < nbatch * nsteps)
    def _():
        wrap = ts == nsteps - 1
        for cp in ring(jnp.where(wrap, bi + 1, bi), jnp.where(wrap, 0, ts + 1), 1 - slot):
            cp.start()

    for cp in ring(bi, ts, slot):
        cp.wait()

    shift = SLOT_CHUNK.bit_length() - 1
    sel = jnp.where(lax.broadcasted_iota(I32, (LANES, kw), 0) == (lax.broadcasted_iota(I32, (LANES, kw), 1) >> shift),
                    1.0, 0.0).astype(BF16)

    def spread(v):
        low_bits = 5
        hi = (v >> low_bits).astype(F32).astype(BF16)
        lo = (v & ((1 << low_bits) - 1)).astype(F32).astype(BF16)
        return (float(1 << low_bits) * jnp.dot(hi, sel, preferred_element_type=F32)
                + jnp.dot(lo, sel, preferred_element_type=F32))

    within = (lax.broadcasted_iota(I32, (1, kw), 1) & (SLOT_CHUNK - 1)).astype(F32)
    lane = lax.broadcasted_iota(I32, (SUBLANES, LANES), 1)

    def base(u, e):
        return (bi * N_EXPERTS + e) * (ntile + 1) + ts * tiles + u

    def windows(u, w, rows_of_y):
        pos_w = spread(pos_ref[u * MOE_TILE:(u + 1) * MOE_TILE, :] + 1) - 1.0
        starts = jnp.zeros((SUBLANES, LANES), I32)
        los = jnp.zeros((SUBLANES, LANES), I32)
        for e in range(N_EXPERTS):
            lo, start = _window_start(cnt_ref[base(u, e)], w, cap)
            starts = jnp.where(lane == e, start, starts)
            los = jnp.where(lane == e, lo, los)
        want = spread(starts)[0:1] + within
        onehot = jnp.where((pos_w == want) & (pos_w >= spread(los)[0:1]), 1.0, 0.0).astype(BF16)
        return jnp.dot(onehot, rows_of_y, preferred_element_type=F32)

    nwin = 1
    for u in range(tiles):
        acc_scr[u] = windows(u, 0, ybuf[slot, u])
        for e in range(N_EXPERTS):
            nwin = jnp.maximum(nwin, _num_windows(cnt_ref[base(u, e)], cnt_ref[base(u, e) + 1]))

    def overflow(w, _):
        extra = window_copies(bi, ts, w, lambda u: yovf.at[u], ovf_sem.at[0])
        for cp in extra:
            cp.start()
        for cp in extra:
            cp.wait()
        for u in range(tiles):
            acc_scr[u] += windows(u, w, yovf[u])
        return 0

    lax.fori_loop(1, nwin, overflow, 0)
    for u in range(tiles):
        rows = slice(u * MOE_TILE, (u + 1) * MOE_TILE)
        out = x_ref[rows, :] + gt2_ref[...] * acc_scr[u]
        if final:
            out = out * lax.rsqrt(jnp.mean(out * out, axis=-1, keepdims=True) + EPS) * fg_ref[...]
        o_ref[rows, :] = out


def _moe_combine(cnt, pos, x, mods, layer, y, final_g, final, tiles_per_step=4):
    b, n, d = x.shape
    ntile = n // MOE_TILE
    cap = y.shape[2]
    rows = tiles_per_step * MOE_TILE
    grid_spec = pltpu.PrefetchScalarGridSpec(
        num_scalar_prefetch=1, grid=(b, ntile // tiles_per_step),
        in_specs=[pl.BlockSpec((None, rows, LANES), lambda bi, t, c: (bi, t, 0)),
                  pl.BlockSpec((None, rows, d), lambda bi, t, c: (bi, t, 0)),
                  _mod_spec(layer, GATE2),
                  pl.BlockSpec(memory_space=pl.ANY),
                  pl.BlockSpec((1, d), lambda bi, t, c: (0, 0))],
        out_specs=pl.BlockSpec((None, rows, d), lambda bi, t, c: (bi, t, 0)),
        scratch_shapes=[pltpu.VMEM((tiles_per_step, MOE_TILE, d), F32),
                        pltpu.VMEM((2, tiles_per_step, N_EXPERTS * SLOT_CHUNK, d), BF16),
                        pltpu.VMEM((tiles_per_step, N_EXPERTS * SLOT_CHUNK, d), BF16),
                        pltpu.SemaphoreType.DMA((2,)), pltpu.SemaphoreType.DMA((1,))])
    return pl.pallas_call(
        functools.partial(_moe_combine_kernel, ntile=ntile, cap=cap, final=final, nbatch=b,
                          nsteps=ntile // tiles_per_step),
        out_shape=jax.ShapeDtypeStruct((b, n, d), F32),
        grid_spec=grid_spec, compiler_params=_cparams(("arbitrary", "arbitrary")), name="moe_combine",
    )(cnt, pos, x, mods, y, final_g)


def _ec_moe(x_mid, h2, aff, mods, wg, wu, wd, layer, final_g, final):
    b, n, _ = x_mid.shape
    cap = max(1, EC_FACTOR * n // N_EXPERTS)
    ntile = n // MOE_TILE
    pos, post, offs, afft = _route(aff, cap)
    per_tile = lambda a: a.transpose(0, 2, 1, 3).reshape(b, N_EXPERTS, ntile, 1, MOE_TILE)
    starts = offs[:, ::MOE_TILE // CHUNK, 0, :N_EXPERTS].transpose(0, 2, 1)
    cnt = jnp.concatenate([starts, jnp.full((b, N_EXPERTS, 1), cap, I32)], axis=2).reshape(-1)
    xe, gate = _moe_gather(cnt, per_tile(post), per_tile(afft), h2, cap)
    y = _moe_ffn(xe, gate, wg, wu, wd, layer)
    return _moe_combine(cnt, pos, x_mid, mods, layer, y, final_g, final)


def _rope_table(n):
    rows = n // GRID_W
    row = jnp.repeat(jnp.arange(rows), GRID_W).astype(F32)
    col = jnp.tile(jnp.arange(GRID_W), rows).astype(F32)
    nf = ATTN_DIM // 4
    inv = ROPE_BASE ** (-jnp.arange(nf, dtype=F32) / nf)
    ang = jnp.concatenate([row[:, None] * inv, col[:, None] * inv], axis=-1)
    cos, sin = jnp.cos(ang), jnp.sin(ang)
    reps = LANES // ATTN_DIM
    return jnp.concatenate([jnp.tile(jnp.concatenate([cos, cos], -1), (1, reps)),
                            jnp.tile(jnp.concatenate([-sin, sin], -1), (1, reps))], axis=-1)


def _pad_lanes(a):
    return jnp.pad(a, ((0, 0), (0, LANES - a.shape[1])))


def kernel(x, c, ctx, c_ctx, w_mod, b_mod, norm_mix_g, norm_ffn_g, final_norm_g, ab_w_in, ab_conv_w, ab_gate_b,
           ab_head_g, ab_sink, ab_w_out, gm_w_in, gm_ln_g, gm_ln_b, gm_w_s, gm_b_s, gm_w_out, moe_w_router,
           moe_w_gate, moe_w_up, moe_w_down):
    b, n, d = x.shape
    depth = w_mod.shape[0]
    assert depth <= 2, "context stream is only advanced for deeper stacks; not supported here"
    cond = jnp.zeros((BF16_ROWS, d), F32).at[:b].set(c).at[b].set(c_ctx)
    mods = _adaln(cond, w_mod, b_mod).reshape(depth, BF16_ROWS, 6, 1, d)

    row = lambda v: v.reshape(1, -1)
    for layer in range(depth):
        g1, g2 = row(norm_mix_g[layer]), row(norm_ffn_g[layer])
        wr = _hi_lo_rhs(_pad_lanes(moe_w_router[layer]))
        if layer % 2 == 0:
            e = layer // 2
            w_in = ab_w_in[e]
            g_lo = 4 * LSTM_WIDTH
            w_main = jnp.concatenate([w_in[:, :g_lo], w_in[:, g_lo + N_GATES:]], axis=1).astype(BF16)
            w_gate = _hi_lo_rhs(_pad_lanes(w_in[:, g_lo:g_lo + N_GATES]))
            p, gts, vt, avt = _modmm(x, g1, mods, layer, w_main, w_gate, tm=1024, name="ab_in")
            pc, gtc, vtc, avtc = _modmm(ctx, g1, mods, layer, w_main, w_gate, mod_row=b, name="ab_in_ctx")
            hf, hb = _mlstm(_conv_silu(p, ab_conv_w[e]), vt, gts, _conv_silu(pc, ab_conv_w[e]), vtc, gtc,
                            _pad_lanes(row(ab_gate_b[e])))
            at = _attn(p, avt, pc, avtc, _rope_table(n), _pad_lanes(row(ab_sink[e])))
            x_mid, h2, aff = _ab_out(hf, hb, p, at, x, row(ab_head_g[e]), ab_w_out[e].astype(BF16),
                                     mods, layer, g2, wr)
        else:
            o = layer // 2
            uv = _modmm(x, g1, mods, layer, gm_w_in[o].astype(BF16), act="gelu", tm=1024, name="gm_in")
            x_mid, h2, aff = _gm_out(uv, x, row(gm_ln_g[o]), row(gm_ln_b[o]), gm_w_s[o].astype(BF16),
                                     _pad_lanes(gm_b_s[o].T), gm_w_out[o].astype(BF16), mods, layer, g2, wr)
        x = _ec_moe(x_mid, h2, aff, mods, moe_w_gate, moe_w_up, moe_w_down, layer,
                    row(final_norm_g), layer == depth - 1)
    return x
```
